```python
import math
import jax, jax.numpy as jnp
from jax import lax
import numpy as np

D_MODEL = 1024
BATCH = 8
SEQ = 8192
DEPTH = 2

D_FF = ((8 * D_MODEL // 3 + 127) // 128) * 128
D_CONV = D_MODEL
CONV_WIDTH = 31
CONV_GROUPS = 8
D_RNN = ((4 * D_MODEL // 3 + 127) // 128) * 128
RNN_BLOCKS = 16
RNN_BLOCK = D_RNN // RNN_BLOCKS
RNN_CONV_WIDTH = 4
RG_LRU_C = 8.0
LN_EPS = 1e-5
D_IN = 2 * D_CONV + 2 * D_RNN + 2 * D_MODEL
SPLITS = [D_CONV, 2 * D_CONV, 2 * D_CONV + D_RNN, 2 * D_CONV + 2 * D_RNN, 2 * D_CONV + 2 * D_RNN + D_MODEL]
DEEPNORM_ALPHA = (2 * DEPTH) ** 0.25
DEEPNORM_BETA = (8 * DEPTH) ** -0.25

kernel_name = "hybrid_conformer_conv_rglru_deepnorm"


def layer_norm(x, g, b):
    xf = x.astype(jnp.float32)
    mu = jnp.mean(xf, axis=-1, keepdims=True)
    xc = xf - mu
    var = jnp.mean(xc * xc, axis=-1, keepdims=True)
    y = xc * lax.rsqrt(var + LN_EPS) * g.astype(jnp.float32) + b.astype(jnp.float32)
    return y.astype(x.dtype)


def group_norm(x, g, b, groups):
    B, S, C = x.shape
    xf = x.astype(jnp.float32).reshape(B, S, groups, C // groups)
    mu = jnp.mean(xf, axis=-1, keepdims=True)
    xc = xf - mu
    var = jnp.mean(xc * xc, axis=-1, keepdims=True)
    y = (xc * lax.rsqrt(var + LN_EPS)).reshape(B, S, C)
    y = y * g.astype(jnp.float32) + b.astype(jnp.float32)
    return y.astype(x.dtype)


def swiglu_ffn(x, w_gu, w_down):
    gate, up = jnp.split(x @ w_gu, 2, axis=-1)
    return (jax.nn.silu(gate) * up) @ w_down


def causal_depthwise_conv(x, w, b):
    K, C = w.shape
    y = lax.conv_general_dilated(
        x, w[:, None, :].astype(x.dtype),
        window_strides=(1,), padding=[(K - 1, 0)],
        dimension_numbers=("NWC", "WIO", "NWC"),
        feature_group_count=C)
    return y + b


def block_diag_linear(x, w, b):
    B, S, _ = x.shape
    xh = x.reshape(B, S, RNN_BLOCKS, RNN_BLOCK)
    y = jnp.einsum("bshi,hij->bshj", xh, w)
    return y.reshape(B, S, D_RNN) + b


def rg_lru(x, w_a, b_a, w_x, b_x, lam):
    r = jax.nn.sigmoid(block_diag_linear(x, w_a, b_a)).astype(jnp.float32)
    i = jax.nn.sigmoid(block_diag_linear(x, w_x, b_x)).astype(jnp.float32)
    log_a = -RG_LRU_C * r * jax.nn.softplus(-lam.astype(jnp.float32))
    a = jnp.exp(log_a)
    mult = jnp.sqrt(-jnp.expm1(2.0 * log_a))
    u = mult * (i * x.astype(jnp.float32))

    def combine(left, right):
        a1, b1 = left
        a2, b2 = right
        return a1 * a2, a2 * b1 + b2

    _, h = lax.associative_scan(combine, (a, u), axis=1)
    return h.astype(x.dtype)


def hybrid_mixer(x, w_in, b_in, dw_w, dw_b, gn_g, gn_b, conv_w_proj,
                 rc_w, rc_b, w_a, b_a, w_x, b_x, lam, rnn_w_proj, w_out):
    u = x @ w_in + b_in
    c_val, c_gate, r_x, r_gate, g_c, g_r = jnp.split(u, SPLITS, axis=-1)
    c = c_val * jax.nn.sigmoid(c_gate)
    c = causal_depthwise_conv(c, dw_w, dw_b)
    c = jax.nn.silu(group_norm(c, gn_g, gn_b, CONV_GROUPS))
    y_conv = c @ conv_w_proj
    r = causal_depthwise_conv(r_x, rc_w, rc_b)
    h = rg_lru(r, w_a, b_a, w_x, b_x, lam)
    y_rnn = (h * jax.nn.gelu(r_gate)) @ rnn_w_proj
    m = jax.nn.sigmoid(g_c) * y_conv + jax.nn.sigmoid(g_r) * y_rnn
    return m @ w_out


def _fwd_setup_inputs(seed: int = 0) -> dict:
    key = jax.random.key(seed)
    ks = jax.random.split(key, 32)
    L = DEPTH
    f32 = jnp.float32

    def nrm(k, shape, scale):
        return jax.random.normal(k, shape, f32) * scale

    def gain(k, shape):
        return 1.0 + nrm(k, shape, 0.02)

    u = jax.random.uniform(ks[18], (L, D_RNN), f32, minval=0.9, maxval=0.999)
    log_a = jnp.log(u) / RG_LRU_C
    lam = log_a - jnp.log(-jnp.expm1(log_a))

    return {
        "x": nrm(ks[0], (BATCH, SEQ, D_MODEL), 1.0),
        "ffn1_w_gu": nrm(ks[1], (L, D_MODEL, 2 * D_FF), D_MODEL ** -0.5),
        "ffn1_w_down": nrm(ks[2], (L, D_FF, D_MODEL), DEEPNORM_BETA * D_FF ** -0.5),
        "ln1_g": gain(ks[3], (L, D_MODEL)),
        "ln1_b": nrm(ks[4], (L, D_MODEL), 0.02),
        "mix_w_in": nrm(ks[5], (L, D_MODEL, D_IN), D_MODEL ** -0.5),
        "mix_b_in": nrm(ks[6], (L, D_IN), 0.02),
        "conv_dw_w": nrm(ks[7], (L, CONV_WIDTH, D_CONV), CONV_WIDTH ** -0.5),
        "conv_dw_b": nrm(ks[8], (L, D_CONV), 0.02),
        "conv_gn_g": gain(ks[9], (L, D_CONV)),
        "conv_gn_b": nrm(ks[10], (L, D_CONV), 0.02),
        "conv_w_proj": nrm(ks[11], (L, D_CONV, D_MODEL), DEEPNORM_BETA * D_CONV ** -0.5),
        "rnn_conv_w": nrm(ks[12], (L, RNN_CONV_WIDTH, D_RNN), RNN_CONV_WIDTH ** -0.5),
        "rnn_conv_b": nrm(ks[13], (L, D_RNN), 0.02),
        "rnn_w_a": nrm(ks[14], (L, RNN_BLOCKS, RNN_BLOCK, RNN_BLOCK), RNN_BLOCK ** -0.5),
        "rnn_b_a": nrm(ks[15], (L, D_RNN), 0.02),
        "rnn_w_x": nrm(ks[16], (L, RNN_BLOCKS, RNN_BLOCK, RNN_BLOCK), RNN_BLOCK ** -0.5),
        "rnn_b_x": nrm(ks[17], (L, D_RNN), 0.02),
        "rnn_lambda": lam,
        "rnn_w_proj": nrm(ks[19], (L, D_RNN, D_MODEL), DEEPNORM_BETA * D_RNN ** -0.5),
        "mix_w_out": nrm(ks[20], (L, D_MODEL, D_MODEL), DEEPNORM_BETA * D_MODEL ** -0.5),
        "ln2_g": gain(ks[21], (L, D_MODEL)),
        "ln2_b": nrm(ks[22], (L, D_MODEL), 0.02),
        "ffn2_w_gu": nrm(ks[23], (L, D_MODEL, 2 * D_FF), D_MODEL ** -0.5),
        "ffn2_w_down": nrm(ks[24], (L, D_FF, D_MODEL), DEEPNORM_BETA * D_FF ** -0.5),
        "ln3_g": gain(ks[25], (L, D_MODEL)),
        "ln3_b": nrm(ks[26], (L, D_MODEL), 0.02),
    }


def _fwd_reference(x, ffn1_w_gu, ffn1_w_down, ln1_g, ln1_b, mix_w_in, mix_b_in,
              conv_dw_w, conv_dw_b, conv_gn_g, conv_gn_b, conv_w_proj,
              rnn_conv_w, rnn_conv_b, rnn_w_a, rnn_b_a, rnn_w_x, rnn_b_x,
              rnn_lambda, rnn_w_proj, mix_w_out, ln2_g, ln2_b,
              ffn2_w_gu, ffn2_w_down, ln3_g, ln3_b):
    for l in range(DEPTH):
        x = layer_norm(DEEPNORM_ALPHA * x + 0.5 * swiglu_ffn(x, ffn1_w_gu[l], ffn1_w_down[l]),
                       ln1_g[l], ln1_b[l])
        mix = hybrid_mixer(x, mix_w_in[l], mix_b_in[l], conv_dw_w[l], conv_dw_b[l],
                           conv_gn_g[l], conv_gn_b[l], conv_w_proj[l],
                           rnn_conv_w[l], rnn_conv_b[l], rnn_w_a[l], rnn_b_a[l],
                           rnn_w_x[l], rnn_b_x[l], rnn_lambda[l], rnn_w_proj[l], mix_w_out[l])
        x = layer_norm(DEEPNORM_ALPHA * x + mix, ln2_g[l], ln2_b[l])
        x = layer_norm(DEEPNORM_ALPHA * x + 0.5 * swiglu_ffn(x, ffn2_w_gu[l], ffn2_w_down[l]),
                       ln3_g[l], ln3_b[l])
    return x


import jax as _jax
import jax.numpy as _jnp

TWIN_FORMAT = 'train_step'
FWD_PARAMS = ['x', 'ffn1_w_gu', 'ffn1_w_down', 'ln1_g', 'ln1_b', 'mix_w_in', 'mix_b_in', 'conv_dw_w', 'conv_dw_b', 'conv_gn_g', 'conv_gn_b', 'conv_w_proj', 'rnn_conv_w', 'rnn_conv_b', 'rnn_w_a', 'rnn_b_a', 'rnn_w_x', 'rnn_b_x', 'rnn_lambda', 'rnn_w_proj', 'mix_w_out', 'ln2_g', 'ln2_b', 'ffn2_w_gu', 'ffn2_w_down', 'ln3_g', 'ln3_b']
TWIN_WEIGHTS = ['ffn1_w_gu', 'ffn1_w_down', 'ln1_g', 'ln1_b', 'mix_w_in', 'mix_b_in', 'conv_dw_w', 'conv_dw_b', 'conv_gn_g', 'conv_gn_b', 'conv_w_proj', 'rnn_conv_w', 'rnn_conv_b', 'rnn_w_a', 'rnn_b_a', 'rnn_w_x', 'rnn_b_x', 'rnn_lambda', 'rnn_w_proj', 'mix_w_out', 'ln2_g', 'ln2_b', 'ffn2_w_gu', 'ffn2_w_down', 'ln3_g', 'ln3_b']
TWIN_DIFF_INPUT = 'x'
TWIN_INPUTS = ['x', 'ffn1_w_gu', 'ffn1_w_down', 'ln1_g', 'ln1_b', 'mix_w_in', 'mix_b_in', 'conv_dw_w', 'conv_dw_b', 'conv_gn_g', 'conv_gn_b', 'conv_w_proj', 'rnn_conv_w', 'rnn_conv_b', 'rnn_w_a', 'rnn_b_a', 'rnn_w_x', 'rnn_b_x', 'rnn_lambda', 'rnn_w_proj', 'mix_w_out', 'ln2_g', 'ln2_b', 'ffn2_w_gu', 'ffn2_w_down', 'ln3_g', 'ln3_b', 'loss_target', 'm_ffn1_w_gu', 'm_ffn1_w_down', 'm_ln1_g', 'm_ln1_b', 'm_mix_w_in', 'm_mix_b_in', 'm_conv_dw_w', 'm_conv_dw_b', 'm_conv_gn_g', 'm_conv_gn_b', 'm_conv_w_proj', 'm_rnn_conv_w', 'm_rnn_conv_b', 'm_rnn_w_a', 'm_rnn_b_a', 'm_rnn_w_x', 'm_rnn_b_x', 'm_rnn_lambda', 'm_rnn_w_proj', 'm_mix_w_out', 'm_ln2_g', 'm_ln2_b', 'm_ffn2_w_gu', 'm_ffn2_w_down', 'm_ln3_g', 'm_ln3_b', 'v_ffn1_w_gu', 'v_ffn1_w_down', 'v_ln1_g', 'v_ln1_b', 'v_mix_w_in', 'v_mix_b_in', 'v_conv_dw_w', 'v_conv_dw_b', 'v_conv_gn_g', 'v_conv_gn_b', 'v_conv_w_proj', 'v_rnn_conv_w', 'v_rnn_conv_b', 'v_rnn_w_a', 'v_rnn_b_a', 'v_rnn_w_x', 'v_rnn_b_x', 'v_rnn_lambda', 'v_rnn_w_proj', 'v_mix_w_out', 'v_ln2_g', 'v_ln2_b', 'v_ffn2_w_gu', 'v_ffn2_w_down', 'v_ln3_g', 'v_ln3_b']
TWIN_OUTPUTS = ['loss', 'grad_x', 'grad_ffn1_w_gu', 'grad_ffn1_w_down', 'grad_ln1_g', 'grad_ln1_b', 'grad_mix_w_in', 'grad_mix_b_in', 'grad_conv_dw_w', 'grad_conv_dw_b', 'grad_conv_gn_g', 'grad_conv_gn_b', 'grad_conv_w_proj', 'grad_rnn_conv_w', 'grad_rnn_conv_b', 'grad_rnn_w_a', 'grad_rnn_b_a', 'grad_rnn_w_x', 'grad_rnn_b_x', 'grad_rnn_lambda', 'grad_rnn_w_proj', 'grad_mix_w_out', 'grad_ln2_g', 'grad_ln2_b', 'grad_ffn2_w_gu', 'grad_ffn2_w_down', 'grad_ln3_g', 'grad_ln3_b', 'delta_ffn1_w_gu', 'delta_ffn1_w_down', 'delta_ln1_g', 'delta_ln1_b', 'delta_mix_w_in', 'delta_mix_b_in', 'delta_conv_dw_w', 'delta_conv_dw_b', 'delta_conv_gn_g', 'delta_conv_gn_b', 'delta_conv_w_proj', 'delta_rnn_conv_w', 'delta_rnn_conv_b', 'delta_rnn_w_a', 'delta_rnn_b_a', 'delta_rnn_w_x', 'delta_rnn_b_x', 'delta_rnn_lambda', 'delta_rnn_w_proj', 'delta_mix_w_out', 'delta_ln2_g', 'delta_ln2_b', 'delta_ffn2_w_gu', 'delta_ffn2_w_down', 'delta_ln3_g', 'delta_ln3_b', 'new_m_ffn1_w_gu', 'new_m_ffn1_w_down', 'new_m_ln1_g', 'new_m_ln1_b', 'new_m_mix_w_in', 'new_m_mix_b_in', 'new_m_conv_dw_w', 'new_m_conv_dw_b', 'new_m_conv_gn_g', 'new_m_conv_gn_b', 'new_m_conv_w_proj', 'new_m_rnn_conv_w', 'new_m_rnn_conv_b', 'new_m_rnn_w_a', 'new_m_rnn_b_a', 'new_m_rnn_w_x', 'new_m_rnn_b_x', 'new_m_rnn_lambda', 'new_m_rnn_w_proj', 'new_m_mix_w_out', 'new_m_ln2_g', 'new_m_ln2_b', 'new_m_ffn2_w_gu', 'new_m_ffn2_w_down', 'new_m_ln3_g', 'new_m_ln3_b', 'new_v_ffn1_w_gu', 'new_v_ffn1_w_down', 'new_v_ln1_g', 'new_v_ln1_b', 'new_v_mix_w_in', 'new_v_mix_b_in', 'new_v_conv_dw_w', 'new_v_conv_dw_b', 'new_v_conv_gn_g', 'new_v_conv_gn_b', 'new_v_conv_w_proj', 'new_v_rnn_conv_w', 'new_v_rnn_conv_b', 'new_v_rnn_w_a', 'new_v_rnn_b_a', 'new_v_rnn_w_x', 'new_v_rnn_b_x', 'new_v_rnn_lambda', 'new_v_rnn_w_proj', 'new_v_mix_w_out', 'new_v_ln2_g', 'new_v_ln2_b', 'new_v_ffn2_w_gu', 'new_v_ffn2_w_down', 'new_v_ln3_g', 'new_v_ln3_b']
TWIN_LEAF_KINDS = {'loss': 'loss', 'grad_x': 'grad_x', 'grad_ffn1_w_gu': 'grad_w', 'grad_ffn1_w_down': 'grad_w', 'grad_ln1_g': 'grad_w', 'grad_ln1_b': 'grad_w', 'grad_mix_w_in': 'grad_w', 'grad_mix_b_in': 'grad_w', 'grad_conv_dw_w': 'grad_w', 'grad_conv_dw_b': 'grad_w', 'grad_conv_gn_g': 'grad_w', 'grad_conv_gn_b': 'grad_w', 'grad_conv_w_proj': 'grad_w', 'grad_rnn_conv_w': 'grad_w', 'grad_rnn_conv_b': 'grad_w', 'grad_rnn_w_a': 'grad_w', 'grad_rnn_b_a': 'grad_w', 'grad_rnn_w_x': 'grad_w', 'grad_rnn_b_x': 'grad_w', 'grad_rnn_lambda': 'grad_w', 'grad_rnn_w_proj': 'grad_w', 'grad_mix_w_out': 'grad_w', 'grad_ln2_g': 'grad_w', 'grad_ln2_b': 'grad_w', 'grad_ffn2_w_gu': 'grad_w', 'grad_ffn2_w_down': 'grad_w', 'grad_ln3_g': 'grad_w', 'grad_ln3_b': 'grad_w', 'delta_ffn1_w_gu': 'delta_w', 'delta_ffn1_w_down': 'delta_w', 'delta_ln1_g': 'delta_w', 'delta_ln1_b': 'delta_w', 'delta_mix_w_in': 'delta_w', 'delta_mix_b_in': 'delta_w', 'delta_conv_dw_w': 'delta_w', 'delta_conv_dw_b': 'delta_w', 'delta_conv_gn_g': 'delta_w', 'delta_conv_gn_b': 'delta_w', 'delta_conv_w_proj': 'delta_w', 'delta_rnn_conv_w': 'delta_w', 'delta_rnn_conv_b': 'delta_w', 'delta_rnn_w_a': 'delta_w', 'delta_rnn_b_a': 'delta_w', 'delta_rnn_w_x': 'delta_w', 'delta_rnn_b_x': 'delta_w', 'delta_rnn_lambda': 'delta_w', 'delta_rnn_w_proj': 'delta_w', 'delta_mix_w_out': 'delta_w', 'delta_ln2_g': 'delta_w', 'delta_ln2_b': 'delta_w', 'delta_ffn2_w_gu': 'delta_w', 'delta_ffn2_w_down': 'delta_w', 'delta_ln3_g': 'delta_w', 'delta_ln3_b': 'delta_w', 'new_m_ffn1_w_gu': 'new_m', 'new_m_ffn1_w_down': 'new_m', 'new_m_ln1_g': 'new_m', 'new_m_ln1_b': 'new_m', 'new_m_mix_w_in': 'new_m', 'new_m_mix_b_in': 'new_m', 'new_m_conv_dw_w': 'new_m', 'new_m_conv_dw_b': 'new_m', 'new_m_conv_gn_g': 'new_m', 'new_m_conv_gn_b': 'new_m', 'new_m_conv_w_proj': 'new_m', 'new_m_rnn_conv_w': 'new_m', 'new_m_rnn_conv_b': 'new_m', 'new_m_rnn_w_a': 'new_m', 'new_m_rnn_b_a': 'new_m', 'new_m_rnn_w_x': 'new_m', 'new_m_rnn_b_x': 'new_m', 'new_m_rnn_lambda': 'new_m', 'new_m_rnn_w_proj': 'new_m', 'new_m_mix_w_out': 'new_m', 'new_m_ln2_g': 'new_m', 'new_m_ln2_b': 'new_m', 'new_m_ffn2_w_gu': 'new_m', 'new_m_ffn2_w_down': 'new_m', 'new_m_ln3_g': 'new_m', 'new_m_ln3_b': 'new_m', 'new_v_ffn1_w_gu': 'new_v', 'new_v_ffn1_w_down': 'new_v', 'new_v_ln1_g': 'new_v', 'new_v_ln1_b': 'new_v', 'new_v_mix_w_in': 'new_v', 'new_v_mix_b_in': 'new_v', 'new_v_conv_dw_w': 'new_v', 'new_v_conv_dw_b': 'new_v', 'new_v_conv_gn_g': 'new_v', 'new_v_conv_gn_b': 'new_v', 'new_v_conv_w_proj': 'new_v', 'new_v_rnn_conv_w': 'new_v', 'new_v_rnn_conv_b': 'new_v', 'new_v_rnn_w_a': 'new_v', 'new_v_rnn_b_a': 'new_v', 'new_v_rnn_w_x': 'new_v', 'new_v_rnn_b_x': 'new_v', 'new_v_rnn_lambda': 'new_v', 'new_v_rnn_w_proj': 'new_v', 'new_v_mix_w_out': 'new_v', 'new_v_ln2_g': 'new_v', 'new_v_ln2_b': 'new_v', 'new_v_ffn2_w_gu': 'new_v', 'new_v_ffn2_w_down': 'new_v', 'new_v_ln3_g': 'new_v', 'new_v_ln3_b': 'new_v'}


def _forward(args):
    return _fwd_reference(*[args[k] for k in FWD_PARAMS])


def _output_shape():
    def fwd():
        inp = _fwd_setup_inputs(0)
        return _fwd_reference(*[inp[k] for k in FWD_PARAMS])
    out = _jax.eval_shape(fwd)
    return out.shape, out.dtype

N_MICROBATCH = 1
ADAM_LR = 0.001
ADAM_B1 = 0.9
ADAM_B2 = 0.999
ADAM_EPS = 1e-08
ADAM_WD = 0.01
ADAM_STEP = 10
PER_EXAMPLE_BATCH_AXIS = {'x': 0, 'loss_target': 0}
SHARED_INPUTS = []
_WEIGHT_DTYPES = {'ffn1_w_gu': _jnp.float32, 'ffn1_w_down': _jnp.float32, 'ln1_g': _jnp.float32, 'ln1_b': _jnp.float32, 'mix_w_in': _jnp.float32, 'mix_b_in': _jnp.float32, 'conv_dw_w': _jnp.float32, 'conv_dw_b': _jnp.float32, 'conv_gn_g': _jnp.float32, 'conv_gn_b': _jnp.float32, 'conv_w_proj': _jnp.float32, 'rnn_conv_w': _jnp.float32, 'rnn_conv_b': _jnp.float32, 'rnn_w_a': _jnp.float32, 'rnn_b_a': _jnp.float32, 'rnn_w_x': _jnp.float32, 'rnn_b_x': _jnp.float32, 'rnn_lambda': _jnp.float32, 'rnn_w_proj': _jnp.float32, 'mix_w_out': _jnp.float32, 'ln2_g': _jnp.float32, 'ln2_b': _jnp.float32, 'ffn2_w_gu': _jnp.float32, 'ffn2_w_down': _jnp.float32, 'ln3_g': _jnp.float32, 'ln3_b': _jnp.float32}
MOMENT_SCALE = {'ffn1_w_gu': 1.628456e-02, 'ffn1_w_down': 5.316007e-02, 'ln1_g': 2.252432e+00, 'ln1_b': 9.694640e-01, 'mix_w_in': 9.401035e-03, 'mix_b_in': 7.916571e-02, 'conv_dw_w': 1.530520e-02, 'conv_dw_b': 6.983930e-02, 'conv_gn_g': 2.866952e-02, 'conv_gn_b': 4.208244e-02, 'conv_w_proj': 3.829711e-02, 'rnn_conv_w': 1.163268e-02, 'rnn_conv_b': 1.608526e-01, 'rnn_w_a': 4.547249e-03, 'rnn_b_a': 3.105387e-03, 'rnn_w_x': 8.310932e-03, 'rnn_b_x': 4.197084e-03, 'rnn_lambda': 5.486503e-03, 'rnn_w_proj': 2.939889e-02, 'mix_w_out': 4.927900e-02, 'ln2_g': 2.264107e+00, 'ln2_b': 9.511273e-01, 'ffn2_w_gu': 1.623158e-02, 'ffn2_w_down': 5.310554e-02, 'ln3_g': 4.535835e+01, 'ln3_b': 2.393397e+00}


def _to_microbatches(a, axis):
    t = _jnp.moveaxis(a, axis, 0)
    t = t.reshape((N_MICROBATCH, t.shape[0] // N_MICROBATCH) + t.shape[1:])
    return _jnp.moveaxis(t, 1, axis + 1)


def setup_inputs(seed: int = 0) -> dict:
    inp = _fwd_setup_inputs(seed)
    key = _jax.random.fold_in(_jax.random.key(seed), 7919)
    shape, _ = _output_shape()
    out = dict(inp)
    out["loss_target"] = _jax.random.normal(_jax.random.fold_in(key, 0), shape, _jnp.float32)
    for i, name in enumerate(TWIN_WEIGHTS):
        w = inp[name].astype(_jnp.float32)
        if MOMENT_SCALE is None:
            s = _jnp.sqrt(_jnp.mean(_jnp.square(w)) + 1e-30)
        else:
            s = MOMENT_SCALE[name]
        km, kv = _jax.random.split(_jax.random.fold_in(key, i + 1))
        out[name] = w
        out["m_" + name] = s * _jax.random.normal(km, w.shape, _jnp.float32)
        out["v_" + name] = (s * s) * _jax.random.uniform(kv, w.shape, _jnp.float32, 0.5, 1.5)
    if N_MICROBATCH > 1:
        for name, axis in PER_EXAMPLE_BATCH_AXIS.items():
            out[name] = _to_microbatches(out[name], axis)
    return {'x': out['x'], 'ffn1_w_gu': out['ffn1_w_gu'], 'ffn1_w_down': out['ffn1_w_down'], 'ln1_g': out['ln1_g'], 'ln1_b': out['ln1_b'], 'mix_w_in': out['mix_w_in'], 'mix_b_in': out['mix_b_in'], 'conv_dw_w': out['conv_dw_w'], 'conv_dw_b': out['conv_dw_b'], 'conv_gn_g': out['conv_gn_g'], 'conv_gn_b': out['conv_gn_b'], 'conv_w_proj': out['conv_w_proj'], 'rnn_conv_w': out['rnn_conv_w'], 'rnn_conv_b': out['rnn_conv_b'], 'rnn_w_a': out['rnn_w_a'], 'rnn_b_a': out['rnn_b_a'], 'rnn_w_x': out['rnn_w_x'], 'rnn_b_x': out['rnn_b_x'], 'rnn_lambda': out['rnn_lambda'], 'rnn_w_proj': out['rnn_w_proj'], 'mix_w_out': out['mix_w_out'], 'ln2_g': out['ln2_g'], 'ln2_b': out['ln2_b'], 'ffn2_w_gu': out['ffn2_w_gu'], 'ffn2_w_down': out['ffn2_w_down'], 'ln3_g': out['ln3_g'], 'ln3_b': out['ln3_b'], 'loss_target': out['loss_target'], 'm_ffn1_w_gu': out['m_ffn1_w_gu'], 'm_ffn1_w_down': out['m_ffn1_w_down'], 'm_ln1_g': out['m_ln1_g'], 'm_ln1_b': out['m_ln1_b'], 'm_mix_w_in': out['m_mix_w_in'], 'm_mix_b_in': out['m_mix_b_in'], 'm_conv_dw_w': out['m_conv_dw_w'], 'm_conv_dw_b': out['m_conv_dw_b'], 'm_conv_gn_g': out['m_conv_gn_g'], 'm_conv_gn_b': out['m_conv_gn_b'], 'm_conv_w_proj': out['m_conv_w_proj'], 'm_rnn_conv_w': out['m_rnn_conv_w'], 'm_rnn_conv_b': out['m_rnn_conv_b'], 'm_rnn_w_a': out['m_rnn_w_a'], 'm_rnn_b_a': out['m_rnn_b_a'], 'm_rnn_w_x': out['m_rnn_w_x'], 'm_rnn_b_x': out['m_rnn_b_x'], 'm_rnn_lambda': out['m_rnn_lambda'], 'm_rnn_w_proj': out['m_rnn_w_proj'], 'm_mix_w_out': out['m_mix_w_out'], 'm_ln2_g': out['m_ln2_g'], 'm_ln2_b': out['m_ln2_b'], 'm_ffn2_w_gu': out['m_ffn2_w_gu'], 'm_ffn2_w_down': out['m_ffn2_w_down'], 'm_ln3_g': out['m_ln3_g'], 'm_ln3_b': out['m_ln3_b'], 'v_ffn1_w_gu': out['v_ffn1_w_gu'], 'v_ffn1_w_down': out['v_ffn1_w_down'], 'v_ln1_g': out['v_ln1_g'], 'v_ln1_b': out['v_ln1_b'], 'v_mix_w_in': out['v_mix_w_in'], 'v_mix_b_in': out['v_mix_b_in'], 'v_conv_dw_w': out['v_conv_dw_w'], 'v_conv_dw_b': out['v_conv_dw_b'], 'v_conv_gn_g': out['v_conv_gn_g'], 'v_conv_gn_b': out['v_conv_gn_b'], 'v_conv_w_proj': out['v_conv_w_proj'], 'v_rnn_conv_w': out['v_rnn_conv_w'], 'v_rnn_conv_b': out['v_rnn_conv_b'], 'v_rnn_w_a': out['v_rnn_w_a'], 'v_rnn_b_a': out['v_rnn_b_a'], 'v_rnn_w_x': out['v_rnn_w_x'], 'v_rnn_b_x': out['v_rnn_b_x'], 'v_rnn_lambda': out['v_rnn_lambda'], 'v_rnn_w_proj': out['v_rnn_w_proj'], 'v_mix_w_out': out['v_mix_w_out'], 'v_ln2_g': out['v_ln2_g'], 'v_ln2_b': out['v_ln2_b'], 'v_ffn2_w_gu': out['v_ffn2_w_gu'], 'v_ffn2_w_down': out['v_ffn2_w_down'], 'v_ln3_g': out['v_ln3_g'], 'v_ln3_b': out['v_ln3_b']}


def _loss(weights, diff, rest, loss_target):
    with _jax.named_scope("forward"):
        args = {**rest, TWIN_DIFF_INPUT: diff, **{k: w.astype(_WEIGHT_DTYPES[k]) for k, w in weights.items()}}
        y = _forward(args)
    with _jax.named_scope("loss_head"):
        err = _jnp.square(y.astype(_jnp.float32) - loss_target)
        return 0.5 * _jnp.sum(_jnp.mean(err, axis=-1)) if err.ndim else 0.5 * err


def _adamw(w, g, m, v):
    m = ADAM_B1 * m + (1.0 - ADAM_B1) * g
    v = ADAM_B2 * v + (1.0 - ADAM_B2) * _jnp.square(g)
    m_hat = m / (1.0 - ADAM_B1 ** ADAM_STEP)
    v_hat = v / (1.0 - ADAM_B2 ** ADAM_STEP)
    delta = -ADAM_LR * (m_hat / (_jnp.sqrt(v_hat) + ADAM_EPS) + ADAM_WD * w)
    return delta, m, v


def reference(x, ffn1_w_gu, ffn1_w_down, ln1_g, ln1_b, mix_w_in, mix_b_in, conv_dw_w, conv_dw_b, conv_gn_g, conv_gn_b, conv_w_proj, rnn_conv_w, rnn_conv_b, rnn_w_a, rnn_b_a, rnn_w_x, rnn_b_x, rnn_lambda, rnn_w_proj, mix_w_out, ln2_g, ln2_b, ffn2_w_gu, ffn2_w_down, ln3_g, ln3_b, loss_target, m_ffn1_w_gu, m_ffn1_w_down, m_ln1_g, m_ln1_b, m_mix_w_in, m_mix_b_in, m_conv_dw_w, m_conv_dw_b, m_conv_gn_g, m_conv_gn_b, m_conv_w_proj, m_rnn_conv_w, m_rnn_conv_b, m_rnn_w_a, m_rnn_b_a, m_rnn_w_x, m_rnn_b_x, m_rnn_lambda, m_rnn_w_proj, m_mix_w_out, m_ln2_g, m_ln2_b, m_ffn2_w_gu, m_ffn2_w_down, m_ln3_g, m_ln3_b, v_ffn1_w_gu, v_ffn1_w_down, v_ln1_g, v_ln1_b, v_mix_w_in, v_mix_b_in, v_conv_dw_w, v_conv_dw_b, v_conv_gn_g, v_conv_gn_b, v_conv_w_proj, v_rnn_conv_w, v_rnn_conv_b, v_rnn_w_a, v_rnn_b_a, v_rnn_w_x, v_rnn_b_x, v_rnn_lambda, v_rnn_w_proj, v_mix_w_out, v_ln2_g, v_ln2_b, v_ffn2_w_gu, v_ffn2_w_down, v_ln3_g, v_ln3_b):
    given = dict(x=x, ffn1_w_gu=ffn1_w_gu, ffn1_w_down=ffn1_w_down, ln1_g=ln1_g, ln1_b=ln1_b, mix_w_in=mix_w_in, mix_b_in=mix_b_in, conv_dw_w=conv_dw_w, conv_dw_b=conv_dw_b, conv_gn_g=conv_gn_g, conv_gn_b=conv_gn_b, conv_w_proj=conv_w_proj, rnn_conv_w=rnn_conv_w, rnn_conv_b=rnn_conv_b, rnn_w_a=rnn_w_a, rnn_b_a=rnn_b_a, rnn_w_x=rnn_w_x, rnn_b_x=rnn_b_x, rnn_lambda=rnn_lambda, rnn_w_proj=rnn_w_proj, mix_w_out=mix_w_out, ln2_g=ln2_g, ln2_b=ln2_b, ffn2_w_gu=ffn2_w_gu, ffn2_w_down=ffn2_w_down, ln3_g=ln3_g, ln3_b=ln3_b, loss_target=loss_target, m_ffn1_w_gu=m_ffn1_w_gu, m_ffn1_w_down=m_ffn1_w_down, m_ln1_g=m_ln1_g, m_ln1_b=m_ln1_b, m_mix_w_in=m_mix_w_in, m_mix_b_in=m_mix_b_in, m_conv_dw_w=m_conv_dw_w, m_conv_dw_b=m_conv_dw_b, m_conv_gn_g=m_conv_gn_g, m_conv_gn_b=m_conv_gn_b, m_conv_w_proj=m_conv_w_proj, m_rnn_conv_w=m_rnn_conv_w, m_rnn_conv_b=m_rnn_conv_b, m_rnn_w_a=m_rnn_w_a, m_rnn_b_a=m_rnn_b_a, m_rnn_w_x=m_rnn_w_x, m_rnn_b_x=m_rnn_b_x, m_rnn_lambda=m_rnn_lambda, m_rnn_w_proj=m_rnn_w_proj, m_mix_w_out=m_mix_w_out, m_ln2_g=m_ln2_g, m_ln2_b=m_ln2_b, m_ffn2_w_gu=m_ffn2_w_gu, m_ffn2_w_down=m_ffn2_w_down, m_ln3_g=m_ln3_g, m_ln3_b=m_ln3_b, v_ffn1_w_gu=v_ffn1_w_gu, v_ffn1_w_down=v_ffn1_w_down, v_ln1_g=v_ln1_g, v_ln1_b=v_ln1_b, v_mix_w_in=v_mix_w_in, v_mix_b_in=v_mix_b_in, v_conv_dw_w=v_conv_dw_w, v_conv_dw_b=v_conv_dw_b, v_conv_gn_g=v_conv_gn_g, v_conv_gn_b=v_conv_gn_b, v_conv_w_proj=v_conv_w_proj, v_rnn_conv_w=v_rnn_conv_w, v_rnn_conv_b=v_rnn_conv_b, v_rnn_w_a=v_rnn_w_a, v_rnn_b_a=v_rnn_b_a, v_rnn_w_x=v_rnn_w_x, v_rnn_b_x=v_rnn_b_x, v_rnn_lambda=v_rnn_lambda, v_rnn_w_proj=v_rnn_w_proj, v_mix_w_out=v_mix_w_out, v_ln2_g=v_ln2_g, v_ln2_b=v_ln2_b, v_ffn2_w_gu=v_ffn2_w_gu, v_ffn2_w_down=v_ffn2_w_down, v_ln3_g=v_ln3_g, v_ln3_b=v_ln3_b)
    weights = {n: given[n] for n in TWIN_WEIGHTS}
    shared = {n: given[n] for n in SHARED_INPUTS}
    per_example = {n: given[n] for n in ['x']}
    grad_fn = _jax.value_and_grad(_loss, argnums=(0, 1))

    def one_microbatch(ex, loss_target):
        ex = dict(ex)
        diff = ex.pop(TWIN_DIFF_INPUT)
        return grad_fn(weights, diff, {**shared, **ex}, loss_target)

    if N_MICROBATCH == 1:
        loss, (grad_w, grad_x) = one_microbatch(per_example, given["loss_target"])
    else:
        def body(carry, xs):
            loss_sum, grad_sum = carry
            l_k, (gw_k, gx_k) = one_microbatch(xs[0], xs[1])
            with _jax.named_scope("update"):
                return (loss_sum + l_k, _jax.tree.map(_jnp.add, grad_sum, gw_k)), gx_k

        init = (_jnp.zeros((), _jnp.float32), _jax.tree.map(_jnp.zeros_like, weights))
        (loss, grad_w), grad_x = _jax.lax.scan(body, init, (per_example, given["loss_target"]))
    with _jax.named_scope("update"):
        delta_w, new_m, new_v = {}, {}, {}
        for n in TWIN_WEIGHTS:
            delta_w[n], new_m[n], new_v[n] = _adamw(weights[n], grad_w[n], given["m_" + n], given["v_" + n])
    return (loss, grad_x, *[grad_w[n] for n in TWIN_WEIGHTS], *[delta_w[n] for n in TWIN_WEIGHTS],
            *[new_m[n] for n in TWIN_WEIGHTS], *[new_v[n] for n in TWIN_WEIGHTS])
```

```python
import functools
import math

import jax
import jax.numpy as jnp
from jax import lax
from jax.experimental import pallas as pl
from jax.experimental.pallas import tpu as pltpu

F32 = jnp.float32
BF16 = jnp.bfloat16
MESH = pl.DeviceIdType.MESH

DEPTH = 2
ALPHA = (2 * DEPTH) ** 0.25
LN_EPS = 1e-5
RG_LRU_C = 8.0
CONV_WIDTH = 31
RNN_CONV_WIDTH = 4
RNN_BLOCKS = 16
N_DEV = 8
ADAM_LR, ADAM_B1, ADAM_B2, ADAM_EPS, ADAM_WD, ADAM_STEP = 0.001, 0.9, 0.999, 1e-08, 0.01, 10

LANES = 128
SUBLANES = 8
VMEM_LIMIT = 56 * 1024 * 1024
CONV_PAD = 32
RNN_PAD = 8
CONV_ROWS = 128

WEIGHTS = ['ffn1_w_gu', 'ffn1_w_down', 'ln1_g', 'ln1_b', 'mix_w_in', 'mix_b_in', 'conv_dw_w', 'conv_dw_b',
           'conv_gn_g', 'conv_gn_b', 'conv_w_proj', 'rnn_conv_w', 'rnn_conv_b', 'rnn_w_a', 'rnn_b_a', 'rnn_w_x',
           'rnn_b_x', 'rnn_lambda', 'rnn_w_proj', 'mix_w_out', 'ln2_g', 'ln2_b', 'ffn2_w_gu', 'ffn2_w_down',
           'ln3_g', 'ln3_b']
COL_SHARDED = ['ffn1_w_gu', 'mix_w_in', 'ffn2_w_gu', 'conv_dw_w', 'rnn_conv_w']
ROW_SHARDED = ['ffn1_w_down', 'conv_w_proj', 'rnn_w_proj', 'mix_w_out', 'ffn2_w_down']
SHARDED = COL_SHARDED + ROW_SHARDED
BF16_ON_WIRE = ['ffn1_w_gu', 'mix_w_in', 'ffn2_w_gu', 'ffn1_w_down', 'conv_w_proj', 'rnn_w_proj', 'mix_w_out',
                'ffn2_w_down']
REPLICATED = [n for n in WEIGHTS if n not in SHARDED]


def _cp(n_axes=1):
    return pltpu.CompilerParams(dimension_semantics=("arbitrary",) * n_axes, vmem_limit_bytes=VMEM_LIMIT)


def _rows(tm, c):
    return pl.BlockSpec((tm, c), lambda i: (i, 0))


def _res(shape):
    nd = len(shape)
    return pl.BlockSpec(tuple(shape), lambda *_: (0,) * nd, pipeline_mode=pl.Buffered(1))


def _acc(shape):
    nd = len(shape)
    return pl.BlockSpec(tuple(shape), lambda *_: (0,) * nd)


def _tile(t, want):
    return want if t % want == 0 else t


def _sds(shape, dtype):
    return jax.ShapeDtypeStruct(tuple(shape), dtype)


def _ln_fwd(r, g, b):
    mu = jnp.mean(r, axis=-1, keepdims=True)
    xc = r - mu
    var = jnp.mean(xc * xc, axis=-1, keepdims=True)
    return xc * lax.rsqrt(var + LN_EPS) * g + b


def _ln_bwd(dy, r, g):
    mu = jnp.mean(r, axis=-1, keepdims=True)
    xc = r - mu
    var = jnp.mean(xc * xc, axis=-1, keepdims=True)
    rstd = lax.rsqrt(var + LN_EPS)
    xhat = xc * rstd
    dxh = dy * g
    m1 = jnp.mean(dxh, axis=-1, keepdims=True)
    m2 = jnp.mean(dxh * xhat, axis=-1, keepdims=True)
    dr = rstd * (dxh - m1 - xhat * m2)
    return dr, jnp.sum(dy * xhat, axis=0, keepdims=True), jnp.sum(dy, axis=0, keepdims=True)


def _sigmoid(x):
    return jax.nn.sigmoid(x)


_GELU_K = math.sqrt(2.0 / math.pi)


def _gelu(x):
    t = jnp.tanh(_GELU_K * (x + 0.044715 * x * x * x))
    return 0.5 * x * (1.0 + t), t


def _gelu_grad(x, t):
    return 0.5 * (1.0 + t) + 0.5 * x * (1.0 - t * t) * (_GELU_K * (1.0 + 3.0 * 0.044715 * x * x))


def _expm1(x):
    taylor = x * (1.0 + x * (0.5 + x * (1.0 / 6.0 + x * (1.0 / 24.0 + x * (1.0 / 120.0)))))
    return jnp.where(jnp.abs(x) < 0.03, taylor, jnp.exp(x) - 1.0)


def _softplus_neg(lam):
    return jnp.maximum(-lam, 0.0) + jnp.log1p(jnp.exp(-jnp.abs(lam)))


def _dot(a, b):
    return jnp.dot(a, b, preferred_element_type=F32)


def _dot_nt(a, b):
    return lax.dot_general(a, b, (((1,), (1,)), ((), ())), preferred_element_type=F32)


def _dot_tn(a, b):
    return lax.dot_general(a, b, (((0,), (0,)), ((), ())), preferred_element_type=F32)


def _chunks(width, cn):
    return [(j, min(cn, width - j)) for j in range(0, width, cn)]


def _ffn_up(xb, w, name):
    T, D = xb.shape
    F = w.shape[1] // 2
    tm = _tile(T, 512)

    def body(x_ref, w_ref, gu_ref, a_ref):
        x = x_ref[...]
        for j, cw in _chunks(F, 256):
            g = _dot(x, w_ref[:, j:j + cw])
            u = _dot(x, w_ref[:, F + j:F + j + cw])
            gu_ref[:, j:j + cw] = g.astype(BF16)
            gu_ref[:, F + j:F + j + cw] = u.astype(BF16)
            a_ref[:, j:j + cw] = (g * _sigmoid(g) * u).astype(BF16)

    return pl.pallas_call(
        body, name=name, grid=(T // tm,),
        in_specs=[_rows(tm, D), _res(w.shape)],
        out_specs=[_rows(tm, 2 * F), _rows(tm, F)],
        out_shape=[_sds((T, 2 * F), BF16), _sds((T, F), BF16)],
        compiler_params=_cp())(xb, w)


def _ffn_down_ln(a, wd, xres, g, b, name):
    T, F = a.shape
    D = wd.shape[1]
    tm = _tile(T, 512)

    def body(a_ref, wd_ref, x_ref, g_ref, b_ref, r_ref, y_ref, yb_ref):
        r = ALPHA * x_ref[...] + 0.5 * _dot(a_ref[...], wd_ref[...])
        y = _ln_fwd(r, g_ref[...], b_ref[...])
        r_ref[...] = r
        y_ref[...] = y
        yb_ref[...] = y.astype(BF16)

    return pl.pallas_call(
        body, name=name, grid=(T // tm,),
        in_specs=[_rows(tm, F), _res(wd.shape), _rows(tm, D), _res((1, D)), _res((1, D))],
        out_specs=[_rows(tm, D), _rows(tm, D), _rows(tm, D)],
        out_shape=[_sds((T, D), F32), _sds((T, D), F32), _sds((T, D), BF16)],
        compiler_params=_cp())(a, wd, xres, g, b)


def _mix_in(hb, w, bias, d_rnn, name):
    T, D = hb.shape
    R = d_rnn
    tm = _tile(T, 256)
    o_cv, o_cg, o_rx, o_rg, o_gc, o_gr = 0, D, 2 * D, 2 * D + R, 2 * D + 2 * R, 3 * D + 2 * R

    def body(x_ref, w_ref, b_ref, c_ref, cv_ref, cg_ref, rx_ref, rg_ref, gc_ref, gr_ref):
        x = x_ref[...]

        def seg(off, j, cw):
            return _dot(x, w_ref[:, off + j:off + j + cw]) + b_ref[:, off + j:off + j + cw]

        for j, cw in _chunks(D, 256):
            cv = seg(o_cv, j, cw)
            cg = seg(o_cg, j, cw)
            cv_ref[:, j:j + cw] = cv.astype(BF16)
            cg_ref[:, j:j + cw] = cg.astype(BF16)
            c_ref[:, j:j + cw] = cv * _sigmoid(cg)
            gc_ref[:, j:j + cw] = seg(o_gc, j, cw).astype(BF16)
            gr_ref[:, j:j + cw] = seg(o_gr, j, cw).astype(BF16)
        for j, cw in _chunks(R, 256):
            rx_ref[:, j:j + cw] = seg(o_rx, j, cw)
            rg_ref[:, j:j + cw] = seg(o_rg, j, cw).astype(BF16)

    return pl.pallas_call(
        body, name=name, grid=(T // tm,),
        in_specs=[_rows(tm, D), _res(w.shape), _res(bias.shape)],
        out_specs=[_rows(tm, D), _rows(tm, D), _rows(tm, D), _rows(tm, R), _rows(tm, R), _rows(tm, D),
                   _rows(tm, D)],
        out_shape=[_sds((T, D), F32), _sds((T, D), BF16), _sds((T, D), BF16), _sds((T, R), F32),
                   _sds((T, R), BF16), _sds((T, D), BF16), _sds((T, D), BF16)],
        compiler_params=_cp())(hb, w, bias)


def _cols(t, rows=None):
    return pl.BlockSpec((t if rows is None else rows, LANES), lambda j: (0, j))


def _gn_stats(cc):
    mu = jnp.mean(cc, axis=-1, keepdims=True)
    xc = cc - mu
    var = jnp.mean(xc * xc, axis=-1, keepdims=True)
    rstd = lax.rsqrt(var + LN_EPS)
    return xc * rstd, rstd


def _conv31_gn(c, w, bias, gg, gb, name):
    T, D = c.shape
    K = w.shape[0]
    R = CONV_ROWS
    assert D % LANES == 0 and T % R == 0 and K - 1 <= CONV_PAD

    def body(c_ref, w_ref, b_ref, gg_ref, gb_ref, cc_ref, cs_ref, xpad):
        xpad[0:CONV_PAD, :] = jnp.zeros((CONV_PAD, LANES), F32)
        xpad[CONV_PAD:CONV_PAD + T, :] = c_ref[...]
        wv = w_ref[...]
        bv, ggv, gbv = b_ref[...], gg_ref[...], gb_ref[...]

        def step(i, carry):
            base = pl.multiple_of(i * R, R)
            win = xpad[pl.ds(base, R + CONV_PAD), :]
            acc = jnp.zeros((R, LANES), F32)
            for k in range(K):
                s = CONV_PAD - (K - 1) + k
                acc = acc + wv[k:k + 1, :] * win[s:s + R, :]
            cc = acc + bv
            cc_ref[pl.ds(base, R), :] = cc
            xhat, _ = _gn_stats(cc)
            gn = xhat * ggv + gbv
            cs_ref[pl.ds(base, R), :] = (gn * _sigmoid(gn)).astype(BF16)
            return carry

        lax.fori_loop(0, T // R, step, 0)

    return pl.pallas_call(
        body, name=name, grid=(D // LANES,),
        in_specs=[_cols(T), _cols(T, K), _cols(T, 1), _cols(T, 1), _cols(T, 1)],
        out_specs=[_cols(T), _cols(T)],
        out_shape=[_sds((T, D), F32), _sds((T, D), BF16)],
        scratch_shapes=[pltpu.VMEM((T + CONV_PAD, LANES), F32)],
        compiler_params=_cp())(c, w, bias, gg, gb)


def _conv4(rx, w, bias, name):
    T, C = rx.shape
    K = w.shape[0]
    R = CONV_ROWS
    assert C % LANES == 0 and T % R == 0 and K - 1 <= RNN_PAD

    def body(x_ref, w_ref, b_ref, r_ref, xpad):
        xpad[0:RNN_PAD, :] = jnp.zeros((RNN_PAD, LANES), F32)
        xpad[RNN_PAD:RNN_PAD + T, :] = x_ref[...]
        wv, bv = w_ref[...], b_ref[...]

        def step(i, carry):
            base = pl.multiple_of(i * R, R)
            win = xpad[pl.ds(base, R + RNN_PAD), :]
            acc = jnp.zeros((R, LANES), F32)
            for k in range(K):
                s = RNN_PAD - (K - 1) + k
                acc = acc + wv[k:k + 1, :] * win[s:s + R, :]
            r_ref[pl.ds(base, R), :] = acc + bv
            return carry

        lax.fori_loop(0, T // R, step, 0)

    return pl.pallas_call(
        body, name=name, grid=(C // LANES,),
        in_specs=[_cols(T), _cols(T, K), _cols(T, 1)],
        out_specs=_cols(T),
        out_shape=_sds((T, C), F32),
        scratch_shapes=[pltpu.VMEM((T + RNN_PAD, LANES), F32)],
        compiler_params=_cp())(rx, w, bias)


def _gates(r, bda, bdx, b_a, b_x, lam, name):
    T, C = r.shape
    tm = _tile(T, 512)

    def body(r_ref, wa_ref, wx_ref, ba_ref, bx_ref, lam_ref, ra_ref, ri_ref, a_ref, u_ref):
        rv = r_ref[...]
        rb = rv.astype(BF16)
        ra = _sigmoid(_dot(rb, wa_ref[...]) + ba_ref[...])
        ri = _sigmoid(_dot(rb, wx_ref[...]) + bx_ref[...])
        log_a = (-RG_LRU_C) * ra * _softplus_neg(lam_ref[...])
        ra_ref[...] = ra
        ri_ref[...] = ri
        a_ref[...] = jnp.exp(log_a)
        u_ref[...] = jnp.sqrt(-_expm1(2.0 * log_a)) * (ri * rv)

    return pl.pallas_call(
        body, name=name, grid=(T // tm,),
        in_specs=[_rows(tm, C), _res(bda.shape), _res(bdx.shape), _res((1, C)), _res((1, C)), _res((1, C))],
        out_specs=[_rows(tm, C)] * 4,
        out_shape=[_sds((T, C), F32)] * 4,
        compiler_params=_cp())(r, bda, bdx, b_a, b_x, lam)


def _scan_fwd(a, u, rg, name):
    T, C = a.shape
    tt = _tile(T, 512)

    def body(a_ref, u_ref, rg_ref, h_ref, hp_ref, hg_ref, carry):
        @pl.when(pl.program_id(0) == 0)
        def _():
            carry[...] = jnp.zeros_like(carry)

        row = lax.broadcasted_iota(jnp.int32, (SUBLANES, C), 0)

        def group(i, hprev):
            base = pl.multiple_of(i * SUBLANES, SUBLANES)
            av = a_ref[pl.ds(base, SUBLANES), :]
            uv = u_ref[pl.ds(base, SUBLANES), :]
            for s in (1, 2, 4):
                a_s = jnp.where(row >= s, pltpu.roll(av, s, 0), 1.0)
                u_s = jnp.where(row >= s, pltpu.roll(uv, s, 0), 0.0)
                uv = av * u_s + uv
                av = av * a_s
            h = av * hprev + uv
            h_ref[pl.ds(base, SUBLANES), :] = h
            hp_ref[pl.ds(base, SUBLANES), :] = jnp.where(row >= 1, pltpu.roll(h, 1, 0), hprev)
            return h[SUBLANES - 1:SUBLANES, :]

        carry[...] = lax.fori_loop(0, tt // SUBLANES, group, carry[...])
        gel, _ = _gelu(rg_ref[...].astype(F32))
        hg_ref[...] = (h_ref[...] * gel).astype(BF16)

    return pl.pallas_call(
        body, name=name, grid=(T // tt,),
        in_specs=[_rows(tt, C)] * 3,
        out_specs=[_rows(tt, C)] * 3,
        out_shape=[_sds((T, C), F32), _sds((T, C), F32), _sds((T, C), BF16)],
        scratch_shapes=[pltpu.VMEM((1, C), F32)],
        compiler_params=_cp())(a, u, rg)


def _mix_out_ln(cs, hg, gc, gr, hres, wcp, wrp, wout, g, b, name):
    T, D = cs.shape
    C = hg.shape[1]
    tm = _tile(T, 512)

    def body(cs_ref, hg_ref, gc_ref, gr_ref, h_ref, wcp_ref, wrp_ref, wo_ref, g_ref, b_ref,
             yc_ref, yr_ref, m_ref, r_ref, y_ref, yb_ref):
        yc = _dot(cs_ref[...], wcp_ref[...])
        yr = _dot(hg_ref[...], wrp_ref[...])
        m = (_sigmoid(gc_ref[...].astype(F32)) * yc + _sigmoid(gr_ref[...].astype(F32)) * yr).astype(BF16)
        r = ALPHA * h_ref[...] + _dot(m, wo_ref[...])
        y = _ln_fwd(r, g_ref[...], b_ref[...])
        yc_ref[...] = yc.astype(BF16)
        yr_ref[...] = yr.astype(BF16)
        m_ref[...] = m
        r_ref[...] = r
        y_ref[...] = y
        yb_ref[...] = y.astype(BF16)

    return pl.pallas_call(
        body, name=name, grid=(T // tm,),
        in_specs=[_rows(tm, D), _rows(tm, C), _rows(tm, D), _rows(tm, D), _rows(tm, D), _res(wcp.shape),
                  _res(wrp.shape), _res(wout.shape), _res((1, D)), _res((1, D))],
        out_specs=[_rows(tm, D)] * 6,
        out_shape=[_sds((T, D), BF16), _sds((T, D), BF16), _sds((T, D), BF16), _sds((T, D), F32),
                   _sds((T, D), F32), _sds((T, D), BF16)],
        compiler_params=_cp())(cs, hg, gc, gr, hres, wcp, wrp, wout, g, b)


def _loss_ln_bwd(y, target, r, g, name):
    T, D = y.shape
    tm = _tile(T, 512)

    def body(y_ref, t_ref, r_ref, g_ref, loss_ref, dr_ref, drb_ref, dg_ref, db_ref):
        @pl.when(pl.program_id(0) == 0)
        def _():
            loss_ref[...] = jnp.zeros_like(loss_ref)
            dg_ref[...] = jnp.zeros_like(dg_ref)
            db_ref[...] = jnp.zeros_like(db_ref)

        e = y_ref[...] - t_ref[...]
        loss_ref[...] += (0.5 / D) * jnp.sum(e * e)
        dr, dg, db = _ln_bwd(e * (1.0 / D), r_ref[...], g_ref[...])
        dr_ref[...] = dr
        drb_ref[...] = dr.astype(BF16)
        dg_ref[...] += dg
        db_ref[...] += db

    return pl.pallas_call(
        body, name=name, grid=(T // tm,),
        in_specs=[_rows(tm, D), _rows(tm, D), _rows(tm, D), _res((1, D))],
        out_specs=[_acc((SUBLANES, LANES)), _rows(tm, D), _rows(tm, D), _acc((1, D)), _acc((1, D))],
        out_shape=[_sds((SUBLANES, LANES), F32), _sds((T, D), F32), _sds((T, D), BF16), _sds((1, D), F32),
                   _sds((1, D), F32)],
        compiler_params=_cp())(y, target, r, g)


def _ffn_bwd_a(drb, wd, gu, name):
    T, D = drb.shape
    F = wd.shape[0]
    tm = _tile(T, 512)

    def body(d_ref, wd_ref, gu_ref, o_ref):
        d = d_ref[...]
        for j, cw in _chunks(F, 256):
            da = 0.5 * _dot_nt(d, wd_ref[j:j + cw, :])
            gt = gu_ref[:, j:j + cw].astype(F32)
            up = gu_ref[:, F + j:F + j + cw].astype(F32)
            sg = _sigmoid(gt)
            o_ref[:, j:j + cw] = (da * up * (sg * (1.0 + gt * (1.0 - sg)))).astype(BF16)
            o_ref[:, F + j:F + j + cw] = (da * (gt * sg)).astype(BF16)

    return pl.pallas_call(
        body, name=name, grid=(T // tm,),
        in_specs=[_rows(tm, D), _res(wd.shape), _rows(tm, 2 * F)],
        out_specs=_rows(tm, 2 * F),
        out_shape=_sds((T, 2 * F), BF16),
        compiler_params=_cp())(drb, wd, gu)


def _nt_res(du, w, dres, name, ln=None, colsum=False):
    T, K = du.shape
    D = w.shape[0]
    tm = _tile(T, 256 if K > 6000 else 512)
    n_in = 3 + (2 if ln else 0)

    def body(*refs):
        du_ref, w_ref, dres_ref = refs[:3]
        outs = refs[n_in:]
        dy = ALPHA * dres_ref[...] + _dot_nt(du_ref[...], w_ref[...])
        if ln:
            r_ref, g_ref = refs[3:5]

            @pl.when(pl.program_id(0) == 0)
            def _():
                outs[2][...] = jnp.zeros_like(outs[2])
                outs[3][...] = jnp.zeros_like(outs[3])

            dr, dg, db = _ln_bwd(dy, r_ref[...], g_ref[...])
            outs[0][...] = dr
            outs[1][...] = dr.astype(BF16)
            outs[2][...] += dg
            outs[3][...] += db
        else:
            outs[0][...] = dy
        if colsum:
            cs_ref = outs[-1]

            @pl.when(pl.program_id(0) == 0)
            def _():
                cs_ref[...] = jnp.zeros_like(cs_ref)

            cs_ref[...] += jnp.sum(du_ref[...].astype(F32), axis=0, keepdims=True)

    in_specs = [_rows(tm, K), _res(w.shape), _rows(tm, D)]
    args = [du, w, dres]
    if ln:
        in_specs += [_rows(tm, D), _res((1, D))]
        args += list(ln)
        out_specs = [_rows(tm, D), _rows(tm, D), _acc((1, D)), _acc((1, D))]
        out_shape = [_sds((T, D), F32), _sds((T, D), BF16), _sds((1, D), F32), _sds((1, D), F32)]
    else:
        out_specs = [_rows(tm, D)]
        out_shape = [_sds((T, D), F32)]
    if colsum:
        out_specs.append(_acc((1, K)))
        out_shape.append(_sds((1, K), F32))
    return pl.pallas_call(
        body, name=name, grid=(T // tm,), in_specs=in_specs, out_specs=out_specs, out_shape=out_shape,
        compiler_params=_cp())(*args)


def _mm_tn(x, dy, name, scale=1.0):
    T, K = x.shape
    N = dy.shape[1]
    tt = _tile(T, 1024)
    tn = next((c for c in (512, 768) if N % c == 0), N)
    nt = T // tt

    def body(x_ref, dy_ref, o_ref):
        t = pl.program_id(1)

        @pl.when(t == 0)
        def _():
            o_ref[...] = jnp.zeros_like(o_ref)

        o_ref[...] += _dot_tn(x_ref[...].astype(BF16), dy_ref[...])
        if scale != 1.0:
            @pl.when(t == nt - 1)
            def _():
                o_ref[...] = o_ref[...] * scale

    return pl.pallas_call(
        body, name=name, grid=(N // tn, nt),
        in_specs=[pl.BlockSpec((tt, K), lambda j, t: (t, 0)), pl.BlockSpec((tt, tn), lambda j, t: (t, j))],
        out_specs=pl.BlockSpec((K, tn), lambda j, t: (0, j)),
        out_shape=_sds((K, N), F32),
        compiler_params=_cp(2))(x, dy)


def _mix_bwd1(drb, wout, wcp, wrp, gc, gr, yc, yr, rg, h, name):
    T, D = drb.shape
    C = rg.shape[1]
    tm = _tile(T, 512)

    def body(d_ref, wo_ref, wcp_ref, wrp_ref, gc_ref, gr_ref, yc_ref, yr_ref, rg_ref, h_ref,
             dyc_ref, dyr_ref, dgc_ref, dgr_ref, dcs_ref, dh_ref, drg_ref):
        dm = _dot_nt(d_ref[...], wo_ref[...])
        sc = _sigmoid(gc_ref[...].astype(F32))
        sr = _sigmoid(gr_ref[...].astype(F32))
        dyc = (dm * sc).astype(BF16)
        dyr = (dm * sr).astype(BF16)
        dyc_ref[...] = dyc
        dyr_ref[...] = dyr
        dgc_ref[...] = (dm * yc_ref[...].astype(F32) * sc * (1.0 - sc)).astype(BF16)
        dgr_ref[...] = (dm * yr_ref[...].astype(F32) * sr * (1.0 - sr)).astype(BF16)
        dcs_ref[...] = _dot_nt(dyc, wcp_ref[...])
        dhg = _dot_nt(dyr, wrp_ref[...])
        rgv = rg_ref[...].astype(F32)
        gel, t = _gelu(rgv)
        dh_ref[...] = dhg * gel
        drg_ref[...] = (dhg * h_ref[...] * _gelu_grad(rgv, t)).astype(BF16)

    return pl.pallas_call(
        body, name=name, grid=(T // tm,),
        in_specs=[_rows(tm, D), _res(wout.shape), _res(wcp.shape), _res(wrp.shape), _rows(tm, D), _rows(tm, D),
                  _rows(tm, D), _rows(tm, D), _rows(tm, C), _rows(tm, C)],
        out_specs=[_rows(tm, D)] * 5 + [_rows(tm, C)] * 2,
        out_shape=[_sds((T, D), BF16)] * 4 + [_sds((T, D), F32), _sds((T, C), F32), _sds((T, C), BF16)],
        compiler_params=_cp())(drb, wout, wcp, wrp, gc, gr, yc, yr, rg, h)


def _scan_bwd(dh, a, name):
    T, C = dh.shape
    tt = _tile(T, 512)
    nt = T // tt
    ng = tt // SUBLANES

    def body(d_ref, a_ref, g_ref, carry):
        @pl.when(pl.program_id(0) == 0)
        def _():
            carry[...] = jnp.zeros_like(carry)

        row = lax.broadcasted_iota(jnp.int32, (SUBLANES, C), 0)

        def group(j, enext):
            base = pl.multiple_of((ng - 1 - j) * SUBLANES, SUBLANES)
            av = a_ref[pl.ds(base, SUBLANES), :]
            dv = d_ref[pl.ds(base, SUBLANES), :]
            bv = av * dv
            for s in (1, 2, 4):
                keep = row < SUBLANES - s
                a_s = jnp.where(keep, pltpu.roll(av, SUBLANES - s, 0), 1.0)
                b_s = jnp.where(keep, pltpu.roll(bv, SUBLANES - s, 0), 0.0)
                bv = av * b_s + bv
                av = av * a_s
            e = av * enext + bv
            e_up = jnp.where(row < SUBLANES - 1, pltpu.roll(e, SUBLANES - 1, 0), enext)
            g_ref[pl.ds(base, SUBLANES), :] = dv + e_up
            return e[0:1, :]

        carry[...] = lax.fori_loop(0, ng, group, carry[...])

    rev = pl.BlockSpec((tt, C), lambda i: (nt - 1 - i, 0))
    return pl.pallas_call(
        body, name=name, grid=(nt,), in_specs=[rev, rev], out_specs=rev,
        out_shape=_sds((T, C), F32),
        scratch_shapes=[pltpu.VMEM((1, C), F32)],
        compiler_params=_cp())(dh, a)


def _gates_bwd(g, hp, ra, ri, r, lam, bda, bdx, name):
    T, C = g.shape
    tm = _tile(T, 512)
    nt = T // tm

    def body(g_ref, hp_ref, ra_ref, ri_ref, r_ref, lam_ref, wa_ref, wx_ref,
             dr_ref, dpa_ref, dpx_ref, dlam_ref, dba_ref, dbx_ref):
        @pl.when(pl.program_id(0) == 0)
        def _():
            dlam_ref[...] = jnp.zeros_like(dlam_ref)
            dba_ref[...] = jnp.zeros_like(dba_ref)
            dbx_ref[...] = jnp.zeros_like(dbx_ref)

        gv, rav, riv, rv, lamv = g_ref[...], ra_ref[...], ri_ref[...], r_ref[...], lam_ref[...]
        sp = _softplus_neg(lamv)
        log_a = (-RG_LRU_C) * rav * sp
        av = jnp.exp(log_a)
        mult = jnp.sqrt(-_expm1(2.0 * log_a))
        d_mult = gv * riv * rv
        d_i = gv * mult * rv
        d_loga = gv * hp_ref[...] * av - d_mult * (av * av) / mult
        d_ra = d_loga * ((-RG_LRU_C) * sp)
        dpa = d_ra * rav * (1.0 - rav)
        dpx = d_i * riv * (1.0 - riv)
        dpab = dpa.astype(BF16)
        dpxb = dpx.astype(BF16)
        dpa_ref[...] = dpab
        dpx_ref[...] = dpxb
        dr_ref[...] = gv * mult * riv + _dot_nt(dpab, wa_ref[...]) + _dot_nt(dpxb, wx_ref[...])
        dlam_ref[...] += jnp.sum(d_loga * ((-RG_LRU_C) * rav), axis=0, keepdims=True)
        dba_ref[...] += jnp.sum(dpa, axis=0, keepdims=True)
        dbx_ref[...] += jnp.sum(dpx, axis=0, keepdims=True)

        @pl.when(pl.program_id(0) == nt - 1)
        def _():
            dlam_ref[...] = dlam_ref[...] * (-_sigmoid(-lamv))

    return pl.pallas_call(
        body, name=name, grid=(nt,),
        in_specs=[_rows(tm, C)] * 5 + [_res((1, C)), _res(bda.shape), _res(bdx.shape)],
        out_specs=[_rows(tm, C)] * 3 + [_acc((1, C))] * 3,
        out_shape=[_sds((T, C), F32), _sds((T, C), BF16), _sds((T, C), BF16)] + [_sds((1, C), F32)] * 3,
        compiler_params=_cp())(g, hp, ra, ri, r, lam, bda, bdx)


def _conv4_bwd(dr, rx, w, name):
    T, C = dr.shape
    K = w.shape[0]
    R = CONV_ROWS
    P = RNN_PAD

    def body(d_ref, x_ref, w_ref, dx_ref, dw_ref, db_ref, dpad, xpad):
        dpad[0:T, :] = d_ref[...]
        dpad[T:T + P, :] = jnp.zeros((P, LANES), F32)
        xpad[0:P, :] = jnp.zeros((P, LANES), F32)
        xpad[P:P + T, :] = x_ref[...]
        dw_ref[...] = jnp.zeros_like(dw_ref)
        db_ref[...] = jnp.zeros_like(db_ref)
        wv = w_ref[...]

        def step(i, carry):
            base = pl.multiple_of(i * R, R)
            dwin = dpad[pl.ds(base, R + P), :]
            xwin = xpad[pl.ds(base, R + P), :]
            dcur = dwin[0:R, :]
            acc = jnp.zeros((R, LANES), F32)
            for k in range(K):
                acc = acc + wv[k:k + 1, :] * dwin[K - 1 - k:K - 1 - k + R, :]
                s = P - (K - 1) + k
                dw_ref[k:k + 1, :] += jnp.sum(dcur * xwin[s:s + R, :], axis=0, keepdims=True)
            dx_ref[pl.ds(base, R), :] = acc.astype(BF16)
            db_ref[...] += jnp.sum(dcur, axis=0, keepdims=True)
            return carry

        lax.fori_loop(0, T // R, step, 0)

    return pl.pallas_call(
        body, name=name, grid=(C // LANES,),
        in_specs=[_cols(T), _cols(T), _cols(T, K)],
        out_specs=[_cols(T), _cols(T, SUBLANES), _cols(T, 1)],
        out_shape=[_sds((T, C), BF16), _sds((SUBLANES, C), F32), _sds((1, C), F32)],
        scratch_shapes=[pltpu.VMEM((T + P, LANES), F32), pltpu.VMEM((T + P, LANES), F32)],
        compiler_params=_cp())(dr, rx, w)


def _conv31_bwd(dcs, cc, c, cv, cg, w, gg, gb, name):
    T, D = dcs.shape
    K = w.shape[0]
    R = CONV_ROWS
    P = CONV_PAD

    def body(dcs_ref, cc_ref, c_ref, cv_ref, cg_ref, w_ref, gg_ref, gb_ref,
             dcv_ref, dcg_ref, dw_ref, db_ref, dgg_ref, dgb_ref, dpad, xpad):
        dpad[T:T + P, :] = jnp.zeros((P, LANES), F32)
        xpad[0:P, :] = jnp.zeros((P, LANES), F32)
        xpad[P:P + T, :] = c_ref[...]
        dw_ref[...] = jnp.zeros_like(dw_ref)
        db_ref[...] = jnp.zeros_like(db_ref)
        dgg_ref[...] = jnp.zeros_like(dgg_ref)
        dgb_ref[...] = jnp.zeros_like(dgb_ref)
        wv, ggv, gbv = w_ref[...], gg_ref[...], gb_ref[...]

        def norm_step(i, carry):
            base = pl.multiple_of(i * R, R)
            xhat, rstd = _gn_stats(cc_ref[pl.ds(base, R), :])
            gn = xhat * ggv + gbv
            sg = _sigmoid(gn)
            dgn = dcs_ref[pl.ds(base, R), :] * (sg * (1.0 + gn * (1.0 - sg)))
            dgg_ref[...] += jnp.sum(dgn * xhat, axis=0, keepdims=True)
            dgb_ref[...] += jnp.sum(dgn, axis=0, keepdims=True)
            dxh = dgn * ggv
            m1 = jnp.mean(dxh, axis=-1, keepdims=True)
            m2 = jnp.mean(dxh * xhat, axis=-1, keepdims=True)
            dcc = rstd * (dxh - m1 - xhat * m2)
            dpad[pl.ds(base, R), :] = dcc
            db_ref[...] += jnp.sum(dcc, axis=0, keepdims=True)
            return carry

        lax.fori_loop(0, T // R, norm_step, 0)

        def conv_step(i, carry):
            base = pl.multiple_of(i * R, R)
            dwin = dpad[pl.ds(base, R + P), :]
            xwin = xpad[pl.ds(base, R + P), :]
            dcur = dwin[0:R, :]
            acc = jnp.zeros((R, LANES), F32)
            for k in range(K):
                acc = acc + wv[k:k + 1, :] * dwin[K - 1 - k:K - 1 - k + R, :]
                s = P - (K - 1) + k
                dw_ref[k:k + 1, :] += jnp.sum(dcur * xwin[s:s + R, :], axis=0, keepdims=True)
            cgv = cg_ref[pl.ds(base, R), :].astype(F32)
            cvv = cv_ref[pl.ds(base, R), :].astype(F32)
            sg = _sigmoid(cgv)
            dcv_ref[pl.ds(base, R), :] = (acc * sg).astype(BF16)
            dcg_ref[pl.ds(base, R), :] = (acc * cvv * sg * (1.0 - sg)).astype(BF16)
            return carry

        lax.fori_loop(0, T // R, conv_step, 0)

    return pl.pallas_call(
        body, name=name, grid=(D // LANES,),
        in_specs=[_cols(T)] * 5 + [_cols(T, K), _cols(T, 1), _cols(T, 1)],
        out_specs=[_cols(T), _cols(T), _cols(T, P), _cols(T, 1), _cols(T, 1), _cols(T, 1)],
        out_shape=[_sds((T, D), BF16), _sds((T, D), BF16), _sds((P, D), F32), _sds((1, D), F32),
                   _sds((1, D), F32), _sds((1, D), F32)],
        scratch_shapes=[pltpu.VMEM((T + P, LANES), F32), pltpu.VMEM((T + P, LANES), F32)],
        compiler_params=_cp())(dcs, cc, c, cv, cg, w, gg, gb)


def _reduce_adamw(recv, w, m, v, name):
    L, R, C = w.shape
    tr = next((c for c in (512, 352, 256, 176, 128, 64, 8) if R % c == 0), R)
    c1 = 1.0 - ADAM_B1 ** ADAM_STEP
    c2 = 1.0 - ADAM_B2 ** ADAM_STEP

    def body(recv_ref, w_ref, m_ref, v_ref, g_ref, d_ref, mo_ref, vo_ref):
        g = recv_ref[0, 0].astype(F32)
        for k in range(1, N_DEV):
            g = g + recv_ref[k, 0].astype(F32)
        mn = ADAM_B1 * m_ref[0] + (1.0 - ADAM_B1) * g
        vn = ADAM_B2 * v_ref[0] + (1.0 - ADAM_B2) * (g * g)
        g_ref[0] = g
        mo_ref[0] = mn
        vo_ref[0] = vn
        d_ref[0] = (-ADAM_LR) * ((mn / c1) / (jnp.sqrt(vn / c2) + ADAM_EPS) + ADAM_WD * w_ref[0])

    blk = pl.BlockSpec((1, tr, C), lambda l, i: (l, i, 0))
    return pl.pallas_call(
        body, name=name, grid=(L, R // tr),
        in_specs=[pl.BlockSpec((N_DEV, 1, tr, C), lambda l, i: (0, l, i, 0)), blk, blk, blk],
        out_specs=[blk] * 4,
        out_shape=[_sds((L, R, C), F32)] * 4,
        compiler_params=_cp(2))(recv, w, m, v)


def _position():
    return lax.axis_index("x"), lax.axis_index("y"), lax.axis_index("c")


def _index(p):
    return 4 * p[0] + 2 * p[1] + p[2]


def _allgather(arrs, name):
    n = len(arrs)
    hbm = pl.BlockSpec(memory_space=pl.ANY)

    def body(*refs):
        xs, outs = refs[:n], refs[n:2 * n]
        send_sems, recv_sems, local_sems = refs[2 * n:]
        x, y, c = _position()
        me, sibling = (x, y, c), (x, y, 1 - c)
        chips = [(1 - x, y), (x, 1 - y), (1 - x, 1 - y)]

        def copy(a, k, block, to, src=None):
            dst = outs[a].at[_index(block)]
            return pltpu.make_async_remote_copy(
                src_ref=dst if src is None else src, dst_ref=dst, send_sem=send_sems.at[a, k],
                recv_sem=recv_sems.at[a, k], device_id=to, device_id_type=MESH)

        mine = [pltpu.make_async_copy(xs[a], outs[a].at[_index(me)], local_sems.at[a]) for a in range(n)]
        for cp in mine:
            cp.start()
        first = []
        for a in range(n):
            first.append(copy(a, 0, me, sibling, src=xs[a]))
            first += [copy(a, 1 + j, me, (*chip, c), src=xs[a]) for j, chip in enumerate(chips)]
        for cp in first:
            cp.start()
        passed = []
        for j, chip in enumerate(chips):
            for a in range(n):
                copy(a, 1 + j, (*chip, c), me).wait_recv()
                fwd = copy(a, 4 + j, (*chip, c), sibling)
                fwd.start()
                passed.append(fwd)
        for a in range(n):
            copy(a, 0, sibling, me).wait_recv()
        for j, chip in enumerate(chips):
            for a in range(n):
                copy(a, 4 + j, (*chip, 1 - c), me).wait_recv()
        for cp in first + passed:
            cp.wait_send()
        for cp in mine:
            cp.wait()

    return pl.pallas_call(
        body, name=name,
        in_specs=[hbm] * n, out_specs=[hbm] * n,
        out_shape=[_sds((N_DEV,) + a.shape, a.dtype) for a in arrs],
        scratch_shapes=[pltpu.SemaphoreType.DMA((n, 7)), pltpu.SemaphoreType.DMA((n, 7)),
                        pltpu.SemaphoreType.DMA((n,))],
    )(*arrs)


def _alltoall(arrs, name):
    n = len(arrs)
    hbm = pl.BlockSpec(memory_space=pl.ANY)

    def body(*refs):
        sends, recvs = refs[:n], refs[n:2 * n]
        send_sems, recv_sems, local_sems = refs[2 * n:]
        x, y, c = _position()
        me = (x, y, c)
        copies = []
        for a in range(n):
            own = pltpu.make_async_copy(sends[a].at[_index(me)], recvs[a].at[0], local_sems.at[a])
            own.start()
            copies.append(own)
        for k in range(1, N_DEV):
            peer = (1 - x if k & 4 else x, 1 - y if k & 2 else y, 1 - c if k & 1 else c)
            for a in range(n):
                cp = pltpu.make_async_remote_copy(
                    src_ref=sends[a].at[_index(peer)], dst_ref=recvs[a].at[k], send_sem=send_sems.at[a, k - 1],
                    recv_sem=recv_sems.at[a, k - 1], device_id=peer, device_id_type=MESH)
                cp.start()
                copies.append(cp)
        for cp in copies:
            cp.wait()

    return pl.pallas_call(
        body, name=name,
        in_specs=[hbm] * n, out_specs=[hbm] * n,
        out_shape=[_sds(a.shape, a.dtype) for a in arrs],
        scratch_shapes=[pltpu.SemaphoreType.DMA((n, 7)), pltpu.SemaphoreType.DMA((n, 7)),
                        pltpu.SemaphoreType.DMA((n,))],
    )(*arrs)


def _unshard(name, gathered):
    n, L, r, c = gathered.shape
    if name in COL_SHARDED:
        return gathered.transpose(1, 2, 0, 3).reshape(L, r, n * c)
    return gathered.transpose(1, 0, 2, 3).reshape(L, n * r, c)


def _to_shards(name, full):
    L, R, C = full.shape
    if name in COL_SHARDED:
        return full.reshape(L, R, N_DEV, C // N_DEV).transpose(2, 0, 1, 3)
    return full.reshape(L, N_DEV, R // N_DEV, C).transpose(1, 0, 2, 3)


def _block_diag(w):
    H, b, _ = w.shape
    rows = [jnp.pad(w[h], ((0, 0), (h * b, (H - 1 - h) * b))) for h in range(H)]
    return jnp.concatenate(rows, axis=0)


def _diag_blocks(dense, H):
    b = dense.shape[0] // H
    return jnp.stack([dense[h * b:(h + 1) * b, h * b:(h + 1) * b] for h in range(H)])


def _pack(arrs, rows):
    flat = jnp.concatenate([a.reshape(-1) for a in arrs])
    return jnp.pad(flat, (0, rows * 1024 - flat.shape[0])).reshape(1, rows, 1024)


def _unpack(packed, shapes):
    flat = packed.reshape(-1)
    out, off = [], 0
    for s in shapes:
        n = math.prod(s)
        out.append(flat[off:off + n].reshape(s))
        off += n
    return out


def kernel(x, ffn1_w_gu, ffn1_w_down, ln1_g, ln1_b, mix_w_in, mix_b_in, conv_dw_w, conv_dw_b, conv_gn_g, conv_gn_b, conv_w_proj, rnn_conv_w, rnn_conv_b, rnn_w_a, rnn_b_a, rnn_w_x, rnn_b_x, rnn_lambda, rnn_w_proj, mix_w_out, ln2_g, ln2_b, ffn2_w_gu, ffn2_w_down, ln3_g, ln3_b, loss_target, m_ffn1_w_gu, m_ffn1_w_down, m_ln1_g, m_ln1_b, m_mix_w_in, m_mix_b_in, m_conv_dw_w, m_conv_dw_b, m_conv_gn_g, m_conv_gn_b, m_conv_w_proj, m_rnn_conv_w, m_rnn_conv_b, m_rnn_w_a, m_rnn_b_a, m_rnn_w_x, m_rnn_b_x, m_rnn_lambda, m_rnn_w_proj, m_mix_w_out, m_ln2_g, m_ln2_b, m_ffn2_w_gu, m_ffn2_w_down, m_ln3_g, m_ln3_b, v_ffn1_w_gu, v_ffn1_w_down, v_ln1_g, v_ln1_b, v_mix_w_in, v_mix_b_in, v_conv_dw_w, v_conv_dw_b, v_conv_gn_g, v_conv_gn_b, v_conv_w_proj, v_rnn_conv_w, v_rnn_conv_b, v_rnn_w_a, v_rnn_b_a, v_rnn_w_x, v_rnn_b_x, v_rnn_lambda, v_rnn_w_proj, v_mix_w_out, v_ln2_g, v_ln2_b, v_ffn2_w_gu, v_ffn2_w_down, v_ln3_g, v_ln3_b):
    given = dict(locals())
    W = {n: given[n] for n in WEIGHTS}
    M = {n: given["m_" + n] for n in WEIGHTS}
    V = {n: given["v_" + n] for n in WEIGHTS}
    T, D = x.shape[1], x.shape[2]
    L = DEPTH
    x2 = x.reshape(T, D)
    target = loss_target.reshape(T, D)
    d_rnn = rnn_conv_b.shape[1]

    gathered = _allgather([W[n].astype(BF16) if n in BF16_ON_WIRE else W[n] for n in SHARDED], "gather_weights")
    full = {n: _unshard(n, g) for n, g in zip(SHARDED, gathered)}
    bd_a = [_block_diag(rnn_w_a[l]).astype(BF16) for l in range(L)]
    bd_x = [_block_diag(rnn_w_x[l]).astype(BF16) for l in range(L)]

    def vec(name, l):
        return W[name][l:l + 1]

    saved = []
    h, hb = x2, x2.astype(BF16)
    for l in range(L):
        s = {"hb_in": hb}
        s["gu1"], s["a1"] = _ffn_up(hb, full["ffn1_w_gu"][l], f"ffn1_up_{l}")
        s["r1"], y1, s["y1b"] = _ffn_down_ln(s["a1"], full["ffn1_w_down"][l], h, vec("ln1_g", l), vec("ln1_b", l),
                                              f"ffn1_down_ln_{l}")
        s["c"], s["cv"], s["cg"], s["rx"], s["rg"], s["gc"], s["gr"] = _mix_in(
            s["y1b"], full["mix_w_in"][l], vec("mix_b_in", l), d_rnn, f"mix_in_{l}")
        s["cc"], s["cs"] = _conv31_gn(s["c"], full["conv_dw_w"][l], vec("conv_dw_b", l), vec("conv_gn_g", l),
                                      vec("conv_gn_b", l), f"conv31_gn_{l}")
        s["r"] = _conv4(s["rx"], full["rnn_conv_w"][l], vec("rnn_conv_b", l), f"conv4_{l}")
        s["ra"], s["ri"], s["a"], uu = _gates(s["r"], bd_a[l], bd_x[l], vec("rnn_b_a", l), vec("rnn_b_x", l),
                                              vec("rnn_lambda", l), f"gates_{l}")
        s["h"], s["hp"], s["hg"] = _scan_fwd(s["a"], uu, s["rg"], f"scan_{l}")
        s["yc"], s["yr"], s["m"], s["r2"], y2, s["y2b"] = _mix_out_ln(
            s["cs"], s["hg"], s["gc"], s["gr"], y1, full["conv_w_proj"][l], full["rnn_w_proj"][l],
            full["mix_w_out"][l], vec("ln2_g", l), vec("ln2_b", l), f"mix_out_ln_{l}")
        s["gu2"], s["a2"] = _ffn_up(s["y2b"], full["ffn2_w_gu"][l], f"ffn2_up_{l}")
        s["r3"], h, hb = _ffn_down_ln(s["a2"], full["ffn2_w_down"][l], y2, vec("ln3_g", l), vec("ln3_b", l),
                                      f"ffn2_down_ln_{l}")
        saved.append(s)

    G = {n: [None] * L for n in WEIGHTS}
    loss_acc, dr3, drb3, G["ln3_g"][L - 1], G["ln3_b"][L - 1] = _loss_ln_bwd(
        h, target, saved[L - 1]["r3"], vec("ln3_g", L - 1), "loss_ln3_bwd")
    grad_x = None
    for l in reversed(range(L)):
        s = saved[l]
        dgu2 = _ffn_bwd_a(drb3, full["ffn2_w_down"][l], s["gu2"], f"ffn2_bwd_a_{l}")
        G["ffn2_w_down"][l] = _mm_tn(s["a2"], drb3, f"ffn2_dw_down_{l}", scale=0.5)
        G["ffn2_w_gu"][l] = _mm_tn(s["y2b"], dgu2, f"ffn2_dw_gu_{l}")
        dr2, drb2, G["ln2_g"][l], G["ln2_b"][l] = _nt_res(
            dgu2, full["ffn2_w_gu"][l], dr3, f"ffn2_bwd_x_{l}", ln=(s["r2"], vec("ln2_g", l)))
        dyc, dyr, dgc, dgr, dcs, dh, drg = _mix_bwd1(
            drb2, full["mix_w_out"][l], full["conv_w_proj"][l], full["rnn_w_proj"][l], s["gc"], s["gr"], s["yc"],
            s["yr"], s["rg"], s["h"], f"mix_bwd_out_{l}")
        G["mix_w_out"][l] = _mm_tn(s["m"], drb2, f"mix_dw_out_{l}")
        G["conv_w_proj"][l] = _mm_tn(s["cs"], dyc, f"conv_dw_proj_{l}")
        G["rnn_w_proj"][l] = _mm_tn(s["hg"], dyr, f"rnn_dw_proj_{l}")
        gsc = _scan_bwd(dh, s["a"], f"scan_bwd_{l}")
        dr_, dpa, dpx, G["rnn_lambda"][l], G["rnn_b_a"][l], G["rnn_b_x"][l] = _gates_bwd(
            gsc, s["hp"], s["ra"], s["ri"], s["r"], vec("rnn_lambda", l), bd_a[l], bd_x[l], f"gates_bwd_{l}")
        G["rnn_w_a"][l] = _diag_blocks(_mm_tn(s["r"], dpa, f"rnn_dw_a_{l}"), RNN_BLOCKS)
        G["rnn_w_x"][l] = _diag_blocks(_mm_tn(s["r"], dpx, f"rnn_dw_x_{l}"), RNN_BLOCKS)
        drx, dw4, G["rnn_conv_b"][l] = _conv4_bwd(dr_, s["rx"], full["rnn_conv_w"][l], f"conv4_bwd_{l}")
        G["rnn_conv_w"][l] = dw4[:RNN_CONV_WIDTH]
        dcv, dcg, dw31, G["conv_dw_b"][l], G["conv_gn_g"][l], G["conv_gn_b"][l] = _conv31_bwd(
            dcs, s["cc"], s["c"], s["cv"], s["cg"], full["conv_dw_w"][l], vec("conv_gn_g", l),
            vec("conv_gn_b", l), f"conv31_bwd_{l}")
        G["conv_dw_w"][l] = dw31[:CONV_WIDTH]
        du = jnp.concatenate([dcv, dcg, drx, drg, dgc, dgr], axis=1)
        G["mix_w_in"][l] = _mm_tn(s["y1b"], du, f"mix_dw_in_{l}")
        dr1, drb1, G["ln1_g"][l], G["ln1_b"][l], G["mix_b_in"][l] = _nt_res(
            du, full["mix_w_in"][l], dr2, f"mix_bwd_in_{l}", ln=(s["r1"], vec("ln1_g", l)), colsum=True)
        dgu1 = _ffn_bwd_a(drb1, full["ffn1_w_down"][l], s["gu1"], f"ffn1_bwd_a_{l}")
        G["ffn1_w_down"][l] = _mm_tn(s["a1"], drb1, f"ffn1_dw_down_{l}", scale=0.5)
        G["ffn1_w_gu"][l] = _mm_tn(s["hb_in"], dgu1, f"ffn1_dw_gu_{l}")
        if l > 0:
            dr3, drb3, G["ln3_g"][l - 1], G["ln3_b"][l - 1] = _nt_res(
                dgu1, full["ffn1_w_gu"][l], dr1, f"ffn1_bwd_x_{l}", ln=(saved[l - 1]["r3"], vec("ln3_g", l - 1)))
        else:
            (grad_x,) = _nt_res(dgu1, full["ffn1_w_gu"][l], dr1, f"ffn1_bwd_x_{l}")

    loss = lax.psum(loss_acc[0, 0], ("x", "y", "c"))
    local = {n: jnp.stack([g.reshape(W[n].shape[1:]) if n in REPLICATED else g for g in G[n]]) for n in WEIGHTS}

    sends = [_to_shards(n, local[n]).astype(BF16 if n in BF16_ON_WIRE else F32) for n in SHARDED]
    recvs = _alltoall(sends, "scatter_grads")
    out = {}
    for n, recv in zip(SHARDED, recvs):
        out[n] = _reduce_adamw(recv, W[n], M[n], V[n], f"adamw_{n}")
    shapes = [W[n].shape for n in REPLICATED]
    rows = -(-sum(math.prod(s) for s in shapes) // (1024 * SUBLANES)) * SUBLANES
    (small,) = _allgather([_pack([local[n] for n in REPLICATED], rows)[0]], "gather_small_grads")
    packed = _reduce_adamw(small.reshape(N_DEV, 1, rows, 1024), _pack([W[n] for n in REPLICATED], rows),
                           _pack([M[n] for n in REPLICATED], rows), _pack([V[n] for n in REPLICATED], rows),
                           "adamw_small")
    unpacked = [_unpack(p, shapes) for p in packed]
    for i, n in enumerate(REPLICATED):
        out[n] = tuple(u[i] for u in unpacked)

    return (loss, grad_x.reshape(x.shape), *[out[n][0] for n in WEIGHTS], *[out[n][1] for n in WEIGHTS],
            *[out[n][2] for n in WEIGHTS], *[out[n][3] for n in WEIGHTS])
```

```python
import math

import jax
import jax.numpy as jnp
from jax import lax
from jax.experimental import pallas as pl
from jax.experimental.pallas import tpu as pltpu

F32 = jnp.float32
BF16 = jnp.bfloat16
MESH = pl.DeviceIdType.MESH

DEPTH = 2
ALPHA = (2 * DEPTH) ** 0.25
LN_EPS = 1e-5
RG_LRU_C = 8.0
CONV_WIDTH = 31
RNN_CONV_WIDTH = 4
RNN_BLOCKS = 16
N_DEV = 8
ADAM_LR, ADAM_B1, ADAM_B2, ADAM_EPS, ADAM_WD, ADAM_STEP = 0.001, 0.9, 0.999, 1e-08, 0.01, 10

LANES = 128
SUBLANES = 8
VMEM_LIMIT = 56 * 1024 * 1024
CONV_PAD = 32
RNN_PAD = 8
CONV_ROWS = 128
CONV_BLOCK = 32
GATHER_US_PER_MB = 43.0
SCATTER_US_PER_MB = 86.0

WEIGHTS = ['ffn1_w_gu', 'ffn1_w_down', 'ln1_g', 'ln1_b', 'mix_w_in', 'mix_b_in', 'conv_dw_w', 'conv_dw_b',
           'conv_gn_g', 'conv_gn_b', 'conv_w_proj', 'rnn_conv_w', 'rnn_conv_b', 'rnn_w_a', 'rnn_b_a', 'rnn_w_x',
           'rnn_b_x', 'rnn_lambda', 'rnn_w_proj', 'mix_w_out', 'ln2_g', 'ln2_b', 'ffn2_w_gu', 'ffn2_w_down',
           'ln3_g', 'ln3_b']
COL_SHARDED = ['ffn1_w_gu', 'mix_w_in', 'ffn2_w_gu', 'conv_dw_w', 'rnn_conv_w']
ROW_SHARDED = ['ffn1_w_down', 'conv_w_proj', 'rnn_w_proj', 'mix_w_out', 'ffn2_w_down']
SHARDED = COL_SHARDED + ROW_SHARDED
BF16_ON_WIRE = ['ffn1_w_gu', 'mix_w_in', 'ffn2_w_gu', 'ffn1_w_down', 'conv_w_proj', 'rnn_w_proj', 'mix_w_out',
                'ffn2_w_down']
REPLICATED = [n for n in WEIGHTS if n not in SHARDED]
USE_ORDER = ['ffn1_w_gu', 'ffn1_w_down', 'mix_w_in', 'conv_dw_w', 'rnn_conv_w', 'conv_w_proj', 'rnn_w_proj',
             'mix_w_out', 'ffn2_w_gu', 'ffn2_w_down']


def _cp(n_axes=1):
    return pltpu.CompilerParams(dimension_semantics=("arbitrary",) * n_axes, vmem_limit_bytes=VMEM_LIMIT)


def _rows(tm, c):
    return pl.BlockSpec((tm, c), lambda i: (i, 0))


def _res(shape):
    nd = len(shape)
    return pl.BlockSpec(tuple(shape), lambda *_: (0,) * nd, pipeline_mode=pl.Buffered(1))


def _acc(shape):
    nd = len(shape)
    return pl.BlockSpec(tuple(shape), lambda *_: (0,) * nd)


def _tile(t, want):
    return want if t % want == 0 else t


def _sds(shape, dtype):
    return jax.ShapeDtypeStruct(tuple(shape), dtype)


def _mbytes(a):
    return a.size * a.dtype.itemsize / 1e6


def _ln_fwd(r, g, b):
    mu = jnp.mean(r, axis=-1, keepdims=True)
    xc = r - mu
    var = jnp.mean(xc * xc, axis=-1, keepdims=True)
    return xc * lax.rsqrt(var + LN_EPS) * g + b


def _ln_bwd(dy, r, g):
    mu = jnp.mean(r, axis=-1, keepdims=True)
    xc = r - mu
    var = jnp.mean(xc * xc, axis=-1, keepdims=True)
    rstd = lax.rsqrt(var + LN_EPS)
    xhat = xc * rstd
    dxh = dy * g
    m1 = jnp.mean(dxh, axis=-1, keepdims=True)
    m2 = jnp.mean(dxh * xhat, axis=-1, keepdims=True)
    dr = rstd * (dxh - m1 - xhat * m2)
    return dr, jnp.sum(dy * xhat, axis=0, keepdims=True), jnp.sum(dy, axis=0, keepdims=True)


def _sigmoid(x):
    return jax.nn.sigmoid(x)


_GELU_K = math.sqrt(2.0 / math.pi)


def _gelu(x):
    t = jnp.tanh(_GELU_K * (x + 0.044715 * x * x * x))
    return 0.5 * x * (1.0 + t), t


def _gelu_grad(x, t):
    return 0.5 * (1.0 + t) + 0.5 * x * (1.0 - t * t) * (_GELU_K * (1.0 + 3.0 * 0.044715 * x * x))


def _expm1(x):
    taylor = x * (1.0 + x * (0.5 + x * (1.0 / 6.0 + x * (1.0 / 24.0 + x * (1.0 / 120.0)))))
    return jnp.where(jnp.abs(x) < 0.03, taylor, jnp.exp(x) - 1.0)


def _softplus_neg(lam):
    return jnp.maximum(-lam, 0.0) + jnp.log1p(jnp.exp(-jnp.abs(lam)))


def _dot(a, b):
    return jnp.dot(a, b, preferred_element_type=F32)


def _dot_nt(a, b):
    return lax.dot_general(a, b, (((1,), (1,)), ((), ())), preferred_element_type=F32)


def _dot_tn(a, b):
    return lax.dot_general(a, b, (((0,), (0,)), ((), ())), preferred_element_type=F32)


def _chunks(width, cn):
    return [(j, min(cn, width - j)) for j in range(0, width, cn)]


def _position():
    return lax.axis_index("x"), lax.axis_index("y"), lax.axis_index("c")


def _index(p):
    return 4 * p[0] + 2 * p[1] + p[2]


def _comm_out_shapes(kind, arrs):
    return [_sds((N_DEV,) + a.shape if kind == "gather" else a.shape, a.dtype) for a in arrs]


def _comm_scratch(n):
    return [pltpu.SemaphoreType.DMA((n, 7)), pltpu.SemaphoreType.DMA((n, 7)), pltpu.SemaphoreType.DMA((n,))]


def _comm_phases(kind, srcs, dsts, send_sems, recv_sems, local_sems):
    n = len(srcs)
    x, y, c = _position()
    me, sibling = (x, y, c), (x, y, 1 - c)
    chips = [(1 - x, y), (x, 1 - y), (1 - x, 1 - y)]

    if kind == "gather":
        def copy(a, k, block, to, src=None):
            dst = dsts[a].at[_index(block)]
            return pltpu.make_async_remote_copy(
                src_ref=dst if src is None else src, dst_ref=dst, send_sem=send_sems.at[a, k],
                recv_sem=recv_sems.at[a, k], device_id=to, device_id_type=MESH)

        def mine(a):
            return pltpu.make_async_copy(srcs[a], dsts[a].at[_index(me)], local_sems.at[a])

        def first(a):
            return [copy(a, 0, me, sibling, src=srcs[a])] + [
                copy(a, 1 + j, me, (*chip, c), src=srcs[a]) for j, chip in enumerate(chips)]

        def start():
            for a in range(n):
                mine(a).start()
            for a in range(n):
                for cp in first(a):
                    cp.start()

        def mid():
            for j, chip in enumerate(chips):
                for a in range(n):
                    copy(a, 1 + j, (*chip, c), me).wait_recv()
                    copy(a, 4 + j, (*chip, c), sibling).start()

        def end():
            for a in range(n):
                copy(a, 0, sibling, me).wait_recv()
            for j, chip in enumerate(chips):
                for a in range(n):
                    copy(a, 4 + j, (*chip, 1 - c), me).wait_recv()
            for a in range(n):
                for cp in first(a):
                    cp.wait_send()
                for j, chip in enumerate(chips):
                    copy(a, 4 + j, (*chip, c), sibling).wait_send()
                mine(a).wait()

        return start, mid, end

    def peer_of(k):
        return (1 - x if k & 4 else x, 1 - y if k & 2 else y, 1 - c if k & 1 else c)

    def own(a):
        return pltpu.make_async_copy(srcs[a].at[_index(me)], dsts[a].at[0], local_sems.at[a])

    def remote(a, k):
        peer = peer_of(k)
        return pltpu.make_async_remote_copy(
            src_ref=srcs[a].at[_index(peer)], dst_ref=dsts[a].at[k], send_sem=send_sems.at[a, k - 1],
            recv_sem=recv_sems.at[a, k - 1], device_id=peer, device_id_type=MESH)

    def start():
        for a in range(n):
            own(a).start()
        for k in range(1, N_DEV):
            for a in range(n):
                remote(a, k).start()

    def end():
        for k in range(1, N_DEV):
            for a in range(n):
                remote(a, k).wait()
        for a in range(n):
            own(a).wait()

    return start, (lambda: None), end


def _exchange(kind, arrs, name):
    n = len(arrs)
    hbm = pl.BlockSpec(memory_space=pl.ANY)

    def body(*refs):
        start, mid, end = _comm_phases(kind, refs[:n], refs[n:2 * n], *refs[2 * n:])
        start()
        mid()
        end()

    return pl.pallas_call(
        body, name=name, in_specs=[hbm] * n, out_specs=[hbm] * n, out_shape=_comm_out_shapes(kind, arrs),
        scratch_shapes=_comm_scratch(n))(*arrs)


def _pallas(body, *, name, grid, in_specs, out_specs, out_shape, args, scratch_shapes=(), comm=None):
    in_specs, out_specs, out_shape = list(in_specs), list(out_specs), list(out_shape)
    scratch_shapes = list(scratch_shapes)
    if not comm:
        return pl.pallas_call(body, name=name, grid=grid, in_specs=in_specs, out_specs=out_specs,
                              out_shape=out_shape, scratch_shapes=scratch_shapes,
                              compiler_params=_cp(len(grid)))(*args)
    arrs = comm["arrs"]
    ns, n_in, n_out, n_scr = len(arrs), len(in_specs), len(out_specs), len(scratch_shapes)
    hbm = pl.BlockSpec(memory_space=pl.ANY)
    total = math.prod(grid)

    def carrier(*refs):
        ins, srcs = refs[:n_in], refs[n_in:n_in + ns]
        outs, dsts = refs[n_in + ns:n_in + ns + n_out], refs[n_in + ns + n_out:n_in + 2 * ns + n_out]
        scr, sems = refs[n_in + 2 * ns + n_out:n_in + 2 * ns + n_out + n_scr], refs[n_in + 2 * ns + n_out + n_scr:]
        step = pl.program_id(0)
        for ax in range(1, len(grid)):
            step = step * grid[ax] + pl.program_id(ax)
        start, mid, end = _comm_phases(comm["kind"], srcs, dsts, *sems)
        pl.when(step == 0)(start)
        body(*ins, *outs, *scr)
        pl.when(step == total // 2)(mid)
        pl.when(step == total - 1)(end)

    res = pl.pallas_call(
        carrier, name=name, grid=grid, in_specs=in_specs + [hbm] * ns, out_specs=out_specs + [hbm] * ns,
        out_shape=out_shape + _comm_out_shapes(comm["kind"], arrs),
        scratch_shapes=scratch_shapes + _comm_scratch(ns), compiler_params=_cp(len(grid)))(*args, *arrs)
    comm["recv"] = res[n_out:]
    return res[:n_out]


def _ffn_up(xb, w, name, comm=None):
    T, D = xb.shape
    F = w.shape[1] // 2
    tm = _tile(T, 512)

    def body(x_ref, w_ref, gu_ref, a_ref):
        x = x_ref[...]
        for j, cw in _chunks(F, 256):
            g = _dot(x, w_ref[:, j:j + cw])
            u = _dot(x, w_ref[:, F + j:F + j + cw])
            gu_ref[:, j:j + cw] = g.astype(BF16)
            gu_ref[:, F + j:F + j + cw] = u.astype(BF16)
            a_ref[:, j:j + cw] = (g * _sigmoid(g) * u).astype(BF16)

    return _pallas(
        body, name=name, grid=(T // tm,),
        in_specs=[_rows(tm, D), _res(w.shape)],
        out_specs=[_rows(tm, 2 * F), _rows(tm, F)],
        out_shape=[_sds((T, 2 * F), BF16), _sds((T, F), BF16)],
        args=(xb, w), comm=comm)


def _ffn_down_ln(a, wd, xres, g, b, name, comm=None):
    T, F = a.shape
    D = wd.shape[1]
    tm = _tile(T, 512)

    def body(a_ref, wd_ref, x_ref, g_ref, b_ref, r_ref, y_ref, yb_ref):
        r = ALPHA * x_ref[...] + 0.5 * _dot(a_ref[...], wd_ref[...])
        y = _ln_fwd(r, g_ref[...], b_ref[...])
        r_ref[...] = r
        y_ref[...] = y
        yb_ref[...] = y.astype(BF16)

    return _pallas(
        body, name=name, grid=(T // tm,),
        in_specs=[_rows(tm, F), _res(wd.shape), _rows(tm, D), _res((1, D)), _res((1, D))],
        out_specs=[_rows(tm, D), _rows(tm, D), _rows(tm, D)],
        out_shape=[_sds((T, D), F32), _sds((T, D), F32), _sds((T, D), BF16)],
        args=(a, wd, xres, g, b), comm=comm)


def _mix_in(hb, w, bias, d_rnn, name, comm=None):
    T, D = hb.shape
    R = d_rnn
    tm = _tile(T, 256)
    o_cv, o_cg, o_rx, o_rg, o_gc, o_gr = 0, D, 2 * D, 2 * D + R, 2 * D + 2 * R, 3 * D + 2 * R

    def body(x_ref, w_ref, b_ref, c_ref, cv_ref, cg_ref, rx_ref, rg_ref, gc_ref, gr_ref):
        x = x_ref[...]

        def seg(off, j, cw):
            return _dot(x, w_ref[:, off + j:off + j + cw]) + b_ref[:, off + j:off + j + cw]

        for j, cw in _chunks(D, 256):
            cv = seg(o_cv, j, cw)
            cg = seg(o_cg, j, cw)
            cv_ref[:, j:j + cw] = cv.astype(BF16)
            cg_ref[:, j:j + cw] = cg.astype(BF16)
            c_ref[:, j:j + cw] = cv * _sigmoid(cg)
            gc_ref[:, j:j + cw] = seg(o_gc, j, cw).astype(BF16)
            gr_ref[:, j:j + cw] = seg(o_gr, j, cw).astype(BF16)
        for j, cw in _chunks(R, 256):
            rx_ref[:, j:j + cw] = seg(o_rx, j, cw)
            rg_ref[:, j:j + cw] = seg(o_rg, j, cw).astype(BF16)

    return _pallas(
        body, name=name, grid=(T // tm,),
        in_specs=[_rows(tm, D), _res(w.shape), _res(bias.shape)],
        out_specs=[_rows(tm, D), _rows(tm, D), _rows(tm, D), _rows(tm, R), _rows(tm, R), _rows(tm, D),
                   _rows(tm, D)],
        out_shape=[_sds((T, D), F32), _sds((T, D), BF16), _sds((T, D), BF16), _sds((T, R), F32),
                   _sds((T, R), BF16), _sds((T, D), BF16), _sds((T, D), BF16)],
        args=(hb, w, bias), comm=comm)


def _cols(t, rows=None):
    return pl.BlockSpec((t if rows is None else rows, LANES), lambda j: (0, j))


def _gn_stats(cc):
    mu = jnp.mean(cc, axis=-1, keepdims=True)
    xc = cc - mu
    var = jnp.mean(xc * xc, axis=-1, keepdims=True)
    rstd = lax.rsqrt(var + LN_EPS)
    return xc * rstd, rstd


def _tap_groups(offsets):
    groups = {}
    for k, s in enumerate(offsets):
        groups.setdefault(s % SUBLANES, []).append((k, s - s % SUBLANES))
    return sorted(groups.items())


def _shifted(win, phase):
    return win if phase == 0 else pltpu.roll(win, win.shape[0] - phase, 0)


def _conv31_gn(c, w, bias, gg, gb, name, comm=None):
    T, D = c.shape
    K = w.shape[0]
    B, P = CONV_BLOCK, CONV_PAD
    assert D % LANES == 0 and T % B == 0 and K - 1 <= P
    groups = _tap_groups([P - (K - 1) + k for k in range(K)])

    def body(c_ref, w_ref, b_ref, gg_ref, gb_ref, cc_ref, cs_ref, xpad):
        xpad[0:P, :] = jnp.zeros((P, LANES), F32)
        xpad[P:P + T, :] = c_ref[...]
        wv = w_ref[...]
        bv, ggv, gbv = b_ref[...], gg_ref[...], gb_ref[...]

        def step(i, carry):
            base = pl.multiple_of(i * B, B)
            win = xpad[pl.ds(base, B + P), :]
            acc = jnp.zeros((B, LANES), F32)
            for phase, taps in groups:
                sh = _shifted(win, phase)
                for k, off in taps:
                    acc = acc + wv[k:k + 1, :] * sh[off:off + B, :]
            cc = acc + bv
            cc_ref[pl.ds(base, B), :] = cc
            xhat, _ = _gn_stats(cc)
            gn = xhat * ggv + gbv
            cs_ref[pl.ds(base, B), :] = (gn * _sigmoid(gn)).astype(BF16)
            return carry

        lax.fori_loop(0, T // B, step, 0)

    return _pallas(
        body, name=name, grid=(D // LANES,),
        in_specs=[_cols(T), _cols(T, K), _cols(T, 1), _cols(T, 1), _cols(T, 1)],
        out_specs=[_cols(T), _cols(T)],
        out_shape=[_sds((T, D), F32), _sds((T, D), BF16)],
        scratch_shapes=[pltpu.VMEM((T + P, LANES), F32)],
        args=(c, w, bias, gg, gb), comm=comm)


def _conv4(rx, w, bias, name):
    T, C = rx.shape
    K = w.shape[0]
    R = CONV_ROWS
    assert C % LANES == 0 and T % R == 0 and K - 1 <= RNN_PAD

    def body(x_ref, w_ref, b_ref, r_ref, xpad):
        xpad[0:RNN_PAD, :] = jnp.zeros((RNN_PAD, LANES), F32)
        xpad[RNN_PAD:RNN_PAD + T, :] = x_ref[...]
        wv, bv = w_ref[...], b_ref[...]

        def step(i, carry):
            base = pl.multiple_of(i * R, R)
            win = xpad[pl.ds(base, R + RNN_PAD), :]
            acc = jnp.zeros((R, LANES), F32)
            for k in range(K):
                s = RNN_PAD - (K - 1) + k
                acc = acc + wv[k:k + 1, :] * win[s:s + R, :]
            r_ref[pl.ds(base, R), :] = acc + bv
            return carry

        lax.fori_loop(0, T // R, step, 0)

    (r,) = _pallas(
        body, name=name, grid=(C // LANES,),
        in_specs=[_cols(T), _cols(T, K), _cols(T, 1)],
        out_specs=[_cols(T)],
        out_shape=[_sds((T, C), F32)],
        scratch_shapes=[pltpu.VMEM((T + RNN_PAD, LANES), F32)],
        args=(rx, w, bias))
    return r


def _gates(r, bda, bdx, b_a, b_x, lam, name, comm=None):
    T, C = r.shape
    tm = _tile(T, 512)

    def body(r_ref, wa_ref, wx_ref, ba_ref, bx_ref, lam_ref, ra_ref, ri_ref, a_ref, u_ref):
        rv = r_ref[...]
        rb = rv.astype(BF16)
        ra = _sigmoid(_dot(rb, wa_ref[...]) + ba_ref[...])
        ri = _sigmoid(_dot(rb, wx_ref[...]) + bx_ref[...])
        log_a = (-RG_LRU_C) * ra * _softplus_neg(lam_ref[...])
        ra_ref[...] = ra
        ri_ref[...] = ri
        a_ref[...] = jnp.exp(log_a)
        u_ref[...] = jnp.sqrt(-_expm1(2.0 * log_a)) * (ri * rv)

    return _pallas(
        body, name=name, grid=(T // tm,),
        in_specs=[_rows(tm, C), _res(bda.shape), _res(bdx.shape), _res((1, C)), _res((1, C)), _res((1, C))],
        out_specs=[_rows(tm, C)] * 4,
        out_shape=[_sds((T, C), F32)] * 4,
        args=(r, bda, bdx, b_a, b_x, lam), comm=comm)


def _scan_fwd(a, u, rg, name):
    T, C = a.shape
    tt = _tile(T, 512)

    def body(a_ref, u_ref, rg_ref, h_ref, hp_ref, hg_ref, carry):
        @pl.when(pl.program_id(0) == 0)
        def _():
            carry[...] = jnp.zeros_like(carry)

        row = lax.broadcasted_iota(jnp.int32, (SUBLANES, C), 0)

        def group(i, hprev):
            base = pl.multiple_of(i * SUBLANES, SUBLANES)
            av = a_ref[pl.ds(base, SUBLANES), :]
            uv = u_ref[pl.ds(base, SUBLANES), :]
            for s in (1, 2, 4):
                a_s = jnp.where(row >= s, pltpu.roll(av, s, 0), 1.0)
                u_s = jnp.where(row >= s, pltpu.roll(uv, s, 0), 0.0)
                uv = av * u_s + uv
                av = av * a_s
            h = av * hprev + uv
            h_ref[pl.ds(base, SUBLANES), :] = h
            hp_ref[pl.ds(base, SUBLANES), :] = jnp.where(row >= 1, pltpu.roll(h, 1, 0), hprev)
            return h[SUBLANES - 1:SUBLANES, :]

        carry[...] = lax.fori_loop(0, tt // SUBLANES, group, carry[...])
        gel, _ = _gelu(rg_ref[...].astype(F32))
        hg_ref[...] = (h_ref[...] * gel).astype(BF16)

    return _pallas(
        body, name=name, grid=(T // tt,),
        in_specs=[_rows(tt, C)] * 3,
        out_specs=[_rows(tt, C)] * 3,
        out_shape=[_sds((T, C), F32), _sds((T, C), F32), _sds((T, C), BF16)],
        scratch_shapes=[pltpu.VMEM((1, C), F32)],
        args=(a, u, rg))


def _mix_out_ln(cs, hg, gc, gr, hres, wcp, wrp, wout, g, b, name, comm=None):
    T, D = cs.shape
    C = hg.shape[1]
    tm = _tile(T, 512)

    def body(cs_ref, hg_ref, gc_ref, gr_ref, h_ref, wcp_ref, wrp_ref, wo_ref, g_ref, b_ref,
             yc_ref, yr_ref, m_ref, r_ref, y_ref, yb_ref):
        yc = _dot(cs_ref[...], wcp_ref[...])
        yr = _dot(hg_ref[...], wrp_ref[...])
        m = (_sigmoid(gc_ref[...].astype(F32)) * yc + _sigmoid(gr_ref[...].astype(F32)) * yr).astype(BF16)
        r = ALPHA * h_ref[...] + _dot(m, wo_ref[...])
        y = _ln_fwd(r, g_ref[...], b_ref[...])
        yc_ref[...] = yc.astype(BF16)
        yr_ref[...] = yr.astype(BF16)
        m_ref[...] = m
        r_ref[...] = r
        y_ref[...] = y
        yb_ref[...] = y.astype(BF16)

    return _pallas(
        body, name=name, grid=(T // tm,),
        in_specs=[_rows(tm, D), _rows(tm, C), _rows(tm, D), _rows(tm, D), _rows(tm, D), _res(wcp.shape),
                  _res(wrp.shape), _res(wout.shape), _res((1, D)), _res((1, D))],
        out_specs=[_rows(tm, D)] * 6,
        out_shape=[_sds((T, D), BF16), _sds((T, D), BF16), _sds((T, D), BF16), _sds((T, D), F32),
                   _sds((T, D), F32), _sds((T, D), BF16)],
        args=(cs, hg, gc, gr, hres, wcp, wrp, wout, g, b), comm=comm)


def _loss_ln_bwd(y, target, r, g, name):
    T, D = y.shape
    tm = _tile(T, 512)

    def body(y_ref, t_ref, r_ref, g_ref, loss_ref, dr_ref, drb_ref, dg_ref, db_ref):
        @pl.when(pl.program_id(0) == 0)
        def _():
            loss_ref[...] = jnp.zeros_like(loss_ref)
            dg_ref[...] = jnp.zeros_like(dg_ref)
            db_ref[...] = jnp.zeros_like(db_ref)

        e = y_ref[...] - t_ref[...]
        loss_ref[...] += (0.5 / D) * jnp.sum(e * e)
        dr, dg, db = _ln_bwd(e * (1.0 / D), r_ref[...], g_ref[...])
        dr_ref[...] = dr
        drb_ref[...] = dr.astype(BF16)
        dg_ref[...] += dg
        db_ref[...] += db

    return _pallas(
        body, name=name, grid=(T // tm,),
        in_specs=[_rows(tm, D), _rows(tm, D), _rows(tm, D), _res((1, D))],
        out_specs=[_acc((SUBLANES, LANES)), _rows(tm, D), _rows(tm, D), _acc((1, D)), _acc((1, D))],
        out_shape=[_sds((SUBLANES, LANES), F32), _sds((T, D), F32), _sds((T, D), BF16), _sds((1, D), F32),
                   _sds((1, D), F32)],
        args=(y, target, r, g))


def _ffn_bwd_a(drb, wd, gu, name, comm=None):
    T, D = drb.shape
    F = wd.shape[0]
    tm = _tile(T, 512)

    def body(d_ref, wd_ref, gu_ref, o_ref):
        d = d_ref[...]
        for j, cw in _chunks(F, 256):
            da = 0.5 * _dot_nt(d, wd_ref[j:j + cw, :])
            gt = gu_ref[:, j:j + cw].astype(F32)
            up = gu_ref[:, F + j:F + j + cw].astype(F32)
            sg = _sigmoid(gt)
            o_ref[:, j:j + cw] = (da * up * (sg * (1.0 + gt * (1.0 - sg)))).astype(BF16)
            o_ref[:, F + j:F + j + cw] = (da * (gt * sg)).astype(BF16)

    (dgu,) = _pallas(
        body, name=name, grid=(T // tm,),
        in_specs=[_rows(tm, D), _res(wd.shape), _rows(tm, 2 * F)],
        out_specs=[_rows(tm, 2 * F)],
        out_shape=[_sds((T, 2 * F), BF16)],
        args=(drb, wd, gu), comm=comm)
    return dgu


def _nt_res(du, w, dres, name, ln=None, colsum=False, comm=None):
    T, K = du.shape
    D = w.shape[0]
    tm = _tile(T, 256 if K > 6000 else 512)
    n_in = 3 + (2 if ln else 0)

    def body(*refs):
        du_ref, w_ref, dres_ref = refs[:3]
        outs = refs[n_in:]
        dy = ALPHA * dres_ref[...] + _dot_nt(du_ref[...], w_ref[...])
        if ln:
            r_ref, g_ref = refs[3:5]

            @pl.when(pl.program_id(0) == 0)
            def _():
                outs[2][...] = jnp.zeros_like(outs[2])
                outs[3][...] = jnp.zeros_like(outs[3])

            dr, dg, db = _ln_bwd(dy, r_ref[...], g_ref[...])
            outs[0][...] = dr
            outs[1][...] = dr.astype(BF16)
            outs[2][...] += dg
            outs[3][...] += db
        else:
            outs[0][...] = dy
        if colsum:
            cs_ref = outs[-1]

            @pl.when(pl.program_id(0) == 0)
            def _():
                cs_ref[...] = jnp.zeros_like(cs_ref)

            cs_ref[...] += jnp.sum(du_ref[...].astype(F32), axis=0, keepdims=True)

    in_specs = [_rows(tm, K), _res(w.shape), _rows(tm, D)]
    args = [du, w, dres]
    if ln:
        in_specs += [_rows(tm, D), _res((1, D))]
        args += list(ln)
        out_specs = [_rows(tm, D), _rows(tm, D), _acc((1, D)), _acc((1, D))]
        out_shape = [_sds((T, D), F32), _sds((T, D), BF16), _sds((1, D), F32), _sds((1, D), F32)]
    else:
        out_specs = [_rows(tm, D)]
        out_shape = [_sds((T, D), F32)]
    if colsum:
        out_specs.append(_acc((1, K)))
        out_shape.append(_sds((1, K), F32))
    return _pallas(body, name=name, grid=(T // tm,), in_specs=in_specs, out_specs=out_specs, out_shape=out_shape,
                   args=args, comm=comm)


def _mm_tn(x, dy, name, scale=1.0):
    T, K = x.shape
    N = dy.shape[1]
    tt = _tile(T, 1024)
    tn = next((c for c in (512, 768) if N % c == 0), N)
    nt = T // tt

    def body(x_ref, dy_ref, o_ref):
        t = pl.program_id(1)

        @pl.when(t == 0)
        def _():
            o_ref[...] = jnp.zeros_like(o_ref)

        o_ref[...] += _dot_tn(x_ref[...].astype(BF16), dy_ref[...])
        if scale != 1.0:
            @pl.when(t == nt - 1)
            def _():
                o_ref[...] = o_ref[...] * scale

    (out,) = _pallas(
        body, name=name, grid=(N // tn, nt),
        in_specs=[pl.BlockSpec((tt, K), lambda j, t: (t, 0)), pl.BlockSpec((tt, tn), lambda j, t: (t, j))],
        out_specs=[pl.BlockSpec((K, tn), lambda j, t: (0, j))],
        out_shape=[_sds((K, N), F32)],
        args=(x, dy))
    return out


def _mix_bwd1(drb, wout, wcp, wrp, gc, gr, yc, yr, rg, h, name, comm=None):
    T, D = drb.shape
    C = rg.shape[1]
    tm = _tile(T, 512)

    def body(d_ref, wo_ref, wcp_ref, wrp_ref, gc_ref, gr_ref, yc_ref, yr_ref, rg_ref, h_ref,
             dyc_ref, dyr_ref, dgc_ref, dgr_ref, dcs_ref, dh_ref, drg_ref):
        dm = _dot_nt(d_ref[...], wo_ref[...])
        sc = _sigmoid(gc_ref[...].astype(F32))
        sr = _sigmoid(gr_ref[...].astype(F32))
        dyc = (dm * sc).astype(BF16)
        dyr = (dm * sr).astype(BF16)
        dyc_ref[...] = dyc
        dyr_ref[...] = dyr
        dgc_ref[...] = (dm * yc_ref[...].astype(F32) * sc * (1.0 - sc)).astype(BF16)
        dgr_ref[...] = (dm * yr_ref[...].astype(F32) * sr * (1.0 - sr)).astype(BF16)
        dcs_ref[...] = _dot_nt(dyc, wcp_ref[...])
        dhg = _dot_nt(dyr, wrp_ref[...])
        rgv = rg_ref[...].astype(F32)
        gel, t = _gelu(rgv)
        dh_ref[...] = dhg * gel
        drg_ref[...] = (dhg * h_ref[...] * _gelu_grad(rgv, t)).astype(BF16)

    return _pallas(
        body, name=name, grid=(T // tm,),
        in_specs=[_rows(tm, D), _res(wout.shape), _res(wcp.shape), _res(wrp.shape), _rows(tm, D), _rows(tm, D),
                  _rows(tm, D), _rows(tm, D), _rows(tm, C), _rows(tm, C)],
        out_specs=[_rows(tm, D)] * 5 + [_rows(tm, C)] * 2,
        out_shape=[_sds((T, D), BF16)] * 4 + [_sds((T, D), F32), _sds((T, C), F32), _sds((T, C), BF16)],
        args=(drb, wout, wcp, wrp, gc, gr, yc, yr, rg, h), comm=comm)


def _scan_bwd(dh, a, name):
    T, C = dh.shape
    tt = _tile(T, 512)
    nt = T // tt
    ng = tt // SUBLANES

    def body(d_ref, a_ref, g_ref, carry):
        @pl.when(pl.program_id(0) == 0)
        def _():
            carry[...] = jnp.zeros_like(carry)

        row = lax.broadcasted_iota(jnp.int32, (SUBLANES, C), 0)

        def group(j, enext):
            base = pl.multiple_of((ng - 1 - j) * SUBLANES, SUBLANES)
            av = a_ref[pl.ds(base, SUBLANES), :]
            dv = d_ref[pl.ds(base, SUBLANES), :]
            bv = av * dv
            for s in (1, 2, 4):
                keep = row < SUBLANES - s
                a_s = jnp.where(keep, pltpu.roll(av, SUBLANES - s, 0), 1.0)
                b_s = jnp.where(keep, pltpu.roll(bv, SUBLANES - s, 0), 0.0)
                bv = av * b_s + bv
                av = av * a_s
            e = av * enext + bv
            e_up = jnp.where(row < SUBLANES - 1, pltpu.roll(e, SUBLANES - 1, 0), enext)
            g_ref[pl.ds(base, SUBLANES), :] = dv + e_up
            return e[0:1, :]

        carry[...] = lax.fori_loop(0, ng, group, carry[...])

    rev = pl.BlockSpec((tt, C), lambda i: (nt - 1 - i, 0))
    (g,) = _pallas(
        body, name=name, grid=(nt,), in_specs=[rev, rev], out_specs=[rev],
        out_shape=[_sds((T, C), F32)],
        scratch_shapes=[pltpu.VMEM((1, C), F32)],
        args=(dh, a))
    return g


def _gates_bwd(g, hp, ra, ri, r, lam, bda, bdx, name, comm=None):
    T, C = g.shape
    tm = _tile(T, 512)
    nt = T // tm

    def body(g_ref, hp_ref, ra_ref, ri_ref, r_ref, lam_ref, wa_ref, wx_ref,
             dr_ref, dpa_ref, dpx_ref, dlam_ref, dba_ref, dbx_ref):
        @pl.when(pl.program_id(0) == 0)
        def _():
            dlam_ref[...] = jnp.zeros_like(dlam_ref)
            dba_ref[...] = jnp.zeros_like(dba_ref)
            dbx_ref[...] = jnp.zeros_like(dbx_ref)

        gv, rav, riv, rv, lamv = g_ref[...], ra_ref[...], ri_ref[...], r_ref[...], lam_ref[...]
        sp = _softplus_neg(lamv)
        log_a = (-RG_LRU_C) * rav * sp
        av = jnp.exp(log_a)
        mult = jnp.sqrt(-_expm1(2.0 * log_a))
        d_mult = gv * riv * rv
        d_i = gv * mult * rv
        d_loga = gv * hp_ref[...] * av - d_mult * (av * av) / mult
        d_ra = d_loga * ((-RG_LRU_C) * sp)
        dpa = d_ra * rav * (1.0 - rav)
        dpx = d_i * riv * (1.0 - riv)
        dpab = dpa.astype(BF16)
        dpxb = dpx.astype(BF16)
        dpa_ref[...] = dpab
        dpx_ref[...] = dpxb
        dr_ref[...] = gv * mult * riv + _dot_nt(dpab, wa_ref[...]) + _dot_nt(dpxb, wx_ref[...])
        dlam_ref[...] += jnp.sum(d_loga * ((-RG_LRU_C) * rav), axis=0, keepdims=True)
        dba_ref[...] += jnp.sum(dpa, axis=0, keepdims=True)
        dbx_ref[...] += jnp.sum(dpx, axis=0, keepdims=True)

        @pl.when(pl.program_id(0) == nt - 1)
        def _():
            dlam_ref[...] = dlam_ref[...] * (-_sigmoid(-lamv))

    return _pallas(
        body, name=name, grid=(nt,),
        in_specs=[_rows(tm, C)] * 5 + [_res((1, C)), _res(bda.shape), _res(bdx.shape)],
        out_specs=[_rows(tm, C)] * 3 + [_acc((1, C))] * 3,
        out_shape=[_sds((T, C), F32), _sds((T, C), BF16), _sds((T, C), BF16)] + [_sds((1, C), F32)] * 3,
        args=(g, hp, ra, ri, r, lam, bda, bdx), comm=comm)


def _conv4_bwd(dr, rx, w, name):
    T, C = dr.shape
    K = w.shape[0]
    R = CONV_ROWS
    P = RNN_PAD

    def body(d_ref, x_ref, w_ref, dx_ref, dw_ref, db_ref, dpad, xpad):
        dpad[0:T, :] = d_ref[...]
        dpad[T:T + P, :] = jnp.zeros((P, LANES), F32)
        xpad[0:P, :] = jnp.zeros((P, LANES), F32)
        xpad[P:P + T, :] = x_ref[...]
        dw_ref[...] = jnp.zeros_like(dw_ref)
        db_ref[...] = jnp.zeros_like(db_ref)
        wv = w_ref[...]

        def step(i, carry):
            base = pl.multiple_of(i * R, R)
            dwin = dpad[pl.ds(base, R + P), :]
            xwin = xpad[pl.ds(base, R + P), :]
            dcur = dwin[0:R, :]
            acc = jnp.zeros((R, LANES), F32)
            for k in range(K):
                acc = acc + wv[k:k + 1, :] * dwin[K - 1 - k:K - 1 - k + R, :]
                s = P - (K - 1) + k
                dw_ref[k:k + 1, :] += jnp.sum(dcur * xwin[s:s + R, :], axis=0, keepdims=True)
            dx_ref[pl.ds(base, R), :] = acc.astype(BF16)
            db_ref[...] += jnp.sum(dcur, axis=0, keepdims=True)
            return carry

        lax.fori_loop(0, T // R, step, 0)

    return _pallas(
        body, name=name, grid=(C // LANES,),
        in_specs=[_cols(T), _cols(T), _cols(T, K)],
        out_specs=[_cols(T), _cols(T, SUBLANES), _cols(T, 1)],
        out_shape=[_sds((T, C), BF16), _sds((SUBLANES, C), F32), _sds((1, C), F32)],
        scratch_shapes=[pltpu.VMEM((T + P, LANES), F32), pltpu.VMEM((T + P, LANES), F32)],
        args=(dr, rx, w))


def _conv31_bwd(dcs, cc, c, cv, cg, w, gg, gb, name, comm=None):
    T, D = dcs.shape
    K = w.shape[0]
    R, B, P = CONV_ROWS, CONV_BLOCK, CONV_PAD
    d_groups = _tap_groups([K - 1 - k for k in range(K)])
    x_groups = _tap_groups([P - (K - 1) + k for k in range(K)])

    def body(dcs_ref, cc_ref, c_ref, cv_ref, cg_ref, w_ref, gg_ref, gb_ref,
             dcv_ref, dcg_ref, dw_ref, db_ref, dgg_ref, dgb_ref, dpad, xpad, dwacc):
        dpad[T:T + P, :] = jnp.zeros((P, LANES), F32)
        xpad[0:P, :] = jnp.zeros((P, LANES), F32)
        xpad[P:P + T, :] = c_ref[...]
        dwacc[...] = jnp.zeros_like(dwacc)
        db_ref[...] = jnp.zeros_like(db_ref)
        dgg_ref[...] = jnp.zeros_like(dgg_ref)
        dgb_ref[...] = jnp.zeros_like(dgb_ref)
        wv, ggv, gbv = w_ref[...], gg_ref[...], gb_ref[...]

        def norm_step(i, carry):
            base = pl.multiple_of(i * R, R)
            xhat, rstd = _gn_stats(cc_ref[pl.ds(base, R), :])
            gn = xhat * ggv + gbv
            sg = _sigmoid(gn)
            dgn = dcs_ref[pl.ds(base, R), :] * (sg * (1.0 + gn * (1.0 - sg)))
            dgg_ref[...] += jnp.sum(dgn * xhat, axis=0, keepdims=True)
            dgb_ref[...] += jnp.sum(dgn, axis=0, keepdims=True)
            dxh = dgn * ggv
            m1 = jnp.mean(dxh, axis=-1, keepdims=True)
            m2 = jnp.mean(dxh * xhat, axis=-1, keepdims=True)
            dcc = rstd * (dxh - m1 - xhat * m2)
            dpad[pl.ds(base, R), :] = dcc
            db_ref[...] += jnp.sum(dcc, axis=0, keepdims=True)
            return carry

        lax.fori_loop(0, T // R, norm_step, 0)

        def conv_step(i, carry):
            base = pl.multiple_of(i * B, B)
            dwin = dpad[pl.ds(base, B + P), :]
            xwin = xpad[pl.ds(base, B + P), :]
            dcur = dwin[0:B, :]
            acc = jnp.zeros((B, LANES), F32)
            for phase, taps in d_groups:
                sh = _shifted(dwin, phase)
                for k, off in taps:
                    acc = acc + wv[k:k + 1, :] * sh[off:off + B, :]
            for phase, taps in x_groups:
                sh = _shifted(xwin, phase)
                for k, off in taps:
                    prod = dcur * sh[off:off + B, :]
                    part = prod[0:SUBLANES, :]
                    for q in range(1, B // SUBLANES):
                        part = part + prod[q * SUBLANES:(q + 1) * SUBLANES, :]
                    dwacc[k * SUBLANES:(k + 1) * SUBLANES, :] += part
            cgv = cg_ref[pl.ds(base, B), :].astype(F32)
            cvv = cv_ref[pl.ds(base, B), :].astype(F32)
            sg = _sigmoid(cgv)
            dcv_ref[pl.ds(base, B), :] = (acc * sg).astype(BF16)
            dcg_ref[pl.ds(base, B), :] = (acc * cvv * sg * (1.0 - sg)).astype(BF16)
            return carry

        lax.fori_loop(0, T // B, conv_step, 0)
        dw_ref[...] = jnp.zeros_like(dw_ref)
        for k in range(K):
            dw_ref[k:k + 1, :] = jnp.sum(dwacc[k * SUBLANES:(k + 1) * SUBLANES, :], axis=0, keepdims=True)

    return _pallas(
        body, name=name, grid=(D // LANES,),
        in_specs=[_cols(T)] * 5 + [_cols(T, K), _cols(T, 1), _cols(T, 1)],
        out_specs=[_cols(T), _cols(T), _cols(T, P), _cols(T, 1), _cols(T, 1), _cols(T, 1)],
        out_shape=[_sds((T, D), BF16), _sds((T, D), BF16), _sds((P, D), F32), _sds((1, D), F32),
                   _sds((1, D), F32), _sds((1, D), F32)],
        scratch_shapes=[pltpu.VMEM((T + P, LANES), F32), pltpu.VMEM((T + P, LANES), F32),
                        pltpu.VMEM((P * SUBLANES, LANES), F32)],
        args=(dcs, cc, c, cv, cg, w, gg, gb), comm=comm)


def _reduce_adamw(recvs, w, m, v, name):
    L, R, C = w.shape
    assert len(recvs) == L
    tr = next((c for c in (256, 176, 128, 64, 8) if R % c == 0), R)
    nr = R // tr
    c1 = 1.0 - ADAM_B1 ** ADAM_STEP
    c2 = 1.0 - ADAM_B2 ** ADAM_STEP

    def body(*refs):
        recv_refs = refs[:L]
        w_ref, m_ref, v_ref, g_ref, d_ref, mo_ref, vo_ref = refs[L:]

        def update(recv_ref):
            g = recv_ref[0].astype(F32)
            for k in range(1, N_DEV):
                g = g + recv_ref[k].astype(F32)
            mn = ADAM_B1 * m_ref[0] + (1.0 - ADAM_B1) * g
            vn = ADAM_B2 * v_ref[0] + (1.0 - ADAM_B2) * (g * g)
            g_ref[0] = g
            mo_ref[0] = mn
            vo_ref[0] = vn
            d_ref[0] = (-ADAM_LR) * ((mn / c1) / (jnp.sqrt(vn / c2) + ADAM_EPS) + ADAM_WD * w_ref[0])

        for l in range(L):
            pl.when(pl.program_id(0) == l)(lambda l=l: update(recv_refs[l]))

    def recv_spec(l):
        return pl.BlockSpec((N_DEV, tr, C), lambda j, i: (0, jnp.where(j == l, i, jnp.where(j < l, 0, nr - 1)), 0))

    blk = pl.BlockSpec((1, tr, C), lambda j, i: (j, i, 0))
    return _pallas(
        body, name=name, grid=(L, nr),
        in_specs=[recv_spec(l) for l in range(L)] + [blk, blk, blk],
        out_specs=[blk] * 4,
        out_shape=[_sds((L, R, C), F32)] * 4,
        args=(*recvs, w, m, v))


def _unshard(name, gathered):
    n, r, c = gathered.shape
    if name in COL_SHARDED:
        return gathered.transpose(1, 0, 2).reshape(r, n * c)
    return gathered.reshape(n * r, c)


def _to_shards(name, full):
    R, C = full.shape
    wire = BF16 if name in BF16_ON_WIRE else F32
    if name in COL_SHARDED:
        return full.reshape(R, N_DEV, C // N_DEV).transpose(1, 0, 2).astype(wire)
    return full.reshape(N_DEV, R // N_DEV, C).astype(wire)


def _block_diag(w):
    H, b, _ = w.shape
    rows = [jnp.pad(w[h], ((0, 0), (h * b, (H - 1 - h) * b))) for h in range(H)]
    return jnp.concatenate(rows, axis=0)


def _diag_blocks(dense, H):
    b = dense.shape[0] // H
    return jnp.stack([dense[h * b:(h + 1) * b, h * b:(h + 1) * b] for h in range(H)])


def _pack(arrs, rows):
    flat = jnp.concatenate([a.reshape(-1) for a in arrs])
    return jnp.pad(flat, (0, rows * 1024 - flat.shape[0])).reshape(1, rows, 1024)


def _unpack(packed, shapes):
    flat = packed.reshape(-1)
    out, off = [], 0
    for s in shapes:
        n = math.prod(s)
        out.append(flat[off:off + n].reshape(s))
        off += n
    return out


class _Queue:
    def __init__(self, kind, us_per_mb):
        self.kind, self.us_per_mb, self.items, self.done = kind, us_per_mb, [], {}

    def push(self, key, arr):
        self.items.append((key, arr))

    def take(self, micros):
        taken, budget = [], micros / self.us_per_mb
        while self.items and (not taken or _mbytes(self.items[0][1]) <= budget):
            budget -= _mbytes(self.items[0][1])
            taken.append(self.items.pop(0))
        return {"kind": self.kind, "keys": [k for k, _ in taken], "arrs": [a for _, a in taken]} if taken else None

    def landed(self, comm):
        if comm:
            self.done.update(zip(comm["keys"], comm["recv"]))

    def flush(self, name, upto=None):
        n = len(self.items)
        if upto is not None:
            keys = [k for k, _ in self.items]
            n = keys.index(upto) + 1 if upto in keys else 0
        if n:
            taken, self.items = self.items[:n], self.items[n:]
            self.done.update(zip([k for k, _ in taken], _exchange(self.kind, [a for _, a in taken], name)))


def kernel(x, ffn1_w_gu, ffn1_w_down, ln1_g, ln1_b, mix_w_in, mix_b_in, conv_dw_w, conv_dw_b, conv_gn_g, conv_gn_b, conv_w_proj, rnn_conv_w, rnn_conv_b, rnn_w_a, rnn_b_a, rnn_w_x, rnn_b_x, rnn_lambda, rnn_w_proj, mix_w_out, ln2_g, ln2_b, ffn2_w_gu, ffn2_w_down, ln3_g, ln3_b, loss_target, m_ffn1_w_gu, m_ffn1_w_down, m_ln1_g, m_ln1_b, m_mix_w_in, m_mix_b_in, m_conv_dw_w, m_conv_dw_b, m_conv_gn_g, m_conv_gn_b, m_conv_w_proj, m_rnn_conv_w, m_rnn_conv_b, m_rnn_w_a, m_rnn_b_a, m_rnn_w_x, m_rnn_b_x, m_rnn_lambda, m_rnn_w_proj, m_mix_w_out, m_ln2_g, m_ln2_b, m_ffn2_w_gu, m_ffn2_w_down, m_ln3_g, m_ln3_b, v_ffn1_w_gu, v_ffn1_w_down, v_ln1_g, v_ln1_b, v_mix_w_in, v_mix_b_in, v_conv_dw_w, v_conv_dw_b, v_conv_gn_g, v_conv_gn_b, v_conv_w_proj, v_rnn_conv_w, v_rnn_conv_b, v_rnn_w_a, v_rnn_b_a, v_rnn_w_x, v_rnn_b_x, v_rnn_lambda, v_rnn_w_proj, v_mix_w_out, v_ln2_g, v_ln2_b, v_ffn2_w_gu, v_ffn2_w_down, v_ln3_g, v_ln3_b):
    given = dict(locals())
    W = {n: given[n] for n in WEIGHTS}
    M = {n: given["m_" + n] for n in WEIGHTS}
    V = {n: given["v_" + n] for n in WEIGHTS}
    T, D = x.shape[1], x.shape[2]
    L = DEPTH
    x2 = x.reshape(T, D)
    target = loss_target.reshape(T, D)
    d_rnn = rnn_conv_b.shape[1]

    gather = _Queue("gather", GATHER_US_PER_MB)
    for l in range(L):
        for n in USE_ORDER:
            gather.push((n, l), W[n][l].astype(BF16) if n in BF16_ON_WIRE else W[n][l])
    gather.flush("gather_first", upto=("ffn1_w_down", 0))
    full_cache = {}

    def full(n, l):
        if (n, l) not in full_cache:
            gather.flush(f"gather_{n}_{l}", upto=(n, l))
            full_cache[(n, l)] = _unshard(n, gather.done[(n, l)])
        return full_cache[(n, l)]

    def fwd_comm(micros):
        return gather.take(micros)

    bd_a = [_block_diag(rnn_w_a[l]).astype(BF16) for l in range(L)]
    bd_x = [_block_diag(rnn_w_x[l]).astype(BF16) for l in range(L)]

    def vec(name, l):
        return W[name][l:l + 1]

    saved = []
    h, hb = x2, x2.astype(BF16)
    for l in range(L):
        s = {"hb_in": hb}
        w_gu, w_down = full("ffn1_w_gu", l), full("ffn1_w_down", l)
        comm = fwd_comm(105)
        s["gu1"], s["a1"] = _ffn_up(hb, w_gu, f"ffn1_up_{l}", comm=comm)
        gather.landed(comm)
        comm = fwd_comm(65)
        s["r1"], y1, s["y1b"] = _ffn_down_ln(s["a1"], w_down, h, vec("ln1_g", l), vec("ln1_b", l),
                                              f"ffn1_down_ln_{l}", comm=comm)
        gather.landed(comm)
        w_in = full("mix_w_in", l)
        comm = fwd_comm(135)
        s["c"], s["cv"], s["cg"], s["rx"], s["rg"], s["gc"], s["gr"] = _mix_in(
            s["y1b"], w_in, vec("mix_b_in", l), d_rnn, f"mix_in_{l}", comm=comm)
        gather.landed(comm)
        w_dw = full("conv_dw_w", l)
        comm = fwd_comm(150)
        s["cc"], s["cs"] = _conv31_gn(s["c"], w_dw, vec("conv_dw_b", l), vec("conv_gn_g", l),
                                      vec("conv_gn_b", l), f"conv31_gn_{l}", comm=comm)
        gather.landed(comm)
        s["r"] = _conv4(s["rx"], full("rnn_conv_w", l), vec("rnn_conv_b", l), f"conv4_{l}")
        comm = fwd_comm(115)
        s["ra"], s["ri"], s["a"], uu = _gates(s["r"], bd_a[l], bd_x[l], vec("rnn_b_a", l), vec("rnn_b_x", l),
                                              vec("rnn_lambda", l), f"gates_{l}", comm=comm)
        gather.landed(comm)
        s["h"], s["hp"], s["hg"] = _scan_fwd(s["a"], uu, s["rg"], f"scan_{l}")
        w_cp, w_rp, w_out = full("conv_w_proj", l), full("rnn_w_proj", l), full("mix_w_out", l)
        comm = fwd_comm(85)
        s["yc"], s["yr"], s["m"], s["r2"], y2, s["y2b"] = _mix_out_ln(
            s["cs"], s["hg"], s["gc"], s["gr"], y1, w_cp, w_rp, w_out, vec("ln2_g", l), vec("ln2_b", l),
            f"mix_out_ln_{l}", comm=comm)
        gather.landed(comm)
        w_gu2, w_down2 = full("ffn2_w_gu", l), full("ffn2_w_down", l)
        comm = fwd_comm(105)
        s["gu2"], s["a2"] = _ffn_up(s["y2b"], w_gu2, f"ffn2_up_{l}", comm=comm)
        gather.landed(comm)
        comm = fwd_comm(65)
        s["r3"], h, hb = _ffn_down_ln(s["a2"], w_down2, y2, vec("ln3_g", l), vec("ln3_b", l),
                                      f"ffn2_down_ln_{l}", comm=comm)
        gather.landed(comm)
        saved.append(s)

    scatter = _Queue("scatter", SCATTER_US_PER_MB)
    G = {n: [None] * L for n in WEIGHTS}

    def ready(n, l, grad):
        G[n][l] = grad
        scatter.push((n, l), _to_shards(n, grad))

    def bwd_comm(micros):
        return scatter.take(micros)

    loss_acc, dr3, drb3, G["ln3_g"][L - 1], G["ln3_b"][L - 1] = _loss_ln_bwd(
        h, target, saved[L - 1]["r3"], vec("ln3_g", L - 1), "loss_ln3_bwd")
    grad_x = small = None
    small_shapes = [W[n].shape for n in REPLICATED]
    small_rows = -(-sum(math.prod(s) for s in small_shapes) // (1024 * SUBLANES)) * SUBLANES
    for l in reversed(range(L)):
        s = saved[l]
        comm = bwd_comm(95)
        dgu2 = _ffn_bwd_a(drb3, full("ffn2_w_down", l), s["gu2"], f"ffn2_bwd_a_{l}", comm=comm)
        scatter.landed(comm)
        ready("ffn2_w_down", l, _mm_tn(s["a2"], drb3, f"ffn2_dw_down_{l}", scale=0.5))
        ready("ffn2_w_gu", l, _mm_tn(s["y2b"], dgu2, f"ffn2_dw_gu_{l}"))
        comm = bwd_comm(140)
        dr2, drb2, G["ln2_g"][l], G["ln2_b"][l] = _nt_res(
            dgu2, full("ffn2_w_gu", l), dr3, f"ffn2_bwd_x_{l}", ln=(s["r2"], vec("ln2_g", l)), comm=comm)
        scatter.landed(comm)
        comm = bwd_comm(120)
        dyc, dyr, dgc, dgr, dcs, dh, drg = _mix_bwd1(
            drb2, full("mix_w_out", l), full("conv_w_proj", l), full("rnn_w_proj", l), s["gc"], s["gr"], s["yc"],
            s["yr"], s["rg"], s["h"], f"mix_bwd_out_{l}", comm=comm)
        scatter.landed(comm)
        ready("mix_w_out", l, _mm_tn(s["m"], drb2, f"mix_dw_out_{l}"))
        ready("conv_w_proj", l, _mm_tn(s["cs"], dyc, f"conv_dw_proj_{l}"))
        ready("rnn_w_proj", l, _mm_tn(s["hg"], dyr, f"rnn_dw_proj_{l}"))
        gsc = _scan_bwd(dh, s["a"], f"scan_bwd_{l}")
        comm = bwd_comm(160)
        dr_, dpa, dpx, G["rnn_lambda"][l], G["rnn_b_a"][l], G["rnn_b_x"][l] = _gates_bwd(
            gsc, s["hp"], s["ra"], s["ri"], s["r"], vec("rnn_lambda", l), bd_a[l], bd_x[l], f"gates_bwd_{l}",
            comm=comm)
        scatter.landed(comm)
        G["rnn_w_a"][l] = _diag_blocks(_mm_tn(s["r"], dpa, f"rnn_dw_a_{l}"), RNN_BLOCKS)
        G["rnn_w_x"][l] = _diag_blocks(_mm_tn(s["r"], dpx, f"rnn_dw_x_{l}"), RNN_BLOCKS)
        drx, dw4, G["rnn_conv_b"][l] = _conv4_bwd(dr_, s["rx"], full("rnn_conv_w", l), f"conv4_bwd_{l}")
        ready("rnn_conv_w", l, dw4[:RNN_CONV_WIDTH])
        comm = bwd_comm(250)
        dcv, dcg, dw31, G["conv_dw_b"][l], G["conv_gn_g"][l], G["conv_gn_b"][l] = _conv31_bwd(
            dcs, s["cc"], s["c"], s["cv"], s["cg"], full("conv_dw_w", l), vec("conv_gn_g", l),
            vec("conv_gn_b", l), f"conv31_bwd_{l}", comm=comm)
        scatter.landed(comm)
        ready("conv_dw_w", l, dw31[:CONV_WIDTH])
        du = jnp.concatenate([dcv, dcg, drx, drg, dgc, dgr], axis=1)
        ready("mix_w_in", l, _mm_tn(s["y1b"], du, f"mix_dw_in_{l}"))
        comm = bwd_comm(175)
        dr1, drb1, G["ln1_g"][l], G["ln1_b"][l], G["mix_b_in"][l] = _nt_res(
            du, full("mix_w_in", l), dr2, f"mix_bwd_in_{l}", ln=(s["r1"], vec("ln1_g", l)), colsum=True, comm=comm)
        scatter.landed(comm)
        if l == 0:
            local_small = [jnp.stack([g.reshape(W[n].shape[1:]) for g in G[n]]) for n in REPLICATED]
            comm = {"kind": "gather", "arrs": [_pack(local_small, small_rows)[0]]}
            dgu1 = _ffn_bwd_a(drb1, full("ffn1_w_down", l), s["gu1"], f"ffn1_bwd_a_{l}", comm=comm)
            (small,) = comm["recv"]
        else:
            comm = bwd_comm(95)
            dgu1 = _ffn_bwd_a(drb1, full("ffn1_w_down", l), s["gu1"], f"ffn1_bwd_a_{l}", comm=comm)
            scatter.landed(comm)
        ready("ffn1_w_down", l, _mm_tn(s["a1"], drb1, f"ffn1_dw_down_{l}", scale=0.5))
        ready("ffn1_w_gu", l, _mm_tn(s["hb_in"], dgu1, f"ffn1_dw_gu_{l}"))
        if l > 0:
            comm = bwd_comm(130)
            dr3, drb3, G["ln3_g"][l - 1], G["ln3_b"][l - 1] = _nt_res(
                dgu1, full("ffn1_w_gu", l), dr1, f"ffn1_bwd_x_{l}", ln=(saved[l - 1]["r3"], vec("ln3_g", l - 1)),
                comm=comm)
        else:
            comm = bwd_comm(1e9)
            (grad_x,) = _nt_res(dgu1, full("ffn1_w_gu", l), dr1, f"ffn1_bwd_x_{l}", comm=comm)
        scatter.landed(comm)
    scatter.flush("scatter_rest")

    loss = lax.psum(loss_acc[0, 0], ("x", "y", "c"))

    out = {}
    for n in SHARDED:
        out[n] = _reduce_adamw([scatter.done[(n, l)] for l in range(L)], W[n], M[n], V[n], f"adamw_{n}")
    packed = _reduce_adamw([small], _pack([W[n] for n in REPLICATED], small_rows),
                           _pack([M[n] for n in REPLICATED], small_rows),
                           _pack([V[n] for n in REPLICATED], small_rows), "adamw_small")
    unpacked = [_unpack(p, small_shapes) for p in packed]
    for i, n in enumerate(REPLICATED):
        out[n] = tuple(u[i] for u in unpacked)

    return (loss, grad_x.reshape(x.shape), *[out[n][0] for n in WEIGHTS], *[out[n][1] for n in WEIGHTS],
            *[out[n][2] for n in WEIGHTS], *[out[n][3] for n in WEIGHTS])
```

```python
import math

import jax
import jax.numpy as jnp
from jax import lax
from jax.experimental import pallas as pl
from jax.experimental.pallas import tpu as pltpu

F32 = jnp.float32
BF16 = jnp.bfloat16
MESH = pl.DeviceIdType.MESH

DEPTH = 2
ALPHA = (2 * DEPTH) ** 0.25
LN_EPS = 1e-5
RG_LRU_C = 8.0
CONV_WIDTH = 31
RNN_CONV_WIDTH = 4
RNN_BLOCKS = 16
N_DEV = 8
ADAM_LR, ADAM_B1, ADAM_B2, ADAM_EPS, ADAM_WD, ADAM_STEP = 0.001, 0.9, 0.999, 1e-08, 0.01, 10

LANES = 128
SUBLANES = 8
VMEM_LIMIT = 56 * 1024 * 1024
CONV_PAD = 32
RNN_PAD = 8
CONV_ROWS = 128
CONV_BLOCK = 32
GATHER_US_PER_MB = 43.0
SCATTER_US_PER_MB = 86.0

WEIGHTS = ['ffn1_w_gu', 'ffn1_w_down', 'ln1_g', 'ln1_b', 'mix_w_in', 'mix_b_in', 'conv_dw_w', 'conv_dw_b',
           'conv_gn_g', 'conv_gn_b', 'conv_w_proj', 'rnn_conv_w', 'rnn_conv_b', 'rnn_w_a', 'rnn_b_a', 'rnn_w_x',
           'rnn_b_x', 'rnn_lambda', 'rnn_w_proj', 'mix_w_out', 'ln2_g', 'ln2_b', 'ffn2_w_gu', 'ffn2_w_down',
           'ln3_g', 'ln3_b']
COL_SHARDED = ['ffn1_w_gu', 'mix_w_in', 'ffn2_w_gu', 'conv_dw_w', 'rnn_conv_w']
ROW_SHARDED = ['ffn1_w_down', 'conv_w_proj', 'rnn_w_proj', 'mix_w_out', 'ffn2_w_down']
SHARDED = COL_SHARDED + ROW_SHARDED
BF16_ON_WIRE = ['ffn1_w_gu', 'mix_w_in', 'ffn2_w_gu', 'ffn1_w_down', 'conv_w_proj', 'rnn_w_proj', 'mix_w_out',
                'ffn2_w_down']
REPLICATED = [n for n in WEIGHTS if n not in SHARDED]
USE_ORDER = ['ffn1_w_gu', 'ffn1_w_down', 'mix_w_in', 'conv_dw_w', 'rnn_conv_w', 'conv_w_proj', 'rnn_w_proj',
             'mix_w_out', 'ffn2_w_gu', 'ffn2_w_down']


def _cp(n_axes=1):
    return pltpu.CompilerParams(dimension_semantics=("arbitrary",) * n_axes, vmem_limit_bytes=VMEM_LIMIT)


def _rows(tm, c):
    return pl.BlockSpec((tm, c), lambda i: (i, 0))


def _res(shape):
    nd = len(shape)
    return pl.BlockSpec(tuple(shape), lambda *_: (0,) * nd, pipeline_mode=pl.Buffered(1))


def _acc(shape):
    nd = len(shape)
    return pl.BlockSpec(tuple(shape), lambda *_: (0,) * nd)


def _tile(t, want):
    return want if t % want == 0 else t


def _sds(shape, dtype):
    return jax.ShapeDtypeStruct(tuple(shape), dtype)


def _mbytes(a):
    return a.size * a.dtype.itemsize / 1e6


def _ln_fwd(r, g, b):
    mu = jnp.mean(r, axis=-1, keepdims=True)
    xc = r - mu
    var = jnp.mean(xc * xc, axis=-1, keepdims=True)
    return xc * lax.rsqrt(var + LN_EPS) * g + b


def _ln_bwd(dy, r, g):
    mu = jnp.mean(r, axis=-1, keepdims=True)
    xc = r - mu
    var = jnp.mean(xc * xc, axis=-1, keepdims=True)
    rstd = lax.rsqrt(var + LN_EPS)
    xhat = xc * rstd
    dxh = dy * g
    m1 = jnp.mean(dxh, axis=-1, keepdims=True)
    m2 = jnp.mean(dxh * xhat, axis=-1, keepdims=True)
    dr = rstd * (dxh - m1 - xhat * m2)
    return dr, jnp.sum(dy * xhat, axis=0, keepdims=True), jnp.sum(dy, axis=0, keepdims=True)


def _sigmoid(x):
    return jax.nn.sigmoid(x)


_GELU_K = math.sqrt(2.0 / math.pi)


def _gelu(x):
    t = jnp.tanh(_GELU_K * (x + 0.044715 * x * x * x))
    return 0.5 * x * (1.0 + t), t


def _gelu_grad(x, t):
    return 0.5 * (1.0 + t) + 0.5 * x * (1.0 - t * t) * (_GELU_K * (1.0 + 3.0 * 0.044715 * x * x))


def _expm1(x):
    taylor = x * (1.0 + x * (0.5 + x * (1.0 / 6.0 + x * (1.0 / 24.0 + x * (1.0 / 120.0)))))
    return jnp.where(jnp.abs(x) < 0.03, taylor, jnp.exp(x) - 1.0)


def _softplus_neg(lam):
    return jnp.maximum(-lam, 0.0) + jnp.log1p(jnp.exp(-jnp.abs(lam)))


def _dot(a, b):
    return jnp.dot(a, b, preferred_element_type=F32)


def _dot_nt(a, b):
    return lax.dot_general(a, b, (((1,), (1,)), ((), ())), preferred_element_type=F32)


def _dot_tn(a, b):
    return lax.dot_general(a, b, (((0,), (0,)), ((), ())), preferred_element_type=F32)


def _chunks(width, cn):
    return [(j, min(cn, width - j)) for j in range(0, width, cn)]


def _position():
    return lax.axis_index("x"), lax.axis_index("y"), lax.axis_index("c")


def _index(p):
    return 4 * p[0] + 2 * p[1] + p[2]


def _comm_out_shapes(kind, arrs):
    return [_sds((N_DEV,) + a.shape if kind == "gather" else a.shape, a.dtype) for a in arrs]


def _comm_scratch(n):
    return [pltpu.SemaphoreType.DMA((n, 7)), pltpu.SemaphoreType.DMA((n, 7)), pltpu.SemaphoreType.DMA((n,))]


def _comm_phases(kind, srcs, dsts, send_sems, recv_sems, local_sems):
    n = len(srcs)
    x, y, c = _position()
    me, sibling = (x, y, c), (x, y, 1 - c)
    chips = [(1 - x, y), (x, 1 - y), (1 - x, 1 - y)]

    if kind == "gather":
        def copy(a, k, block, to, src=None):
            dst = dsts[a].at[_index(block)]
            return pltpu.make_async_remote_copy(
                src_ref=dst if src is None else src, dst_ref=dst, send_sem=send_sems.at[a, k],
                recv_sem=recv_sems.at[a, k], device_id=to, device_id_type=MESH)

        def mine(a):
            return pltpu.make_async_copy(srcs[a], dsts[a].at[_index(me)], local_sems.at[a])

        def first(a):
            return [copy(a, 0, me, sibling, src=srcs[a])] + [
                copy(a, 1 + j, me, (*chip, c), src=srcs[a]) for j, chip in enumerate(chips)]

        def start():
            for a in range(n):
                mine(a).start()
            for a in range(n):
                for cp in first(a):
                    cp.start()

        def mid():
            for j, chip in enumerate(chips):
                for a in range(n):
                    copy(a, 1 + j, (*chip, c), me).wait_recv()
                    copy(a, 4 + j, (*chip, c), sibling).start()

        def end():
            for a in range(n):
                copy(a, 0, sibling, me).wait_recv()
            for j, chip in enumerate(chips):
                for a in range(n):
                    copy(a, 4 + j, (*chip, 1 - c), me).wait_recv()
            for a in range(n):
                for cp in first(a):
                    cp.wait_send()
                for j, chip in enumerate(chips):
                    copy(a, 4 + j, (*chip, c), sibling).wait_send()
                mine(a).wait()

        return start, mid, end

    def peer_of(k):
        return (1 - x if k & 4 else x, 1 - y if k & 2 else y, 1 - c if k & 1 else c)

    def own(a):
        return pltpu.make_async_copy(srcs[a].at[_index(me)], dsts[a].at[0], local_sems.at[a])

    def remote(a, k):
        peer = peer_of(k)
        return pltpu.make_async_remote_copy(
            src_ref=srcs[a].at[_index(peer)], dst_ref=dsts[a].at[k], send_sem=send_sems.at[a, k - 1],
            recv_sem=recv_sems.at[a, k - 1], device_id=peer, device_id_type=MESH)

    def start():
        for a in range(n):
            own(a).start()
        for k in range(1, N_DEV):
            for a in range(n):
                remote(a, k).start()

    def end():
        for k in range(1, N_DEV):
            for a in range(n):
                remote(a, k).wait()
        for a in range(n):
            own(a).wait()

    return start, (lambda: None), end


def _exchange(kind, arrs, name):
    n = len(arrs)
    hbm = pl.BlockSpec(memory_space=pl.ANY)

    def body(*refs):
        start, mid, end = _comm_phases(kind, refs[:n], refs[n:2 * n], *refs[2 * n:])
        start()
        mid()
        end()

    return pl.pallas_call(
        body, name=name, in_specs=[hbm] * n, out_specs=[hbm] * n, out_shape=_comm_out_shapes(kind, arrs),
        scratch_shapes=_comm_scratch(n))(*arrs)


def _pallas(body, *, name, grid, in_specs, out_specs, out_shape, args, scratch_shapes=(), comm=None):
    in_specs, out_specs, out_shape = list(in_specs), list(out_specs), list(out_shape)
    scratch_shapes = list(scratch_shapes)
    if not comm:
        return pl.pallas_call(body, name=name, grid=grid, in_specs=in_specs, out_specs=out_specs,
                              out_shape=out_shape, scratch_shapes=scratch_shapes,
                              compiler_params=_cp(len(grid)))(*args)
    arrs = comm["arrs"]
    ns, n_in, n_out, n_scr = len(arrs), len(in_specs), len(out_specs), len(scratch_shapes)
    hbm = pl.BlockSpec(memory_space=pl.ANY)
    total = math.prod(grid)

    def carrier(*refs):
        ins, srcs = refs[:n_in], refs[n_in:n_in + ns]
        outs, dsts = refs[n_in + ns:n_in + ns + n_out], refs[n_in + ns + n_out:n_in + 2 * ns + n_out]
        scr, sems = refs[n_in + 2 * ns + n_out:n_in + 2 * ns + n_out + n_scr], refs[n_in + 2 * ns + n_out + n_scr:]
        step = pl.program_id(0)
        for ax in range(1, len(grid)):
            step = step * grid[ax] + pl.program_id(ax)
        start, mid, end = _comm_phases(comm["kind"], srcs, dsts, *sems)
        pl.when(step == 0)(start)
        body(*ins, *outs, *scr)
        pl.when(step == total // 2)(mid)
        pl.when(step == total - 1)(end)

    res = pl.pallas_call(
        carrier, name=name, grid=grid, in_specs=in_specs + [hbm] * ns, out_specs=out_specs + [hbm] * ns,
        out_shape=out_shape + _comm_out_shapes(comm["kind"], arrs),
        scratch_shapes=scratch_shapes + _comm_scratch(ns), compiler_params=_cp(len(grid)))(*args, *arrs)
    comm["recv"] = res[n_out:]
    return res[:n_out]


def _ffn_up(xb, w, name, comm=None):
    T, D = xb.shape
    F = w.shape[1] // 2
    tm = _tile(T, 512)

    def body(x_ref, w_ref, gu_ref, a_ref):
        x = x_ref[...]
        for j, cw in _chunks(F, 256):
            g = _dot(x, w_ref[:, j:j + cw])
            u = _dot(x, w_ref[:, F + j:F + j + cw])
            gu_ref[:, j:j + cw] = g.astype(BF16)
            gu_ref[:, F + j:F + j + cw] = u.astype(BF16)
            a_ref[:, j:j + cw] = (g * _sigmoid(g) * u).astype(BF16)

    return _pallas(
        body, name=name, grid=(T // tm,),
        in_specs=[_rows(tm, D), _res(w.shape)],
        out_specs=[_rows(tm, 2 * F), _rows(tm, F)],
        out_shape=[_sds((T, 2 * F), BF16), _sds((T, F), BF16)],
        args=(xb, w), comm=comm)


def _ffn_down_ln(a, wd, xres, g, b, name, comm=None):
    T, F = a.shape
    D = wd.shape[1]
    tm = _tile(T, 512)

    def body(a_ref, wd_ref, x_ref, g_ref, b_ref, r_ref, y_ref, yb_ref):
        r = ALPHA * x_ref[...] + 0.5 * _dot(a_ref[...], wd_ref[...])
        y = _ln_fwd(r, g_ref[...], b_ref[...])
        r_ref[...] = r
        y_ref[...] = y
        yb_ref[...] = y.astype(BF16)

    return _pallas(
        body, name=name, grid=(T // tm,),
        in_specs=[_rows(tm, F), _res(wd.shape), _rows(tm, D), _res((1, D)), _res((1, D))],
        out_specs=[_rows(tm, D), _rows(tm, D), _rows(tm, D)],
        out_shape=[_sds((T, D), F32), _sds((T, D), F32), _sds((T, D), BF16)],
        args=(a, wd, xres, g, b), comm=comm)


def _mix_in(hb, w, bias, d_rnn, name, comm=None):
    T, D = hb.shape
    R = d_rnn
    tm = _tile(T, 256)
    o_cv, o_cg, o_rx, o_rg, o_gc, o_gr = 0, D, 2 * D, 2 * D + R, 2 * D + 2 * R, 3 * D + 2 * R

    def body(x_ref, w_ref, b_ref, c_ref, cv_ref, cg_ref, rx_ref, rg_ref, gc_ref, gr_ref):
        x = x_ref[...]

        def seg(off, j, cw):
            return _dot(x, w_ref[:, off + j:off + j + cw]) + b_ref[:, off + j:off + j + cw]

        for j, cw in _chunks(D, 256):
            cv = seg(o_cv, j, cw)
            cg = seg(o_cg, j, cw)
            cv_ref[:, j:j + cw] = cv.astype(BF16)
            cg_ref[:, j:j + cw] = cg.astype(BF16)
            c_ref[:, j:j + cw] = cv * _sigmoid(cg)
            gc_ref[:, j:j + cw] = seg(o_gc, j, cw).astype(BF16)
            gr_ref[:, j:j + cw] = seg(o_gr, j, cw).astype(BF16)
        for j, cw in _chunks(R, 256):
            rx_ref[:, j:j + cw] = seg(o_rx, j, cw)
            rg_ref[:, j:j + cw] = seg(o_rg, j, cw).astype(BF16)

    return _pallas(
        body, name=name, grid=(T // tm,),
        in_specs=[_rows(tm, D), _res(w.shape), _res(bias.shape)],
        out_specs=[_rows(tm, D), _rows(tm, D), _rows(tm, D), _rows(tm, R), _rows(tm, R), _rows(tm, D),
                   _rows(tm, D)],
        out_shape=[_sds((T, D), F32), _sds((T, D), BF16), _sds((T, D), BF16), _sds((T, R), F32),
                   _sds((T, R), BF16), _sds((T, D), BF16), _sds((T, D), BF16)],
        args=(hb, w, bias), comm=comm)


def _cols(t, rows=None):
    return pl.BlockSpec((t if rows is None else rows, LANES), lambda j: (0, j))


def _gn_stats(cc):
    mu = jnp.mean(cc, axis=-1, keepdims=True)
    xc = cc - mu
    var = jnp.mean(xc * xc, axis=-1, keepdims=True)
    rstd = lax.rsqrt(var + LN_EPS)
    return xc * rstd, rstd


def _tap_groups(offsets):
    groups = {}
    for k, s in enumerate(offsets):
        groups.setdefault(s % SUBLANES, []).append((k, s - s % SUBLANES))
    return sorted(groups.items())


def _shifted(win, phase):
    return win if phase == 0 else pltpu.roll(win, win.shape[0] - phase, 0)


def _tap_sum(win, wv, groups, rows):
    parts, t = [None] * 4, 0
    for phase, taps in groups:
        sh = _shifted(win, phase)
        for k, off in taps:
            term = wv[k:k + 1, :] * sh[off:off + rows, :]
            parts[t % 4] = term if parts[t % 4] is None else parts[t % 4] + term
            t += 1
    return (parts[0] + parts[1]) + (parts[2] + parts[3])


def _conv31_gn(c, w, bias, gg, gb, name, comm=None):
    T, D = c.shape
    K = w.shape[0]
    B, P = CONV_BLOCK, CONV_PAD
    assert D % LANES == 0 and T % B == 0 and K - 1 <= P
    groups = _tap_groups([P - (K - 1) + k for k in range(K)])

    def body(c_ref, w_ref, b_ref, gg_ref, gb_ref, cc_ref, cs_ref, xpad):
        xpad[0:P, :] = jnp.zeros((P, LANES), F32)
        xpad[P:P + T, :] = c_ref[...]
        wv = w_ref[...]
        bv, ggv, gbv = b_ref[...], gg_ref[...], gb_ref[...]

        def step(i, carry):
            base = pl.multiple_of(i * B, B)
            win = xpad[pl.ds(base, B + P), :]
            cc = _tap_sum(win, wv, groups, B) + bv
            cc_ref[pl.ds(base, B), :] = cc
            xhat, _ = _gn_stats(cc)
            gn = xhat * ggv + gbv
            cs_ref[pl.ds(base, B), :] = (gn * _sigmoid(gn)).astype(BF16)
            return carry

        lax.fori_loop(0, T // B, step, 0)

    return _pallas(
        body, name=name, grid=(D // LANES,),
        in_specs=[_cols(T), _cols(T, K), _cols(T, 1), _cols(T, 1), _cols(T, 1)],
        out_specs=[_cols(T), _cols(T)],
        out_shape=[_sds((T, D), F32), _sds((T, D), BF16)],
        scratch_shapes=[pltpu.VMEM((T + P, LANES), F32)],
        args=(c, w, bias, gg, gb), comm=comm)


def _conv4(rx, w, bias, name):
    T, C = rx.shape
    K = w.shape[0]
    R = CONV_ROWS
    assert C % LANES == 0 and T % R == 0 and K - 1 <= RNN_PAD

    def body(x_ref, w_ref, b_ref, r_ref, xpad):
        xpad[0:RNN_PAD, :] = jnp.zeros((RNN_PAD, LANES), F32)
        xpad[RNN_PAD:RNN_PAD + T, :] = x_ref[...]
        wv, bv = w_ref[...], b_ref[...]

        def step(i, carry):
            base = pl.multiple_of(i * R, R)
            win = xpad[pl.ds(base, R + RNN_PAD), :]
            acc = jnp.zeros((R, LANES), F32)
            for k in range(K):
                s = RNN_PAD - (K - 1) + k
                acc = acc + wv[k:k + 1, :] * win[s:s + R, :]
            r_ref[pl.ds(base, R), :] = acc + bv
            return carry

        lax.fori_loop(0, T // R, step, 0)

    (r,) = _pallas(
        body, name=name, grid=(C // LANES,),
        in_specs=[_cols(T), _cols(T, K), _cols(T, 1)],
        out_specs=[_cols(T)],
        out_shape=[_sds((T, C), F32)],
        scratch_shapes=[pltpu.VMEM((T + RNN_PAD, LANES), F32)],
        args=(rx, w, bias))
    return r


def _gates(r, bda, bdx, b_a, b_x, lam, name, comm=None):
    T, C = r.shape
    tm = _tile(T, 512)

    def body(r_ref, wa_ref, wx_ref, ba_ref, bx_ref, lam_ref, ra_ref, ri_ref, a_ref, u_ref):
        rv = r_ref[...]
        rb = rv.astype(BF16)
        ra = _sigmoid(_dot(rb, wa_ref[...]) + ba_ref[...])
        ri = _sigmoid(_dot(rb, wx_ref[...]) + bx_ref[...])
        log_a = (-RG_LRU_C) * ra * _softplus_neg(lam_ref[...])
        ra_ref[...] = ra
        ri_ref[...] = ri
        a_ref[...] = jnp.exp(log_a)
        u_ref[...] = jnp.sqrt(-_expm1(2.0 * log_a)) * (ri * rv)

    return _pallas(
        body, name=name, grid=(T // tm,),
        in_specs=[_rows(tm, C), _res(bda.shape), _res(bdx.shape), _res((1, C)), _res((1, C)), _res((1, C))],
        out_specs=[_rows(tm, C)] * 4,
        out_shape=[_sds((T, C), F32)] * 4,
        args=(r, bda, bdx, b_a, b_x, lam), comm=comm)


def _scan_fwd(a, u, rg, name):
    T, C = a.shape
    tt = _tile(T, 512)

    def body(a_ref, u_ref, rg_ref, h_ref, hp_ref, hg_ref, carry):
        @pl.when(pl.program_id(0) == 0)
        def _():
            carry[...] = jnp.zeros_like(carry)

        row = lax.broadcasted_iota(jnp.int32, (SUBLANES, C), 0)

        def group(i, hprev):
            base = pl.multiple_of(i * SUBLANES, SUBLANES)
            av = a_ref[pl.ds(base, SUBLANES), :]
            uv = u_ref[pl.ds(base, SUBLANES), :]
            for s in (1, 2, 4):
                a_s = jnp.where(row >= s, pltpu.roll(av, s, 0), 1.0)
                u_s = jnp.where(row >= s, pltpu.roll(uv, s, 0), 0.0)
                uv = av * u_s + uv
                av = av * a_s
            h = av * hprev + uv
            h_ref[pl.ds(base, SUBLANES), :] = h
            hp_ref[pl.ds(base, SUBLANES), :] = jnp.where(row >= 1, pltpu.roll(h, 1, 0), hprev)
            return h[SUBLANES - 1:SUBLANES, :]

        carry[...] = lax.fori_loop(0, tt // SUBLANES, group, carry[...])
        gel, _ = _gelu(rg_ref[...].astype(F32))
        hg_ref[...] = (h_ref[...] * gel).astype(BF16)

    return _pallas(
        body, name=name, grid=(T // tt,),
        in_specs=[_rows(tt, C)] * 3,
        out_specs=[_rows(tt, C)] * 3,
        out_shape=[_sds((T, C), F32), _sds((T, C), F32), _sds((T, C), BF16)],
        scratch_shapes=[pltpu.VMEM((1, C), F32)],
        args=(a, u, rg))


def _mix_out_ln(cs, hg, gc, gr, hres, wcp, wrp, wout, g, b, name, comm=None):
    T, D = cs.shape
    C = hg.shape[1]
    tm = _tile(T, 512)

    def body(cs_ref, hg_ref, gc_ref, gr_ref, h_ref, wcp_ref, wrp_ref, wo_ref, g_ref, b_ref,
             yc_ref, yr_ref, m_ref, r_ref, y_ref, yb_ref):
        yc = _dot(cs_ref[...], wcp_ref[...])
        yr = _dot(hg_ref[...], wrp_ref[...])
        m = (_sigmoid(gc_ref[...].astype(F32)) * yc + _sigmoid(gr_ref[...].astype(F32)) * yr).astype(BF16)
        r = ALPHA * h_ref[...] + _dot(m, wo_ref[...])
        y = _ln_fwd(r, g_ref[...], b_ref[...])
        yc_ref[...] = yc.astype(BF16)
        yr_ref[...] = yr.astype(BF16)
        m_ref[...] = m
        r_ref[...] = r
        y_ref[...] = y
        yb_ref[...] = y.astype(BF16)

    return _pallas(
        body, name=name, grid=(T // tm,),
        in_specs=[_rows(tm, D), _rows(tm, C), _rows(tm, D), _rows(tm, D), _rows(tm, D), _res(wcp.shape),
                  _res(wrp.shape), _res(wout.shape), _res((1, D)), _res((1, D))],
        out_specs=[_rows(tm, D)] * 6,
        out_shape=[_sds((T, D), BF16), _sds((T, D), BF16), _sds((T, D), BF16), _sds((T, D), F32),
                   _sds((T, D), F32), _sds((T, D), BF16)],
        args=(cs, hg, gc, gr, hres, wcp, wrp, wout, g, b), comm=comm)


def _loss_ln_bwd(y, target, r, g, name):
    T, D = y.shape
    tm = _tile(T, 512)

    def body(y_ref, t_ref, r_ref, g_ref, loss_ref, dr_ref, drb_ref, dg_ref, db_ref):
        @pl.when(pl.program_id(0) == 0)
        def _():
            loss_ref[...] = jnp.zeros_like(loss_ref)
            dg_ref[...] = jnp.zeros_like(dg_ref)
            db_ref[...] = jnp.zeros_like(db_ref)

        e = y_ref[...] - t_ref[...]
        loss_ref[...] += (0.5 / D) * jnp.sum(e * e)
        dr, dg, db = _ln_bwd(e * (1.0 / D), r_ref[...], g_ref[...])
        dr_ref[...] = dr
        drb_ref[...] = dr.astype(BF16)
        dg_ref[...] += dg
        db_ref[...] += db

    return _pallas(
        body, name=name, grid=(T // tm,),
        in_specs=[_rows(tm, D), _rows(tm, D), _rows(tm, D), _res((1, D))],
        out_specs=[_acc((SUBLANES, LANES)), _rows(tm, D), _rows(tm, D), _acc((1, D)), _acc((1, D))],
        out_shape=[_sds((SUBLANES, LANES), F32), _sds((T, D), F32), _sds((T, D), BF16), _sds((1, D), F32),
                   _sds((1, D), F32)],
        args=(y, target, r, g))


def _ffn_bwd_a(drb, wd, gu, name, comm=None):
    T, D = drb.shape
    F = wd.shape[0]
    tm = _tile(T, 512)

    def body(d_ref, wd_ref, gu_ref, o_ref):
        d = d_ref[...]
        for j, cw in _chunks(F, 256):
            da = 0.5 * _dot_nt(d, wd_ref[j:j + cw, :])
            gt = gu_ref[:, j:j + cw].astype(F32)
            up = gu_ref[:, F + j:F + j + cw].astype(F32)
            sg = _sigmoid(gt)
            o_ref[:, j:j + cw] = (da * up * (sg * (1.0 + gt * (1.0 - sg)))).astype(BF16)
            o_ref[:, F + j:F + j + cw] = (da * (gt * sg)).astype(BF16)

    (dgu,) = _pallas(
        body, name=name, grid=(T // tm,),
        in_specs=[_rows(tm, D), _res(wd.shape), _rows(tm, 2 * F)],
        out_specs=[_rows(tm, 2 * F)],
        out_shape=[_sds((T, 2 * F), BF16)],
        args=(drb, wd, gu), comm=comm)
    return dgu


def _nt_res(du, w, dres, name, ln=None, colsum=False, comm=None):
    T, K = du.shape
    D = w.shape[0]
    tm = _tile(T, 256 if K > 6000 else 512)
    n_in = 3 + (2 if ln else 0)

    def body(*refs):
        du_ref, w_ref, dres_ref = refs[:3]
        outs = refs[n_in:]
        dy = ALPHA * dres_ref[...] + _dot_nt(du_ref[...], w_ref[...])
        if ln:
            r_ref, g_ref = refs[3:5]

            @pl.when(pl.program_id(0) == 0)
            def _():
                outs[2][...] = jnp.zeros_like(outs[2])
                outs[3][...] = jnp.zeros_like(outs[3])

            dr, dg, db = _ln_bwd(dy, r_ref[...], g_ref[...])
            outs[0][...] = dr
            outs[1][...] = dr.astype(BF16)
            outs[2][...] += dg
            outs[3][...] += db
        else:
            outs[0][...] = dy
        if colsum:
            cs_ref = outs[-1]

            @pl.when(pl.program_id(0) == 0)
            def _():
                cs_ref[...] = jnp.zeros_like(cs_ref)

            cs_ref[...] += jnp.sum(du_ref[...].astype(F32), axis=0, keepdims=True)

    in_specs = [_rows(tm, K), _res(w.shape), _rows(tm, D)]
    args = [du, w, dres]
    if ln:
        in_specs += [_rows(tm, D), _res((1, D))]
        args += list(ln)
        out_specs = [_rows(tm, D), _rows(tm, D), _acc((1, D)), _acc((1, D))]
        out_shape = [_sds((T, D), F32), _sds((T, D), BF16), _sds((1, D), F32), _sds((1, D), F32)]
    else:
        out_specs = [_rows(tm, D)]
        out_shape = [_sds((T, D), F32)]
    if colsum:
        out_specs.append(_acc((1, K)))
        out_shape.append(_sds((1, K), F32))
    return _pallas(body, name=name, grid=(T // tm,), in_specs=in_specs, out_specs=out_specs, out_shape=out_shape,
                   args=args, comm=comm)


def _mm_tn(x, dy, name, scale=1.0, comm=None):
    T, K = x.shape
    N = dy.shape[1]
    tt = _tile(T, 1024)
    tn = next((c for c in (512, 768) if N % c == 0), N)
    nt = T // tt

    def body(x_ref, dy_ref, o_ref):
        t = pl.program_id(1)

        @pl.when(t == 0)
        def _():
            o_ref[...] = jnp.zeros_like(o_ref)

        o_ref[...] += _dot_tn(x_ref[...].astype(BF16), dy_ref[...])
        if scale != 1.0:
            @pl.when(t == nt - 1)
            def _():
                o_ref[...] = o_ref[...] * scale

    (out,) = _pallas(
        body, name=name, grid=(N // tn, nt),
        in_specs=[pl.BlockSpec((tt, K), lambda j, t: (t, 0)), pl.BlockSpec((tt, tn), lambda j, t: (t, j))],
        out_specs=[pl.BlockSpec((K, tn), lambda j, t: (0, j))],
        out_shape=[_sds((K, N), F32)],
        args=(x, dy), comm=comm)
    return out


def _mix_bwd1(drb, wout, wcp, wrp, gc, gr, yc, yr, rg, h, name, comm=None):
    T, D = drb.shape
    C = rg.shape[1]
    tm = _tile(T, 512)

    def body(d_ref, wo_ref, wcp_ref, wrp_ref, gc_ref, gr_ref, yc_ref, yr_ref, rg_ref, h_ref,
             dyc_ref, dyr_ref, dgc_ref, dgr_ref, dcs_ref, dh_ref, drg_ref):
        dm = _dot_nt(d_ref[...], wo_ref[...])
        sc = _sigmoid(gc_ref[...].astype(F32))
        sr = _sigmoid(gr_ref[...].astype(F32))
        dyc = (dm * sc).astype(BF16)
        dyr = (dm * sr).astype(BF16)
        dyc_ref[...] = dyc
        dyr_ref[...] = dyr
        dgc_ref[...] = (dm * yc_ref[...].astype(F32) * sc * (1.0 - sc)).astype(BF16)
        dgr_ref[...] = (dm * yr_ref[...].astype(F32) * sr * (1.0 - sr)).astype(BF16)
        dcs_ref[...] = _dot_nt(dyc, wcp_ref[...])
        dhg = _dot_nt(dyr, wrp_ref[...])
        rgv = rg_ref[...].astype(F32)
        gel, t = _gelu(rgv)
        dh_ref[...] = dhg * gel
        drg_ref[...] = (dhg * h_ref[...] * _gelu_grad(rgv, t)).astype(BF16)

    return _pallas(
        body, name=name, grid=(T // tm,),
        in_specs=[_rows(tm, D), _res(wout.shape), _res(wcp.shape), _res(wrp.shape), _rows(tm, D), _rows(tm, D),
                  _rows(tm, D), _rows(tm, D), _rows(tm, C), _rows(tm, C)],
        out_specs=[_rows(tm, D)] * 5 + [_rows(tm, C)] * 2,
        out_shape=[_sds((T, D), BF16)] * 4 + [_sds((T, D), F32), _sds((T, C), F32), _sds((T, C), BF16)],
        args=(drb, wout, wcp, wrp, gc, gr, yc, yr, rg, h), comm=comm)


def _scan_bwd(dh, a, name):
    T, C = dh.shape
    tt = _tile(T, 512)
    nt = T // tt
    ng = tt // SUBLANES

    def body(d_ref, a_ref, g_ref, carry):
        @pl.when(pl.program_id(0) == 0)
        def _():
            carry[...] = jnp.zeros_like(carry)

        row = lax.broadcasted_iota(jnp.int32, (SUBLANES, C), 0)

        def group(j, enext):
            base = pl.multiple_of((ng - 1 - j) * SUBLANES, SUBLANES)
            av = a_ref[pl.ds(base, SUBLANES), :]
            dv = d_ref[pl.ds(base, SUBLANES), :]
            bv = av * dv
            for s in (1, 2, 4):
                keep = row < SUBLANES - s
                a_s = jnp.where(keep, pltpu.roll(av, SUBLANES - s, 0), 1.0)
                b_s = jnp.where(keep, pltpu.roll(bv, SUBLANES - s, 0), 0.0)
                bv = av * b_s + bv
                av = av * a_s
            e = av * enext + bv
            e_up = jnp.where(row < SUBLANES - 1, pltpu.roll(e, SUBLANES - 1, 0), enext)
            g_ref[pl.ds(base, SUBLANES), :] = dv + e_up
            return e[0:1, :]

        carry[...] = lax.fori_loop(0, ng, group, carry[...])

    rev = pl.BlockSpec((tt, C), lambda i: (nt - 1 - i, 0))
    (g,) = _pallas(
        body, name=name, grid=(nt,), in_specs=[rev, rev], out_specs=[rev],
        out_shape=[_sds((T, C), F32)],
        scratch_shapes=[pltpu.VMEM((1, C), F32)],
        args=(dh, a))
    return g


def _gates_bwd(g, hp, ra, ri, r, lam, bda, bdx, name, comm=None):
    T, C = g.shape
    tm = _tile(T, 512)
    nt = T // tm

    def body(g_ref, hp_ref, ra_ref, ri_ref, r_ref, lam_ref, wa_ref, wx_ref,
             dr_ref, dpa_ref, dpx_ref, dlam_ref, dba_ref, dbx_ref):
        @pl.when(pl.program_id(0) == 0)
        def _():
            dlam_ref[...] = jnp.zeros_like(dlam_ref)
            dba_ref[...] = jnp.zeros_like(dba_ref)
            dbx_ref[...] = jnp.zeros_like(dbx_ref)

        gv, rav, riv, rv, lamv = g_ref[...], ra_ref[...], ri_ref[...], r_ref[...], lam_ref[...]
        sp = _softplus_neg(lamv)
        log_a = (-RG_LRU_C) * rav * sp
        av = jnp.exp(log_a)
        mult = jnp.sqrt(-_expm1(2.0 * log_a))
        d_mult = gv * riv * rv
        d_i = gv * mult * rv
        d_loga = gv * hp_ref[...] * av - d_mult * (av * av) / mult
        d_ra = d_loga * ((-RG_LRU_C) * sp)
        dpa = d_ra * rav * (1.0 - rav)
        dpx = d_i * riv * (1.0 - riv)
        dpab = dpa.astype(BF16)
        dpxb = dpx.astype(BF16)
        dpa_ref[...] = dpab
        dpx_ref[...] = dpxb
        dr_ref[...] = gv * mult * riv + _dot_nt(dpab, wa_ref[...]) + _dot_nt(dpxb, wx_ref[...])
        dlam_ref[...] += jnp.sum(d_loga * ((-RG_LRU_C) * rav), axis=0, keepdims=True)
        dba_ref[...] += jnp.sum(dpa, axis=0, keepdims=True)
        dbx_ref[...] += jnp.sum(dpx, axis=0, keepdims=True)

        @pl.when(pl.program_id(0) == nt - 1)
        def _():
            dlam_ref[...] = dlam_ref[...] * (-_sigmoid(-lamv))

    return _pallas(
        body, name=name, grid=(nt,),
        in_specs=[_rows(tm, C)] * 5 + [_res((1, C)), _res(bda.shape), _res(bdx.shape)],
        out_specs=[_rows(tm, C)] * 3 + [_acc((1, C))] * 3,
        out_shape=[_sds((T, C), F32), _sds((T, C), BF16), _sds((T, C), BF16)] + [_sds((1, C), F32)] * 3,
        args=(g, hp, ra, ri, r, lam, bda, bdx), comm=comm)


def _conv4_bwd(dr, rx, w, name):
    T, C = dr.shape
    K = w.shape[0]
    R = CONV_ROWS
    P = RNN_PAD

    def body(d_ref, x_ref, w_ref, dx_ref, dw_ref, db_ref, dpad, xpad):
        dpad[0:T, :] = d_ref[...]
        dpad[T:T + P, :] = jnp.zeros((P, LANES), F32)
        xpad[0:P, :] = jnp.zeros((P, LANES), F32)
        xpad[P:P + T, :] = x_ref[...]
        dw_ref[...] = jnp.zeros_like(dw_ref)
        db_ref[...] = jnp.zeros_like(db_ref)
        wv = w_ref[...]

        def step(i, carry):
            base = pl.multiple_of(i * R, R)
            dwin = dpad[pl.ds(base, R + P), :]
            xwin = xpad[pl.ds(base, R + P), :]
            dcur = dwin[0:R, :]
            acc = jnp.zeros((R, LANES), F32)
            for k in range(K):
                acc = acc + wv[k:k + 1, :] * dwin[K - 1 - k:K - 1 - k + R, :]
                s = P - (K - 1) + k
                dw_ref[k:k + 1, :] += jnp.sum(dcur * xwin[s:s + R, :], axis=0, keepdims=True)
            dx_ref[pl.ds(base, R), :] = acc.astype(BF16)
            db_ref[...] += jnp.sum(dcur, axis=0, keepdims=True)
            return carry

        lax.fori_loop(0, T // R, step, 0)

    return _pallas(
        body, name=name, grid=(C // LANES,),
        in_specs=[_cols(T), _cols(T), _cols(T, K)],
        out_specs=[_cols(T), _cols(T, SUBLANES), _cols(T, 1)],
        out_shape=[_sds((T, C), BF16), _sds((SUBLANES, C), F32), _sds((1, C), F32)],
        scratch_shapes=[pltpu.VMEM((T + P, LANES), F32), pltpu.VMEM((T + P, LANES), F32)],
        args=(dr, rx, w))


def _conv31_bwd(dcs, cc, c, cv, cg, w, gg, gb, name, comm=None):
    T, D = dcs.shape
    K = w.shape[0]
    R, B, P = CONV_ROWS, CONV_BLOCK, CONV_PAD
    d_groups = _tap_groups([K - 1 - k for k in range(K)])
    x_groups = _tap_groups([P - (K - 1) + k for k in range(K)])

    def body(dcs_ref, cc_ref, c_ref, cv_ref, cg_ref, w_ref, gg_ref, gb_ref,
             dcv_ref, dcg_ref, dw_ref, db_ref, dgg_ref, dgb_ref, dpad, xpad, dwacc):
        dpad[T:T + P, :] = jnp.zeros((P, LANES), F32)
        xpad[0:P, :] = jnp.zeros((P, LANES), F32)
        xpad[P:P + T, :] = c_ref[...]
        dwacc[...] = jnp.zeros_like(dwacc)
        db_ref[...] = jnp.zeros_like(db_ref)
        dgg_ref[...] = jnp.zeros_like(dgg_ref)
        dgb_ref[...] = jnp.zeros_like(dgb_ref)
        wv, ggv, gbv = w_ref[...], gg_ref[...], gb_ref[...]

        def norm_step(i, carry):
            base = pl.multiple_of(i * R, R)
            xhat, rstd = _gn_stats(cc_ref[pl.ds(base, R), :])
            gn = xhat * ggv + gbv
            sg = _sigmoid(gn)
            dgn = dcs_ref[pl.ds(base, R), :] * (sg * (1.0 + gn * (1.0 - sg)))
            dgg_ref[...] += jnp.sum(dgn * xhat, axis=0, keepdims=True)
            dgb_ref[...] += jnp.sum(dgn, axis=0, keepdims=True)
            dxh = dgn * ggv
            m1 = jnp.mean(dxh, axis=-1, keepdims=True)
            m2 = jnp.mean(dxh * xhat, axis=-1, keepdims=True)
            dcc = rstd * (dxh - m1 - xhat * m2)
            dpad[pl.ds(base, R), :] = dcc
            db_ref[...] += jnp.sum(dcc, axis=0, keepdims=True)
            return carry

        lax.fori_loop(0, T // R, norm_step, 0)

        def conv_step(i, carry):
            base = pl.multiple_of(i * B, B)
            dwin = dpad[pl.ds(base, B + P), :]
            xwin = xpad[pl.ds(base, B + P), :]
            dcur = dwin[0:B, :]
            acc = _tap_sum(dwin, wv, d_groups, B)
            for phase, taps in x_groups:
                sh = _shifted(xwin, phase)
                for k, off in taps:
                    prod = dcur * sh[off:off + B, :]
                    part = prod[0:SUBLANES, :]
                    for q in range(1, B // SUBLANES):
                        part = part + prod[q * SUBLANES:(q + 1) * SUBLANES, :]
                    dwacc[k * SUBLANES:(k + 1) * SUBLANES, :] += part
            cgv = cg_ref[pl.ds(base, B), :].astype(F32)
            cvv = cv_ref[pl.ds(base, B), :].astype(F32)
            sg = _sigmoid(cgv)
            dcv_ref[pl.ds(base, B), :] = (acc * sg).astype(BF16)
            dcg_ref[pl.ds(base, B), :] = (acc * cvv * sg * (1.0 - sg)).astype(BF16)
            return carry

        lax.fori_loop(0, T // B, conv_step, 0)
        dw_ref[...] = jnp.zeros_like(dw_ref)
        for k in range(K):
            dw_ref[k:k + 1, :] = jnp.sum(dwacc[k * SUBLANES:(k + 1) * SUBLANES, :], axis=0, keepdims=True)

    return _pallas(
        body, name=name, grid=(D // LANES,),
        in_specs=[_cols(T)] * 5 + [_cols(T, K), _cols(T, 1), _cols(T, 1)],
        out_specs=[_cols(T), _cols(T), _cols(T, P), _cols(T, 1), _cols(T, 1), _cols(T, 1)],
        out_shape=[_sds((T, D), BF16), _sds((T, D), BF16), _sds((P, D), F32), _sds((1, D), F32),
                   _sds((1, D), F32), _sds((1, D), F32)],
        scratch_shapes=[pltpu.VMEM((T + P, LANES), F32), pltpu.VMEM((T + P, LANES), F32),
                        pltpu.VMEM((P * SUBLANES, LANES), F32)],
        args=(dcs, cc, c, cv, cg, w, gg, gb), comm=comm)


def _reduce_adamw(recvs, w, m, v, name):
    L, R, C = w.shape
    assert len(recvs) == L
    tr = next((c for c in (256, 176, 128, 64, 8) if R % c == 0), R)
    nr = R // tr
    c1 = 1.0 - ADAM_B1 ** ADAM_STEP
    c2 = 1.0 - ADAM_B2 ** ADAM_STEP

    def body(*refs):
        recv_refs = refs[:L]
        w_ref, m_ref, v_ref, g_ref, d_ref, mo_ref, vo_ref = refs[L:]

        def update(recv_ref):
            g = recv_ref[0].astype(F32)
            for k in range(1, N_DEV):
                g = g + recv_ref[k].astype(F32)
            mn = ADAM_B1 * m_ref[0] + (1.0 - ADAM_B1) * g
            vn = ADAM_B2 * v_ref[0] + (1.0 - ADAM_B2) * (g * g)
            g_ref[0] = g
            mo_ref[0] = mn
            vo_ref[0] = vn
            d_ref[0] = (-ADAM_LR) * ((mn / c1) / (jnp.sqrt(vn / c2) + ADAM_EPS) + ADAM_WD * w_ref[0])

        for l in range(L):
            pl.when(pl.program_id(0) == l)(lambda l=l: update(recv_refs[l]))

    def recv_spec(l):
        return pl.BlockSpec((N_DEV, tr, C), lambda j, i: (0, jnp.where(j == l, i, jnp.where(j < l, 0, nr - 1)), 0))

    blk = pl.BlockSpec((1, tr, C), lambda j, i: (j, i, 0))
    return _pallas(
        body, name=name, grid=(L, nr),
        in_specs=[recv_spec(l) for l in range(L)] + [blk, blk, blk],
        out_specs=[blk] * 4,
        out_shape=[_sds((L, R, C), F32)] * 4,
        args=(*recvs, w, m, v))


def _unshard(name, gathered):
    n, r, c = gathered.shape
    if name in COL_SHARDED:
        return gathered.transpose(1, 0, 2).reshape(r, n * c)
    return gathered.reshape(n * r, c)


def _to_shards(name, full):
    R, C = full.shape
    wire = BF16 if name in BF16_ON_WIRE else F32
    if name in COL_SHARDED:
        return full.reshape(R, N_DEV, C // N_DEV).transpose(1, 0, 2).astype(wire)
    return full.reshape(N_DEV, R // N_DEV, C).astype(wire)


def _block_diag(w):
    H, b, _ = w.shape
    rows = [jnp.pad(w[h], ((0, 0), (h * b, (H - 1 - h) * b))) for h in range(H)]
    return jnp.concatenate(rows, axis=0)


def _diag_blocks(dense, H):
    b = dense.shape[0] // H
    return jnp.stack([dense[h * b:(h + 1) * b, h * b:(h + 1) * b] for h in range(H)])


def _pack(arrs, rows):
    flat = jnp.concatenate([a.reshape(-1) for a in arrs])
    return jnp.pad(flat, (0, rows * 1024 - flat.shape[0])).reshape(1, rows, 1024)


def _unpack(packed, shapes):
    flat = packed.reshape(-1)
    out, off = [], 0
    for s in shapes:
        n = math.prod(s)
        out.append(flat[off:off + n].reshape(s))
        off += n
    return out


class _Queue:
    def __init__(self, kind, us_per_mb):
        self.kind, self.us_per_mb, self.items, self.done = kind, us_per_mb, [], {}

    def push(self, key, arr):
        self.items.append((key, arr))

    def take(self, micros):
        taken, budget = [], micros / self.us_per_mb
        while self.items and (not taken or _mbytes(self.items[0][1]) <= budget):
            budget -= _mbytes(self.items[0][1])
            taken.append(self.items.pop(0))
        return {"kind": self.kind, "keys": [k for k, _ in taken], "arrs": [a for _, a in taken]} if taken else None

    def landed(self, comm):
        if comm:
            self.done.update(zip(comm["keys"], comm["recv"]))

    def flush(self, name, upto=None):
        n = len(self.items)
        if upto is not None:
            keys = [k for k, _ in self.items]
            n = keys.index(upto) + 1 if upto in keys else 0
        if n:
            taken, self.items = self.items[:n], self.items[n:]
            self.done.update(zip([k for k, _ in taken], _exchange(self.kind, [a for _, a in taken], name)))


def kernel(x, ffn1_w_gu, ffn1_w_down, ln1_g, ln1_b, mix_w_in, mix_b_in, conv_dw_w, conv_dw_b, conv_gn_g, conv_gn_b, conv_w_proj, rnn_conv_w, rnn_conv_b, rnn_w_a, rnn_b_a, rnn_w_x, rnn_b_x, rnn_lambda, rnn_w_proj, mix_w_out, ln2_g, ln2_b, ffn2_w_gu, ffn2_w_down, ln3_g, ln3_b, loss_target, m_ffn1_w_gu, m_ffn1_w_down, m_ln1_g, m_ln1_b, m_mix_w_in, m_mix_b_in, m_conv_dw_w, m_conv_dw_b, m_conv_gn_g, m_conv_gn_b, m_conv_w_proj, m_rnn_conv_w, m_rnn_conv_b, m_rnn_w_a, m_rnn_b_a, m_rnn_w_x, m_rnn_b_x, m_rnn_lambda, m_rnn_w_proj, m_mix_w_out, m_ln2_g, m_ln2_b, m_ffn2_w_gu, m_ffn2_w_down, m_ln3_g, m_ln3_b, v_ffn1_w_gu, v_ffn1_w_down, v_ln1_g, v_ln1_b, v_mix_w_in, v_mix_b_in, v_conv_dw_w, v_conv_dw_b, v_conv_gn_g, v_conv_gn_b, v_conv_w_proj, v_rnn_conv_w, v_rnn_conv_b, v_rnn_w_a, v_rnn_b_a, v_rnn_w_x, v_rnn_b_x, v_rnn_lambda, v_rnn_w_proj, v_mix_w_out, v_ln2_g, v_ln2_b, v_ffn2_w_gu, v_ffn2_w_down, v_ln3_g, v_ln3_b):
    given = dict(locals())
    W = {n: given[n] for n in WEIGHTS}
    M = {n: given["m_" + n] for n in WEIGHTS}
    V = {n: given["v_" + n] for n in WEIGHTS}
    T, D = x.shape[1], x.shape[2]
    L = DEPTH
    x2 = x.reshape(T, D)
    target = loss_target.reshape(T, D)
    d_rnn = rnn_conv_b.shape[1]

    gather = _Queue("gather", GATHER_US_PER_MB)
    for l in range(L):
        for n in USE_ORDER:
            gather.push((n, l), W[n][l].astype(BF16) if n in BF16_ON_WIRE else W[n][l])
    gather.flush("gather_first", upto=("ffn1_w_gu", 0))
    full_cache = {}

    def full(n, l):
        if (n, l) not in full_cache:
            gather.flush(f"gather_{n}_{l}", upto=(n, l))
            full_cache[(n, l)] = _unshard(n, gather.done[(n, l)])
        return full_cache[(n, l)]

    def fwd_comm(micros):
        return gather.take(micros)

    bd_a = [_block_diag(rnn_w_a[l]).astype(BF16) for l in range(L)]
    bd_x = [_block_diag(rnn_w_x[l]).astype(BF16) for l in range(L)]

    def vec(name, l):
        return W[name][l:l + 1]

    saved = []
    h, hb = x2, x2.astype(BF16)
    for l in range(L):
        s = {"hb_in": hb}
        w_gu = full("ffn1_w_gu", l)
        comm = fwd_comm(105)
        s["gu1"], s["a1"] = _ffn_up(hb, w_gu, f"ffn1_up_{l}", comm=comm)
        gather.landed(comm)
        w_down = full("ffn1_w_down", l)
        comm = fwd_comm(65)
        s["r1"], y1, s["y1b"] = _ffn_down_ln(s["a1"], w_down, h, vec("ln1_g", l), vec("ln1_b", l),
                                              f"ffn1_down_ln_{l}", comm=comm)
        gather.landed(comm)
        w_in = full("mix_w_in", l)
        comm = fwd_comm(135)
        s["c"], s["cv"], s["cg"], s["rx"], s["rg"], s["gc"], s["gr"] = _mix_in(
            s["y1b"], w_in, vec("mix_b_in", l), d_rnn, f"mix_in_{l}", comm=comm)
        gather.landed(comm)
        w_dw = full("conv_dw_w", l)
        comm = fwd_comm(150)
        s["cc"], s["cs"] = _conv31_gn(s["c"], w_dw, vec("conv_dw_b", l), vec("conv_gn_g", l),
                                      vec("conv_gn_b", l), f"conv31_gn_{l}", comm=comm)
        gather.landed(comm)
        s["r"] = _conv4(s["rx"], full("rnn_conv_w", l), vec("rnn_conv_b", l), f"conv4_{l}")
        comm = fwd_comm(115)
        s["ra"], s["ri"], s["a"], uu = _gates(s["r"], bd_a[l], bd_x[l], vec("rnn_b_a", l), vec("rnn_b_x", l),
                                              vec("rnn_lambda", l), f"gates_{l}", comm=comm)
        gather.landed(comm)
        s["h"], s["hp"], s["hg"] = _scan_fwd(s["a"], uu, s["rg"], f"scan_{l}")
        w_cp, w_rp, w_out = full("conv_w_proj", l), full("rnn_w_proj", l), full("mix_w_out", l)
        comm = fwd_comm(85)
        s["yc"], s["yr"], s["m"], s["r2"], y2, s["y2b"] = _mix_out_ln(
            s["cs"], s["hg"], s["gc"], s["gr"], y1, w_cp, w_rp, w_out, vec("ln2_g", l), vec("ln2_b", l),
            f"mix_out_ln_{l}", comm=comm)
        gather.landed(comm)
        w_gu2 = full("ffn2_w_gu", l)
        comm = fwd_comm(105)
        s["gu2"], s["a2"] = _ffn_up(s["y2b"], w_gu2, f"ffn2_up_{l}", comm=comm)
        gather.landed(comm)
        w_down2 = full("ffn2_w_down", l)
        comm = fwd_comm(65)
        s["r3"], h, hb = _ffn_down_ln(s["a2"], w_down2, y2, vec("ln3_g", l), vec("ln3_b", l),
                                      f"ffn2_down_ln_{l}", comm=comm)
        gather.landed(comm)
        saved.append(s)

    scatter = _Queue("scatter", SCATTER_US_PER_MB)
    G = {n: [None] * L for n in WEIGHTS}

    def ready(n, l, grad):
        G[n][l] = grad
        scatter.push((n, l), _to_shards(n, grad))

    def bwd_comm(micros):
        return scatter.take(micros)

    loss_acc, dr3, drb3, G["ln3_g"][L - 1], G["ln3_b"][L - 1] = _loss_ln_bwd(
        h, target, saved[L - 1]["r3"], vec("ln3_g", L - 1), "loss_ln3_bwd")
    grad_x = small = None
    small_shapes = [W[n].shape for n in REPLICATED]
    small_rows = -(-sum(math.prod(s) for s in small_shapes) // (1024 * SUBLANES)) * SUBLANES
    for l in reversed(range(L)):
        s = saved[l]
        comm = bwd_comm(95)
        dgu2 = _ffn_bwd_a(drb3, full("ffn2_w_down", l), s["gu2"], f"ffn2_bwd_a_{l}", comm=comm)
        scatter.landed(comm)
        ready("ffn2_w_down", l, _mm_tn(s["a2"], drb3, f"ffn2_dw_down_{l}", scale=0.5))
        ready("ffn2_w_gu", l, _mm_tn(s["y2b"], dgu2, f"ffn2_dw_gu_{l}"))
        comm = bwd_comm(140)
        dr2, drb2, G["ln2_g"][l], G["ln2_b"][l] = _nt_res(
            dgu2, full("ffn2_w_gu", l), dr3, f"ffn2_bwd_x_{l}", ln=(s["r2"], vec("ln2_g", l)), comm=comm)
        scatter.landed(comm)
        comm = bwd_comm(120)
        dyc, dyr, dgc, dgr, dcs, dh, drg = _mix_bwd1(
            drb2, full("mix_w_out", l), full("conv_w_proj", l), full("rnn_w_proj", l), s["gc"], s["gr"], s["yc"],
            s["yr"], s["rg"], s["h"], f"mix_bwd_out_{l}", comm=comm)
        scatter.landed(comm)
        ready("mix_w_out", l, _mm_tn(s["m"], drb2, f"mix_dw_out_{l}"))
        ready("conv_w_proj", l, _mm_tn(s["cs"], dyc, f"conv_dw_proj_{l}"))
        ready("rnn_w_proj", l, _mm_tn(s["hg"], dyr, f"rnn_dw_proj_{l}"))
        gsc = _scan_bwd(dh, s["a"], f"scan_bwd_{l}")
        comm = bwd_comm(160)
        dr_, dpa, dpx, G["rnn_lambda"][l], G["rnn_b_a"][l], G["rnn_b_x"][l] = _gates_bwd(
            gsc, s["hp"], s["ra"], s["ri"], s["r"], vec("rnn_lambda", l), bd_a[l], bd_x[l], f"gates_bwd_{l}",
            comm=comm)
        scatter.landed(comm)
        G["rnn_w_a"][l] = _diag_blocks(_mm_tn(s["r"], dpa, f"rnn_dw_a_{l}"), RNN_BLOCKS)
        G["rnn_w_x"][l] = _diag_blocks(_mm_tn(s["r"], dpx, f"rnn_dw_x_{l}"), RNN_BLOCKS)
        drx, dw4, G["rnn_conv_b"][l] = _conv4_bwd(dr_, s["rx"], full("rnn_conv_w", l), f"conv4_bwd_{l}")
        ready("rnn_conv_w", l, dw4[:RNN_CONV_WIDTH])
        comm = bwd_comm(250)
        dcv, dcg, dw31, G["conv_dw_b"][l], G["conv_gn_g"][l], G["conv_gn_b"][l] = _conv31_bwd(
            dcs, s["cc"], s["c"], s["cv"], s["cg"], full("conv_dw_w", l), vec("conv_gn_g", l),
            vec("conv_gn_b", l), f"conv31_bwd_{l}", comm=comm)
        scatter.landed(comm)
        ready("conv_dw_w", l, dw31[:CONV_WIDTH])
        du = jnp.concatenate([dcv, dcg, drx, drg, dgc, dgr], axis=1)
        ready("mix_w_in", l, _mm_tn(s["y1b"], du, f"mix_dw_in_{l}"))
        comm = bwd_comm(175)
        dr1, drb1, G["ln1_g"][l], G["ln1_b"][l], G["mix_b_in"][l] = _nt_res(
            du, full("mix_w_in", l), dr2, f"mix_bwd_in_{l}", ln=(s["r1"], vec("ln1_g", l)), colsum=True, comm=comm)
        scatter.landed(comm)
        ready("ffn1_w_down", l, _mm_tn(s["a1"], drb1, f"ffn1_dw_down_{l}", scale=0.5))
        comm = bwd_comm(95)
        dgu1 = _ffn_bwd_a(drb1, full("ffn1_w_down", l), s["gu1"], f"ffn1_bwd_a_{l}", comm=comm)
        scatter.landed(comm)
        if l == 0:
            local_small = [jnp.stack([g.reshape(W[n].shape[1:]) for g in G[n]]) for n in REPLICATED]
            comm = {"kind": "gather", "arrs": [_pack(local_small, small_rows)[0]]}
            ready("ffn1_w_gu", l, _mm_tn(s["hb_in"], dgu1, f"ffn1_dw_gu_{l}", comm=comm))
            (small,) = comm["recv"]
        else:
            ready("ffn1_w_gu", l, _mm_tn(s["hb_in"], dgu1, f"ffn1_dw_gu_{l}"))
        if l > 0:
            comm = bwd_comm(130)
            dr3, drb3, G["ln3_g"][l - 1], G["ln3_b"][l - 1] = _nt_res(
                dgu1, full("ffn1_w_gu", l), dr1, f"ffn1_bwd_x_{l}", ln=(saved[l - 1]["r3"], vec("ln3_g", l - 1)),
                comm=comm)
        else:
            comm = bwd_comm(1e9)
            (grad_x,) = _nt_res(dgu1, full("ffn1_w_gu", l), dr1, f"ffn1_bwd_x_{l}", comm=comm)
        scatter.landed(comm)
    scatter.flush("scatter_rest")

    loss = lax.psum(loss_acc[0, 0], ("x", "y", "c"))

    out = {}
    for n in SHARDED:
        out[n] = _reduce_adamw([scatter.done[(n, l)] for l in range(L)], W[n], M[n], V[n], f"adamw_{n}")
    packed = _reduce_adamw([small], _pack([W[n] for n in REPLICATED], small_rows),
                           _pack([M[n] for n in REPLICATED], small_rows),
                           _pack([V[n] for n in REPLICATED], small_rows), "adamw_small")
    unpacked = [_unpack(p, small_shapes) for p in packed]
    for i, n in enumerate(REPLICATED):
        out[n] = tuple(u[i] for u in unpacked)

    return (loss, grad_x.reshape(x.shape), *[out[n][0] for n in WEIGHTS], *[out[n][1] for n in WEIGHTS],
            *[out[n][2] for n in WEIGHTS], *[out[n][3] for n in WEIGHTS])
```

```python
import math

import jax
import jax.numpy as jnp
from jax import lax
from jax.experimental import pallas as pl
from jax.experimental.pallas import tpu as pltpu

F32 = jnp.float32
BF16 = jnp.bfloat16
MESH = pl.DeviceIdType.MESH

DEPTH = 2
ALPHA = (2 * DEPTH) ** 0.25
LN_EPS = 1e-5
RG_LRU_C = 8.0
CONV_WIDTH = 31
RNN_CONV_WIDTH = 4
RNN_BLOCKS = 16
N_DEV = 8
ADAM_LR, ADAM_B1, ADAM_B2, ADAM_EPS, ADAM_WD, ADAM_STEP = 0.001, 0.9, 0.999, 1e-08, 0.01, 10

LANES = 128
SUBLANES = 8
VMEM_LIMIT = 56 * 1024 * 1024
CONV_PAD = 32
RNN_PAD = 8
CONV_ROWS = 128
NORM_ROWS = 256
CONV_BLOCK = 32
GATHER_US_PER_MB = 43.0
SCATTER_US_PER_MB = 86.0

WEIGHTS = ['ffn1_w_gu', 'ffn1_w_down', 'ln1_g', 'ln1_b', 'mix_w_in', 'mix_b_in', 'conv_dw_w', 'conv_dw_b',
           'conv_gn_g', 'conv_gn_b', 'conv_w_proj', 'rnn_conv_w', 'rnn_conv_b', 'rnn_w_a', 'rnn_b_a', 'rnn_w_x',
           'rnn_b_x', 'rnn_lambda', 'rnn_w_proj', 'mix_w_out', 'ln2_g', 'ln2_b', 'ffn2_w_gu', 'ffn2_w_down',
           'ln3_g', 'ln3_b']
COL_SHARDED = ['ffn1_w_gu', 'mix_w_in', 'ffn2_w_gu', 'conv_dw_w', 'rnn_conv_w']
ROW_SHARDED = ['ffn1_w_down', 'conv_w_proj', 'rnn_w_proj', 'mix_w_out', 'ffn2_w_down']
SHARDED = COL_SHARDED + ROW_SHARDED
BF16_ON_WIRE = ['ffn1_w_gu', 'mix_w_in', 'ffn2_w_gu', 'ffn1_w_down', 'conv_w_proj', 'rnn_w_proj', 'mix_w_out',
                'ffn2_w_down']
REPLICATED = [n for n in WEIGHTS if n not in SHARDED]
USE_ORDER = ['ffn1_w_gu', 'ffn1_w_down', 'mix_w_in', 'conv_dw_w', 'rnn_conv_w', 'conv_w_proj', 'rnn_w_proj',
             'mix_w_out', 'ffn2_w_gu', 'ffn2_w_down']


def _cp(n_axes=1):
    return pltpu.CompilerParams(dimension_semantics=("arbitrary",) * n_axes, vmem_limit_bytes=VMEM_LIMIT)


def _rows(tm, c):
    return pl.BlockSpec((tm, c), lambda i: (i, 0))


def _res(shape):
    nd = len(shape)
    return pl.BlockSpec(tuple(shape), lambda *_: (0,) * nd, pipeline_mode=pl.Buffered(1))


def _acc(shape):
    nd = len(shape)
    return pl.BlockSpec(tuple(shape), lambda *_: (0,) * nd)


def _tile(t, want):
    return want if t % want == 0 else t


def _sds(shape, dtype):
    return jax.ShapeDtypeStruct(tuple(shape), dtype)


def _mbytes(a):
    return a.size * a.dtype.itemsize / 1e6


def _ln_fwd(r, g, b):
    mu = jnp.mean(r, axis=-1, keepdims=True)
    xc = r - mu
    var = jnp.mean(xc * xc, axis=-1, keepdims=True)
    return xc * lax.rsqrt(var + LN_EPS) * g + b


def _ln_bwd(dy, r, g):
    mu = jnp.mean(r, axis=-1, keepdims=True)
    xc = r - mu
    var = jnp.mean(xc * xc, axis=-1, keepdims=True)
    rstd = lax.rsqrt(var + LN_EPS)
    xhat = xc * rstd
    dxh = dy * g
    m1 = jnp.mean(dxh, axis=-1, keepdims=True)
    m2 = jnp.mean(dxh * xhat, axis=-1, keepdims=True)
    dr = rstd * (dxh - m1 - xhat * m2)
    return dr, jnp.sum(dy * xhat, axis=0, keepdims=True), jnp.sum(dy, axis=0, keepdims=True)


def _sigmoid(x):
    return jax.nn.sigmoid(x)


_GELU_K = math.sqrt(2.0 / math.pi)


def _gelu(x):
    t = jnp.tanh(_GELU_K * (x + 0.044715 * x * x * x))
    return 0.5 * x * (1.0 + t), t


def _gelu_grad(x, t):
    return 0.5 * (1.0 + t) + 0.5 * x * (1.0 - t * t) * (_GELU_K * (1.0 + 3.0 * 0.044715 * x * x))


def _expm1(x):
    taylor = x * (1.0 + x * (0.5 + x * (1.0 / 6.0 + x * (1.0 / 24.0 + x * (1.0 / 120.0)))))
    return jnp.where(jnp.abs(x) < 0.03, taylor, jnp.exp(x) - 1.0)


def _softplus_neg(lam):
    return jnp.maximum(-lam, 0.0) + jnp.log1p(jnp.exp(-jnp.abs(lam)))


def _dot(a, b):
    return jnp.dot(a, b, preferred_element_type=F32)


def _dot_nt(a, b):
    return lax.dot_general(a, b, (((1,), (1,)), ((), ())), preferred_element_type=F32)


def _dot_tn(a, b):
    return lax.dot_general(a, b, (((0,), (0,)), ((), ())), preferred_element_type=F32)


def _chunks(width, cn):
    return [(j, min(cn, width - j)) for j in range(0, width, cn)]


def _position():
    return lax.axis_index("x"), lax.axis_index("y"), lax.axis_index("c")


def _index(p):
    return 4 * p[0] + 2 * p[1] + p[2]


def _comm_out_shapes(kind, arrs):
    return [_sds((N_DEV,) + a.shape if kind == "gather" else a.shape, a.dtype) for a in arrs]


def _comm_scratch(n):
    return [pltpu.SemaphoreType.DMA((n, 7)), pltpu.SemaphoreType.DMA((n, 7)), pltpu.SemaphoreType.DMA((n,))]


def _comm_phases(kind, srcs, dsts, send_sems, recv_sems, local_sems):
    n = len(srcs)
    x, y, c = _position()
    me, sibling = (x, y, c), (x, y, 1 - c)
    chips = [(1 - x, y), (x, 1 - y), (1 - x, 1 - y)]

    if kind == "gather":
        def copy(a, k, block, to, src=None):
            dst = dsts[a].at[_index(block)]
            return pltpu.make_async_remote_copy(
                src_ref=dst if src is None else src, dst_ref=dst, send_sem=send_sems.at[a, k],
                recv_sem=recv_sems.at[a, k], device_id=to, device_id_type=MESH)

        def mine(a):
            return pltpu.make_async_copy(srcs[a], dsts[a].at[_index(me)], local_sems.at[a])

        def first(a):
            return [copy(a, 0, me, sibling, src=srcs[a])] + [
                copy(a, 1 + j, me, (*chip, c), src=srcs[a]) for j, chip in enumerate(chips)]

        def start():
            for a in range(n):
                mine(a).start()
            for a in range(n):
                for cp in first(a):
                    cp.start()

        def mid():
            for j, chip in enumerate(chips):
                for a in range(n):
                    copy(a, 1 + j, (*chip, c), me).wait_recv()
                    copy(a, 4 + j, (*chip, c), sibling).start()

        def end():
            for a in range(n):
                copy(a, 0, sibling, me).wait_recv()
            for j, chip in enumerate(chips):
                for a in range(n):
                    copy(a, 4 + j, (*chip, 1 - c), me).wait_recv()
            for a in range(n):
                for cp in first(a):
                    cp.wait_send()
                for j, chip in enumerate(chips):
                    copy(a, 4 + j, (*chip, c), sibling).wait_send()
                mine(a).wait()

        return start, mid, end

    def peer_of(k):
        return (1 - x if k & 4 else x, 1 - y if k & 2 else y, 1 - c if k & 1 else c)

    def own(a):
        return pltpu.make_async_copy(srcs[a].at[_index(me)], dsts[a].at[0], local_sems.at[a])

    def remote(a, k):
        peer = peer_of(k)
        return pltpu.make_async_remote_copy(
            src_ref=srcs[a].at[_index(peer)], dst_ref=dsts[a].at[k], send_sem=send_sems.at[a, k - 1],
            recv_sem=recv_sems.at[a, k - 1], device_id=peer, device_id_type=MESH)

    def start():
        for a in range(n):
            own(a).start()
        for k in range(1, N_DEV):
            for a in range(n):
                remote(a, k).start()

    def end():
        for k in range(1, N_DEV):
            for a in range(n):
                remote(a, k).wait()
        for a in range(n):
            own(a).wait()

    return start, (lambda: None), end


def _exchange(kind, arrs, name):
    n = len(arrs)
    hbm = pl.BlockSpec(memory_space=pl.ANY)

    def body(*refs):
        start, mid, end = _comm_phases(kind, refs[:n], refs[n:2 * n], *refs[2 * n:])
        start()
        mid()
        end()

    return pl.pallas_call(
        body, name=name, in_specs=[hbm] * n, out_specs=[hbm] * n, out_shape=_comm_out_shapes(kind, arrs),
        scratch_shapes=_comm_scratch(n))(*arrs)


def _pallas(body, *, name, grid, in_specs, out_specs, out_shape, args, scratch_shapes=(), comm=None):
    in_specs, out_specs, out_shape = list(in_specs), list(out_specs), list(out_shape)
    scratch_shapes = list(scratch_shapes)
    if not comm:
        return pl.pallas_call(body, name=name, grid=grid, in_specs=in_specs, out_specs=out_specs,
                              out_shape=out_shape, scratch_shapes=scratch_shapes,
                              compiler_params=_cp(len(grid)))(*args)
    arrs = comm["arrs"]
    ns, n_in, n_out, n_scr = len(arrs), len(in_specs), len(out_specs), len(scratch_shapes)
    hbm = pl.BlockSpec(memory_space=pl.ANY)
    total = math.prod(grid)

    def carrier(*refs):
        ins, srcs = refs[:n_in], refs[n_in:n_in + ns]
        outs, dsts = refs[n_in + ns:n_in + ns + n_out], refs[n_in + ns + n_out:n_in + 2 * ns + n_out]
        scr, sems = refs[n_in + 2 * ns + n_out:n_in + 2 * ns + n_out + n_scr], refs[n_in + 2 * ns + n_out + n_scr:]
        step = pl.program_id(0)
        for ax in range(1, len(grid)):
            step = step * grid[ax] + pl.program_id(ax)
        start, mid, end = _comm_phases(comm["kind"], srcs, dsts, *sems)
        pl.when(step == 0)(start)
        body(*ins, *outs, *scr)
        pl.when(step == total // 2)(mid)
        pl.when(step == total - 1)(end)

    res = pl.pallas_call(
        carrier, name=name, grid=grid, in_specs=in_specs + [hbm] * ns, out_specs=out_specs + [hbm] * ns,
        out_shape=out_shape + _comm_out_shapes(comm["kind"], arrs),
        scratch_shapes=scratch_shapes + _comm_scratch(ns), compiler_params=_cp(len(grid)))(*args, *arrs)
    comm["recv"] = res[n_out:]
    return res[:n_out]


def _ffn_up(xb, w, name, comm=None):
    T, D = xb.shape
    F = w.shape[1] // 2
    tm = _tile(T, 512)

    def body(x_ref, w_ref, gu_ref, a_ref):
        x = x_ref[...]
        for j, cw in _chunks(F, 256):
            g = _dot(x, w_ref[:, j:j + cw])
            u = _dot(x, w_ref[:, F + j:F + j + cw])
            gu_ref[:, j:j + cw] = g.astype(BF16)
            gu_ref[:, F + j:F + j + cw] = u.astype(BF16)
            a_ref[:, j:j + cw] = (g * _sigmoid(g) * u).astype(BF16)

    return _pallas(
        body, name=name, grid=(T // tm,),
        in_specs=[_rows(tm, D), _res(w.shape)],
        out_specs=[_rows(tm, 2 * F), _rows(tm, F)],
        out_shape=[_sds((T, 2 * F), BF16), _sds((T, F), BF16)],
        args=(xb, w), comm=comm)


def _ffn_down_ln(a, wd, xres, g, b, name, comm=None):
    T, F = a.shape
    D = wd.shape[1]
    tm = _tile(T, 512)

    def body(a_ref, wd_ref, x_ref, g_ref, b_ref, r_ref, y_ref, yb_ref):
        r = ALPHA * x_ref[...] + 0.5 * _dot(a_ref[...], wd_ref[...])
        y = _ln_fwd(r, g_ref[...], b_ref[...])
        r_ref[...] = r
        y_ref[...] = y
        yb_ref[...] = y.astype(BF16)

    return _pallas(
        body, name=name, grid=(T // tm,),
        in_specs=[_rows(tm, F), _res(wd.shape), _rows(tm, D), _res((1, D)), _res((1, D))],
        out_specs=[_rows(tm, D), _rows(tm, D), _rows(tm, D)],
        out_shape=[_sds((T, D), F32), _sds((T, D), F32), _sds((T, D), BF16)],
        args=(a, wd, xres, g, b), comm=comm)


def _mix_in(hb, w, bias, d_rnn, name, comm=None):
    T, D = hb.shape
    R = d_rnn
    tm = _tile(T, 256)
    o_cv, o_cg, o_rx, o_rg, o_gc, o_gr = 0, D, 2 * D, 2 * D + R, 2 * D + 2 * R, 3 * D + 2 * R

    def body(x_ref, w_ref, b_ref, c_ref, cv_ref, cg_ref, rx_ref, rg_ref, gc_ref, gr_ref):
        x = x_ref[...]

        def seg(off, j, cw):
            return _dot(x, w_ref[:, off + j:off + j + cw]) + b_ref[:, off + j:off + j + cw]

        for j, cw in _chunks(D, 256):
            cv = seg(o_cv, j, cw)
            cg = seg(o_cg, j, cw)
            cv_ref[:, j:j + cw] = cv.astype(BF16)
            cg_ref[:, j:j + cw] = cg.astype(BF16)
            c_ref[:, j:j + cw] = cv * _sigmoid(cg)
            gc_ref[:, j:j + cw] = seg(o_gc, j, cw).astype(BF16)
            gr_ref[:, j:j + cw] = seg(o_gr, j, cw).astype(BF16)
        for j, cw in _chunks(R, 256):
            rx_ref[:, j:j + cw] = seg(o_rx, j, cw)
            rg_ref[:, j:j + cw] = seg(o_rg, j, cw).astype(BF16)

    return _pallas(
        body, name=name, grid=(T // tm,),
        in_specs=[_rows(tm, D), _res(w.shape), _res(bias.shape)],
        out_specs=[_rows(tm, D), _rows(tm, D), _rows(tm, D), _rows(tm, R), _rows(tm, R), _rows(tm, D),
                   _rows(tm, D)],
        out_shape=[_sds((T, D), F32), _sds((T, D), BF16), _sds((T, D), BF16), _sds((T, R), F32),
                   _sds((T, R), BF16), _sds((T, D), BF16), _sds((T, D), BF16)],
        args=(hb, w, bias), comm=comm)


def _cols(t, rows=None):
    return pl.BlockSpec((t if rows is None else rows, LANES), lambda j: (0, j))


def _gn_stats(cc):
    mu = jnp.mean(cc, axis=-1, keepdims=True)
    xc = cc - mu
    var = jnp.mean(xc * xc, axis=-1, keepdims=True)
    rstd = lax.rsqrt(var + LN_EPS)
    return xc * rstd, rstd


def _tap_groups(offsets):
    groups = {}
    for k, s in enumerate(offsets):
        groups.setdefault(s % SUBLANES, []).append((k, s - s % SUBLANES))
    return sorted(groups.items())


def _shifted(win, phase):
    return win if phase == 0 else pltpu.roll(win, win.shape[0] - phase, 0)


def _tap_sum(win, wv, groups, rows):
    parts, t = [None] * 4, 0
    for phase, taps in groups:
        sh = _shifted(win, phase)
        for k, off in taps:
            term = wv[k:k + 1, :] * sh[off:off + rows, :]
            parts[t % 4] = term if parts[t % 4] is None else parts[t % 4] + term
            t += 1
    return (parts[0] + parts[1]) + (parts[2] + parts[3])


def _conv31_gn(c, w, bias, gg, gb, name, comm=None):
    T, D = c.shape
    K = w.shape[0]
    B, P, N = CONV_BLOCK, CONV_PAD, _tile(T, NORM_ROWS)
    assert D % LANES == 0 and T % B == 0 and K - 1 <= P
    groups = _tap_groups([P - (K - 1) + k for k in range(K)])

    def body(c_ref, w_ref, b_ref, gg_ref, gb_ref, cc_ref, cs_ref, xpad):
        xpad[0:P, :] = jnp.zeros((P, LANES), F32)
        xpad[P:P + T, :] = c_ref[...]
        wv = w_ref[...]
        bv, ggv, gbv = b_ref[...], gg_ref[...], gb_ref[...]

        def conv_step(i, carry):
            base = pl.multiple_of(i * B, B)
            win = xpad[pl.ds(base, B + P), :]
            cc_ref[pl.ds(base, B), :] = _tap_sum(win, wv, groups, B) + bv
            return carry

        lax.fori_loop(0, T // B, conv_step, 0)

        def norm_step(i, carry):
            base = pl.multiple_of(i * N, N)
            xhat, _ = _gn_stats(cc_ref[pl.ds(base, N), :])
            gn = xhat * ggv + gbv
            cs_ref[pl.ds(base, N), :] = (gn * _sigmoid(gn)).astype(BF16)
            return carry

        lax.fori_loop(0, T // N, norm_step, 0)

    return _pallas(
        body, name=name, grid=(D // LANES,),
        in_specs=[_cols(T), _cols(T, K), _cols(T, 1), _cols(T, 1), _cols(T, 1)],
        out_specs=[_cols(T), _cols(T)],
        out_shape=[_sds((T, D), F32), _sds((T, D), BF16)],
        scratch_shapes=[pltpu.VMEM((T + P, LANES), F32)],
        args=(c, w, bias, gg, gb), comm=comm)


def _conv4(rx, w, bias, name):
    T, C = rx.shape
    K = w.shape[0]
    R = CONV_ROWS
    assert C % LANES == 0 and T % R == 0 and K - 1 <= RNN_PAD

    def body(x_ref, w_ref, b_ref, r_ref, xpad):
        xpad[0:RNN_PAD, :] = jnp.zeros((RNN_PAD, LANES), F32)
        xpad[RNN_PAD:RNN_PAD + T, :] = x_ref[...]
        wv, bv = w_ref[...], b_ref[...]

        def step(i, carry):
            base = pl.multiple_of(i * R, R)
            win = xpad[pl.ds(base, R + RNN_PAD), :]
            acc = jnp.zeros((R, LANES), F32)
            for k in range(K):
                s = RNN_PAD - (K - 1) + k
                acc = acc + wv[k:k + 1, :] * win[s:s + R, :]
            r_ref[pl.ds(base, R), :] = acc + bv
            return carry

        lax.fori_loop(0, T // R, step, 0)

    (r,) = _pallas(
        body, name=name, grid=(C // LANES,),
        in_specs=[_cols(T), _cols(T, K), _cols(T, 1)],
        out_specs=[_cols(T)],
        out_shape=[_sds((T, C), F32)],
        scratch_shapes=[pltpu.VMEM((T + RNN_PAD, LANES), F32)],
        args=(rx, w, bias))
    return r


def _gates(r, bda, bdx, b_a, b_x, lam, name, comm=None):
    T, C = r.shape
    tm = _tile(T, 512)

    def body(r_ref, wa_ref, wx_ref, ba_ref, bx_ref, lam_ref, ra_ref, ri_ref, a_ref, u_ref):
        rv = r_ref[...]
        rb = rv.astype(BF16)
        ra = _sigmoid(_dot(rb, wa_ref[...]) + ba_ref[...])
        ri = _sigmoid(_dot(rb, wx_ref[...]) + bx_ref[...])
        log_a = (-RG_LRU_C) * ra * _softplus_neg(lam_ref[...])
        ra_ref[...] = ra
        ri_ref[...] = ri
        a_ref[...] = jnp.exp(log_a)
        u_ref[...] = jnp.sqrt(-_expm1(2.0 * log_a)) * (ri * rv)

    return _pallas(
        body, name=name, grid=(T // tm,),
        in_specs=[_rows(tm, C), _res(bda.shape), _res(bdx.shape), _res((1, C)), _res((1, C)), _res((1, C))],
        out_specs=[_rows(tm, C)] * 4,
        out_shape=[_sds((T, C), F32)] * 4,
        args=(r, bda, bdx, b_a, b_x, lam), comm=comm)


def _scan_fwd(a, u, rg, name):
    T, C = a.shape
    tt = _tile(T, 512)

    def body(a_ref, u_ref, rg_ref, h_ref, hp_ref, hg_ref, carry):
        @pl.when(pl.program_id(0) == 0)
        def _():
            carry[...] = jnp.zeros_like(carry)

        row = lax.broadcasted_iota(jnp.int32, (SUBLANES, C), 0)

        def group(i, hprev):
            base = pl.multiple_of(i * SUBLANES, SUBLANES)
            av = a_ref[pl.ds(base, SUBLANES), :]
            uv = u_ref[pl.ds(base, SUBLANES), :]
            for s in (1, 2, 4):
                a_s = jnp.where(row >= s, pltpu.roll(av, s, 0), 1.0)
                u_s = jnp.where(row >= s, pltpu.roll(uv, s, 0), 0.0)
                uv = av * u_s + uv
                av = av * a_s
            h = av * hprev + uv
            h_ref[pl.ds(base, SUBLANES), :] = h
            hp_ref[pl.ds(base, SUBLANES), :] = jnp.where(row >= 1, pltpu.roll(h, 1, 0), hprev)
            return h[SUBLANES - 1:SUBLANES, :]

        carry[...] = lax.fori_loop(0, tt // SUBLANES, group, carry[...])
        gel, _ = _gelu(rg_ref[...].astype(F32))
        hg_ref[...] = (h_ref[...] * gel).astype(BF16)

    return _pallas(
        body, name=name, grid=(T // tt,),
        in_specs=[_rows(tt, C)] * 3,
        out_specs=[_rows(tt, C)] * 3,
        out_shape=[_sds((T, C), F32), _sds((T, C), F32), _sds((T, C), BF16)],
        scratch_shapes=[pltpu.VMEM((1, C), F32)],
        args=(a, u, rg))


def _mix_out_ln(cs, hg, gc, gr, hres, wcp, wrp, wout, g, b, name, comm=None):
    T, D = cs.shape
    C = hg.shape[1]
    tm = _tile(T, 512)

    def body(cs_ref, hg_ref, gc_ref, gr_ref, h_ref, wcp_ref, wrp_ref, wo_ref, g_ref, b_ref,
             yc_ref, yr_ref, m_ref, r_ref, y_ref, yb_ref):
        yc = _dot(cs_ref[...], wcp_ref[...])
        yr = _dot(hg_ref[...], wrp_ref[...])
        m = (_sigmoid(gc_ref[...].astype(F32)) * yc + _sigmoid(gr_ref[...].astype(F32)) * yr).astype(BF16)
        r = ALPHA * h_ref[...] + _dot(m, wo_ref[...])
        y = _ln_fwd(r, g_ref[...], b_ref[...])
        yc_ref[...] = yc.astype(BF16)
        yr_ref[...] = yr.astype(BF16)
        m_ref[...] = m
        r_ref[...] = r
        y_ref[...] = y
        yb_ref[...] = y.astype(BF16)

    return _pallas(
        body, name=name, grid=(T // tm,),
        in_specs=[_rows(tm, D), _rows(tm, C), _rows(tm, D), _rows(tm, D), _rows(tm, D), _res(wcp.shape),
                  _res(wrp.shape), _res(wout.shape), _res((1, D)), _res((1, D))],
        out_specs=[_rows(tm, D)] * 6,
        out_shape=[_sds((T, D), BF16), _sds((T, D), BF16), _sds((T, D), BF16), _sds((T, D), F32),
                   _sds((T, D), F32), _sds((T, D), BF16)],
        args=(cs, hg, gc, gr, hres, wcp, wrp, wout, g, b), comm=comm)


def _loss_ln_bwd(y, target, r, g, name):
    T, D = y.shape
    tm = _tile(T, 512)

    def body(y_ref, t_ref, r_ref, g_ref, loss_ref, dr_ref, drb_ref, dg_ref, db_ref):
        @pl.when(pl.program_id(0) == 0)
        def _():
            loss_ref[...] = jnp.zeros_like(loss_ref)
            dg_ref[...] = jnp.zeros_like(dg_ref)
            db_ref[...] = jnp.zeros_like(db_ref)

        e = y_ref[...] - t_ref[...]
        loss_ref[...] += (0.5 / D) * jnp.sum(e * e)
        dr, dg, db = _ln_bwd(e * (1.0 / D), r_ref[...], g_ref[...])
        dr_ref[...] = dr
        drb_ref[...] = dr.astype(BF16)
        dg_ref[...] += dg
        db_ref[...] += db

    return _pallas(
        body, name=name, grid=(T // tm,),
        in_specs=[_rows(tm, D), _rows(tm, D), _rows(tm, D), _res((1, D))],
        out_specs=[_acc((SUBLANES, LANES)), _rows(tm, D), _rows(tm, D), _acc((1, D)), _acc((1, D))],
        out_shape=[_sds((SUBLANES, LANES), F32), _sds((T, D), F32), _sds((T, D), BF16), _sds((1, D), F32),
                   _sds((1, D), F32)],
        args=(y, target, r, g))


def _ffn_bwd_a(drb, wd, gu, name, comm=None):
    T, D = drb.shape
    F = wd.shape[0]
    tm = _tile(T, 512)

    def body(d_ref, wd_ref, gu_ref, o_ref):
        d = d_ref[...]
        for j, cw in _chunks(F, 256):
            da = 0.5 * _dot_nt(d, wd_ref[j:j + cw, :])
            gt = gu_ref[:, j:j + cw].astype(F32)
            up = gu_ref[:, F + j:F + j + cw].astype(F32)
            sg = _sigmoid(gt)
            o_ref[:, j:j + cw] = (da * up * (sg * (1.0 + gt * (1.0 - sg)))).astype(BF16)
            o_ref[:, F + j:F + j + cw] = (da * (gt * sg)).astype(BF16)

    (dgu,) = _pallas(
        body, name=name, grid=(T // tm,),
        in_specs=[_rows(tm, D), _res(wd.shape), _rows(tm, 2 * F)],
        out_specs=[_rows(tm, 2 * F)],
        out_shape=[_sds((T, 2 * F), BF16)],
        args=(drb, wd, gu), comm=comm)
    return dgu


def _nt_res(du, w, dres, name, ln=None, colsum=False, comm=None):
    T, K = du.shape
    D = w.shape[0]
    tm = _tile(T, 256 if K > 6000 else 512)
    n_in = 3 + (2 if ln else 0)

    def body(*refs):
        du_ref, w_ref, dres_ref = refs[:3]
        outs = refs[n_in:]
        dy = ALPHA * dres_ref[...] + _dot_nt(du_ref[...], w_ref[...])
        if ln:
            r_ref, g_ref = refs[3:5]

            @pl.when(pl.program_id(0) == 0)
            def _():
                outs[2][...] = jnp.zeros_like(outs[2])
                outs[3][...] = jnp.zeros_like(outs[3])

            dr, dg, db = _ln_bwd(dy, r_ref[...], g_ref[...])
            outs[0][...] = dr
            outs[1][...] = dr.astype(BF16)
            outs[2][...] += dg
            outs[3][...] += db
        else:
            outs[0][...] = dy
        if colsum:
            cs_ref = outs[-1]

            @pl.when(pl.program_id(0) == 0)
            def _():
                cs_ref[...] = jnp.zeros_like(cs_ref)

            cs_ref[...] += jnp.sum(du_ref[...].astype(F32), axis=0, keepdims=True)

    in_specs = [_rows(tm, K), _res(w.shape), _rows(tm, D)]
    args = [du, w, dres]
    if ln:
        in_specs += [_rows(tm, D), _res((1, D))]
        args += list(ln)
        out_specs = [_rows(tm, D), _rows(tm, D), _acc((1, D)), _acc((1, D))]
        out_shape = [_sds((T, D), F32), _sds((T, D), BF16), _sds((1, D), F32), _sds((1, D), F32)]
    else:
        out_specs = [_rows(tm, D)]
        out_shape = [_sds((T, D), F32)]
    if colsum:
        out_specs.append(_acc((1, K)))
        out_shape.append(_sds((1, K), F32))
    return _pallas(body, name=name, grid=(T // tm,), in_specs=in_specs, out_specs=out_specs, out_shape=out_shape,
                   args=args, comm=comm)


def _mm_tn(x, dy, name, scale=1.0, comm=None):
    T, K = x.shape
    N = dy.shape[1]
    tt = _tile(T, 1024)
    tn = next((c for c in (512, 768) if N % c == 0), N)
    nt = T // tt

    def body(x_ref, dy_ref, o_ref):
        t = pl.program_id(1)

        @pl.when(t == 0)
        def _():
            o_ref[...] = jnp.zeros_like(o_ref)

        o_ref[...] += _dot_tn(x_ref[...].astype(BF16), dy_ref[...])
        if scale != 1.0:
            @pl.when(t == nt - 1)
            def _():
                o_ref[...] = o_ref[...] * scale

    (out,) = _pallas(
        body, name=name, grid=(N // tn, nt),
        in_specs=[pl.BlockSpec((tt, K), lambda j, t: (t, 0)), pl.BlockSpec((tt, tn), lambda j, t: (t, j))],
        out_specs=[pl.BlockSpec((K, tn), lambda j, t: (0, j))],
        out_shape=[_sds((K, N), F32)],
        args=(x, dy), comm=comm)
    return out


def _mix_bwd1(drb, wout, wcp, wrp, gc, gr, yc, yr, rg, h, name, comm=None):
    T, D = drb.shape
    C = rg.shape[1]
    tm = _tile(T, 512)

    def body(d_ref, wo_ref, wcp_ref, wrp_ref, gc_ref, gr_ref, yc_ref, yr_ref, rg_ref, h_ref,
             dyc_ref, dyr_ref, dgc_ref, dgr_ref, dcs_ref, dh_ref, drg_ref):
        dm = _dot_nt(d_ref[...], wo_ref[...])
        sc = _sigmoid(gc_ref[...].astype(F32))
        sr = _sigmoid(gr_ref[...].astype(F32))
        dyc = (dm * sc).astype(BF16)
        dyr = (dm * sr).astype(BF16)
        dyc_ref[...] = dyc
        dyr_ref[...] = dyr
        dgc_ref[...] = (dm * yc_ref[...].astype(F32) * sc * (1.0 - sc)).astype(BF16)
        dgr_ref[...] = (dm * yr_ref[...].astype(F32) * sr * (1.0 - sr)).astype(BF16)
        dcs_ref[...] = _dot_nt(dyc, wcp_ref[...])
        dhg = _dot_nt(dyr, wrp_ref[...])
        rgv = rg_ref[...].astype(F32)
        gel, t = _gelu(rgv)
        dh_ref[...] = dhg * gel
        drg_ref[...] = (dhg * h_ref[...] * _gelu_grad(rgv, t)).astype(BF16)

    return _pallas(
        body, name=name, grid=(T // tm,),
        in_specs=[_rows(tm, D), _res(wout.shape), _res(wcp.shape), _res(wrp.shape), _rows(tm, D), _rows(tm, D),
                  _rows(tm, D), _rows(tm, D), _rows(tm, C), _rows(tm, C)],
        out_specs=[_rows(tm, D)] * 5 + [_rows(tm, C)] * 2,
        out_shape=[_sds((T, D), BF16)] * 4 + [_sds((T, D), F32), _sds((T, C), F32), _sds((T, C), BF16)],
        args=(drb, wout, wcp, wrp, gc, gr, yc, yr, rg, h), comm=comm)


def _scan_bwd(dh, a, name):
    T, C = dh.shape
    tt = _tile(T, 512)
    nt = T // tt
    ng = tt // SUBLANES

    def body(d_ref, a_ref, g_ref, carry):
        @pl.when(pl.program_id(0) == 0)
        def _():
            carry[...] = jnp.zeros_like(carry)

        row = lax.broadcasted_iota(jnp.int32, (SUBLANES, C), 0)

        def group(j, enext):
            base = pl.multiple_of((ng - 1 - j) * SUBLANES, SUBLANES)
            av = a_ref[pl.ds(base, SUBLANES), :]
            dv = d_ref[pl.ds(base, SUBLANES), :]
            bv = av * dv
            for s in (1, 2, 4):
                keep = row < SUBLANES - s
                a_s = jnp.where(keep, pltpu.roll(av, SUBLANES - s, 0), 1.0)
                b_s = jnp.where(keep, pltpu.roll(bv, SUBLANES - s, 0), 0.0)
                bv = av * b_s + bv
                av = av * a_s
            e = av * enext + bv
            e_up = jnp.where(row < SUBLANES - 1, pltpu.roll(e, SUBLANES - 1, 0), enext)
            g_ref[pl.ds(base, SUBLANES), :] = dv + e_up
            return e[0:1, :]

        carry[...] = lax.fori_loop(0, ng, group, carry[...])

    rev = pl.BlockSpec((tt, C), lambda i: (nt - 1 - i, 0))
    (g,) = _pallas(
        body, name=name, grid=(nt,), in_specs=[rev, rev], out_specs=[rev],
        out_shape=[_sds((T, C), F32)],
        scratch_shapes=[pltpu.VMEM((1, C), F32)],
        args=(dh, a))
    return g


def _gates_bwd(g, hp, ra, ri, r, lam, bda, bdx, name, comm=None):
    T, C = g.shape
    tm = _tile(T, 512)
    nt = T // tm

    def body(g_ref, hp_ref, ra_ref, ri_ref, r_ref, lam_ref, wa_ref, wx_ref,
             dr_ref, dpa_ref, dpx_ref, dlam_ref, dba_ref, dbx_ref):
        @pl.when(pl.program_id(0) == 0)
        def _():
            dlam_ref[...] = jnp.zeros_like(dlam_ref)
            dba_ref[...] = jnp.zeros_like(dba_ref)
            dbx_ref[...] = jnp.zeros_like(dbx_ref)

        gv, rav, riv, rv, lamv = g_ref[...], ra_ref[...], ri_ref[...], r_ref[...], lam_ref[...]
        sp = _softplus_neg(lamv)
        log_a = (-RG_LRU_C) * rav * sp
        av = jnp.exp(log_a)
        mult = jnp.sqrt(-_expm1(2.0 * log_a))
        d_mult = gv * riv * rv
        d_i = gv * mult * rv
        d_loga = gv * hp_ref[...] * av - d_mult * (av * av) / mult
        d_ra = d_loga * ((-RG_LRU_C) * sp)
        dpa = d_ra * rav * (1.0 - rav)
        dpx = d_i * riv * (1.0 - riv)
        dpab = dpa.astype(BF16)
        dpxb = dpx.astype(BF16)
        dpa_ref[...] = dpab
        dpx_ref[...] = dpxb
        dr_ref[...] = gv * mult * riv + _dot_nt(dpab, wa_ref[...]) + _dot_nt(dpxb, wx_ref[...])
        dlam_ref[...] += jnp.sum(d_loga * ((-RG_LRU_C) * rav), axis=0, keepdims=True)
        dba_ref[...] += jnp.sum(dpa, axis=0, keepdims=True)
        dbx_ref[...] += jnp.sum(dpx, axis=0, keepdims=True)

        @pl.when(pl.program_id(0) == nt - 1)
        def _():
            dlam_ref[...] = dlam_ref[...] * (-_sigmoid(-lamv))

    return _pallas(
        body, name=name, grid=(nt,),
        in_specs=[_rows(tm, C)] * 5 + [_res((1, C)), _res(bda.shape), _res(bdx.shape)],
        out_specs=[_rows(tm, C)] * 3 + [_acc((1, C))] * 3,
        out_shape=[_sds((T, C), F32), _sds((T, C), BF16), _sds((T, C), BF16)] + [_sds((1, C), F32)] * 3,
        args=(g, hp, ra, ri, r, lam, bda, bdx), comm=comm)


def _conv4_bwd(dr, rx, w, name):
    T, C = dr.shape
    K = w.shape[0]
    R = CONV_ROWS
    P = RNN_PAD

    def body(d_ref, x_ref, w_ref, dx_ref, dw_ref, db_ref, dpad, xpad):
        dpad[0:T, :] = d_ref[...]
        dpad[T:T + P, :] = jnp.zeros((P, LANES), F32)
        xpad[0:P, :] = jnp.zeros((P, LANES), F32)
        xpad[P:P + T, :] = x_ref[...]
        dw_ref[...] = jnp.zeros_like(dw_ref)
        db_ref[...] = jnp.zeros_like(db_ref)
        wv = w_ref[...]

        def step(i, carry):
            base = pl.multiple_of(i * R, R)
            dwin = dpad[pl.ds(base, R + P), :]
            xwin = xpad[pl.ds(base, R + P), :]
            dcur = dwin[0:R, :]
            acc = jnp.zeros((R, LANES), F32)
            for k in range(K):
                acc = acc + wv[k:k + 1, :] * dwin[K - 1 - k:K - 1 - k + R, :]
                s = P - (K - 1) + k
                dw_ref[k:k + 1, :] += jnp.sum(dcur * xwin[s:s + R, :], axis=0, keepdims=True)
            dx_ref[pl.ds(base, R), :] = acc.astype(BF16)
            db_ref[...] += jnp.sum(dcur, axis=0, keepdims=True)
            return carry

        lax.fori_loop(0, T // R, step, 0)

    return _pallas(
        body, name=name, grid=(C // LANES,),
        in_specs=[_cols(T), _cols(T), _cols(T, K)],
        out_specs=[_cols(T), _cols(T, SUBLANES), _cols(T, 1)],
        out_shape=[_sds((T, C), BF16), _sds((SUBLANES, C), F32), _sds((1, C), F32)],
        scratch_shapes=[pltpu.VMEM((T + P, LANES), F32), pltpu.VMEM((T + P, LANES), F32)],
        args=(dr, rx, w))


def _conv31_bwd(dcs, cc, c, cv, cg, w, gg, gb, name, comm=None):
    T, D = dcs.shape
    K = w.shape[0]
    R, B, P = _tile(T, NORM_ROWS), CONV_BLOCK, CONV_PAD
    d_groups = _tap_groups([K - 1 - k for k in range(K)])
    x_groups = _tap_groups([P - (K - 1) + k for k in range(K)])

    def body(dcs_ref, cc_ref, c_ref, cv_ref, cg_ref, w_ref, gg_ref, gb_ref,
             dcv_ref, dcg_ref, dw_ref, db_ref, dgg_ref, dgb_ref, dpad, xpad, dwacc):
        dpad[T:T + P, :] = jnp.zeros((P, LANES), F32)
        xpad[0:P, :] = jnp.zeros((P, LANES), F32)
        xpad[P:P + T, :] = c_ref[...]
        dwacc[...] = jnp.zeros_like(dwacc)
        db_ref[...] = jnp.zeros_like(db_ref)
        dgg_ref[...] = jnp.zeros_like(dgg_ref)
        dgb_ref[...] = jnp.zeros_like(dgb_ref)
        wv, ggv, gbv = w_ref[...], gg_ref[...], gb_ref[...]

        def norm_step(i, carry):
            base = pl.multiple_of(i * R, R)
            xhat, rstd = _gn_stats(cc_ref[pl.ds(base, R), :])
            gn = xhat * ggv + gbv
            sg = _sigmoid(gn)
            dgn = dcs_ref[pl.ds(base, R), :] * (sg * (1.0 + gn * (1.0 - sg)))
            dgg_ref[...] += jnp.sum(dgn * xhat, axis=0, keepdims=True)
            dgb_ref[...] += jnp.sum(dgn, axis=0, keepdims=True)
            dxh = dgn * ggv
            m1 = jnp.mean(dxh, axis=-1, keepdims=True)
            m2 = jnp.mean(dxh * xhat, axis=-1, keepdims=True)
            dcc = rstd * (dxh - m1 - xhat * m2)
            dpad[pl.ds(base, R), :] = dcc
            db_ref[...] += jnp.sum(dcc, axis=0, keepdims=True)
            return carry

        lax.fori_loop(0, T // R, norm_step, 0)

        def conv_step(i, carry):
            base = pl.multiple_of(i * B, B)
            dwin = dpad[pl.ds(base, B + P), :]
            xwin = xpad[pl.ds(base, B + P), :]
            dcur = dwin[0:B, :]
            acc = _tap_sum(dwin, wv, d_groups, B)
            for phase, taps in x_groups:
                sh = _shifted(xwin, phase)
                for k, off in taps:
                    prod = dcur * sh[off:off + B, :]
                    part = prod[0:SUBLANES, :]
                    for q in range(1, B // SUBLANES):
                        part = part + prod[q * SUBLANES:(q + 1) * SUBLANES, :]
                    dwacc[k * SUBLANES:(k + 1) * SUBLANES, :] += part
            cgv = cg_ref[pl.ds(base, B), :].astype(F32)
            cvv = cv_ref[pl.ds(base, B), :].astype(F32)
            sg = _sigmoid(cgv)
            dcv_ref[pl.ds(base, B), :] = (acc * sg).astype(BF16)
            dcg_ref[pl.ds(base, B), :] = (acc * cvv * sg * (1.0 - sg)).astype(BF16)
            return carry

        lax.fori_loop(0, T // B, conv_step, 0)
        dw_ref[...] = jnp.zeros_like(dw_ref)
        for k in range(K):
            dw_ref[k:k + 1, :] = jnp.sum(dwacc[k * SUBLANES:(k + 1) * SUBLANES, :], axis=0, keepdims=True)

    return _pallas(
        body, name=name, grid=(D // LANES,),
        in_specs=[_cols(T)] * 5 + [_cols(T, K), _cols(T, 1), _cols(T, 1)],
        out_specs=[_cols(T), _cols(T), _cols(T, P), _cols(T, 1), _cols(T, 1), _cols(T, 1)],
        out_shape=[_sds((T, D), BF16), _sds((T, D), BF16), _sds((P, D), F32), _sds((1, D), F32),
                   _sds((1, D), F32), _sds((1, D), F32)],
        scratch_shapes=[pltpu.VMEM((T + P, LANES), F32), pltpu.VMEM((T + P, LANES), F32),
                        pltpu.VMEM((P * SUBLANES, LANES), F32)],
        args=(dcs, cc, c, cv, cg, w, gg, gb), comm=comm)


def _reduce_adamw(recvs, w, m, v, name):
    L, R, C = w.shape
    assert len(recvs) == L
    tr = next((c for c in (256, 176, 128, 64, 8) if R % c == 0), R)
    nr = R // tr
    c1 = 1.0 - ADAM_B1 ** ADAM_STEP
    c2 = 1.0 - ADAM_B2 ** ADAM_STEP

    def body(*refs):
        recv_refs = refs[:L]
        w_ref, m_ref, v_ref, g_ref, d_ref, mo_ref, vo_ref = refs[L:]

        def update(recv_ref):
            g = recv_ref[0].astype(F32)
            for k in range(1, N_DEV):
                g = g + recv_ref[k].astype(F32)
            mn = ADAM_B1 * m_ref[0] + (1.0 - ADAM_B1) * g
            vn = ADAM_B2 * v_ref[0] + (1.0 - ADAM_B2) * (g * g)
            g_ref[0] = g
            mo_ref[0] = mn
            vo_ref[0] = vn
            d_ref[0] = (-ADAM_LR) * ((mn / c1) / (jnp.sqrt(vn / c2) + ADAM_EPS) + ADAM_WD * w_ref[0])

        for l in range(L):
            pl.when(pl.program_id(0) == l)(lambda l=l: update(recv_refs[l]))

    def recv_spec(l):
        return pl.BlockSpec((N_DEV, tr, C), lambda j, i: (0, jnp.where(j == l, i, jnp.where(j < l, 0, nr - 1)), 0))

    blk = pl.BlockSpec((1, tr, C), lambda j, i: (j, i, 0))
    return _pallas(
        body, name=name, grid=(L, nr),
        in_specs=[recv_spec(l) for l in range(L)] + [blk, blk, blk],
        out_specs=[blk] * 4,
        out_shape=[_sds((L, R, C), F32)] * 4,
        args=(*recvs, w, m, v))


def _unshard(name, gathered):
    n, r, c = gathered.shape
    if name in COL_SHARDED:
        return gathered.transpose(1, 0, 2).reshape(r, n * c)
    return gathered.reshape(n * r, c)


def _to_shards(name, full):
    R, C = full.shape
    wire = BF16 if name in BF16_ON_WIRE else F32
    if name in COL_SHARDED:
        return full.reshape(R, N_DEV, C // N_DEV).transpose(1, 0, 2).astype(wire)
    return full.reshape(N_DEV, R // N_DEV, C).astype(wire)


def _block_diag(w):
    H, b, _ = w.shape
    eye = jnp.eye(H, dtype=w.dtype)
    return (w[:, :, None, :] * eye[:, None, :, None]).reshape(H * b, H * b)


def _diag_blocks(dense, H):
    b = dense.shape[0] // H
    eye = jnp.eye(H, dtype=dense.dtype)
    return jnp.sum(dense.reshape(H, b, H, b) * eye[:, None, :, None], axis=2)


def _pack(arrs, rows):
    flat = jnp.concatenate([a.reshape(-1) for a in arrs])
    return jnp.pad(flat, (0, rows * 1024 - flat.shape[0])).reshape(1, rows, 1024)


def _unpack(packed, shapes):
    flat = packed.reshape(-1)
    out, off = [], 0
    for s in shapes:
        n = math.prod(s)
        out.append(flat[off:off + n].reshape(s))
        off += n
    return out


class _Queue:
    def __init__(self, kind, us_per_mb):
        self.kind, self.us_per_mb, self.items, self.done = kind, us_per_mb, [], {}

    def push(self, key, arr):
        self.items.append((key, arr))

    def mb(self, arr):
        return _mbytes(arr) / (N_DEV if self.kind == "scatter" else 1)

    def take(self, micros):
        taken, budget = [], micros / self.us_per_mb
        while self.items and (not taken or self.mb(self.items[0][1]) <= budget):
            budget -= self.mb(self.items[0][1])
            taken.append(self.items.pop(0))
        return {"kind": self.kind, "keys": [k for k, _ in taken], "arrs": [a for _, a in taken]} if taken else None

    def landed(self, comm):
        if comm:
            self.done.update(zip(comm["keys"], comm["recv"]))

    def flush(self, name, upto=None):
        n = len(self.items)
        if upto is not None:
            keys = [k for k, _ in self.items]
            n = keys.index(upto) + 1 if upto in keys else 0
        if n:
            taken, self.items = self.items[:n], self.items[n:]
            self.done.update(zip([k for k, _ in taken], _exchange(self.kind, [a for _, a in taken], name)))


def kernel(x, ffn1_w_gu, ffn1_w_down, ln1_g, ln1_b, mix_w_in, mix_b_in, conv_dw_w, conv_dw_b, conv_gn_g, conv_gn_b, conv_w_proj, rnn_conv_w, rnn_conv_b, rnn_w_a, rnn_b_a, rnn_w_x, rnn_b_x, rnn_lambda, rnn_w_proj, mix_w_out, ln2_g, ln2_b, ffn2_w_gu, ffn2_w_down, ln3_g, ln3_b, loss_target, m_ffn1_w_gu, m_ffn1_w_down, m_ln1_g, m_ln1_b, m_mix_w_in, m_mix_b_in, m_conv_dw_w, m_conv_dw_b, m_conv_gn_g, m_conv_gn_b, m_conv_w_proj, m_rnn_conv_w, m_rnn_conv_b, m_rnn_w_a, m_rnn_b_a, m_rnn_w_x, m_rnn_b_x, m_rnn_lambda, m_rnn_w_proj, m_mix_w_out, m_ln2_g, m_ln2_b, m_ffn2_w_gu, m_ffn2_w_down, m_ln3_g, m_ln3_b, v_ffn1_w_gu, v_ffn1_w_down, v_ln1_g, v_ln1_b, v_mix_w_in, v_mix_b_in, v_conv_dw_w, v_conv_dw_b, v_conv_gn_g, v_conv_gn_b, v_conv_w_proj, v_rnn_conv_w, v_rnn_conv_b, v_rnn_w_a, v_rnn_b_a, v_rnn_w_x, v_rnn_b_x, v_rnn_lambda, v_rnn_w_proj, v_mix_w_out, v_ln2_g, v_ln2_b, v_ffn2_w_gu, v_ffn2_w_down, v_ln3_g, v_ln3_b):
    given = dict(locals())
    W = {n: given[n] for n in WEIGHTS}
    M = {n: given["m_" + n] for n in WEIGHTS}
    V = {n: given["v_" + n] for n in WEIGHTS}
    T, D = x.shape[1], x.shape[2]
    L = DEPTH
    x2 = x.reshape(T, D)
    target = loss_target.reshape(T, D)
    d_rnn = rnn_conv_b.shape[1]

    gather = _Queue("gather", GATHER_US_PER_MB)
    for l in range(L):
        for n in USE_ORDER:
            gather.push((n, l), W[n][l].astype(BF16) if n in BF16_ON_WIRE else W[n][l])
    gather.flush("gather_first", upto=("ffn1_w_gu", 0))
    full_cache = {}

    def full(n, l):
        if (n, l) not in full_cache:
            gather.flush(f"gather_{n}_{l}", upto=(n, l))
            full_cache[(n, l)] = _unshard(n, gather.done[(n, l)])
        return full_cache[(n, l)]

    def fwd_comm(micros):
        return gather.take(micros)

    bd_a = [_block_diag(rnn_w_a[l]).astype(BF16) for l in range(L)]
    bd_x = [_block_diag(rnn_w_x[l]).astype(BF16) for l in range(L)]

    def vec(name, l):
        return W[name][l:l + 1]

    saved = []
    h, hb = x2, x2.astype(BF16)
    for l in range(L):
        s = {"hb_in": hb}
        w_gu = full("ffn1_w_gu", l)
        comm = fwd_comm(105)
        s["gu1"], s["a1"] = _ffn_up(hb, w_gu, f"ffn1_up_{l}", comm=comm)
        gather.landed(comm)
        w_down = full("ffn1_w_down", l)
        comm = fwd_comm(65)
        s["r1"], y1, s["y1b"] = _ffn_down_ln(s["a1"], w_down, h, vec("ln1_g", l), vec("ln1_b", l),
                                              f"ffn1_down_ln_{l}", comm=comm)
        gather.landed(comm)
        w_in = full("mix_w_in", l)
        comm = fwd_comm(135)
        s["c"], s["cv"], s["cg"], s["rx"], s["rg"], s["gc"], s["gr"] = _mix_in(
            s["y1b"], w_in, vec("mix_b_in", l), d_rnn, f"mix_in_{l}", comm=comm)
        gather.landed(comm)
        w_dw = full("conv_dw_w", l)
        comm = fwd_comm(150)
        s["cc"], s["cs"] = _conv31_gn(s["c"], w_dw, vec("conv_dw_b", l), vec("conv_gn_g", l),
                                      vec("conv_gn_b", l), f"conv31_gn_{l}", comm=comm)
        gather.landed(comm)
        s["r"] = _conv4(s["rx"], full("rnn_conv_w", l), vec("rnn_conv_b", l), f"conv4_{l}")
        comm = fwd_comm(115)
        s["ra"], s["ri"], s["a"], uu = _gates(s["r"], bd_a[l], bd_x[l], vec("rnn_b_a", l), vec("rnn_b_x", l),
                                              vec("rnn_lambda", l), f"gates_{l}", comm=comm)
        gather.landed(comm)
        s["h"], s["hp"], s["hg"] = _scan_fwd(s["a"], uu, s["rg"], f"scan_{l}")
        w_cp, w_rp, w_out = full("conv_w_proj", l), full("rnn_w_proj", l), full("mix_w_out", l)
        comm = fwd_comm(85)
        s["yc"], s["yr"], s["m"], s["r2"], y2, s["y2b"] = _mix_out_ln(
            s["cs"], s["hg"], s["gc"], s["gr"], y1, w_cp, w_rp, w_out, vec("ln2_g", l), vec("ln2_b", l),
            f"mix_out_ln_{l}", comm=comm)
        gather.landed(comm)
        w_gu2 = full("ffn2_w_gu", l)
        comm = fwd_comm(105)
        s["gu2"], s["a2"] = _ffn_up(s["y2b"], w_gu2, f"ffn2_up_{l}", comm=comm)
        gather.landed(comm)
        w_down2 = full("ffn2_w_down", l)
        comm = fwd_comm(65)
        s["r3"], h, hb = _ffn_down_ln(s["a2"], w_down2, y2, vec("ln3_g", l), vec("ln3_b", l),
                                      f"ffn2_down_ln_{l}", comm=comm)
        gather.landed(comm)
        saved.append(s)

    scatter = _Queue("scatter", SCATTER_US_PER_MB)
    G = {n: [None] * L for n in WEIGHTS}

    def ready(n, l, grad):
        G[n][l] = grad
        scatter.push((n, l), _to_shards(n, grad))

    def bwd_comm(micros):
        return scatter.take(micros)

    loss_acc, dr3, drb3, G["ln3_g"][L - 1], G["ln3_b"][L - 1] = _loss_ln_bwd(
        h, target, saved[L - 1]["r3"], vec("ln3_g", L - 1), "loss_ln3_bwd")
    grad_x = small = None
    small_shapes = [W[n].shape for n in REPLICATED]
    small_rows = -(-sum(math.prod(s) for s in small_shapes) // (1024 * SUBLANES)) * SUBLANES
    for l in reversed(range(L)):
        s = saved[l]
        comm = bwd_comm(95)
        dgu2 = _ffn_bwd_a(drb3, full("ffn2_w_down", l), s["gu2"], f"ffn2_bwd_a_{l}", comm=comm)
        scatter.landed(comm)
        ready("ffn2_w_down", l, _mm_tn(s["a2"], drb3, f"ffn2_dw_down_{l}", scale=0.5))
        ready("ffn2_w_gu", l, _mm_tn(s["y2b"], dgu2, f"ffn2_dw_gu_{l}"))
        comm = bwd_comm(140)
        dr2, drb2, G["ln2_g"][l], G["ln2_b"][l] = _nt_res(
            dgu2, full("ffn2_w_gu", l), dr3, f"ffn2_bwd_x_{l}", ln=(s["r2"], vec("ln2_g", l)), comm=comm)
        scatter.landed(comm)
        comm = bwd_comm(120)
        dyc, dyr, dgc, dgr, dcs, dh, drg = _mix_bwd1(
            drb2, full("mix_w_out", l), full("conv_w_proj", l), full("rnn_w_proj", l), s["gc"], s["gr"], s["yc"],
            s["yr"], s["rg"], s["h"], f"mix_bwd_out_{l}", comm=comm)
        scatter.landed(comm)
        ready("mix_w_out", l, _mm_tn(s["m"], drb2, f"mix_dw_out_{l}"))
        ready("conv_w_proj", l, _mm_tn(s["cs"], dyc, f"conv_dw_proj_{l}"))
        ready("rnn_w_proj", l, _mm_tn(s["hg"], dyr, f"rnn_dw_proj_{l}"))
        gsc = _scan_bwd(dh, s["a"], f"scan_bwd_{l}")
        comm = bwd_comm(160)
        dr_, dpa, dpx, G["rnn_lambda"][l], G["rnn_b_a"][l], G["rnn_b_x"][l] = _gates_bwd(
            gsc, s["hp"], s["ra"], s["ri"], s["r"], vec("rnn_lambda", l), bd_a[l], bd_x[l], f"gates_bwd_{l}",
            comm=comm)
        scatter.landed(comm)
        G["rnn_w_a"][l] = _diag_blocks(_mm_tn(s["r"], dpa, f"rnn_dw_a_{l}"), RNN_BLOCKS)
        G["rnn_w_x"][l] = _diag_blocks(_mm_tn(s["r"], dpx, f"rnn_dw_x_{l}"), RNN_BLOCKS)
        drx, dw4, G["rnn_conv_b"][l] = _conv4_bwd(dr_, s["rx"], full("rnn_conv_w", l), f"conv4_bwd_{l}")
        ready("rnn_conv_w", l, dw4[:RNN_CONV_WIDTH])
        comm = bwd_comm(250)
        dcv, dcg, dw31, G["conv_dw_b"][l], G["conv_gn_g"][l], G["conv_gn_b"][l] = _conv31_bwd(
            dcs, s["cc"], s["c"], s["cv"], s["cg"], full("conv_dw_w", l), vec("conv_gn_g", l),
            vec("conv_gn_b", l), f"conv31_bwd_{l}", comm=comm)
        scatter.landed(comm)
        ready("conv_dw_w", l, dw31[:CONV_WIDTH])
        du = jnp.concatenate([dcv, dcg, drx, drg, dgc, dgr], axis=1)
        ready("mix_w_in", l, _mm_tn(s["y1b"], du, f"mix_dw_in_{l}"))
        comm = bwd_comm(175)
        dr1, drb1, G["ln1_g"][l], G["ln1_b"][l], G["mix_b_in"][l] = _nt_res(
            du, full("mix_w_in", l), dr2, f"mix_bwd_in_{l}", ln=(s["r1"], vec("ln1_g", l)), colsum=True, comm=comm)
        scatter.landed(comm)
        ready("ffn1_w_down", l, _mm_tn(s["a1"], drb1, f"ffn1_dw_down_{l}", scale=0.5))
        comm = bwd_comm(95)
        dgu1 = _ffn_bwd_a(drb1, full("ffn1_w_down", l), s["gu1"], f"ffn1_bwd_a_{l}", comm=comm)
        scatter.landed(comm)
        if l == 0:
            local_small = [jnp.stack([g.reshape(W[n].shape[1:]) for g in G[n]]) for n in REPLICATED]
            comm = {"kind": "gather", "arrs": [_pack(local_small, small_rows)[0]]}
            ready("ffn1_w_gu", l, _mm_tn(s["hb_in"], dgu1, f"ffn1_dw_gu_{l}", comm=comm))
            (small,) = comm["recv"]
        else:
            ready("ffn1_w_gu", l, _mm_tn(s["hb_in"], dgu1, f"ffn1_dw_gu_{l}"))
        if l > 0:
            comm = bwd_comm(130)
            dr3, drb3, G["ln3_g"][l - 1], G["ln3_b"][l - 1] = _nt_res(
                dgu1, full("ffn1_w_gu", l), dr1, f"ffn1_bwd_x_{l}", ln=(saved[l - 1]["r3"], vec("ln3_g", l - 1)),
                comm=comm)
        else:
            comm = bwd_comm(1e9)
            (grad_x,) = _nt_res(dgu1, full("ffn1_w_gu", l), dr1, f"ffn1_bwd_x_{l}", comm=comm)
        scatter.landed(comm)
    scatter.flush("scatter_rest")

    loss = lax.psum(loss_acc[0, 0], ("x", "y", "c"))

    out = {}
    for n in SHARDED:
        out[n] = _reduce_adamw([scatter.done[(n, l)] for l in range(L)], W[n], M[n], V[n], f"adamw_{n}")
    packed = _reduce_adamw([small], _pack([W[n] for n in REPLICATED], small_rows),
                           _pack([M[n] for n in REPLICATED], small_rows),
                           _pack([V[n] for n in REPLICATED], small_rows), "adamw_small")
    unpacked = [_unpack(p, small_shapes) for p in packed]
    for i, n in enumerate(REPLICATED):
        out[n] = tuple(u[i] for u in unpacked)

    return (loss, grad_x.reshape(x.shape), *[out[n][0] for n in WEIGHTS], *[out[n][1] for n in WEIGHTS],
            *[out[n][2] for n in WEIGHTS], *[out[n][3] for n in WEIGHTS])
```

```python
import math

import jax
import jax.numpy as jnp
from jax import lax
from jax.experimental import pallas as pl
from jax.experimental.pallas import tpu as pltpu

F32 = jnp.float32
BF16 = jnp.bfloat16
MESH = pl.DeviceIdType.MESH

DEPTH = 2
ALPHA = (2 * DEPTH) ** 0.25
LN_EPS = 1e-5
RG_LRU_C = 8.0
CONV_WIDTH = 31
RNN_CONV_WIDTH = 4
RNN_BLOCKS = 16
N_DEV = 8
ADAM_LR, ADAM_B1, ADAM_B2, ADAM_EPS, ADAM_WD, ADAM_STEP = 0.001, 0.9, 0.999, 1e-08, 0.01, 10

LANES = 128
SUBLANES = 8
VMEM_LIMIT = 56 * 1024 * 1024
CONV_PAD = 32
RNN_PAD = 8
CONV_ROWS = 128
NORM_ROWS = 256
CONV_BLOCK = 32
GATHER_US_PER_MB = 43.0
SCATTER_US_PER_MB = 86.0

WEIGHTS = ['ffn1_w_gu', 'ffn1_w_down', 'ln1_g', 'ln1_b', 'mix_w_in', 'mix_b_in', 'conv_dw_w', 'conv_dw_b',
           'conv_gn_g', 'conv_gn_b', 'conv_w_proj', 'rnn_conv_w', 'rnn_conv_b', 'rnn_w_a', 'rnn_b_a', 'rnn_w_x',
           'rnn_b_x', 'rnn_lambda', 'rnn_w_proj', 'mix_w_out', 'ln2_g', 'ln2_b', 'ffn2_w_gu', 'ffn2_w_down',
           'ln3_g', 'ln3_b']
COL_SHARDED = ['ffn1_w_gu', 'mix_w_in', 'ffn2_w_gu', 'conv_dw_w', 'rnn_conv_w']
ROW_SHARDED = ['ffn1_w_down', 'conv_w_proj', 'rnn_w_proj', 'mix_w_out', 'ffn2_w_down']
SHARDED = COL_SHARDED + ROW_SHARDED
BF16_ON_WIRE = ['ffn1_w_gu', 'mix_w_in', 'ffn2_w_gu', 'ffn1_w_down', 'conv_w_proj', 'rnn_w_proj', 'mix_w_out',
                'ffn2_w_down']
REPLICATED = [n for n in WEIGHTS if n not in SHARDED]
USE_ORDER = ['ffn1_w_gu', 'ffn1_w_down', 'mix_w_in', 'conv_dw_w', 'rnn_conv_w', 'conv_w_proj', 'rnn_w_proj',
             'mix_w_out', 'ffn2_w_gu', 'ffn2_w_down']


def _cp(n_axes=1):
    return pltpu.CompilerParams(dimension_semantics=("arbitrary",) * n_axes, vmem_limit_bytes=VMEM_LIMIT)


def _rows(tm, c):
    return pl.BlockSpec((tm, c), lambda i: (i, 0))


def _res(shape):
    nd = len(shape)
    return pl.BlockSpec(tuple(shape), lambda *_: (0,) * nd, pipeline_mode=pl.Buffered(1))


def _acc(shape):
    nd = len(shape)
    return pl.BlockSpec(tuple(shape), lambda *_: (0,) * nd)


def _tile(t, want):
    return want if t % want == 0 else t


def _sds(shape, dtype):
    return jax.ShapeDtypeStruct(tuple(shape), dtype)


def _mbytes(a):
    return a.size * a.dtype.itemsize / 1e6


def _ln_fwd(r, g, b):
    mu = jnp.mean(r, axis=-1, keepdims=True)
    xc = r - mu
    var = jnp.mean(xc * xc, axis=-1, keepdims=True)
    return xc * lax.rsqrt(var + LN_EPS) * g + b


def _ln_bwd(dy, r, g):
    mu = jnp.mean(r, axis=-1, keepdims=True)
    xc = r - mu
    var = jnp.mean(xc * xc, axis=-1, keepdims=True)
    rstd = lax.rsqrt(var + LN_EPS)
    xhat = xc * rstd
    dxh = dy * g
    m1 = jnp.mean(dxh, axis=-1, keepdims=True)
    m2 = jnp.mean(dxh * xhat, axis=-1, keepdims=True)
    dr = rstd * (dxh - m1 - xhat * m2)
    return dr, jnp.sum(dy * xhat, axis=0, keepdims=True), jnp.sum(dy, axis=0, keepdims=True)


def _sigmoid(x):
    return jax.nn.sigmoid(x)


_GELU_K = math.sqrt(2.0 / math.pi)


def _gelu(x):
    t = jnp.tanh(_GELU_K * (x + 0.044715 * x * x * x))
    return 0.5 * x * (1.0 + t), t


def _gelu_grad(x, t):
    return 0.5 * (1.0 + t) + 0.5 * x * (1.0 - t * t) * (_GELU_K * (1.0 + 3.0 * 0.044715 * x * x))


def _expm1(x):
    taylor = x * (1.0 + x * (0.5 + x * (1.0 / 6.0 + x * (1.0 / 24.0 + x * (1.0 / 120.0)))))
    return jnp.where(jnp.abs(x) < 0.03, taylor, jnp.exp(x) - 1.0)


def _softplus_neg(lam):
    return jnp.maximum(-lam, 0.0) + jnp.log1p(jnp.exp(-jnp.abs(lam)))


def _dot(a, b):
    return jnp.dot(a, b, preferred_element_type=F32)


def _dot_nt(a, b):
    return lax.dot_general(a, b, (((1,), (1,)), ((), ())), preferred_element_type=F32)


def _dot_tn(a, b):
    return lax.dot_general(a, b, (((0,), (0,)), ((), ())), preferred_element_type=F32)


def _chunks(width, cn):
    return [(j, min(cn, width - j)) for j in range(0, width, cn)]


def _band_chunks(width, block):
    out = []
    for c0, cw in _chunks(width, 256):
        lo = (c0 // block) * block
        hi = ((c0 + cw - 1) // block + 1) * block
        out.append((c0, cw, lo // LANES * LANES, min(width, -(-hi // LANES) * LANES)))
    return out


def _position():
    return lax.axis_index("x"), lax.axis_index("y"), lax.axis_index("c")


def _index(p):
    return 4 * p[0] + 2 * p[1] + p[2]


def _comm_out_shapes(kind, arrs):
    return [_sds((N_DEV,) + a.shape if kind == "gather" else a.shape, a.dtype) for a in arrs]


def _comm_scratch(n):
    return [pltpu.SemaphoreType.DMA((n, 7)), pltpu.SemaphoreType.DMA((n, 7)), pltpu.SemaphoreType.DMA((n,))]


def _comm_phases(kind, srcs, dsts, send_sems, recv_sems, local_sems):
    n = len(srcs)
    x, y, c = _position()
    me, sibling = (x, y, c), (x, y, 1 - c)
    chips = [(1 - x, y), (x, 1 - y), (1 - x, 1 - y)]

    if kind == "gather":
        def copy(a, k, block, to, src=None):
            dst = dsts[a].at[_index(block)]
            return pltpu.make_async_remote_copy(
                src_ref=dst if src is None else src, dst_ref=dst, send_sem=send_sems.at[a, k],
                recv_sem=recv_sems.at[a, k], device_id=to, device_id_type=MESH)

        def mine(a):
            return pltpu.make_async_copy(srcs[a], dsts[a].at[_index(me)], local_sems.at[a])

        def first(a):
            return [copy(a, 0, me, sibling, src=srcs[a])] + [
                copy(a, 1 + j, me, (*chip, c), src=srcs[a]) for j, chip in enumerate(chips)]

        def start():
            for a in range(n):
                mine(a).start()
            for a in range(n):
                for cp in first(a):
                    cp.start()

        def mid():
            for j, chip in enumerate(chips):
                for a in range(n):
                    copy(a, 1 + j, (*chip, c), me).wait_recv()
                    copy(a, 4 + j, (*chip, c), sibling).start()

        def end():
            for a in range(n):
                copy(a, 0, sibling, me).wait_recv()
            for j, chip in enumerate(chips):
                for a in range(n):
                    copy(a, 4 + j, (*chip, 1 - c), me).wait_recv()
            for a in range(n):
                for cp in first(a):
                    cp.wait_send()
                for j, chip in enumerate(chips):
                    copy(a, 4 + j, (*chip, c), sibling).wait_send()
                mine(a).wait()

        return start, mid, end

    def peer_of(k):
        return (1 - x if k & 4 else x, 1 - y if k & 2 else y, 1 - c if k & 1 else c)

    def own(a):
        return pltpu.make_async_copy(srcs[a].at[_index(me)], dsts[a].at[0], local_sems.at[a])

    def remote(a, k):
        peer = peer_of(k)
        return pltpu.make_async_remote_copy(
            src_ref=srcs[a].at[_index(peer)], dst_ref=dsts[a].at[k], send_sem=send_sems.at[a, k - 1],
            recv_sem=recv_sems.at[a, k - 1], device_id=peer, device_id_type=MESH)

    def start():
        for a in range(n):
            own(a).start()
        for k in range(1, N_DEV):
            for a in range(n):
                remote(a, k).start()

    def end():
        for k in range(1, N_DEV):
            for a in range(n):
                remote(a, k).wait()
        for a in range(n):
            own(a).wait()

    return start, (lambda: None), end


def _exchange(kind, arrs, name):
    n = len(arrs)
    hbm = pl.BlockSpec(memory_space=pl.ANY)

    def body(*refs):
        start, mid, end = _comm_phases(kind, refs[:n], refs[n:2 * n], *refs[2 * n:])
        start()
        mid()
        end()

    return pl.pallas_call(
        body, name=name, in_specs=[hbm] * n, out_specs=[hbm] * n, out_shape=_comm_out_shapes(kind, arrs),
        scratch_shapes=_comm_scratch(n))(*arrs)


def _pallas(body, *, name, grid, in_specs, out_specs, out_shape, args, scratch_shapes=(), comm=None):
    in_specs, out_specs, out_shape = list(in_specs), list(out_specs), list(out_shape)
    scratch_shapes = list(scratch_shapes)
    if not comm:
        return pl.pallas_call(body, name=name, grid=grid, in_specs=in_specs, out_specs=out_specs,
                              out_shape=out_shape, scratch_shapes=scratch_shapes,
                              compiler_params=_cp(len(grid)))(*args)
    arrs = comm["arrs"]
    ns, n_in, n_out, n_scr = len(arrs), len(in_specs), len(out_specs), len(scratch_shapes)
    hbm = pl.BlockSpec(memory_space=pl.ANY)
    total = math.prod(grid)

    def carrier(*refs):
        ins, srcs = refs[:n_in], refs[n_in:n_in + ns]
        outs, dsts = refs[n_in + ns:n_in + ns + n_out], refs[n_in + ns + n_out:n_in + 2 * ns + n_out]
        scr, sems = refs[n_in + 2 * ns + n_out:n_in + 2 * ns + n_out + n_scr], refs[n_in + 2 * ns + n_out + n_scr:]
        step = pl.program_id(0)
        for ax in range(1, len(grid)):
            step = step * grid[ax] + pl.program_id(ax)
        start, mid, end = _comm_phases(comm["kind"], srcs, dsts, *sems)
        pl.when(step == 0)(start)
        body(*ins, *outs, *scr)
        pl.when(step == total - 1)(mid)
        pl.when(step == total - 1)(end)

    res = pl.pallas_call(
        carrier, name=name, grid=grid, in_specs=in_specs + [hbm] * ns, out_specs=out_specs + [hbm] * ns,
        out_shape=out_shape + _comm_out_shapes(comm["kind"], arrs),
        scratch_shapes=scratch_shapes + _comm_scratch(ns), compiler_params=_cp(len(grid)))(*args, *arrs)
    comm["recv"] = res[n_out:]
    return res[:n_out]


def _ffn_up(xb, w, name, comm=None):
    T, D = xb.shape
    F = w.shape[1] // 2
    tm = _tile(T, 512)

    def body(x_ref, w_ref, gu_ref, a_ref):
        x = x_ref[...].astype(BF16)
        for j, cw in _chunks(F, 256):
            g = _dot(x, w_ref[:, j:j + cw])
            u = _dot(x, w_ref[:, F + j:F + j + cw])
            gu_ref[:, j:j + cw] = g.astype(BF16)
            gu_ref[:, F + j:F + j + cw] = u.astype(BF16)
            a_ref[:, j:j + cw] = (g * _sigmoid(g) * u).astype(BF16)

    return _pallas(
        body, name=name, grid=(T // tm,),
        in_specs=[_rows(tm, D), _res(w.shape)],
        out_specs=[_rows(tm, 2 * F), _rows(tm, F)],
        out_shape=[_sds((T, 2 * F), BF16), _sds((T, F), BF16)],
        args=(xb, w), comm=comm)


def _ffn_down_ln(a, wd, xres, g, b, name, comm=None):
    T, F = a.shape
    D = wd.shape[1]
    tm = _tile(T, 512)

    def body(a_ref, wd_ref, x_ref, g_ref, b_ref, r_ref, y_ref, yb_ref):
        r = ALPHA * x_ref[...] + 0.5 * _dot(a_ref[...], wd_ref[...])
        y = _ln_fwd(r, g_ref[...], b_ref[...])
        r_ref[...] = r
        y_ref[...] = y
        yb_ref[...] = y.astype(BF16)

    return _pallas(
        body, name=name, grid=(T // tm,),
        in_specs=[_rows(tm, F), _res(wd.shape), _rows(tm, D), _res((1, D)), _res((1, D))],
        out_specs=[_rows(tm, D), _rows(tm, D), _rows(tm, D)],
        out_shape=[_sds((T, D), F32), _sds((T, D), F32), _sds((T, D), BF16)],
        args=(a, wd, xres, g, b), comm=comm)


def _mix_in(hb, w, bias, d_rnn, name, comm=None):
    T, D = hb.shape
    R = d_rnn
    tm = _tile(T, 256)
    o_cv, o_cg, o_rx, o_rg, o_gc, o_gr = 0, D, 2 * D, 2 * D + R, 2 * D + 2 * R, 3 * D + 2 * R

    def body(x_ref, w_ref, b_ref, c_ref, cv_ref, cg_ref, rx_ref, rg_ref, gc_ref, gr_ref):
        x = x_ref[...]

        def seg(off, j, cw):
            return _dot(x, w_ref[:, off + j:off + j + cw]) + b_ref[:, off + j:off + j + cw]

        for j, cw in _chunks(D, 256):
            cv = seg(o_cv, j, cw)
            cg = seg(o_cg, j, cw)
            cv_ref[:, j:j + cw] = cv.astype(BF16)
            cg_ref[:, j:j + cw] = cg.astype(BF16)
            c_ref[:, j:j + cw] = cv * _sigmoid(cg)
            gc_ref[:, j:j + cw] = seg(o_gc, j, cw).astype(BF16)
            gr_ref[:, j:j + cw] = seg(o_gr, j, cw).astype(BF16)
        for j, cw in _chunks(R, 256):
            rx_ref[:, j:j + cw] = seg(o_rx, j, cw)
            rg_ref[:, j:j + cw] = seg(o_rg, j, cw).astype(BF16)

    return _pallas(
        body, name=name, grid=(T // tm,),
        in_specs=[_rows(tm, D), _res(w.shape), _res(bias.shape)],
        out_specs=[_rows(tm, D), _rows(tm, D), _rows(tm, D), _rows(tm, R), _rows(tm, R), _rows(tm, D),
                   _rows(tm, D)],
        out_shape=[_sds((T, D), F32), _sds((T, D), BF16), _sds((T, D), BF16), _sds((T, R), F32),
                   _sds((T, R), BF16), _sds((T, D), BF16), _sds((T, D), BF16)],
        args=(hb, w, bias), comm=comm)


def _cols(t, rows=None):
    return pl.BlockSpec((t if rows is None else rows, LANES), lambda j: (0, j))


def _gn_stats(cc):
    mu = jnp.mean(cc, axis=-1, keepdims=True)
    xc = cc - mu
    var = jnp.mean(xc * xc, axis=-1, keepdims=True)
    rstd = lax.rsqrt(var + LN_EPS)
    return xc * rstd, rstd


def _tap_groups(offsets):
    groups = {}
    for k, s in enumerate(offsets):
        groups.setdefault(s % SUBLANES, []).append((k, s - s % SUBLANES))
    return sorted(groups.items())


def _shifted(win, phase):
    return win if phase == 0 else pltpu.roll(win, win.shape[0] - phase, 0)


def _tap_sum(win, wv, groups, rows):
    parts, t = [None] * 4, 0
    for phase, taps in groups:
        sh = _shifted(win, phase)
        for k, off in taps:
            term = wv[k:k + 1, :] * sh[off:off + rows, :]
            parts[t % 4] = term if parts[t % 4] is None else parts[t % 4] + term
            t += 1
    return (parts[0] + parts[1]) + (parts[2] + parts[3])


def _conv31_gn(c, w, bias, gg, gb, name, comm=None):
    T, D = c.shape
    K = w.shape[0]
    B, P, N = CONV_BLOCK, CONV_PAD, _tile(T, NORM_ROWS)
    assert D % LANES == 0 and T % B == 0 and K - 1 <= P
    groups = _tap_groups([P - (K - 1) + k for k in range(K)])

    def body(c_ref, w_ref, b_ref, gg_ref, gb_ref, cc_ref, cs_ref, xpad):
        xpad[0:P, :] = jnp.zeros((P, LANES), F32)
        xpad[P:P + T, :] = c_ref[...]
        wv = w_ref[...]
        bv, ggv, gbv = b_ref[...], gg_ref[...], gb_ref[...]

        def conv_step(i, carry):
            base = pl.multiple_of(i * B, B)
            win = xpad[pl.ds(base, B + P), :]
            cc_ref[pl.ds(base, B), :] = _tap_sum(win, wv, groups, B) + bv
            return carry

        lax.fori_loop(0, T // B, conv_step, 0)

        def norm_step(i, carry):
            base = pl.multiple_of(i * N, N)
            xhat, _ = _gn_stats(cc_ref[pl.ds(base, N), :])
            gn = xhat * ggv + gbv
            cs_ref[pl.ds(base, N), :] = (gn * _sigmoid(gn)).astype(BF16)
            return carry

        lax.fori_loop(0, T // N, norm_step, 0)

    return _pallas(
        body, name=name, grid=(D // LANES,),
        in_specs=[_cols(T), _cols(T, K), _cols(T, 1), _cols(T, 1), _cols(T, 1)],
        out_specs=[_cols(T), _cols(T)],
        out_shape=[_sds((T, D), F32), _sds((T, D), BF16)],
        scratch_shapes=[pltpu.VMEM((T + P, LANES), F32)],
        args=(c, w, bias, gg, gb), comm=comm)


def _conv4(rx, w, bias, name):
    T, C = rx.shape
    K = w.shape[0]
    R = CONV_ROWS
    assert C % LANES == 0 and T % R == 0 and K - 1 <= RNN_PAD

    def body(x_ref, w_ref, b_ref, r_ref, xpad):
        xpad[0:RNN_PAD, :] = jnp.zeros((RNN_PAD, LANES), F32)
        xpad[RNN_PAD:RNN_PAD + T, :] = x_ref[...]
        wv, bv = w_ref[...], b_ref[...]

        def step(i, carry):
            base = pl.multiple_of(i * R, R)
            win = xpad[pl.ds(base, R + RNN_PAD), :]
            acc = jnp.zeros((R, LANES), F32)
            for k in range(K):
                s = RNN_PAD - (K - 1) + k
                acc = acc + wv[k:k + 1, :] * win[s:s + R, :]
            r_ref[pl.ds(base, R), :] = acc + bv
            return carry

        lax.fori_loop(0, T // R, step, 0)

    (r,) = _pallas(
        body, name=name, grid=(C // LANES,),
        in_specs=[_cols(T), _cols(T, K), _cols(T, 1)],
        out_specs=[_cols(T)],
        out_shape=[_sds((T, C), F32)],
        scratch_shapes=[pltpu.VMEM((T + RNN_PAD, LANES), F32)],
        args=(rx, w, bias))
    return r


def _gates(r, bda, bdx, b_a, b_x, lam, name, comm=None):
    T, C = r.shape
    tm = _tile(T, 512)
    chunks = _band_chunks(C, C // RNN_BLOCKS)

    def body(r_ref, wa_ref, wx_ref, ba_ref, bx_ref, lam_ref, ra_ref, ri_ref, a_ref, u_ref):
        for c0, cw, k0, k1 in chunks:
            cols = slice(c0, c0 + cw)
            rb = r_ref[:, k0:k1].astype(BF16)
            ra = _sigmoid(_dot(rb, wa_ref[k0:k1, cols]) + ba_ref[:, cols])
            ri = _sigmoid(_dot(rb, wx_ref[k0:k1, cols]) + bx_ref[:, cols])
            log_a = (-RG_LRU_C) * ra * _softplus_neg(lam_ref[:, cols])
            ra_ref[:, cols] = ra
            ri_ref[:, cols] = ri
            a_ref[:, cols] = jnp.exp(log_a)
            u_ref[:, cols] = jnp.sqrt(-_expm1(2.0 * log_a)) * (ri * r_ref[:, cols])

    return _pallas(
        body, name=name, grid=(T // tm,),
        in_specs=[_rows(tm, C), _res(bda.shape), _res(bdx.shape), _res((1, C)), _res((1, C)), _res((1, C))],
        out_specs=[_rows(tm, C)] * 4,
        out_shape=[_sds((T, C), F32)] * 4,
        args=(r, bda, bdx, b_a, b_x, lam), comm=comm)


def _scan_fwd(a, u, rg, name):
    T, C = a.shape
    tt = _tile(T, 512)

    def body(a_ref, u_ref, rg_ref, h_ref, hp_ref, hg_ref, carry):
        @pl.when(pl.program_id(0) == 0)
        def _():
            carry[...] = jnp.zeros_like(carry)

        row = lax.broadcasted_iota(jnp.int32, (SUBLANES, C), 0)

        def group(i, hprev):
            base = pl.multiple_of(i * SUBLANES, SUBLANES)
            av = a_ref[pl.ds(base, SUBLANES), :]
            uv = u_ref[pl.ds(base, SUBLANES), :]
            for s in (1, 2, 4):
                a_s = jnp.where(row >= s, pltpu.roll(av, s, 0), 1.0)
                u_s = jnp.where(row >= s, pltpu.roll(uv, s, 0), 0.0)
                uv = av * u_s + uv
                av = av * a_s
            h = av * hprev + uv
            h_ref[pl.ds(base, SUBLANES), :] = h
            hp_ref[pl.ds(base, SUBLANES), :] = jnp.where(row >= 1, pltpu.roll(h, 1, 0), hprev)
            return h[SUBLANES - 1:SUBLANES, :]

        carry[...] = lax.fori_loop(0, tt // SUBLANES, group, carry[...])
        gel, _ = _gelu(rg_ref[...].astype(F32))
        hg_ref[...] = (h_ref[...] * gel).astype(BF16)

    return _pallas(
        body, name=name, grid=(T // tt,),
        in_specs=[_rows(tt, C)] * 3,
        out_specs=[_rows(tt, C)] * 3,
        out_shape=[_sds((T, C), F32), _sds((T, C), F32), _sds((T, C), BF16)],
        scratch_shapes=[pltpu.VMEM((1, C), F32)],
        args=(a, u, rg))


def _mix_out_ln(cs, hg, gc, gr, hres, wcp, wrp, wout, g, b, name, comm=None):
    T, D = cs.shape
    C = hg.shape[1]
    tm = _tile(T, 512)

    def body(cs_ref, hg_ref, gc_ref, gr_ref, h_ref, wcp_ref, wrp_ref, wo_ref, g_ref, b_ref,
             yc_ref, yr_ref, m_ref, r_ref, y_ref, yb_ref):
        yc = _dot(cs_ref[...], wcp_ref[...])
        yr = _dot(hg_ref[...], wrp_ref[...])
        m = (_sigmoid(gc_ref[...].astype(F32)) * yc + _sigmoid(gr_ref[...].astype(F32)) * yr).astype(BF16)
        r = ALPHA * h_ref[...] + _dot(m, wo_ref[...])
        y = _ln_fwd(r, g_ref[...], b_ref[...])
        yc_ref[...] = yc.astype(BF16)
        yr_ref[...] = yr.astype(BF16)
        m_ref[...] = m
        r_ref[...] = r
        y_ref[...] = y
        yb_ref[...] = y.astype(BF16)

    return _pallas(
        body, name=name, grid=(T // tm,),
        in_specs=[_rows(tm, D), _rows(tm, C), _rows(tm, D), _rows(tm, D), _rows(tm, D), _res(wcp.shape),
                  _res(wrp.shape), _res(wout.shape), _res((1, D)), _res((1, D))],
        out_specs=[_rows(tm, D)] * 6,
        out_shape=[_sds((T, D), BF16), _sds((T, D), BF16), _sds((T, D), BF16), _sds((T, D), F32),
                   _sds((T, D), F32), _sds((T, D), BF16)],
        args=(cs, hg, gc, gr, hres, wcp, wrp, wout, g, b), comm=comm)


def _loss_ln_bwd(y, target, r, g, name):
    T, D = y.shape
    tm = _tile(T, 512)

    def body(y_ref, t_ref, r_ref, g_ref, loss_ref, dr_ref, drb_ref, dg_ref, db_ref):
        @pl.when(pl.program_id(0) == 0)
        def _():
            loss_ref[...] = jnp.zeros_like(loss_ref)
            dg_ref[...] = jnp.zeros_like(dg_ref)
            db_ref[...] = jnp.zeros_like(db_ref)

        e = y_ref[...] - t_ref[...]
        loss_ref[...] += (0.5 / D) * jnp.sum(e * e)
        dr, dg, db = _ln_bwd(e * (1.0 / D), r_ref[...], g_ref[...])
        dr_ref[...] = dr
        drb_ref[...] = dr.astype(BF16)
        dg_ref[...] += dg
        db_ref[...] += db

    return _pallas(
        body, name=name, grid=(T // tm,),
        in_specs=[_rows(tm, D), _rows(tm, D), _rows(tm, D), _res((1, D))],
        out_specs=[_acc((SUBLANES, LANES)), _rows(tm, D), _rows(tm, D), _acc((1, D)), _acc((1, D))],
        out_shape=[_sds((SUBLANES, LANES), F32), _sds((T, D), F32), _sds((T, D), BF16), _sds((1, D), F32),
                   _sds((1, D), F32)],
        args=(y, target, r, g))


def _ffn_bwd_a(drb, wd, gu, name, comm=None):
    T, D = drb.shape
    F = wd.shape[0]
    tm = _tile(T, 512)

    def body(d_ref, wd_ref, gu_ref, o_ref):
        d = d_ref[...]
        for j, cw in _chunks(F, 256):
            da = 0.5 * _dot_nt(d, wd_ref[j:j + cw, :])
            gt = gu_ref[:, j:j + cw].astype(F32)
            up = gu_ref[:, F + j:F + j + cw].astype(F32)
            sg = _sigmoid(gt)
            o_ref[:, j:j + cw] = (da * up * (sg * (1.0 + gt * (1.0 - sg)))).astype(BF16)
            o_ref[:, F + j:F + j + cw] = (da * (gt * sg)).astype(BF16)

    (dgu,) = _pallas(
        body, name=name, grid=(T // tm,),
        in_specs=[_rows(tm, D), _res(wd.shape), _rows(tm, 2 * F)],
        out_specs=[_rows(tm, 2 * F)],
        out_shape=[_sds((T, 2 * F), BF16)],
        args=(drb, wd, gu), comm=comm)
    return dgu


def _nt_res(du, w, dres, name, ln=None, colsum=False, comm=None):
    T, K = du.shape
    D = w.shape[0]
    tm = _tile(T, 256 if K > 6000 else 512)
    n_in = 3 + (2 if ln else 0)

    def body(*refs):
        du_ref, w_ref, dres_ref = refs[:3]
        outs = refs[n_in:]
        dy = ALPHA * dres_ref[...] + _dot_nt(du_ref[...], w_ref[...])
        if ln:
            r_ref, g_ref = refs[3:5]

            @pl.when(pl.program_id(0) == 0)
            def _():
                outs[2][...] = jnp.zeros_like(outs[2])
                outs[3][...] = jnp.zeros_like(outs[3])

            dr, dg, db = _ln_bwd(dy, r_ref[...], g_ref[...])
            outs[0][...] = dr
            outs[1][...] = dr.astype(BF16)
            outs[2][...] += dg
            outs[3][...] += db
        else:
            outs[0][...] = dy
        if colsum:
            cs_ref = outs[-1]

            @pl.when(pl.program_id(0) == 0)
            def _():
                cs_ref[...] = jnp.zeros_like(cs_ref)

            cs_ref[...] += jnp.sum(du_ref[...].astype(F32), axis=0, keepdims=True)

    in_specs = [_rows(tm, K), _res(w.shape), _rows(tm, D)]
    args = [du, w, dres]
    if ln:
        in_specs += [_rows(tm, D), _res((1, D))]
        args += list(ln)
        out_specs = [_rows(tm, D), _rows(tm, D), _acc((1, D)), _acc((1, D))]
        out_shape = [_sds((T, D), F32), _sds((T, D), BF16), _sds((1, D), F32), _sds((1, D), F32)]
    else:
        out_specs = [_rows(tm, D)]
        out_shape = [_sds((T, D), F32)]
    if colsum:
        out_specs.append(_acc((1, K)))
        out_shape.append(_sds((1, K), F32))
    return _pallas(body, name=name, grid=(T // tm,), in_specs=in_specs, out_specs=out_specs, out_shape=out_shape,
                   args=args, comm=comm)


def _mm_tn(x, dy, name, scale=1.0, comm=None):
    T, K = x.shape
    N = dy.shape[1]
    tt = _tile(T, 1024)
    tn = next((c for c in (512, 768) if N % c == 0), N)
    nt = T // tt

    def body(x_ref, dy_ref, o_ref):
        t = pl.program_id(1)

        @pl.when(t == 0)
        def _():
            o_ref[...] = jnp.zeros_like(o_ref)

        o_ref[...] += _dot_tn(x_ref[...].astype(BF16), dy_ref[...])
        if scale != 1.0:
            @pl.when(t == nt - 1)
            def _():
                o_ref[...] = o_ref[...] * scale

    (out,) = _pallas(
        body, name=name, grid=(N // tn, nt),
        in_specs=[pl.BlockSpec((tt, K), lambda j, t: (t, 0)), pl.BlockSpec((tt, tn), lambda j, t: (t, j))],
        out_specs=[pl.BlockSpec((K, tn), lambda j, t: (0, j))],
        out_shape=[_sds((K, N), F32)],
        args=(x, dy), comm=comm)
    return out


def _mix_bwd1(drb, wout, wcp, wrp, gc, gr, yc, yr, rg, h, name, comm=None):
    T, D = drb.shape
    C = rg.shape[1]
    tm = _tile(T, 512)

    def body(d_ref, wo_ref, wcp_ref, wrp_ref, gc_ref, gr_ref, yc_ref, yr_ref, rg_ref, h_ref,
             dyc_ref, dyr_ref, dgc_ref, dgr_ref, dcs_ref, dh_ref, drg_ref):
        dm = _dot_nt(d_ref[...], wo_ref[...])
        sc = _sigmoid(gc_ref[...].astype(F32))
        sr = _sigmoid(gr_ref[...].astype(F32))
        dyc = (dm * sc).astype(BF16)
        dyr = (dm * sr).astype(BF16)
        dyc_ref[...] = dyc
        dyr_ref[...] = dyr
        dgc_ref[...] = (dm * yc_ref[...].astype(F32) * sc * (1.0 - sc)).astype(BF16)
        dgr_ref[...] = (dm * yr_ref[...].astype(F32) * sr * (1.0 - sr)).astype(BF16)
        dcs_ref[...] = _dot_nt(dyc, wcp_ref[...])
        dhg = _dot_nt(dyr, wrp_ref[...])
        rgv = rg_ref[...].astype(F32)
        gel, t = _gelu(rgv)
        dh_ref[...] = dhg * gel
        drg_ref[...] = (dhg * h_ref[...] * _gelu_grad(rgv, t)).astype(BF16)

    return _pallas(
        body, name=name, grid=(T // tm,),
        in_specs=[_rows(tm, D), _res(wout.shape), _res(wcp.shape), _res(wrp.shape), _rows(tm, D), _rows(tm, D),
                  _rows(tm, D), _rows(tm, D), _rows(tm, C), _rows(tm, C)],
        out_specs=[_rows(tm, D)] * 5 + [_rows(tm, C)] * 2,
        out_shape=[_sds((T, D), BF16)] * 4 + [_sds((T, D), F32), _sds((T, C), F32), _sds((T, C), BF16)],
        args=(drb, wout, wcp, wrp, gc, gr, yc, yr, rg, h), comm=comm)


def _scan_bwd(dh, a, name):
    T, C = dh.shape
    tt = _tile(T, 512)
    nt = T // tt
    ng = tt // SUBLANES

    def body(d_ref, a_ref, g_ref, carry):
        @pl.when(pl.program_id(0) == 0)
        def _():
            carry[...] = jnp.zeros_like(carry)

        row = lax.broadcasted_iota(jnp.int32, (SUBLANES, C), 0)

        def group(j, enext):
            base = pl.multiple_of((ng - 1 - j) * SUBLANES, SUBLANES)
            av = a_ref[pl.ds(base, SUBLANES), :]
            dv = d_ref[pl.ds(base, SUBLANES), :]
            bv = av * dv
            for s in (1, 2, 4):
                keep = row < SUBLANES - s
                a_s = jnp.where(keep, pltpu.roll(av, SUBLANES - s, 0), 1.0)
                b_s = jnp.where(keep, pltpu.roll(bv, SUBLANES - s, 0), 0.0)
                bv = av * b_s + bv
                av = av * a_s
            e = av * enext + bv
            e_up = jnp.where(row < SUBLANES - 1, pltpu.roll(e, SUBLANES - 1, 0), enext)
            g_ref[pl.ds(base, SUBLANES), :] = dv + e_up
            return e[0:1, :]

        carry[...] = lax.fori_loop(0, ng, group, carry[...])

    rev = pl.BlockSpec((tt, C), lambda i: (nt - 1 - i, 0))
    (g,) = _pallas(
        body, name=name, grid=(nt,), in_specs=[rev, rev], out_specs=[rev],
        out_shape=[_sds((T, C), F32)],
        scratch_shapes=[pltpu.VMEM((1, C), F32)],
        args=(dh, a))
    return g


def _gates_bwd(g, hp, ra, ri, r, lam, bda, bdx, name, comm=None):
    T, C = g.shape
    tm = _tile(T, 512)
    nt = T // tm
    chunks = _band_chunks(C, C // RNN_BLOCKS)

    def body(g_ref, hp_ref, ra_ref, ri_ref, r_ref, lam_ref, wa_ref, wx_ref,
             dr_ref, dpa_ref, dpx_ref, dlam_ref, dba_ref, dbx_ref):
        @pl.when(pl.program_id(0) == 0)
        def _():
            dlam_ref[...] = jnp.zeros_like(dlam_ref)
            dba_ref[...] = jnp.zeros_like(dba_ref)
            dbx_ref[...] = jnp.zeros_like(dbx_ref)

        for c0, cw, _, _ in chunks:
            cols = slice(c0, c0 + cw)
            gv, rav, riv, rv = g_ref[:, cols], ra_ref[:, cols], ri_ref[:, cols], r_ref[:, cols]
            sp = _softplus_neg(lam_ref[:, cols])
            log_a = (-RG_LRU_C) * rav * sp
            av = jnp.exp(log_a)
            mult = jnp.sqrt(-_expm1(2.0 * log_a))
            d_mult = gv * riv * rv
            d_i = gv * mult * rv
            d_loga = gv * hp_ref[:, cols] * av - d_mult * (av * av) / mult
            d_ra = d_loga * ((-RG_LRU_C) * sp)
            dpa = d_ra * rav * (1.0 - rav)
            dpx = d_i * riv * (1.0 - riv)
            dpa_ref[:, cols] = dpa.astype(BF16)
            dpx_ref[:, cols] = dpx.astype(BF16)
            dr_ref[:, cols] = gv * mult * riv
            dlam_ref[:, cols] += jnp.sum(d_loga * ((-RG_LRU_C) * rav), axis=0, keepdims=True)
            dba_ref[:, cols] += jnp.sum(dpa, axis=0, keepdims=True)
            dbx_ref[:, cols] += jnp.sum(dpx, axis=0, keepdims=True)
        for c0, cw, k0, k1 in chunks:
            cols = slice(c0, c0 + cw)
            dr_ref[:, k0:k1] += (_dot_nt(dpa_ref[:, cols], wa_ref[k0:k1, cols])
                                 + _dot_nt(dpx_ref[:, cols], wx_ref[k0:k1, cols]))

        @pl.when(pl.program_id(0) == nt - 1)
        def _():
            dlam_ref[...] = dlam_ref[...] * (-_sigmoid(-lam_ref[...]))

    return _pallas(
        body, name=name, grid=(nt,),
        in_specs=[_rows(tm, C)] * 5 + [_res((1, C)), _res(bda.shape), _res(bdx.shape)],
        out_specs=[_rows(tm, C)] * 3 + [_acc((1, C))] * 3,
        out_shape=[_sds((T, C), F32), _sds((T, C), BF16), _sds((T, C), BF16)] + [_sds((1, C), F32)] * 3,
        args=(g, hp, ra, ri, r, lam, bda, bdx), comm=comm)


def _band_dw(r, dpa, dpx, name):
    T, C = r.shape
    tt = _tile(T, 512)
    chunks = _band_chunks(C, C // RNN_BLOCKS)

    def body(r_ref, dpa_ref, dpx_ref, oa_ref, ox_ref):
        @pl.when(pl.program_id(0) == 0)
        def _():
            oa_ref[...] = jnp.zeros_like(oa_ref)
            ox_ref[...] = jnp.zeros_like(ox_ref)

        for c0, cw, k0, k1 in chunks:
            cols = slice(c0, c0 + cw)
            rb = r_ref[:, k0:k1].astype(BF16)
            oa_ref[k0:k1, cols] += _dot_tn(rb, dpa_ref[:, cols])
            ox_ref[k0:k1, cols] += _dot_tn(rb, dpx_ref[:, cols])

    return _pallas(
        body, name=name, grid=(T // tt,),
        in_specs=[_rows(tt, C)] * 3,
        out_specs=[_acc((C, C)), _acc((C, C))],
        out_shape=[_sds((C, C), F32), _sds((C, C), F32)],
        args=(r, dpa, dpx))


def _conv4_bwd(dr, rx, w, name):
    T, C = dr.shape
    K = w.shape[0]
    R = CONV_ROWS
    P = RNN_PAD

    def body(d_ref, x_ref, w_ref, dx_ref, dw_ref, db_ref, dpad, xpad):
        dpad[0:T, :] = d_ref[...]
        dpad[T:T + P, :] = jnp.zeros((P, LANES), F32)
        xpad[0:P, :] = jnp.zeros((P, LANES), F32)
        xpad[P:P + T, :] = x_ref[...]
        dw_ref[...] = jnp.zeros_like(dw_ref)
        db_ref[...] = jnp.zeros_like(db_ref)
        wv = w_ref[...]

        def step(i, carry):
            base = pl.multiple_of(i * R, R)
            dwin = dpad[pl.ds(base, R + P), :]
            xwin = xpad[pl.ds(base, R + P), :]
            dcur = dwin[0:R, :]
            acc = jnp.zeros((R, LANES), F32)
            for k in range(K):
                acc = acc + wv[k:k + 1, :] * dwin[K - 1 - k:K - 1 - k + R, :]
                s = P - (K - 1) + k
                dw_ref[k:k + 1, :] += jnp.sum(dcur * xwin[s:s + R, :], axis=0, keepdims=True)
            dx_ref[pl.ds(base, R), :] = acc.astype(BF16)
            db_ref[...] += jnp.sum(dcur, axis=0, keepdims=True)
            return carry

        lax.fori_loop(0, T // R, step, 0)

    return _pallas(
        body, name=name, grid=(C // LANES,),
        in_specs=[_cols(T), _cols(T), _cols(T, K)],
        out_specs=[_cols(T), _cols(T, SUBLANES), _cols(T, 1)],
        out_shape=[_sds((T, C), BF16), _sds((SUBLANES, C), F32), _sds((1, C), F32)],
        scratch_shapes=[pltpu.VMEM((T + P, LANES), F32), pltpu.VMEM((T + P, LANES), F32)],
        args=(dr, rx, w))


def _conv31_bwd(dcs, cc, c, cv, cg, w, gg, gb, name, comm=None):
    T, D = dcs.shape
    K = w.shape[0]
    R, B, P = _tile(T, NORM_ROWS), CONV_BLOCK, CONV_PAD
    d_groups = _tap_groups([K - 1 - k for k in range(K)])
    x_groups = _tap_groups([P - (K - 1) + k for k in range(K)])

    def body(dcs_ref, cc_ref, c_ref, cv_ref, cg_ref, w_ref, gg_ref, gb_ref,
             dcv_ref, dcg_ref, dw_ref, db_ref, dgg_ref, dgb_ref, dpad, xpad, dwacc):
        dpad[T:T + P, :] = jnp.zeros((P, LANES), F32)
        xpad[0:P, :] = jnp.zeros((P, LANES), F32)
        xpad[P:P + T, :] = c_ref[...]
        dwacc[...] = jnp.zeros_like(dwacc)
        db_ref[...] = jnp.zeros_like(db_ref)
        dgg_ref[...] = jnp.zeros_like(dgg_ref)
        dgb_ref[...] = jnp.zeros_like(dgb_ref)
        wv, ggv, gbv = w_ref[...], gg_ref[...], gb_ref[...]

        def norm_step(i, carry):
            base = pl.multiple_of(i * R, R)
            xhat, rstd = _gn_stats(cc_ref[pl.ds(base, R), :])
            gn = xhat * ggv + gbv
            sg = _sigmoid(gn)
            dgn = dcs_ref[pl.ds(base, R), :] * (sg * (1.0 + gn * (1.0 - sg)))
            dgg_ref[...] += jnp.sum(dgn * xhat, axis=0, keepdims=True)
            dgb_ref[...] += jnp.sum(dgn, axis=0, keepdims=True)
            dxh = dgn * ggv
            m1 = jnp.mean(dxh, axis=-1, keepdims=True)
            m2 = jnp.mean(dxh * xhat, axis=-1, keepdims=True)
            dcc = rstd * (dxh - m1 - xhat * m2)
            dpad[pl.ds(base, R), :] = dcc
            db_ref[...] += jnp.sum(dcc, axis=0, keepdims=True)
            return carry

        lax.fori_loop(0, T // R, norm_step, 0)

        def conv_step(i, carry):
            base = pl.multiple_of(i * B, B)
            dwin = dpad[pl.ds(base, B + P), :]
            xwin = xpad[pl.ds(base, B + P), :]
            dcur = dwin[0:B, :]
            acc = _tap_sum(dwin, wv, d_groups, B)
            for phase, taps in x_groups:
                sh = _shifted(xwin, phase)
                for k, off in taps:
                    prod = dcur * sh[off:off + B, :]
                    part = prod[0:SUBLANES, :]
                    for q in range(1, B // SUBLANES):
                        part = part + prod[q * SUBLANES:(q + 1) * SUBLANES, :]
                    dwacc[k * SUBLANES:(k + 1) * SUBLANES, :] += part
            cgv = cg_ref[pl.ds(base, B), :].astype(F32)
            cvv = cv_ref[pl.ds(base, B), :].astype(F32)
            sg = _sigmoid(cgv)
            dcv_ref[pl.ds(base, B), :] = (acc * sg).astype(BF16)
            dcg_ref[pl.ds(base, B), :] = (acc * cvv * sg * (1.0 - sg)).astype(BF16)
            return carry

        lax.fori_loop(0, T // B, conv_step, 0)
        dw_ref[...] = jnp.zeros_like(dw_ref)
        for k in range(K):
            dw_ref[k:k + 1, :] = jnp.sum(dwacc[k * SUBLANES:(k + 1) * SUBLANES, :], axis=0, keepdims=True)

    return _pallas(
        body, name=name, grid=(D // LANES,),
        in_specs=[_cols(T)] * 5 + [_cols(T, K), _cols(T, 1), _cols(T, 1)],
        out_specs=[_cols(T), _cols(T), _cols(T, P), _cols(T, 1), _cols(T, 1), _cols(T, 1)],
        out_shape=[_sds((T, D), BF16), _sds((T, D), BF16), _sds((P, D), F32), _sds((1, D), F32),
                   _sds((1, D), F32), _sds((1, D), F32)],
        scratch_shapes=[pltpu.VMEM((T + P, LANES), F32), pltpu.VMEM((T + P, LANES), F32),
                        pltpu.VMEM((P * SUBLANES, LANES), F32)],
        args=(dcs, cc, c, cv, cg, w, gg, gb), comm=comm)


def _reduce_adamw(recvs, w, m, v, name):
    L, R, C = w.shape
    assert len(recvs) == L
    tr = next((c for c in (256, 176, 128, 64, 8) if R % c == 0), R)
    nr = R // tr
    c1 = 1.0 - ADAM_B1 ** ADAM_STEP
    c2 = 1.0 - ADAM_B2 ** ADAM_STEP

    def body(*refs):
        recv_refs = refs[:L]
        w_ref, m_ref, v_ref, g_ref, d_ref, mo_ref, vo_ref = refs[L:]

        def update(recv_ref):
            g = recv_ref[0].astype(F32)
            for k in range(1, N_DEV):
                g = g + recv_ref[k].astype(F32)
            mn = ADAM_B1 * m_ref[0] + (1.0 - ADAM_B1) * g
            vn = ADAM_B2 * v_ref[0] + (1.0 - ADAM_B2) * (g * g)
            g_ref[0] = g
            mo_ref[0] = mn
            vo_ref[0] = vn
            d_ref[0] = (-ADAM_LR) * ((mn / c1) / (jnp.sqrt(vn / c2) + ADAM_EPS) + ADAM_WD * w_ref[0])

        for l in range(L):
            pl.when(pl.program_id(0) == l)(lambda l=l: update(recv_refs[l]))

    def recv_spec(l):
        return pl.BlockSpec((N_DEV, tr, C), lambda j, i: (0, jnp.where(j == l, i, jnp.where(j < l, 0, nr - 1)), 0))

    blk = pl.BlockSpec((1, tr, C), lambda j, i: (j, i, 0))
    return _pallas(
        body, name=name, grid=(L, nr),
        in_specs=[recv_spec(l) for l in range(L)] + [blk, blk, blk],
        out_specs=[blk] * 4,
        out_shape=[_sds((L, R, C), F32)] * 4,
        args=(*recvs, w, m, v))


def _unshard(name, gathered):
    n, r, c = gathered.shape
    if name in COL_SHARDED:
        return gathered.transpose(1, 0, 2).reshape(r, n * c)
    return gathered.reshape(n * r, c)


def _to_shards(name, full):
    R, C = full.shape
    wire = BF16 if name in BF16_ON_WIRE else F32
    if name in COL_SHARDED:
        return full.reshape(R, N_DEV, C // N_DEV).transpose(1, 0, 2).astype(wire)
    return full.reshape(N_DEV, R // N_DEV, C).astype(wire)


def _block_diag(w):
    H, b, _ = w.shape
    eye = jnp.eye(H, dtype=w.dtype)
    return (w[:, :, None, :] * eye[:, None, :, None]).reshape(H * b, H * b)


def _diag_blocks(dense, H):
    b = dense.shape[0] // H
    eye = jnp.eye(H, dtype=dense.dtype)
    return jnp.sum(dense.reshape(H, b, H, b) * eye[:, None, :, None], axis=2)


def _pack(arrs, rows):
    flat = jnp.concatenate([a.reshape(-1) for a in arrs])
    return jnp.pad(flat, (0, rows * 1024 - flat.shape[0])).reshape(1, rows, 1024)


def _unpack(packed, shapes):
    flat = packed.reshape(-1)
    out, off = [], 0
    for s in shapes:
        n = math.prod(s)
        out.append(flat[off:off + n].reshape(s))
        off += n
    return out


class _Queue:
    def __init__(self, kind, us_per_mb):
        self.kind, self.us_per_mb, self.items, self.done = kind, us_per_mb, [], {}

    def push(self, key, arr):
        self.items.append((key, arr))

    def mb(self, arr):
        return _mbytes(arr) / (N_DEV if self.kind == "scatter" else 1)

    def take(self, micros):
        taken, budget = [], micros / self.us_per_mb
        while self.items and (not taken or self.mb(self.items[0][1]) <= budget):
            budget -= self.mb(self.items[0][1])
            taken.append(self.items.pop(0))
        return {"kind": self.kind, "keys": [k for k, _ in taken], "arrs": [a for _, a in taken]} if taken else None

    def landed(self, comm):
        if comm:
            self.done.update(zip(comm["keys"], comm["recv"]))

    def flush(self, name, upto=None):
        n = len(self.items)
        if upto is not None:
            keys = [k for k, _ in self.items]
            n = keys.index(upto) + 1 if upto in keys else 0
        if n:
            taken, self.items = self.items[:n], self.items[n:]
            self.done.update(zip([k for k, _ in taken], _exchange(self.kind, [a for _, a in taken], name)))


def kernel(x, ffn1_w_gu, ffn1_w_down, ln1_g, ln1_b, mix_w_in, mix_b_in, conv_dw_w, conv_dw_b, conv_gn_g, conv_gn_b, conv_w_proj, rnn_conv_w, rnn_conv_b, rnn_w_a, rnn_b_a, rnn_w_x, rnn_b_x, rnn_lambda, rnn_w_proj, mix_w_out, ln2_g, ln2_b, ffn2_w_gu, ffn2_w_down, ln3_g, ln3_b, loss_target, m_ffn1_w_gu, m_ffn1_w_down, m_ln1_g, m_ln1_b, m_mix_w_in, m_mix_b_in, m_conv_dw_w, m_conv_dw_b, m_conv_gn_g, m_conv_gn_b, m_conv_w_proj, m_rnn_conv_w, m_rnn_conv_b, m_rnn_w_a, m_rnn_b_a, m_rnn_w_x, m_rnn_b_x, m_rnn_lambda, m_rnn_w_proj, m_mix_w_out, m_ln2_g, m_ln2_b, m_ffn2_w_gu, m_ffn2_w_down, m_ln3_g, m_ln3_b, v_ffn1_w_gu, v_ffn1_w_down, v_ln1_g, v_ln1_b, v_mix_w_in, v_mix_b_in, v_conv_dw_w, v_conv_dw_b, v_conv_gn_g, v_conv_gn_b, v_conv_w_proj, v_rnn_conv_w, v_rnn_conv_b, v_rnn_w_a, v_rnn_b_a, v_rnn_w_x, v_rnn_b_x, v_rnn_lambda, v_rnn_w_proj, v_mix_w_out, v_ln2_g, v_ln2_b, v_ffn2_w_gu, v_ffn2_w_down, v_ln3_g, v_ln3_b):
    given = dict(locals())
    W = {n: given[n] for n in WEIGHTS}
    M = {n: given["m_" + n] for n in WEIGHTS}
    V = {n: given["v_" + n] for n in WEIGHTS}
    T, D = x.shape[1], x.shape[2]
    L = DEPTH
    x2 = x.reshape(T, D)
    target = loss_target.reshape(T, D)
    d_rnn = rnn_conv_b.shape[1]

    gather = _Queue("gather", GATHER_US_PER_MB)
    for l in range(L):
        for n in USE_ORDER:
            gather.push((n, l), W[n][l].astype(BF16) if n in BF16_ON_WIRE else W[n][l])
    gather.flush("gather_first", upto=("ffn1_w_gu", 0))
    full_cache = {}

    def full(n, l):
        if (n, l) not in full_cache:
            gather.flush(f"gather_{n}_{l}", upto=(n, l))
            full_cache[(n, l)] = _unshard(n, gather.done[(n, l)])
        return full_cache[(n, l)]

    def fwd_comm(micros):
        return gather.take(micros)

    bd_a = [_block_diag(rnn_w_a[l]).astype(BF16) for l in range(L)]
    bd_x = [_block_diag(rnn_w_x[l]).astype(BF16) for l in range(L)]

    def vec(name, l):
        return W[name][l:l + 1]

    saved = []
    h, hb = x2, x2
    for l in range(L):
        s = {"hb_in": hb}
        w_gu = full("ffn1_w_gu", l)
        comm = fwd_comm(105)
        s["gu1"], s["a1"] = _ffn_up(hb, w_gu, f"ffn1_up_{l}", comm=comm)
        gather.landed(comm)
        w_down = full("ffn1_w_down", l)
        comm = fwd_comm(65)
        s["r1"], y1, s["y1b"] = _ffn_down_ln(s["a1"], w_down, h, vec("ln1_g", l), vec("ln1_b", l),
                                              f"ffn1_down_ln_{l}", comm=comm)
        gather.landed(comm)
        w_in = full("mix_w_in", l)
        comm = fwd_comm(5)
        s["c"], s["cv"], s["cg"], s["rx"], s["rg"], s["gc"], s["gr"] = _mix_in(
            s["y1b"], w_in, vec("mix_b_in", l), d_rnn, f"mix_in_{l}", comm=comm)
        gather.landed(comm)
        w_dw = full("conv_dw_w", l)
        comm = fwd_comm(195)
        s["cc"], s["cs"] = _conv31_gn(s["c"], w_dw, vec("conv_dw_b", l), vec("conv_gn_g", l),
                                      vec("conv_gn_b", l), f"conv31_gn_{l}", comm=comm)
        gather.landed(comm)
        s["r"] = _conv4(s["rx"], full("rnn_conv_w", l), vec("rnn_conv_b", l), f"conv4_{l}")
        comm = fwd_comm(115)
        s["ra"], s["ri"], s["a"], uu = _gates(s["r"], bd_a[l], bd_x[l], vec("rnn_b_a", l), vec("rnn_b_x", l),
                                              vec("rnn_lambda", l), f"gates_{l}", comm=comm)
        gather.landed(comm)
        s["h"], s["hp"], s["hg"] = _scan_fwd(s["a"], uu, s["rg"], f"scan_{l}")
        w_cp, w_rp, w_out = full("conv_w_proj", l), full("rnn_w_proj", l), full("mix_w_out", l)
        comm = fwd_comm(85)
        s["yc"], s["yr"], s["m"], s["r2"], y2, s["y2b"] = _mix_out_ln(
            s["cs"], s["hg"], s["gc"], s["gr"], y1, w_cp, w_rp, w_out, vec("ln2_g", l), vec("ln2_b", l),
            f"mix_out_ln_{l}", comm=comm)
        gather.landed(comm)
        w_gu2 = full("ffn2_w_gu", l)
        comm = fwd_comm(105)
        s["gu2"], s["a2"] = _ffn_up(s["y2b"], w_gu2, f"ffn2_up_{l}", comm=comm)
        gather.landed(comm)
        w_down2 = full("ffn2_w_down", l)
        comm = fwd_comm(65)
        s["r3"], h, hb = _ffn_down_ln(s["a2"], w_down2, y2, vec("ln3_g", l), vec("ln3_b", l),
                                      f"ffn2_down_ln_{l}", comm=comm)
        gather.landed(comm)
        saved.append(s)

    scatter = _Queue("scatter", SCATTER_US_PER_MB)
    G = {n: [None] * L for n in WEIGHTS}

    def ready(n, l, grad):
        G[n][l] = grad
        scatter.push((n, l), _to_shards(n, grad))

    def bwd_comm(micros):
        return scatter.take(micros)

    loss_acc, dr3, drb3, G["ln3_g"][L - 1], G["ln3_b"][L - 1] = _loss_ln_bwd(
        h, target, saved[L - 1]["r3"], vec("ln3_g", L - 1), "loss_ln3_bwd")
    grad_x = small = None
    small_shapes = [W[n].shape for n in REPLICATED]
    small_rows = -(-sum(math.prod(s) for s in small_shapes) // (1024 * SUBLANES)) * SUBLANES
    for l in reversed(range(L)):
        s = saved[l]
        comm = bwd_comm(95)
        dgu2 = _ffn_bwd_a(drb3, full("ffn2_w_down", l), s["gu2"], f"ffn2_bwd_a_{l}", comm=comm)
        scatter.landed(comm)
        ready("ffn2_w_down", l, _mm_tn(s["a2"], drb3, f"ffn2_dw_down_{l}", scale=0.5))
        ready("ffn2_w_gu", l, _mm_tn(s["y2b"], dgu2, f"ffn2_dw_gu_{l}"))
        comm = bwd_comm(140)
        dr2, drb2, G["ln2_g"][l], G["ln2_b"][l] = _nt_res(
            dgu2, full("ffn2_w_gu", l), dr3, f"ffn2_bwd_x_{l}", ln=(s["r2"], vec("ln2_g", l)), comm=comm)
        scatter.landed(comm)
        comm = bwd_comm(120)
        dyc, dyr, dgc, dgr, dcs, dh, drg = _mix_bwd1(
            drb2, full("mix_w_out", l), full("conv_w_proj", l), full("rnn_w_proj", l), s["gc"], s["gr"], s["yc"],
            s["yr"], s["rg"], s["h"], f"mix_bwd_out_{l}", comm=comm)
        scatter.landed(comm)
        ready("mix_w_out", l, _mm_tn(s["m"], drb2, f"mix_dw_out_{l}"))
        ready("conv_w_proj", l, _mm_tn(s["cs"], dyc, f"conv_dw_proj_{l}"))
        ready("rnn_w_proj", l, _mm_tn(s["hg"], dyr, f"rnn_dw_proj_{l}"))
        gsc = _scan_bwd(dh, s["a"], f"scan_bwd_{l}")
        comm = bwd_comm(160)
        dr_, dpa, dpx, G["rnn_lambda"][l], G["rnn_b_a"][l], G["rnn_b_x"][l] = _gates_bwd(
            gsc, s["hp"], s["ra"], s["ri"], s["r"], vec("rnn_lambda", l), bd_a[l], bd_x[l], f"gates_bwd_{l}",
            comm=comm)
        scatter.landed(comm)
        dwa, dwx = _band_dw(s["r"], dpa, dpx, f"rnn_dw_ax_{l}")
        G["rnn_w_a"][l] = _diag_blocks(dwa, RNN_BLOCKS)
        G["rnn_w_x"][l] = _diag_blocks(dwx, RNN_BLOCKS)
        drx, dw4, G["rnn_conv_b"][l] = _conv4_bwd(dr_, s["rx"], full("rnn_conv_w", l), f"conv4_bwd_{l}")
        ready("rnn_conv_w", l, dw4[:RNN_CONV_WIDTH])
        comm = bwd_comm(250)
        dcv, dcg, dw31, G["conv_dw_b"][l], G["conv_gn_g"][l], G["conv_gn_b"][l] = _conv31_bwd(
            dcs, s["cc"], s["c"], s["cv"], s["cg"], full("conv_dw_w", l), vec("conv_gn_g", l),
            vec("conv_gn_b", l), f"conv31_bwd_{l}", comm=comm)
        scatter.landed(comm)
        ready("conv_dw_w", l, dw31[:CONV_WIDTH])
        du = jnp.concatenate([dcv, dcg, drx, drg, dgc, dgr], axis=1)
        ready("mix_w_in", l, _mm_tn(s["y1b"], du, f"mix_dw_in_{l}"))
        comm = bwd_comm(175)
        dr1, drb1, G["ln1_g"][l], G["ln1_b"][l], G["mix_b_in"][l] = _nt_res(
            du, full("mix_w_in", l), dr2, f"mix_bwd_in_{l}", ln=(s["r1"], vec("ln1_g", l)), colsum=True, comm=comm)
        scatter.landed(comm)
        ready("ffn1_w_down", l, _mm_tn(s["a1"], drb1, f"ffn1_dw_down_{l}", scale=0.5))
        comm = bwd_comm(95)
        dgu1 = _ffn_bwd_a(drb1, full("ffn1_w_down", l), s["gu1"], f"ffn1_bwd_a_{l}", comm=comm)
        scatter.landed(comm)
        if l == 0:
            local_small = [jnp.stack([g.reshape(W[n].shape[1:]) for g in G[n]]) for n in REPLICATED]
            comm = {"kind": "gather", "arrs": [_pack(local_small, small_rows)[0]]}
            ready("ffn1_w_gu", l, _mm_tn(s["hb_in"], dgu1, f"ffn1_dw_gu_{l}", comm=comm))
            (small,) = comm["recv"]
        else:
            ready("ffn1_w_gu", l, _mm_tn(s["hb_in"], dgu1, f"ffn1_dw_gu_{l}"))
        if l > 0:
            comm = bwd_comm(130)
            dr3, drb3, G["ln3_g"][l - 1], G["ln3_b"][l - 1] = _nt_res(
                dgu1, full("ffn1_w_gu", l), dr1, f"ffn1_bwd_x_{l}", ln=(saved[l - 1]["r3"], vec("ln3_g", l - 1)),
                comm=comm)
        else:
            comm = bwd_comm(1e9)
            (grad_x,) = _nt_res(dgu1, full("ffn1_w_gu", l), dr1, f"ffn1_bwd_x_{l}", comm=comm)
        scatter.landed(comm)
    scatter.flush("scatter_rest")

    loss = lax.psum(loss_acc[0, 0], ("x", "y", "c"))

    out = {}
    for n in SHARDED:
        out[n] = _reduce_adamw([scatter.done[(n, l)] for l in range(L)], W[n], M[n], V[n], f"adamw_{n}")
    packed = _reduce_adamw([small], _pack([W[n] for n in REPLICATED], small_rows),
                           _pack([M[n] for n in REPLICATED], small_rows),
                           _pack([V[n] for n in REPLICATED], small_rows), "adamw_small")
    unpacked = [_unpack(p, small_shapes) for p in packed]
    for i, n in enumerate(REPLICATED):
        out[n] = tuple(u[i] for u in unpacked)

    return (loss, grad_x.reshape(x.shape), *[out[n][0] for n in WEIGHTS], *[out[n][1] for n in WEIGHTS],
            *[out[n][2] for n in WEIGHTS], *[out[n][3] for n in WEIGHTS])
```

```python
import math

import jax
import jax.numpy as jnp
from jax import lax
from jax.experimental import pallas as pl
from jax.experimental.pallas import tpu as pltpu

F32 = jnp.float32
BF16 = jnp.bfloat16
MESH = pl.DeviceIdType.MESH

DEPTH = 2
ALPHA = (2 * DEPTH) ** 0.25
LN_EPS = 1e-5
RG_LRU_C = 8.0
CONV_WIDTH = 31
RNN_CONV_WIDTH = 4
RNN_BLOCKS = 16
N_DEV = 8
ADAM_LR, ADAM_B1, ADAM_B2, ADAM_EPS, ADAM_WD, ADAM_STEP = 0.001, 0.9, 0.999, 1e-08, 0.01, 10

LANES = 128
SUBLANES = 8
VMEM_LIMIT = 56 * 1024 * 1024
CONV_PAD = 32
RNN_PAD = 8
CONV_ROWS = 128
MM_TN_OUT_BYTES = 12 * 1024 * 1024
NORM_ROWS = 256
CONV_BLOCK = 32
GATHER_US_PER_MB = 43.0
SCATTER_US_PER_MB = 86.0

WEIGHTS = ['ffn1_w_gu', 'ffn1_w_down', 'ln1_g', 'ln1_b', 'mix_w_in', 'mix_b_in', 'conv_dw_w', 'conv_dw_b',
           'conv_gn_g', 'conv_gn_b', 'conv_w_proj', 'rnn_conv_w', 'rnn_conv_b', 'rnn_w_a', 'rnn_b_a', 'rnn_w_x',
           'rnn_b_x', 'rnn_lambda', 'rnn_w_proj', 'mix_w_out', 'ln2_g', 'ln2_b', 'ffn2_w_gu', 'ffn2_w_down',
           'ln3_g', 'ln3_b']
COL_SHARDED = ['ffn1_w_gu', 'mix_w_in', 'ffn2_w_gu', 'conv_dw_w', 'rnn_conv_w']
ROW_SHARDED = ['ffn1_w_down', 'conv_w_proj', 'rnn_w_proj', 'mix_w_out', 'ffn2_w_down']
SHARDED = COL_SHARDED + ROW_SHARDED
BF16_ON_WIRE = ['ffn1_w_gu', 'mix_w_in', 'ffn2_w_gu', 'ffn1_w_down', 'conv_w_proj', 'rnn_w_proj', 'mix_w_out',
                'ffn2_w_down']
REPLICATED = [n for n in WEIGHTS if n not in SHARDED]
USE_ORDER = ['ffn1_w_gu', 'ffn1_w_down', 'mix_w_in', 'conv_dw_w', 'rnn_conv_w', 'conv_w_proj', 'rnn_w_proj',
             'mix_w_out', 'ffn2_w_gu', 'ffn2_w_down']


def _cp(n_axes=1):
    return pltpu.CompilerParams(dimension_semantics=("arbitrary",) * n_axes, vmem_limit_bytes=VMEM_LIMIT)


def _rows(tm, c):
    return pl.BlockSpec((tm, c), lambda i: (i, 0))


def _res(shape):
    nd = len(shape)
    return pl.BlockSpec(tuple(shape), lambda *_: (0,) * nd, pipeline_mode=pl.Buffered(1))


def _acc(shape):
    nd = len(shape)
    return pl.BlockSpec(tuple(shape), lambda *_: (0,) * nd)


def _tile(t, want):
    return want if t % want == 0 else t


def _sds(shape, dtype):
    return jax.ShapeDtypeStruct(tuple(shape), dtype)


def _mbytes(a):
    return a.size * a.dtype.itemsize / 1e6


def _ln_fwd(r, g, b):
    mu = jnp.mean(r, axis=-1, keepdims=True)
    xc = r - mu
    var = jnp.mean(xc * xc, axis=-1, keepdims=True)
    return xc * lax.rsqrt(var + LN_EPS) * g + b


def _ln_bwd(dy, r, g):
    mu = jnp.mean(r, axis=-1, keepdims=True)
    xc = r - mu
    var = jnp.mean(xc * xc, axis=-1, keepdims=True)
    rstd = lax.rsqrt(var + LN_EPS)
    xhat = xc * rstd
    dxh = dy * g
    m1 = jnp.mean(dxh, axis=-1, keepdims=True)
    m2 = jnp.mean(dxh * xhat, axis=-1, keepdims=True)
    dr = rstd * (dxh - m1 - xhat * m2)
    return dr, jnp.sum(dy * xhat, axis=0, keepdims=True), jnp.sum(dy, axis=0, keepdims=True)


def _sigmoid(x):
    return jax.nn.sigmoid(x)


_GELU_K = math.sqrt(2.0 / math.pi)


def _gelu(x):
    t = jnp.tanh(_GELU_K * (x + 0.044715 * x * x * x))
    return 0.5 * x * (1.0 + t), t


def _gelu_grad(x, t):
    return 0.5 * (1.0 + t) + 0.5 * x * (1.0 - t * t) * (_GELU_K * (1.0 + 3.0 * 0.044715 * x * x))


def _expm1(x):
    taylor = x * (1.0 + x * (0.5 + x * (1.0 / 6.0 + x * (1.0 / 24.0 + x * (1.0 / 120.0)))))
    return jnp.where(jnp.abs(x) < 0.03, taylor, jnp.exp(x) - 1.0)


def _softplus_neg(lam):
    return jnp.maximum(-lam, 0.0) + jnp.log1p(jnp.exp(-jnp.abs(lam)))


def _dot(a, b):
    return jnp.dot(a, b, preferred_element_type=F32)


def _dot_nt(a, b):
    return lax.dot_general(a, b, (((1,), (1,)), ((), ())), preferred_element_type=F32)


def _dot_tn(a, b):
    return lax.dot_general(a, b, (((0,), (0,)), ((), ())), preferred_element_type=F32)


def _chunks(width, cn):
    return [(j, min(cn, width - j)) for j in range(0, width, cn)]


def _band_chunks(width, block):
    out = []
    for c0, cw in _chunks(width, 256):
        lo = (c0 // block) * block
        hi = ((c0 + cw - 1) // block + 1) * block
        out.append((c0, cw, lo // LANES * LANES, min(width, -(-hi // LANES) * LANES)))
    return out


def _position():
    return lax.axis_index("x"), lax.axis_index("y"), lax.axis_index("c")


def _index(p):
    return 4 * p[0] + 2 * p[1] + p[2]


def _comm_out_shapes(kind, arrs):
    return [_sds((N_DEV,) + a.shape if kind == "gather" else a.shape, a.dtype) for a in arrs]


def _comm_scratch(n):
    return [pltpu.SemaphoreType.DMA((n, 7)), pltpu.SemaphoreType.DMA((n, 7)), pltpu.SemaphoreType.DMA((n,))]


def _comm_phases(kind, srcs, dsts, send_sems, recv_sems, local_sems):
    n = len(srcs)
    x, y, c = _position()
    me, sibling = (x, y, c), (x, y, 1 - c)
    chips = [(1 - x, y), (x, 1 - y), (1 - x, 1 - y)]

    if kind == "gather":
        def copy(a, k, block, to, src=None):
            dst = dsts[a].at[_index(block)]
            return pltpu.make_async_remote_copy(
                src_ref=dst if src is None else src, dst_ref=dst, send_sem=send_sems.at[a, k],
                recv_sem=recv_sems.at[a, k], device_id=to, device_id_type=MESH)

        def mine(a):
            return pltpu.make_async_copy(srcs[a], dsts[a].at[_index(me)], local_sems.at[a])

        def first(a):
            return [copy(a, 0, me, sibling, src=srcs[a])] + [
                copy(a, 1 + j, me, (*chip, c), src=srcs[a]) for j, chip in enumerate(chips)]

        def start():
            for a in range(n):
                mine(a).start()
            for a in range(n):
                for cp in first(a):
                    cp.start()

        def mid():
            for j, chip in enumerate(chips):
                for a in range(n):
                    copy(a, 1 + j, (*chip, c), me).wait_recv()
                    copy(a, 4 + j, (*chip, c), sibling).start()

        def end():
            for a in range(n):
                copy(a, 0, sibling, me).wait_recv()
            for j, chip in enumerate(chips):
                for a in range(n):
                    copy(a, 4 + j, (*chip, 1 - c), me).wait_recv()
            for a in range(n):
                for cp in first(a):
                    cp.wait_send()
                for j, chip in enumerate(chips):
                    copy(a, 4 + j, (*chip, c), sibling).wait_send()
                mine(a).wait()

        return start, mid, end

    def peer_of(k):
        return (1 - x if k & 4 else x, 1 - y if k & 2 else y, 1 - c if k & 1 else c)

    def own(a):
        return pltpu.make_async_copy(srcs[a].at[_index(me)], dsts[a].at[0], local_sems.at[a])

    def remote(a, k):
        peer = peer_of(k)
        return pltpu.make_async_remote_copy(
            src_ref=srcs[a].at[_index(peer)], dst_ref=dsts[a].at[k], send_sem=send_sems.at[a, k - 1],
            recv_sem=recv_sems.at[a, k - 1], device_id=peer, device_id_type=MESH)

    def start():
        for a in range(n):
            own(a).start()
        for k in range(1, N_DEV):
            for a in range(n):
                remote(a, k).start()

    def end():
        for k in range(1, N_DEV):
            for a in range(n):
                remote(a, k).wait()
        for a in range(n):
            own(a).wait()

    return start, (lambda: None), end


def _exchange(kind, arrs, name):
    n = len(arrs)
    hbm = pl.BlockSpec(memory_space=pl.ANY)

    def body(*refs):
        start, mid, end = _comm_phases(kind, refs[:n], refs[n:2 * n], *refs[2 * n:])
        start()
        mid()
        end()

    return pl.pallas_call(
        body, name=name, in_specs=[hbm] * n, out_specs=[hbm] * n, out_shape=_comm_out_shapes(kind, arrs),
        scratch_shapes=_comm_scratch(n))(*arrs)


def _pallas(body, *, name, grid, in_specs, out_specs, out_shape, args, scratch_shapes=(), comm=None):
    in_specs, out_specs, out_shape = list(in_specs), list(out_specs), list(out_shape)
    scratch_shapes = list(scratch_shapes)
    if not comm:
        return pl.pallas_call(body, name=name, grid=grid, in_specs=in_specs, out_specs=out_specs,
                              out_shape=out_shape, scratch_shapes=scratch_shapes,
                              compiler_params=_cp(len(grid)))(*args)
    arrs = comm["arrs"]
    ns, n_in, n_out, n_scr = len(arrs), len(in_specs), len(out_specs), len(scratch_shapes)
    hbm = pl.BlockSpec(memory_space=pl.ANY)
    total = math.prod(grid)

    def carrier(*refs):
        ins, srcs = refs[:n_in], refs[n_in:n_in + ns]
        outs, dsts = refs[n_in + ns:n_in + ns + n_out], refs[n_in + ns + n_out:n_in + 2 * ns + n_out]
        scr, sems = refs[n_in + 2 * ns + n_out:n_in + 2 * ns + n_out + n_scr], refs[n_in + 2 * ns + n_out + n_scr:]
        step = pl.program_id(0)
        for ax in range(1, len(grid)):
            step = step * grid[ax] + pl.program_id(ax)
        start, mid, end = _comm_phases(comm["kind"], srcs, dsts, *sems)
        pl.when(step == 0)(start)
        body(*ins, *outs, *scr)
        pl.when(step == total - 1)(mid)
        pl.when(step == total - 1)(end)

    res = pl.pallas_call(
        carrier, name=name, grid=grid, in_specs=in_specs + [hbm] * ns, out_specs=out_specs + [hbm] * ns,
        out_shape=out_shape + _comm_out_shapes(comm["kind"], arrs),
        scratch_shapes=scratch_shapes + _comm_scratch(ns), compiler_params=_cp(len(grid)))(*args, *arrs)
    comm["recv"] = res[n_out:]
    return res[:n_out]


def _ffn_up(xb, w, name, comm=None):
    T, D = xb.shape
    F = w.shape[1] // 2
    tm = _tile(T, 512)

    def body(x_ref, w_ref, gu_ref, a_ref):
        x = x_ref[...].astype(BF16)
        for j, cw in _chunks(F, 256):
            g = _dot(x, w_ref[:, j:j + cw])
            u = _dot(x, w_ref[:, F + j:F + j + cw])
            gu_ref[:, j:j + cw] = g.astype(BF16)
            gu_ref[:, F + j:F + j + cw] = u.astype(BF16)
            a_ref[:, j:j + cw] = (g * _sigmoid(g) * u).astype(BF16)

    return _pallas(
        body, name=name, grid=(T // tm,),
        in_specs=[_rows(tm, D), _res(w.shape)],
        out_specs=[_rows(tm, 2 * F), _rows(tm, F)],
        out_shape=[_sds((T, 2 * F), BF16), _sds((T, F), BF16)],
        args=(xb, w), comm=comm)


def _ffn_down_ln(a, wd, xres, g, b, name, comm=None):
    T, F = a.shape
    D = wd.shape[1]
    tm = _tile(T, 512)

    def body(a_ref, wd_ref, x_ref, g_ref, b_ref, r_ref, y_ref, yb_ref):
        r = ALPHA * x_ref[...] + 0.5 * _dot(a_ref[...], wd_ref[...])
        y = _ln_fwd(r, g_ref[...], b_ref[...])
        r_ref[...] = r
        y_ref[...] = y
        yb_ref[...] = y.astype(BF16)

    return _pallas(
        body, name=name, grid=(T // tm,),
        in_specs=[_rows(tm, F), _res(wd.shape), _rows(tm, D), _res((1, D)), _res((1, D))],
        out_specs=[_rows(tm, D), _rows(tm, D), _rows(tm, D)],
        out_shape=[_sds((T, D), F32), _sds((T, D), F32), _sds((T, D), BF16)],
        args=(a, wd, xres, g, b), comm=comm)


def _mix_in(hb, w, bias, d_rnn, name, comm=None):
    T, D = hb.shape
    R = d_rnn
    tm = _tile(T, 256)
    o_cv, o_cg, o_rx, o_rg, o_gc, o_gr = 0, D, 2 * D, 2 * D + R, 2 * D + 2 * R, 3 * D + 2 * R

    def body(x_ref, w_ref, b_ref, c_ref, cv_ref, cg_ref, rx_ref, rg_ref, gc_ref, gr_ref):
        x = x_ref[...]

        def seg(off, j, cw):
            return _dot(x, w_ref[:, off + j:off + j + cw]) + b_ref[:, off + j:off + j + cw]

        for j, cw in _chunks(D, 256):
            cv = seg(o_cv, j, cw)
            cg = seg(o_cg, j, cw)
            cv_ref[:, j:j + cw] = cv.astype(BF16)
            cg_ref[:, j:j + cw] = cg.astype(BF16)
            c_ref[:, j:j + cw] = cv * _sigmoid(cg)
            gc_ref[:, j:j + cw] = seg(o_gc, j, cw).astype(BF16)
            gr_ref[:, j:j + cw] = seg(o_gr, j, cw).astype(BF16)
        for j, cw in _chunks(R, 256):
            rx_ref[:, j:j + cw] = seg(o_rx, j, cw)
            rg_ref[:, j:j + cw] = seg(o_rg, j, cw).astype(BF16)

    return _pallas(
        body, name=name, grid=(T // tm,),
        in_specs=[_rows(tm, D), _res(w.shape), _res(bias.shape)],
        out_specs=[_rows(tm, D), _rows(tm, D), _rows(tm, D), _rows(tm, R), _rows(tm, R), _rows(tm, D),
                   _rows(tm, D)],
        out_shape=[_sds((T, D), F32), _sds((T, D), BF16), _sds((T, D), BF16), _sds((T, R), F32),
                   _sds((T, R), BF16), _sds((T, D), BF16), _sds((T, D), BF16)],
        args=(hb, w, bias), comm=comm)


def _cols(t, rows=None):
    return pl.BlockSpec((t if rows is None else rows, LANES), lambda j: (0, j))


def _gn_stats(cc):
    mu = jnp.mean(cc, axis=-1, keepdims=True)
    xc = cc - mu
    var = jnp.mean(xc * xc, axis=-1, keepdims=True)
    rstd = lax.rsqrt(var + LN_EPS)
    return xc * rstd, rstd


def _tap_groups(offsets):
    groups = {}
    for k, s in enumerate(offsets):
        groups.setdefault(s % SUBLANES, []).append((k, s - s % SUBLANES))
    return sorted(groups.items())


def _shifted(win, phase):
    return win if phase == 0 else pltpu.roll(win, win.shape[0] - phase, 0)


def _tap_sum(win, wv, groups, rows):
    parts, t = [None] * 4, 0
    for phase, taps in groups:
        sh = _shifted(win, phase)
        for k, off in taps:
            term = wv[k:k + 1, :] * sh[off:off + rows, :]
            parts[t % 4] = term if parts[t % 4] is None else parts[t % 4] + term
            t += 1
    return (parts[0] + parts[1]) + (parts[2] + parts[3])


def _conv31_gn(c, w, bias, gg, gb, name, comm=None):
    T, D = c.shape
    K = w.shape[0]
    B, P, N = CONV_BLOCK, CONV_PAD, _tile(T, NORM_ROWS)
    assert D % LANES == 0 and T % B == 0 and K - 1 <= P
    groups = _tap_groups([P - (K - 1) + k for k in range(K)])

    def body(c_ref, w_ref, b_ref, gg_ref, gb_ref, cc_ref, cs_ref, xpad):
        xpad[0:P, :] = jnp.zeros((P, LANES), F32)
        xpad[P:P + T, :] = c_ref[...]
        wv = w_ref[...]
        bv, ggv, gbv = b_ref[...], gg_ref[...], gb_ref[...]

        def conv_step(i, carry):
            base = pl.multiple_of(i * B, B)
            win = xpad[pl.ds(base, B + P), :]
            cc_ref[pl.ds(base, B), :] = _tap_sum(win, wv, groups, B) + bv
            return carry

        lax.fori_loop(0, T // B, conv_step, 0)

        def norm_step(i, carry):
            base = pl.multiple_of(i * N, N)
            xhat, _ = _gn_stats(cc_ref[pl.ds(base, N), :])
            gn = xhat * ggv + gbv
            cs_ref[pl.ds(base, N), :] = (gn * _sigmoid(gn)).astype(BF16)
            return carry

        lax.fori_loop(0, T // N, norm_step, 0)

    return _pallas(
        body, name=name, grid=(D // LANES,),
        in_specs=[_cols(T), _cols(T, K), _cols(T, 1), _cols(T, 1), _cols(T, 1)],
        out_specs=[_cols(T), _cols(T)],
        out_shape=[_sds((T, D), F32), _sds((T, D), BF16)],
        scratch_shapes=[pltpu.VMEM((T + P, LANES), F32)],
        args=(c, w, bias, gg, gb), comm=comm)


def _conv4(rx, w, bias, name):
    T, C = rx.shape
    K = w.shape[0]
    R = CONV_ROWS
    assert C % LANES == 0 and T % R == 0 and K - 1 <= RNN_PAD

    def body(x_ref, w_ref, b_ref, r_ref, xpad):
        xpad[0:RNN_PAD, :] = jnp.zeros((RNN_PAD, LANES), F32)
        xpad[RNN_PAD:RNN_PAD + T, :] = x_ref[...]
        wv, bv = w_ref[...], b_ref[...]

        def step(i, carry):
            base = pl.multiple_of(i * R, R)
            win = xpad[pl.ds(base, R + RNN_PAD), :]
            acc = jnp.zeros((R, LANES), F32)
            for k in range(K):
                s = RNN_PAD - (K - 1) + k
                acc = acc + wv[k:k + 1, :] * win[s:s + R, :]
            r_ref[pl.ds(base, R), :] = acc + bv
            return carry

        lax.fori_loop(0, T // R, step, 0)

    (r,) = _pallas(
        body, name=name, grid=(C // LANES,),
        in_specs=[_cols(T), _cols(T, K), _cols(T, 1)],
        out_specs=[_cols(T)],
        out_shape=[_sds((T, C), F32)],
        scratch_shapes=[pltpu.VMEM((T + RNN_PAD, LANES), F32)],
        args=(rx, w, bias))
    return r


def _gates(r, bda, bdx, b_a, b_x, lam, name, comm=None):
    T, C = r.shape
    tm = _tile(T, 512)
    chunks = _band_chunks(C, C // RNN_BLOCKS)

    def body(r_ref, wa_ref, wx_ref, ba_ref, bx_ref, lam_ref, ra_ref, ri_ref, a_ref, u_ref):
        for c0, cw, k0, k1 in chunks:
            cols = slice(c0, c0 + cw)
            rb = r_ref[:, k0:k1].astype(BF16)
            ra = _sigmoid(_dot(rb, wa_ref[k0:k1, cols]) + ba_ref[:, cols])
            ri = _sigmoid(_dot(rb, wx_ref[k0:k1, cols]) + bx_ref[:, cols])
            log_a = (-RG_LRU_C) * ra * _softplus_neg(lam_ref[:, cols])
            ra_ref[:, cols] = ra
            ri_ref[:, cols] = ri
            a_ref[:, cols] = jnp.exp(log_a)
            u_ref[:, cols] = jnp.sqrt(-_expm1(2.0 * log_a)) * (ri * r_ref[:, cols])

    return _pallas(
        body, name=name, grid=(T // tm,),
        in_specs=[_rows(tm, C), _res(bda.shape), _res(bdx.shape), _res((1, C)), _res((1, C)), _res((1, C))],
        out_specs=[_rows(tm, C)] * 4,
        out_shape=[_sds((T, C), F32)] * 4,
        args=(r, bda, bdx, b_a, b_x, lam), comm=comm)


def _scan_fwd(a, u, rg, name):
    T, C = a.shape
    tt = _tile(T, 512)

    def body(a_ref, u_ref, rg_ref, h_ref, hp_ref, hg_ref, carry):
        @pl.when(pl.program_id(0) == 0)
        def _():
            carry[...] = jnp.zeros_like(carry)

        row = lax.broadcasted_iota(jnp.int32, (SUBLANES, C), 0)

        def group(i, hprev):
            base = pl.multiple_of(i * SUBLANES, SUBLANES)
            av = a_ref[pl.ds(base, SUBLANES), :]
            uv = u_ref[pl.ds(base, SUBLANES), :]
            for s in (1, 2, 4):
                a_s = jnp.where(row >= s, pltpu.roll(av, s, 0), 1.0)
                u_s = jnp.where(row >= s, pltpu.roll(uv, s, 0), 0.0)
                uv = av * u_s + uv
                av = av * a_s
            h = av * hprev + uv
            h_ref[pl.ds(base, SUBLANES), :] = h
            hp_ref[pl.ds(base, SUBLANES), :] = jnp.where(row >= 1, pltpu.roll(h, 1, 0), hprev)
            return h[SUBLANES - 1:SUBLANES, :]

        carry[...] = lax.fori_loop(0, tt // SUBLANES, group, carry[...])
        gel, _ = _gelu(rg_ref[...].astype(F32))
        hg_ref[...] = (h_ref[...] * gel).astype(BF16)

    return _pallas(
        body, name=name, grid=(T // tt,),
        in_specs=[_rows(tt, C)] * 3,
        out_specs=[_rows(tt, C)] * 3,
        out_shape=[_sds((T, C), F32), _sds((T, C), F32), _sds((T, C), BF16)],
        scratch_shapes=[pltpu.VMEM((1, C), F32)],
        args=(a, u, rg))


def _mix_out_ln(cs, hg, gc, gr, hres, wcp, wrp, wout, g, b, name, comm=None):
    T, D = cs.shape
    C = hg.shape[1]
    tm = _tile(T, 512)

    def body(cs_ref, hg_ref, gc_ref, gr_ref, h_ref, wcp_ref, wrp_ref, wo_ref, g_ref, b_ref,
             yc_ref, yr_ref, m_ref, r_ref, y_ref, yb_ref):
        yc = _dot(cs_ref[...], wcp_ref[...])
        yr = _dot(hg_ref[...], wrp_ref[...])
        m = (_sigmoid(gc_ref[...].astype(F32)) * yc + _sigmoid(gr_ref[...].astype(F32)) * yr).astype(BF16)
        r = ALPHA * h_ref[...] + _dot(m, wo_ref[...])
        y = _ln_fwd(r, g_ref[...], b_ref[...])
        yc_ref[...] = yc.astype(BF16)
        yr_ref[...] = yr.astype(BF16)
        m_ref[...] = m
        r_ref[...] = r
        y_ref[...] = y
        yb_ref[...] = y.astype(BF16)

    return _pallas(
        body, name=name, grid=(T // tm,),
        in_specs=[_rows(tm, D), _rows(tm, C), _rows(tm, D), _rows(tm, D), _rows(tm, D), _res(wcp.shape),
                  _res(wrp.shape), _res(wout.shape), _res((1, D)), _res((1, D))],
        out_specs=[_rows(tm, D)] * 6,
        out_shape=[_sds((T, D), BF16), _sds((T, D), BF16), _sds((T, D), BF16), _sds((T, D), F32),
                   _sds((T, D), F32), _sds((T, D), BF16)],
        args=(cs, hg, gc, gr, hres, wcp, wrp, wout, g, b), comm=comm)


def _loss_ln_bwd(y, target, r, g, name):
    T, D = y.shape
    tm = _tile(T, 512)

    def body(y_ref, t_ref, r_ref, g_ref, loss_ref, dr_ref, drb_ref, dg_ref, db_ref):
        @pl.when(pl.program_id(0) == 0)
        def _():
            loss_ref[...] = jnp.zeros_like(loss_ref)
            dg_ref[...] = jnp.zeros_like(dg_ref)
            db_ref[...] = jnp.zeros_like(db_ref)

        e = y_ref[...] - t_ref[...]
        loss_ref[...] += (0.5 / D) * jnp.sum(e * e)
        dr, dg, db = _ln_bwd(e * (1.0 / D), r_ref[...], g_ref[...])
        dr_ref[...] = dr
        drb_ref[...] = dr.astype(BF16)
        dg_ref[...] += dg
        db_ref[...] += db

    return _pallas(
        body, name=name, grid=(T // tm,),
        in_specs=[_rows(tm, D), _rows(tm, D), _rows(tm, D), _res((1, D))],
        out_specs=[_acc((SUBLANES, LANES)), _rows(tm, D), _rows(tm, D), _acc((1, D)), _acc((1, D))],
        out_shape=[_sds((SUBLANES, LANES), F32), _sds((T, D), F32), _sds((T, D), BF16), _sds((1, D), F32),
                   _sds((1, D), F32)],
        args=(y, target, r, g))


def _ffn_bwd_a(drb, wd, gu, name, comm=None):
    T, D = drb.shape
    F = wd.shape[0]
    tm = _tile(T, 512)

    def body(d_ref, wd_ref, gu_ref, o_ref):
        d = d_ref[...]
        for j, cw in _chunks(F, 256):
            da = 0.5 * _dot_nt(d, wd_ref[j:j + cw, :])
            gt = gu_ref[:, j:j + cw].astype(F32)
            up = gu_ref[:, F + j:F + j + cw].astype(F32)
            sg = _sigmoid(gt)
            o_ref[:, j:j + cw] = (da * up * (sg * (1.0 + gt * (1.0 - sg)))).astype(BF16)
            o_ref[:, F + j:F + j + cw] = (da * (gt * sg)).astype(BF16)

    (dgu,) = _pallas(
        body, name=name, grid=(T // tm,),
        in_specs=[_rows(tm, D), _res(wd.shape), _rows(tm, 2 * F)],
        out_specs=[_rows(tm, 2 * F)],
        out_shape=[_sds((T, 2 * F), BF16)],
        args=(drb, wd, gu), comm=comm)
    return dgu


def _nt_res(dus, w, dres, name, ln=None, colsum=False, comm=None):
    T = dus[0].shape[0]
    widths = [d.shape[1] for d in dus]
    offs = [sum(widths[:p]) for p in range(len(dus))]
    K, D, P = sum(widths), w.shape[0], len(dus)
    tm = _tile(T, 256 if K > 6000 else 512)
    n_in = P + 2 + (2 if ln else 0)

    def body(*refs):
        du_refs, w_ref, dres_ref = refs[:P], refs[P], refs[P + 1]
        outs = refs[n_in:]
        dy = ALPHA * dres_ref[...]
        for du_ref, off, width in zip(du_refs, offs, widths):
            dy = dy + _dot_nt(du_ref[...], w_ref[:, off:off + width])
        if ln:
            r_ref, g_ref = refs[P + 2:P + 4]

            @pl.when(pl.program_id(0) == 0)
            def _():
                outs[2][...] = jnp.zeros_like(outs[2])
                outs[3][...] = jnp.zeros_like(outs[3])

            dr, dg, db = _ln_bwd(dy, r_ref[...], g_ref[...])
            outs[0][...] = dr
            outs[1][...] = dr.astype(BF16)
            outs[2][...] += dg
            outs[3][...] += db
        else:
            outs[0][...] = dy
        if colsum:
            cs_ref = outs[-1]

            @pl.when(pl.program_id(0) == 0)
            def _():
                cs_ref[...] = jnp.zeros_like(cs_ref)

            for du_ref, off, width in zip(du_refs, offs, widths):
                cs_ref[:, off:off + width] += jnp.sum(du_ref[...].astype(F32), axis=0, keepdims=True)

    in_specs = [_rows(tm, width) for width in widths] + [_res(w.shape), _rows(tm, D)]
    args = list(dus) + [w, dres]
    if ln:
        in_specs += [_rows(tm, D), _res((1, D))]
        args += list(ln)
        out_specs = [_rows(tm, D), _rows(tm, D), _acc((1, D)), _acc((1, D))]
        out_shape = [_sds((T, D), F32), _sds((T, D), BF16), _sds((1, D), F32), _sds((1, D), F32)]
    else:
        out_specs = [_rows(tm, D)]
        out_shape = [_sds((T, D), F32)]
    if colsum:
        out_specs.append(_acc((1, K)))
        out_shape.append(_sds((1, K), F32))
    return _pallas(body, name=name, grid=(T // tm,), in_specs=in_specs, out_specs=out_specs, out_shape=out_shape,
                   args=args, comm=comm)


def _mm_tn(x, dy, name, scale=1.0, comm=None):
    T, K = x.shape
    N = dy.shape[1]
    tt = _tile(T, 1024)
    tn = next(N // d for d in range(1, N // LANES + 1)
              if N % d == 0 and (N // d) % LANES == 0 and K * (N // d) * 4 <= MM_TN_OUT_BYTES)
    nt = T // tt

    def body(x_ref, dy_ref, o_ref):
        t = pl.program_id(1)

        @pl.when(t == 0)
        def _():
            o_ref[...] = jnp.zeros_like(o_ref)

        o_ref[...] += _dot_tn(x_ref[...].astype(BF16), dy_ref[...])
        if scale != 1.0:
            @pl.when(t == nt - 1)
            def _():
                o_ref[...] = o_ref[...] * scale

    (out,) = _pallas(
        body, name=name, grid=(N // tn, nt),
        in_specs=[pl.BlockSpec((tt, K), lambda j, t: (t, 0)), pl.BlockSpec((tt, tn), lambda j, t: (t, j))],
        out_specs=[pl.BlockSpec((K, tn), lambda j, t: (0, j))],
        out_shape=[_sds((K, N), F32)],
        args=(x, dy), comm=comm)
    return out


def _mix_bwd1(drb, wout, wcp, wrp, gc, gr, yc, yr, rg, h, name, comm=None):
    T, D = drb.shape
    C = rg.shape[1]
    tm = _tile(T, 512)

    def body(d_ref, wo_ref, wcp_ref, wrp_ref, gc_ref, gr_ref, yc_ref, yr_ref, rg_ref, h_ref,
             dyc_ref, dyr_ref, tail_ref, dcs_ref, dh_ref):
        dm = _dot_nt(d_ref[...], wo_ref[...])
        sc = _sigmoid(gc_ref[...].astype(F32))
        sr = _sigmoid(gr_ref[...].astype(F32))
        dyc = (dm * sc).astype(BF16)
        dyr = (dm * sr).astype(BF16)
        dyc_ref[...] = dyc
        dyr_ref[...] = dyr
        tail_ref[:, C:C + D] = (dm * yc_ref[...].astype(F32) * sc * (1.0 - sc)).astype(BF16)
        tail_ref[:, C + D:] = (dm * yr_ref[...].astype(F32) * sr * (1.0 - sr)).astype(BF16)
        dcs_ref[...] = _dot_nt(dyc, wcp_ref[...])
        dhg = _dot_nt(dyr, wrp_ref[...])
        rgv = rg_ref[...].astype(F32)
        gel, t = _gelu(rgv)
        dh_ref[...] = dhg * gel
        tail_ref[:, 0:C] = (dhg * h_ref[...] * _gelu_grad(rgv, t)).astype(BF16)

    return _pallas(
        body, name=name, grid=(T // tm,),
        in_specs=[_rows(tm, D), _res(wout.shape), _res(wcp.shape), _res(wrp.shape), _rows(tm, D), _rows(tm, D),
                  _rows(tm, D), _rows(tm, D), _rows(tm, C), _rows(tm, C)],
        out_specs=[_rows(tm, D), _rows(tm, D), _rows(tm, C + 2 * D), _rows(tm, D), _rows(tm, C)],
        out_shape=[_sds((T, D), BF16), _sds((T, D), BF16), _sds((T, C + 2 * D), BF16), _sds((T, D), F32),
                   _sds((T, C), F32)],
        args=(drb, wout, wcp, wrp, gc, gr, yc, yr, rg, h), comm=comm)


def _scan_bwd(dh, a, name):
    T, C = dh.shape
    tt = _tile(T, 512)
    nt = T // tt
    ng = tt // SUBLANES

    def body(d_ref, a_ref, g_ref, carry):
        @pl.when(pl.program_id(0) == 0)
        def _():
            carry[...] = jnp.zeros_like(carry)

        row = lax.broadcasted_iota(jnp.int32, (SUBLANES, C), 0)

        def group(j, enext):
            base = pl.multiple_of((ng - 1 - j) * SUBLANES, SUBLANES)
            av = a_ref[pl.ds(base, SUBLANES), :]
            dv = d_ref[pl.ds(base, SUBLANES), :]
            bv = av * dv
            for s in (1, 2, 4):
                keep = row < SUBLANES - s
                a_s = jnp.where(keep, pltpu.roll(av, SUBLANES - s, 0), 1.0)
                b_s = jnp.where(keep, pltpu.roll(bv, SUBLANES - s, 0), 0.0)
                bv = av * b_s + bv
                av = av * a_s
            e = av * enext + bv
            e_up = jnp.where(row < SUBLANES - 1, pltpu.roll(e, SUBLANES - 1, 0), enext)
            g_ref[pl.ds(base, SUBLANES), :] = dv + e_up
            return e[0:1, :]

        carry[...] = lax.fori_loop(0, ng, group, carry[...])

    rev = pl.BlockSpec((tt, C), lambda i: (nt - 1 - i, 0))
    (g,) = _pallas(
        body, name=name, grid=(nt,), in_specs=[rev, rev], out_specs=[rev],
        out_shape=[_sds((T, C), F32)],
        scratch_shapes=[pltpu.VMEM((1, C), F32)],
        args=(dh, a))
    return g


def _gates_bwd(g, hp, ra, ri, r, lam, bda, bdx, name, comm=None):
    T, C = g.shape
    tm = _tile(T, 512)
    nt = T // tm
    chunks = _band_chunks(C, C // RNN_BLOCKS)

    def body(g_ref, hp_ref, ra_ref, ri_ref, r_ref, lam_ref, wa_ref, wx_ref,
             dr_ref, dpa_ref, dpx_ref, dlam_ref, dba_ref, dbx_ref):
        @pl.when(pl.program_id(0) == 0)
        def _():
            dlam_ref[...] = jnp.zeros_like(dlam_ref)
            dba_ref[...] = jnp.zeros_like(dba_ref)
            dbx_ref[...] = jnp.zeros_like(dbx_ref)

        for c0, cw, _, _ in chunks:
            cols = slice(c0, c0 + cw)
            gv, rav, riv, rv = g_ref[:, cols], ra_ref[:, cols], ri_ref[:, cols], r_ref[:, cols]
            sp = _softplus_neg(lam_ref[:, cols])
            log_a = (-RG_LRU_C) * rav * sp
            av = jnp.exp(log_a)
            mult = jnp.sqrt(-_expm1(2.0 * log_a))
            d_mult = gv * riv * rv
            d_i = gv * mult * rv
            d_loga = gv * hp_ref[:, cols] * av - d_mult * (av * av) / mult
            d_ra = d_loga * ((-RG_LRU_C) * sp)
            dpa = d_ra * rav * (1.0 - rav)
            dpx = d_i * riv * (1.0 - riv)
            dpa_ref[:, cols] = dpa.astype(BF16)
            dpx_ref[:, cols] = dpx.astype(BF16)
            dr_ref[:, cols] = gv * mult * riv
            dlam_ref[:, cols] += jnp.sum(d_loga * ((-RG_LRU_C) * rav), axis=0, keepdims=True)
            dba_ref[:, cols] += jnp.sum(dpa, axis=0, keepdims=True)
            dbx_ref[:, cols] += jnp.sum(dpx, axis=0, keepdims=True)
        for c0, cw, k0, k1 in chunks:
            cols = slice(c0, c0 + cw)
            dr_ref[:, k0:k1] += (_dot_nt(dpa_ref[:, cols], wa_ref[k0:k1, cols])
                                 + _dot_nt(dpx_ref[:, cols], wx_ref[k0:k1, cols]))

        @pl.when(pl.program_id(0) == nt - 1)
        def _():
            dlam_ref[...] = dlam_ref[...] * (-_sigmoid(-lam_ref[...]))

    return _pallas(
        body, name=name, grid=(nt,),
        in_specs=[_rows(tm, C)] * 5 + [_res((1, C)), _res(bda.shape), _res(bdx.shape)],
        out_specs=[_rows(tm, C)] * 3 + [_acc((1, C))] * 3,
        out_shape=[_sds((T, C), F32), _sds((T, C), BF16), _sds((T, C), BF16)] + [_sds((1, C), F32)] * 3,
        args=(g, hp, ra, ri, r, lam, bda, bdx), comm=comm)


def _band_dw(r, dpa, dpx, name):
    T, C = r.shape
    tt = _tile(T, 512)
    chunks = _band_chunks(C, C // RNN_BLOCKS)

    def body(r_ref, dpa_ref, dpx_ref, oa_ref, ox_ref):
        @pl.when(pl.program_id(0) == 0)
        def _():
            oa_ref[...] = jnp.zeros_like(oa_ref)
            ox_ref[...] = jnp.zeros_like(ox_ref)

        for c0, cw, k0, k1 in chunks:
            cols = slice(c0, c0 + cw)
            rb = r_ref[:, k0:k1].astype(BF16)
            oa_ref[k0:k1, cols] += _dot_tn(rb, dpa_ref[:, cols])
            ox_ref[k0:k1, cols] += _dot_tn(rb, dpx_ref[:, cols])

    return _pallas(
        body, name=name, grid=(T // tt,),
        in_specs=[_rows(tt, C)] * 3,
        out_specs=[_acc((C, C)), _acc((C, C))],
        out_shape=[_sds((C, C), F32), _sds((C, C), F32)],
        args=(r, dpa, dpx))


def _conv4_bwd(dr, rx, w, name):
    T, C = dr.shape
    K = w.shape[0]
    R = CONV_ROWS
    P = RNN_PAD

    def body(d_ref, x_ref, w_ref, dx_ref, dw_ref, db_ref, dpad, xpad):
        dpad[0:T, :] = d_ref[...]
        dpad[T:T + P, :] = jnp.zeros((P, LANES), F32)
        xpad[0:P, :] = jnp.zeros((P, LANES), F32)
        xpad[P:P + T, :] = x_ref[...]
        dw_ref[...] = jnp.zeros_like(dw_ref)
        db_ref[...] = jnp.zeros_like(db_ref)
        wv = w_ref[...]

        def step(i, carry):
            base = pl.multiple_of(i * R, R)
            dwin = dpad[pl.ds(base, R + P), :]
            xwin = xpad[pl.ds(base, R + P), :]
            dcur = dwin[0:R, :]
            acc = jnp.zeros((R, LANES), F32)
            for k in range(K):
                acc = acc + wv[k:k + 1, :] * dwin[K - 1 - k:K - 1 - k + R, :]
                s = P - (K - 1) + k
                dw_ref[k:k + 1, :] += jnp.sum(dcur * xwin[s:s + R, :], axis=0, keepdims=True)
            dx_ref[pl.ds(base, R), :] = acc.astype(BF16)
            db_ref[...] += jnp.sum(dcur, axis=0, keepdims=True)
            return carry

        lax.fori_loop(0, T // R, step, 0)

    return _pallas(
        body, name=name, grid=(C // LANES,),
        in_specs=[_cols(T), _cols(T), _cols(T, K)],
        out_specs=[_cols(T), _cols(T, SUBLANES), _cols(T, 1)],
        out_shape=[_sds((T, C), BF16), _sds((SUBLANES, C), F32), _sds((1, C), F32)],
        scratch_shapes=[pltpu.VMEM((T + P, LANES), F32), pltpu.VMEM((T + P, LANES), F32)],
        args=(dr, rx, w))


def _conv31_bwd(dcs, cc, c, cv, cg, w, gg, gb, name, comm=None):
    T, D = dcs.shape
    K = w.shape[0]
    R, B, P = _tile(T, NORM_ROWS), CONV_BLOCK, CONV_PAD
    d_groups = _tap_groups([K - 1 - k for k in range(K)])
    x_groups = _tap_groups([P - (K - 1) + k for k in range(K)])

    def body(dcs_ref, cc_ref, c_ref, cv_ref, cg_ref, w_ref, gg_ref, gb_ref,
             dcv_ref, dcg_ref, dw_ref, db_ref, dgg_ref, dgb_ref, dpad, xpad, dwacc):
        dpad[T:T + P, :] = jnp.zeros((P, LANES), F32)
        xpad[0:P, :] = jnp.zeros((P, LANES), F32)
        xpad[P:P + T, :] = c_ref[...]
        dwacc[...] = jnp.zeros_like(dwacc)
        db_ref[...] = jnp.zeros_like(db_ref)
        dgg_ref[...] = jnp.zeros_like(dgg_ref)
        dgb_ref[...] = jnp.zeros_like(dgb_ref)
        wv, ggv, gbv = w_ref[...], gg_ref[...], gb_ref[...]

        def norm_step(i, carry):
            base = pl.multiple_of(i * R, R)
            xhat, rstd = _gn_stats(cc_ref[pl.ds(base, R), :])
            gn = xhat * ggv + gbv
            sg = _sigmoid(gn)
            dgn = dcs_ref[pl.ds(base, R), :] * (sg * (1.0 + gn * (1.0 - sg)))
            dgg_ref[...] += jnp.sum(dgn * xhat, axis=0, keepdims=True)
            dgb_ref[...] += jnp.sum(dgn, axis=0, keepdims=True)
            dxh = dgn * ggv
            m1 = jnp.mean(dxh, axis=-1, keepdims=True)
            m2 = jnp.mean(dxh * xhat, axis=-1, keepdims=True)
            dcc = rstd * (dxh - m1 - xhat * m2)
            dpad[pl.ds(base, R), :] = dcc
            db_ref[...] += jnp.sum(dcc, axis=0, keepdims=True)
            return carry

        lax.fori_loop(0, T // R, norm_step, 0)

        def conv_step(i, carry):
            base = pl.multiple_of(i * B, B)
            dwin = dpad[pl.ds(base, B + P), :]
            xwin = xpad[pl.ds(base, B + P), :]
            dcur = dwin[0:B, :]
            acc = _tap_sum(dwin, wv, d_groups, B)
            for phase, taps in x_groups:
                sh = _shifted(xwin, phase)
                for k, off in taps:
                    prod = dcur * sh[off:off + B, :]
                    part = prod[0:SUBLANES, :]
                    for q in range(1, B // SUBLANES):
                        part = part + prod[q * SUBLANES:(q + 1) * SUBLANES, :]
                    dwacc[k * SUBLANES:(k + 1) * SUBLANES, :] += part
            cgv = cg_ref[pl.ds(base, B), :].astype(F32)
            cvv = cv_ref[pl.ds(base, B), :].astype(F32)
            sg = _sigmoid(cgv)
            dcv_ref[pl.ds(base, B), :] = (acc * sg).astype(BF16)
            dcg_ref[pl.ds(base, B), :] = (acc * cvv * sg * (1.0 - sg)).astype(BF16)
            return carry

        lax.fori_loop(0, T // B, conv_step, 0)
        dw_ref[...] = jnp.zeros_like(dw_ref)
        for k in range(K):
            dw_ref[k:k + 1, :] = jnp.sum(dwacc[k * SUBLANES:(k + 1) * SUBLANES, :], axis=0, keepdims=True)

    return _pallas(
        body, name=name, grid=(D // LANES,),
        in_specs=[_cols(T)] * 5 + [_cols(T, K), _cols(T, 1), _cols(T, 1)],
        out_specs=[_cols(T), _cols(T), _cols(T, P), _cols(T, 1), _cols(T, 1), _cols(T, 1)],
        out_shape=[_sds((T, D), BF16), _sds((T, D), BF16), _sds((P, D), F32), _sds((1, D), F32),
                   _sds((1, D), F32), _sds((1, D), F32)],
        scratch_shapes=[pltpu.VMEM((T + P, LANES), F32), pltpu.VMEM((T + P, LANES), F32),
                        pltpu.VMEM((P * SUBLANES, LANES), F32)],
        args=(dcs, cc, c, cv, cg, w, gg, gb), comm=comm)


def _reduce_adamw(recvs, w, m, v, name):
    L, R, C = w.shape
    assert len(recvs) == L
    tr = next((c for c in (256, 176, 128, 64, 8) if R % c == 0), R)
    nr = R // tr
    c1 = 1.0 - ADAM_B1 ** ADAM_STEP
    c2 = 1.0 - ADAM_B2 ** ADAM_STEP

    def body(*refs):
        recv_refs = refs[:L]
        w_ref, m_ref, v_ref, g_ref, d_ref, mo_ref, vo_ref = refs[L:]

        def update(recv_ref):
            g = recv_ref[0].astype(F32)
            for k in range(1, N_DEV):
                g = g + recv_ref[k].astype(F32)
            mn = ADAM_B1 * m_ref[0] + (1.0 - ADAM_B1) * g
            vn = ADAM_B2 * v_ref[0] + (1.0 - ADAM_B2) * (g * g)
            g_ref[0] = g
            mo_ref[0] = mn
            vo_ref[0] = vn
            d_ref[0] = (-ADAM_LR) * ((mn / c1) / (jnp.sqrt(vn / c2) + ADAM_EPS) + ADAM_WD * w_ref[0])

        for l in range(L):
            pl.when(pl.program_id(0) == l)(lambda l=l: update(recv_refs[l]))

    def recv_spec(l):
        return pl.BlockSpec((N_DEV, tr, C), lambda j, i: (0, jnp.where(j == l, i, jnp.where(j < l, 0, nr - 1)), 0))

    blk = pl.BlockSpec((1, tr, C), lambda j, i: (j, i, 0))
    return _pallas(
        body, name=name, grid=(L, nr),
        in_specs=[recv_spec(l) for l in range(L)] + [blk, blk, blk],
        out_specs=[blk] * 4,
        out_shape=[_sds((L, R, C), F32)] * 4,
        args=(*recvs, w, m, v))


def _unshard(name, gathered):
    n, r, c = gathered.shape
    if name in COL_SHARDED:
        return gathered.transpose(1, 0, 2).reshape(r, n * c)
    return gathered.reshape(n * r, c)


def _to_shards(name, full):
    R, C = full.shape
    wire = BF16 if name in BF16_ON_WIRE else F32
    if name in COL_SHARDED:
        return full.reshape(R, N_DEV, C // N_DEV).transpose(1, 0, 2).astype(wire)
    return full.reshape(N_DEV, R // N_DEV, C).astype(wire)


def _block_diag(w):
    H, b, _ = w.shape
    eye = jnp.eye(H, dtype=w.dtype)
    return (w[:, :, None, :] * eye[:, None, :, None]).reshape(H * b, H * b)


def _diag_blocks(dense, H):
    b = dense.shape[0] // H
    eye = jnp.eye(H, dtype=dense.dtype)
    return jnp.sum(dense.reshape(H, b, H, b) * eye[:, None, :, None], axis=2)


def _pack(arrs, rows):
    flat = jnp.concatenate([a.reshape(-1) for a in arrs])
    return jnp.pad(flat, (0, rows * 1024 - flat.shape[0])).reshape(1, rows, 1024)


def _unpack(packed, shapes):
    flat = packed.reshape(-1)
    out, off = [], 0
    for s in shapes:
        n = math.prod(s)
        out.append(flat[off:off + n].reshape(s))
        off += n
    return out


class _Queue:
    def __init__(self, kind, us_per_mb):
        self.kind, self.us_per_mb, self.items, self.done = kind, us_per_mb, [], {}

    def push(self, key, arr):
        self.items.append((key, arr))

    def mb(self, arr):
        return _mbytes(arr) / (N_DEV if self.kind == "scatter" else 1)

    def take(self, micros):
        taken, budget = [], micros / self.us_per_mb
        while self.items and (not taken or self.mb(self.items[0][1]) <= budget):
            budget -= self.mb(self.items[0][1])
            taken.append(self.items.pop(0))
        return {"kind": self.kind, "keys": [k for k, _ in taken], "arrs": [a for _, a in taken]} if taken else None

    def landed(self, comm):
        if comm:
            self.done.update(zip(comm["keys"], comm["recv"]))

    def flush(self, name, upto=None):
        n = len(self.items)
        if upto is not None:
            keys = [k for k, _ in self.items]
            n = keys.index(upto) + 1 if upto in keys else 0
        if n:
            taken, self.items = self.items[:n], self.items[n:]
            self.done.update(zip([k for k, _ in taken], _exchange(self.kind, [a for _, a in taken], name)))


def kernel(x, ffn1_w_gu, ffn1_w_down, ln1_g, ln1_b, mix_w_in, mix_b_in, conv_dw_w, conv_dw_b, conv_gn_g, conv_gn_b, conv_w_proj, rnn_conv_w, rnn_conv_b, rnn_w_a, rnn_b_a, rnn_w_x, rnn_b_x, rnn_lambda, rnn_w_proj, mix_w_out, ln2_g, ln2_b, ffn2_w_gu, ffn2_w_down, ln3_g, ln3_b, loss_target, m_ffn1_w_gu, m_ffn1_w_down, m_ln1_g, m_ln1_b, m_mix_w_in, m_mix_b_in, m_conv_dw_w, m_conv_dw_b, m_conv_gn_g, m_conv_gn_b, m_conv_w_proj, m_rnn_conv_w, m_rnn_conv_b, m_rnn_w_a, m_rnn_b_a, m_rnn_w_x, m_rnn_b_x, m_rnn_lambda, m_rnn_w_proj, m_mix_w_out, m_ln2_g, m_ln2_b, m_ffn2_w_gu, m_ffn2_w_down, m_ln3_g, m_ln3_b, v_ffn1_w_gu, v_ffn1_w_down, v_ln1_g, v_ln1_b, v_mix_w_in, v_mix_b_in, v_conv_dw_w, v_conv_dw_b, v_conv_gn_g, v_conv_gn_b, v_conv_w_proj, v_rnn_conv_w, v_rnn_conv_b, v_rnn_w_a, v_rnn_b_a, v_rnn_w_x, v_rnn_b_x, v_rnn_lambda, v_rnn_w_proj, v_mix_w_out, v_ln2_g, v_ln2_b, v_ffn2_w_gu, v_ffn2_w_down, v_ln3_g, v_ln3_b):
    given = dict(locals())
    W = {n: given[n] for n in WEIGHTS}
    M = {n: given["m_" + n] for n in WEIGHTS}
    V = {n: given["v_" + n] for n in WEIGHTS}
    T, D = x.shape[1], x.shape[2]
    L = DEPTH
    x2 = x.reshape(T, D)
    target = loss_target.reshape(T, D)
    d_rnn = rnn_conv_b.shape[1]

    gather = _Queue("gather", GATHER_US_PER_MB)
    for l in range(L):
        for n in USE_ORDER:
            gather.push((n, l), W[n][l].astype(BF16) if n in BF16_ON_WIRE else W[n][l])
    gather.flush("gather_first", upto=("ffn1_w_gu", 0))
    full_cache = {}

    def full(n, l):
        if (n, l) not in full_cache:
            gather.flush(f"gather_{n}_{l}", upto=(n, l))
            full_cache[(n, l)] = _unshard(n, gather.done[(n, l)])
        return full_cache[(n, l)]

    def fwd_comm(micros):
        return gather.take(micros)

    bd_a = [_block_diag(rnn_w_a[l]).astype(BF16) for l in range(L)]
    bd_x = [_block_diag(rnn_w_x[l]).astype(BF16) for l in range(L)]

    def vec(name, l):
        return W[name][l:l + 1]

    saved = []
    h, hb = x2, x2
    for l in range(L):
        s = {"hb_in": hb}
        w_gu = full("ffn1_w_gu", l)
        comm = fwd_comm(105)
        s["gu1"], s["a1"] = _ffn_up(hb, w_gu, f"ffn1_up_{l}", comm=comm)
        gather.landed(comm)
        w_down = full("ffn1_w_down", l)
        comm = fwd_comm(65)
        s["r1"], y1, s["y1b"] = _ffn_down_ln(s["a1"], w_down, h, vec("ln1_g", l), vec("ln1_b", l),
                                              f"ffn1_down_ln_{l}", comm=comm)
        gather.landed(comm)
        w_in = full("mix_w_in", l)
        comm = fwd_comm(5)
        s["c"], s["cv"], s["cg"], s["rx"], s["rg"], s["gc"], s["gr"] = _mix_in(
            s["y1b"], w_in, vec("mix_b_in", l), d_rnn, f"mix_in_{l}", comm=comm)
        gather.landed(comm)
        w_dw = full("conv_dw_w", l)
        comm = fwd_comm(195)
        s["cc"], s["cs"] = _conv31_gn(s["c"], w_dw, vec("conv_dw_b", l), vec("conv_gn_g", l),
                                      vec("conv_gn_b", l), f"conv31_gn_{l}", comm=comm)
        gather.landed(comm)
        s["r"] = _conv4(s["rx"], full("rnn_conv_w", l), vec("rnn_conv_b", l), f"conv4_{l}")
        comm = fwd_comm(115)
        s["ra"], s["ri"], s["a"], uu = _gates(s["r"], bd_a[l], bd_x[l], vec("rnn_b_a", l), vec("rnn_b_x", l),
                                              vec("rnn_lambda", l), f"gates_{l}", comm=comm)
        gather.landed(comm)
        s["h"], s["hp"], s["hg"] = _scan_fwd(s["a"], uu, s["rg"], f"scan_{l}")
        w_cp, w_rp, w_out = full("conv_w_proj", l), full("rnn_w_proj", l), full("mix_w_out", l)
        comm = fwd_comm(85)
        s["yc"], s["yr"], s["m"], s["r2"], y2, s["y2b"] = _mix_out_ln(
            s["cs"], s["hg"], s["gc"], s["gr"], y1, w_cp, w_rp, w_out, vec("ln2_g", l), vec("ln2_b", l),
            f"mix_out_ln_{l}", comm=comm)
        gather.landed(comm)
        w_gu2 = full("ffn2_w_gu", l)
        comm = fwd_comm(105)
        s["gu2"], s["a2"] = _ffn_up(s["y2b"], w_gu2, f"ffn2_up_{l}", comm=comm)
        gather.landed(comm)
        w_down2 = full("ffn2_w_down", l)
        comm = fwd_comm(65)
        s["r3"], h, hb = _ffn_down_ln(s["a2"], w_down2, y2, vec("ln3_g", l), vec("ln3_b", l),
                                      f"ffn2_down_ln_{l}", comm=comm)
        gather.landed(comm)
        saved.append(s)

    scatter = _Queue("scatter", SCATTER_US_PER_MB)
    G = {n: [None] * L for n in WEIGHTS}

    def ready(n, l, grad):
        G[n][l] = grad
        scatter.push((n, l), _to_shards(n, grad))

    def bwd_comm(micros):
        return scatter.take(micros)

    loss_acc, dr3, drb3, G["ln3_g"][L - 1], G["ln3_b"][L - 1] = _loss_ln_bwd(
        h, target, saved[L - 1]["r3"], vec("ln3_g", L - 1), "loss_ln3_bwd")
    grad_x = small = None
    small_shapes = [W[n].shape for n in REPLICATED]
    small_rows = -(-sum(math.prod(s) for s in small_shapes) // (1024 * SUBLANES)) * SUBLANES
    for l in reversed(range(L)):
        s = saved[l]
        comm = bwd_comm(95)
        dgu2 = _ffn_bwd_a(drb3, full("ffn2_w_down", l), s["gu2"], f"ffn2_bwd_a_{l}", comm=comm)
        scatter.landed(comm)
        ready("ffn2_w_down", l, _mm_tn(s["a2"], drb3, f"ffn2_dw_down_{l}", scale=0.5))
        ready("ffn2_w_gu", l, _mm_tn(s["y2b"], dgu2, f"ffn2_dw_gu_{l}"))
        comm = bwd_comm(140)
        dr2, drb2, G["ln2_g"][l], G["ln2_b"][l] = _nt_res(
            [dgu2], full("ffn2_w_gu", l), dr3, f"ffn2_bwd_x_{l}", ln=(s["r2"], vec("ln2_g", l)), comm=comm)
        scatter.landed(comm)
        comm = bwd_comm(120)
        dyc, dyr, du_tail, dcs, dh = _mix_bwd1(
            drb2, full("mix_w_out", l), full("conv_w_proj", l), full("rnn_w_proj", l), s["gc"], s["gr"], s["yc"],
            s["yr"], s["rg"], s["h"], f"mix_bwd_out_{l}", comm=comm)
        scatter.landed(comm)
        ready("mix_w_out", l, _mm_tn(s["m"], drb2, f"mix_dw_out_{l}"))
        ready("conv_w_proj", l, _mm_tn(s["cs"], dyc, f"conv_dw_proj_{l}"))
        ready("rnn_w_proj", l, _mm_tn(s["hg"], dyr, f"rnn_dw_proj_{l}"))
        gsc = _scan_bwd(dh, s["a"], f"scan_bwd_{l}")
        comm = bwd_comm(160)
        dr_, dpa, dpx, G["rnn_lambda"][l], G["rnn_b_a"][l], G["rnn_b_x"][l] = _gates_bwd(
            gsc, s["hp"], s["ra"], s["ri"], s["r"], vec("rnn_lambda", l), bd_a[l], bd_x[l], f"gates_bwd_{l}",
            comm=comm)
        scatter.landed(comm)
        dwa, dwx = _band_dw(s["r"], dpa, dpx, f"rnn_dw_ax_{l}")
        G["rnn_w_a"][l] = _diag_blocks(dwa, RNN_BLOCKS)
        G["rnn_w_x"][l] = _diag_blocks(dwx, RNN_BLOCKS)
        drx, dw4, G["rnn_conv_b"][l] = _conv4_bwd(dr_, s["rx"], full("rnn_conv_w", l), f"conv4_bwd_{l}")
        ready("rnn_conv_w", l, dw4[:RNN_CONV_WIDTH])
        comm = bwd_comm(250)
        dcv, dcg, dw31, G["conv_dw_b"][l], G["conv_gn_g"][l], G["conv_gn_b"][l] = _conv31_bwd(
            dcs, s["cc"], s["c"], s["cv"], s["cg"], full("conv_dw_w", l), vec("conv_gn_g", l),
            vec("conv_gn_b", l), f"conv31_bwd_{l}", comm=comm)
        scatter.landed(comm)
        ready("conv_dw_w", l, dw31[:CONV_WIDTH])
        du = [dcv, dcg, drx, du_tail]
        ready("mix_w_in", l, jnp.concatenate(
            [_mm_tn(s["y1b"], piece, f"mix_dw_in_{l}_{p}") for p, piece in enumerate(du)], axis=1))
        comm = bwd_comm(175)
        dr1, drb1, G["ln1_g"][l], G["ln1_b"][l], G["mix_b_in"][l] = _nt_res(
            du, full("mix_w_in", l), dr2, f"mix_bwd_in_{l}", ln=(s["r1"], vec("ln1_g", l)), colsum=True, comm=comm)
        scatter.landed(comm)
        ready("ffn1_w_down", l, _mm_tn(s["a1"], drb1, f"ffn1_dw_down_{l}", scale=0.5))
        comm = bwd_comm(95)
        dgu1 = _ffn_bwd_a(drb1, full("ffn1_w_down", l), s["gu1"], f"ffn1_bwd_a_{l}", comm=comm)
        scatter.landed(comm)
        if l == 0:
            local_small = [jnp.stack([g.reshape(W[n].shape[1:]) for g in G[n]]) for n in REPLICATED]
            comm = {"kind": "gather", "arrs": [_pack(local_small, small_rows)[0]]}
            ready("ffn1_w_gu", l, _mm_tn(s["hb_in"], dgu1, f"ffn1_dw_gu_{l}", comm=comm))
            (small,) = comm["recv"]
        else:
            ready("ffn1_w_gu", l, _mm_tn(s["hb_in"], dgu1, f"ffn1_dw_gu_{l}"))
        if l > 0:
            comm = bwd_comm(130)
            dr3, drb3, G["ln3_g"][l - 1], G["ln3_b"][l - 1] = _nt_res(
                [dgu1], full("ffn1_w_gu", l), dr1, f"ffn1_bwd_x_{l}", ln=(saved[l - 1]["r3"], vec("ln3_g", l - 1)),
                comm=comm)
        else:
            comm = bwd_comm(1e9)
            (grad_x,) = _nt_res([dgu1], full("ffn1_w_gu", l), dr1, f"ffn1_bwd_x_{l}", comm=comm)
        scatter.landed(comm)
    scatter.flush("scatter_rest")

    loss = lax.psum(loss_acc[0, 0], ("x", "y", "c"))

    out = {}
    for n in SHARDED:
        out[n] = _reduce_adamw([scatter.done[(n, l)] for l in range(L)], W[n], M[n], V[n], f"adamw_{n}")
    packed = _reduce_adamw([small], _pack([W[n] for n in REPLICATED], small_rows),
                           _pack([M[n] for n in REPLICATED], small_rows),
                           _pack([V[n] for n in REPLICATED], small_rows), "adamw_small")
    unpacked = [_unpack(p, small_shapes) for p in packed]
    for i, n in enumerate(REPLICATED):
        out[n] = tuple(u[i] for u in unpacked)

    return (loss, grad_x.reshape(x.shape), *[out[n][0] for n in WEIGHTS], *[out[n][1] for n in WEIGHTS],
            *[out[n][2] for n in WEIGHTS], *[out[n][3] for n in WEIGHTS])
```

```python
import math

import jax
import jax.numpy as jnp
from jax import lax
from jax.experimental import pallas as pl
from jax.experimental.pallas import tpu as pltpu

F32 = jnp.float32
BF16 = jnp.bfloat16
MESH = pl.DeviceIdType.MESH

DEPTH = 2
ALPHA = (2 * DEPTH) ** 0.25
LN_EPS = 1e-5
RG_LRU_C = 8.0
CONV_WIDTH = 31
RNN_CONV_WIDTH = 4
RNN_BLOCKS = 16
N_DEV = 8
ADAM_LR, ADAM_B1, ADAM_B2, ADAM_EPS, ADAM_WD, ADAM_STEP = 0.001, 0.9, 0.999, 1e-08, 0.01, 10

LANES = 128
SUBLANES = 8
VMEM_LIMIT = 56 * 1024 * 1024
CONV_PAD = 32
RNN_PAD = 8
CONV_ROWS = 128
MM_TN_OUT_BYTES = 12 * 1024 * 1024
NORM_ROWS = 256
CONV_BLOCK = 32
GATHER_US_PER_MB = 43.0
SCATTER_US_PER_MB = 86.0

WEIGHTS = ['ffn1_w_gu', 'ffn1_w_down', 'ln1_g', 'ln1_b', 'mix_w_in', 'mix_b_in', 'conv_dw_w', 'conv_dw_b',
           'conv_gn_g', 'conv_gn_b', 'conv_w_proj', 'rnn_conv_w', 'rnn_conv_b', 'rnn_w_a', 'rnn_b_a', 'rnn_w_x',
           'rnn_b_x', 'rnn_lambda', 'rnn_w_proj', 'mix_w_out', 'ln2_g', 'ln2_b', 'ffn2_w_gu', 'ffn2_w_down',
           'ln3_g', 'ln3_b']
COL_SHARDED = ['ffn1_w_gu', 'mix_w_in', 'ffn2_w_gu', 'conv_dw_w', 'rnn_conv_w']
TRANSPOSED = ['ffn1_w_gu', 'mix_w_in', 'ffn2_w_gu']
ROW_SHARDED = ['ffn1_w_down', 'conv_w_proj', 'rnn_w_proj', 'mix_w_out', 'ffn2_w_down']
SHARDED = COL_SHARDED + ROW_SHARDED
BF16_ON_WIRE = ['ffn1_w_gu', 'mix_w_in', 'ffn2_w_gu', 'ffn1_w_down', 'conv_w_proj', 'rnn_w_proj', 'mix_w_out',
                'ffn2_w_down']
REPLICATED = [n for n in WEIGHTS if n not in SHARDED]
USE_ORDER = ['ffn1_w_gu', 'ffn1_w_down', 'mix_w_in', 'conv_dw_w', 'rnn_conv_w', 'conv_w_proj', 'rnn_w_proj',
             'mix_w_out', 'ffn2_w_gu', 'ffn2_w_down']


def _cp(n_axes=1):
    return pltpu.CompilerParams(dimension_semantics=("arbitrary",) * n_axes, vmem_limit_bytes=VMEM_LIMIT)


def _rows(tm, c):
    return pl.BlockSpec((tm, c), lambda i: (i, 0))


def _res(shape):
    nd = len(shape)
    return pl.BlockSpec(tuple(shape), lambda *_: (0,) * nd, pipeline_mode=pl.Buffered(1))


def _acc(shape):
    nd = len(shape)
    return pl.BlockSpec(tuple(shape), lambda *_: (0,) * nd)


def _tile(t, want):
    return want if t % want == 0 else t


def _sds(shape, dtype):
    return jax.ShapeDtypeStruct(tuple(shape), dtype)


def _mbytes(a):
    return a.size * a.dtype.itemsize / 1e6


def _ln_fwd(r, g, b):
    mu = jnp.mean(r, axis=-1, keepdims=True)
    xc = r - mu
    var = jnp.mean(xc * xc, axis=-1, keepdims=True)
    return xc * lax.rsqrt(var + LN_EPS) * g + b


def _ln_bwd(dy, r, g):
    mu = jnp.mean(r, axis=-1, keepdims=True)
    xc = r - mu
    var = jnp.mean(xc * xc, axis=-1, keepdims=True)
    rstd = lax.rsqrt(var + LN_EPS)
    xhat = xc * rstd
    dxh = dy * g
    m1 = jnp.mean(dxh, axis=-1, keepdims=True)
    m2 = jnp.mean(dxh * xhat, axis=-1, keepdims=True)
    dr = rstd * (dxh - m1 - xhat * m2)
    return dr, jnp.sum(dy * xhat, axis=0, keepdims=True), jnp.sum(dy, axis=0, keepdims=True)


def _sigmoid(x):
    return jax.nn.sigmoid(x)


_GELU_K = math.sqrt(2.0 / math.pi)


def _gelu(x):
    t = jnp.tanh(_GELU_K * (x + 0.044715 * x * x * x))
    return 0.5 * x * (1.0 + t), t


def _gelu_grad(x, t):
    return 0.5 * (1.0 + t) + 0.5 * x * (1.0 - t * t) * (_GELU_K * (1.0 + 3.0 * 0.044715 * x * x))


def _expm1(x):
    taylor = x * (1.0 + x * (0.5 + x * (1.0 / 6.0 + x * (1.0 / 24.0 + x * (1.0 / 120.0)))))
    return jnp.where(jnp.abs(x) < 0.03, taylor, jnp.exp(x) - 1.0)


def _softplus_neg(lam):
    return jnp.maximum(-lam, 0.0) + jnp.log1p(jnp.exp(-jnp.abs(lam)))


def _dot(a, b):
    return jnp.dot(a, b, preferred_element_type=F32)


def _dot_nt(a, b):
    return lax.dot_general(a, b, (((1,), (1,)), ((), ())), preferred_element_type=F32)


def _dot_tn(a, b):
    return lax.dot_general(a, b, (((0,), (0,)), ((), ())), preferred_element_type=F32)


def _chunks(width, cn):
    return [(j, min(cn, width - j)) for j in range(0, width, cn)]


def _band_chunks(width, block):
    out = []
    for c0, cw in _chunks(width, 256):
        lo = (c0 // block) * block
        hi = ((c0 + cw - 1) // block + 1) * block
        out.append((c0, cw, lo // LANES * LANES, min(width, -(-hi // LANES) * LANES)))
    return out


def _position():
    return lax.axis_index("x"), lax.axis_index("y"), lax.axis_index("c")


def _index(p):
    return 4 * p[0] + 2 * p[1] + p[2]


def _comm_out_shapes(kind, arrs):
    return [_sds((N_DEV,) + a.shape if kind == "gather" else a.shape, a.dtype) for a in arrs]


def _comm_scratch(n):
    return [pltpu.SemaphoreType.DMA((n, 7)), pltpu.SemaphoreType.DMA((n, 7)), pltpu.SemaphoreType.DMA((n,))]


def _comm_phases(kind, srcs, dsts, send_sems, recv_sems, local_sems):
    n = len(srcs)
    x, y, c = _position()
    me, sibling = (x, y, c), (x, y, 1 - c)
    chips = [(1 - x, y), (x, 1 - y), (1 - x, 1 - y)]

    if kind == "gather":
        def copy(a, k, block, to, src=None):
            dst = dsts[a].at[_index(block)]
            return pltpu.make_async_remote_copy(
                src_ref=dst if src is None else src, dst_ref=dst, send_sem=send_sems.at[a, k],
                recv_sem=recv_sems.at[a, k], device_id=to, device_id_type=MESH)

        def mine(a):
            return pltpu.make_async_copy(srcs[a], dsts[a].at[_index(me)], local_sems.at[a])

        def first(a):
            return [copy(a, 0, me, sibling, src=srcs[a])] + [
                copy(a, 1 + j, me, (*chip, c), src=srcs[a]) for j, chip in enumerate(chips)]

        def start():
            for a in range(n):
                mine(a).start()
            for a in range(n):
                for cp in first(a):
                    cp.start()

        def mid():
            for j, chip in enumerate(chips):
                for a in range(n):
                    copy(a, 1 + j, (*chip, c), me).wait_recv()
                    copy(a, 4 + j, (*chip, c), sibling).start()

        def end():
            for a in range(n):
                copy(a, 0, sibling, me).wait_recv()
            for j, chip in enumerate(chips):
                for a in range(n):
                    copy(a, 4 + j, (*chip, 1 - c), me).wait_recv()
            for a in range(n):
                for cp in first(a):
                    cp.wait_send()
                for j, chip in enumerate(chips):
                    copy(a, 4 + j, (*chip, c), sibling).wait_send()
                mine(a).wait()

        return start, mid, end

    def peer_of(k):
        return (1 - x if k & 4 else x, 1 - y if k & 2 else y, 1 - c if k & 1 else c)

    def own(a):
        return pltpu.make_async_copy(srcs[a].at[_index(me)], dsts[a].at[0], local_sems.at[a])

    def remote(a, k):
        peer = peer_of(k)
        return pltpu.make_async_remote_copy(
            src_ref=srcs[a].at[_index(peer)], dst_ref=dsts[a].at[k], send_sem=send_sems.at[a, k - 1],
            recv_sem=recv_sems.at[a, k - 1], device_id=peer, device_id_type=MESH)

    def start():
        for a in range(n):
            own(a).start()
        for k in range(1, N_DEV):
            for a in range(n):
                remote(a, k).start()

    def end():
        for k in range(1, N_DEV):
            for a in range(n):
                remote(a, k).wait()
        for a in range(n):
            own(a).wait()

    return start, (lambda: None), end


def _exchange(kind, arrs, name):
    n = len(arrs)
    hbm = pl.BlockSpec(memory_space=pl.ANY)

    def body(*refs):
        start, mid, end = _comm_phases(kind, refs[:n], refs[n:2 * n], *refs[2 * n:])
        start()
        mid()
        end()

    return pl.pallas_call(
        body, name=name, in_specs=[hbm] * n, out_specs=[hbm] * n, out_shape=_comm_out_shapes(kind, arrs),
        scratch_shapes=_comm_scratch(n))(*arrs)


def _pallas(body, *, name, grid, in_specs, out_specs, out_shape, args, scratch_shapes=(), comm=None):
    in_specs, out_specs, out_shape = list(in_specs), list(out_specs), list(out_shape)
    scratch_shapes = list(scratch_shapes)
    if not comm:
        return pl.pallas_call(body, name=name, grid=grid, in_specs=in_specs, out_specs=out_specs,
                              out_shape=out_shape, scratch_shapes=scratch_shapes,
                              compiler_params=_cp(len(grid)))(*args)
    arrs = comm["arrs"]
    ns, n_in, n_out, n_scr = len(arrs), len(in_specs), len(out_specs), len(scratch_shapes)
    hbm = pl.BlockSpec(memory_space=pl.ANY)
    total = math.prod(grid)

    def carrier(*refs):
        ins, srcs = refs[:n_in], refs[n_in:n_in + ns]
        outs, dsts = refs[n_in + ns:n_in + ns + n_out], refs[n_in + ns + n_out:n_in + 2 * ns + n_out]
        scr, sems = refs[n_in + 2 * ns + n_out:n_in + 2 * ns + n_out + n_scr], refs[n_in + 2 * ns + n_out + n_scr:]
        step = pl.program_id(0)
        for ax in range(1, len(grid)):
            step = step * grid[ax] + pl.program_id(ax)
        start, mid, end = _comm_phases(comm["kind"], srcs, dsts, *sems)
        pl.when(step == 0)(start)
        body(*ins, *outs, *scr)
        pl.when(step == total - 1)(mid)
        pl.when(step == total - 1)(end)

    res = pl.pallas_call(
        carrier, name=name, grid=grid, in_specs=in_specs + [hbm] * ns, out_specs=out_specs + [hbm] * ns,
        out_shape=out_shape + _comm_out_shapes(comm["kind"], arrs),
        scratch_shapes=scratch_shapes + _comm_scratch(ns), compiler_params=_cp(len(grid)))(*args, *arrs)
    comm["recv"] = res[n_out:]
    return res[:n_out]


def _ffn_up(xb, w, name, comm=None):
    T, D = xb.shape
    F = w.shape[0] // 2
    tm = _tile(T, 512)

    def body(x_ref, w_ref, gu_ref, a_ref):
        x = x_ref[...].astype(BF16)
        for j, cw in _chunks(F, 256):
            g = _dot_nt(x, w_ref[j:j + cw, :])
            u = _dot_nt(x, w_ref[F + j:F + j + cw, :])
            gu_ref[:, j:j + cw] = g.astype(BF16)
            gu_ref[:, F + j:F + j + cw] = u.astype(BF16)
            a_ref[:, j:j + cw] = (g * _sigmoid(g) * u).astype(BF16)

    return _pallas(
        body, name=name, grid=(T // tm,),
        in_specs=[_rows(tm, D), _res(w.shape)],
        out_specs=[_rows(tm, 2 * F), _rows(tm, F)],
        out_shape=[_sds((T, 2 * F), BF16), _sds((T, F), BF16)],
        args=(xb, w), comm=comm)


def _ffn_down_ln(a, wd, xres, g, b, name, comm=None):
    T, F = a.shape
    D = wd.shape[1]
    tm = _tile(T, 512)

    def body(a_ref, wd_ref, x_ref, g_ref, b_ref, r_ref, y_ref, yb_ref):
        r = ALPHA * x_ref[...] + 0.5 * _dot(a_ref[...], wd_ref[...])
        y = _ln_fwd(r, g_ref[...], b_ref[...])
        r_ref[...] = r
        y_ref[...] = y
        yb_ref[...] = y.astype(BF16)

    return _pallas(
        body, name=name, grid=(T // tm,),
        in_specs=[_rows(tm, F), _res(wd.shape), _rows(tm, D), _res((1, D)), _res((1, D))],
        out_specs=[_rows(tm, D), _rows(tm, D), _rows(tm, D)],
        out_shape=[_sds((T, D), F32), _sds((T, D), F32), _sds((T, D), BF16)],
        args=(a, wd, xres, g, b), comm=comm)


def _mix_in(hb, w, bias, d_rnn, name, comm=None):
    T, D = hb.shape
    R = d_rnn
    tm = _tile(T, 256)
    o_cv, o_cg, o_rx, o_rg, o_gc, o_gr = 0, D, 2 * D, 2 * D + R, 2 * D + 2 * R, 3 * D + 2 * R

    def body(x_ref, w_ref, b_ref, c_ref, cv_ref, cg_ref, rx_ref, rg_ref, gc_ref, gr_ref):
        x = x_ref[...]

        def seg(off, j, cw):
            return _dot_nt(x, w_ref[off + j:off + j + cw, :]) + b_ref[:, off + j:off + j + cw]

        for j, cw in _chunks(D, 256):
            cv = seg(o_cv, j, cw)
            cg = seg(o_cg, j, cw)
            cv_ref[:, j:j + cw] = cv.astype(BF16)
            cg_ref[:, j:j + cw] = cg.astype(BF16)
            c_ref[:, j:j + cw] = cv * _sigmoid(cg)
            gc_ref[:, j:j + cw] = seg(o_gc, j, cw).astype(BF16)
            gr_ref[:, j:j + cw] = seg(o_gr, j, cw).astype(BF16)
        for j, cw in _chunks(R, 256):
            rx_ref[:, j:j + cw] = seg(o_rx, j, cw)
            rg_ref[:, j:j + cw] = seg(o_rg, j, cw).astype(BF16)

    return _pallas(
        body, name=name, grid=(T // tm,),
        in_specs=[_rows(tm, D), _res(w.shape), _res(bias.shape)],
        out_specs=[_rows(tm, D), _rows(tm, D), _rows(tm, D), _rows(tm, R), _rows(tm, R), _rows(tm, D),
                   _rows(tm, D)],
        out_shape=[_sds((T, D), F32), _sds((T, D), BF16), _sds((T, D), BF16), _sds((T, R), F32),
                   _sds((T, R), BF16), _sds((T, D), BF16), _sds((T, D), BF16)],
        args=(hb, w, bias), comm=comm)


def _cols(t, rows=None):
    return pl.BlockSpec((t if rows is None else rows, LANES), lambda j: (0, j))


def _gn_stats(cc):
    mu = jnp.mean(cc, axis=-1, keepdims=True)
    xc = cc - mu
    var = jnp.mean(xc * xc, axis=-1, keepdims=True)
    rstd = lax.rsqrt(var + LN_EPS)
    return xc * rstd, rstd


def _tap_groups(offsets):
    groups = {}
    for k, s in enumerate(offsets):
        groups.setdefault(s % SUBLANES, []).append((k, s - s % SUBLANES))
    return sorted(groups.items())


def _shifted(win, phase):
    return win if phase == 0 else pltpu.roll(win, win.shape[0] - phase, 0)


def _tap_sum(win, wv, groups, rows):
    parts, t = [None] * 4, 0
    for phase, taps in groups:
        sh = _shifted(win, phase)
        for k, off in taps:
            term = wv[k:k + 1, :] * sh[off:off + rows, :]
            parts[t % 4] = term if parts[t % 4] is None else parts[t % 4] + term
            t += 1
    return (parts[0] + parts[1]) + (parts[2] + parts[3])


def _conv31_gn(c, w, bias, gg, gb, name, comm=None):
    T, D = c.shape
    K = w.shape[0]
    B, P, N = CONV_BLOCK, CONV_PAD, _tile(T, NORM_ROWS)
    assert D % LANES == 0 and T % B == 0 and K - 1 <= P
    groups = _tap_groups([P - (K - 1) + k for k in range(K)])

    def body(c_ref, w_ref, b_ref, gg_ref, gb_ref, cc_ref, cs_ref, xpad):
        xpad[0:P, :] = jnp.zeros((P, LANES), F32)
        xpad[P:P + T, :] = c_ref[...]
        wv = w_ref[...]
        bv, ggv, gbv = b_ref[...], gg_ref[...], gb_ref[...]

        def conv_step(i, carry):
            base = pl.multiple_of(i * B, B)
            win = xpad[pl.ds(base, B + P), :]
            cc_ref[pl.ds(base, B), :] = _tap_sum(win, wv, groups, B) + bv
            return carry

        lax.fori_loop(0, T // B, conv_step, 0)

        def norm_step(i, carry):
            base = pl.multiple_of(i * N, N)
            xhat, _ = _gn_stats(cc_ref[pl.ds(base, N), :])
            gn = xhat * ggv + gbv
            cs_ref[pl.ds(base, N), :] = (gn * _sigmoid(gn)).astype(BF16)
            return carry

        lax.fori_loop(0, T // N, norm_step, 0)

    return _pallas(
        body, name=name, grid=(D // LANES,),
        in_specs=[_cols(T), _cols(T, K), _cols(T, 1), _cols(T, 1), _cols(T, 1)],
        out_specs=[_cols(T), _cols(T)],
        out_shape=[_sds((T, D), F32), _sds((T, D), BF16)],
        scratch_shapes=[pltpu.VMEM((T + P, LANES), F32)],
        args=(c, w, bias, gg, gb), comm=comm)


def _conv4(rx, w, bias, name):
    T, C = rx.shape
    K = w.shape[0]
    R = CONV_ROWS
    assert C % LANES == 0 and T % R == 0 and K - 1 <= RNN_PAD

    def body(x_ref, w_ref, b_ref, r_ref, xpad):
        xpad[0:RNN_PAD, :] = jnp.zeros((RNN_PAD, LANES), F32)
        xpad[RNN_PAD:RNN_PAD + T, :] = x_ref[...]
        wv, bv = w_ref[...], b_ref[...]

        def step(i, carry):
            base = pl.multiple_of(i * R, R)
            win = xpad[pl.ds(base, R + RNN_PAD), :]
            acc = jnp.zeros((R, LANES), F32)
            for k in range(K):
                s = RNN_PAD - (K - 1) + k
                acc = acc + wv[k:k + 1, :] * win[s:s + R, :]
            r_ref[pl.ds(base, R), :] = acc + bv
            return carry

        lax.fori_loop(0, T // R, step, 0)

    (r,) = _pallas(
        body, name=name, grid=(C // LANES,),
        in_specs=[_cols(T), _cols(T, K), _cols(T, 1)],
        out_specs=[_cols(T)],
        out_shape=[_sds((T, C), F32)],
        scratch_shapes=[pltpu.VMEM((T + RNN_PAD, LANES), F32)],
        args=(rx, w, bias))
    return r


def _gates(r, bda, bdx, b_a, b_x, lam, name, comm=None):
    T, C = r.shape
    tm = _tile(T, 512)
    chunks = _band_chunks(C, C // RNN_BLOCKS)

    def body(r_ref, wa_ref, wx_ref, ba_ref, bx_ref, lam_ref, ra_ref, ri_ref, a_ref, u_ref):
        for c0, cw, k0, k1 in chunks:
            cols = slice(c0, c0 + cw)
            rb = r_ref[:, k0:k1].astype(BF16)
            ra = _sigmoid(_dot(rb, wa_ref[k0:k1, cols]) + ba_ref[:, cols])
            ri = _sigmoid(_dot(rb, wx_ref[k0:k1, cols]) + bx_ref[:, cols])
            log_a = (-RG_LRU_C) * ra * _softplus_neg(lam_ref[:, cols])
            ra_ref[:, cols] = ra
            ri_ref[:, cols] = ri
            a_ref[:, cols] = jnp.exp(log_a)
            u_ref[:, cols] = jnp.sqrt(-_expm1(2.0 * log_a)) * (ri * r_ref[:, cols])

    return _pallas(
        body, name=name, grid=(T // tm,),
        in_specs=[_rows(tm, C), _res(bda.shape), _res(bdx.shape), _res((1, C)), _res((1, C)), _res((1, C))],
        out_specs=[_rows(tm, C)] * 4,
        out_shape=[_sds((T, C), F32)] * 4,
        args=(r, bda, bdx, b_a, b_x, lam), comm=comm)


def _scan_fwd(a, u, rg, name):
    T, C = a.shape
    tt = _tile(T, 512)

    def body(a_ref, u_ref, rg_ref, h_ref, hp_ref, hg_ref, carry):
        @pl.when(pl.program_id(0) == 0)
        def _():
            carry[...] = jnp.zeros_like(carry)

        row = lax.broadcasted_iota(jnp.int32, (SUBLANES, C), 0)

        def group(i, hprev):
            base = pl.multiple_of(i * SUBLANES, SUBLANES)
            av = a_ref[pl.ds(base, SUBLANES), :]
            uv = u_ref[pl.ds(base, SUBLANES), :]
            for s in (1, 2, 4):
                a_s = jnp.where(row >= s, pltpu.roll(av, s, 0), 1.0)
                u_s = jnp.where(row >= s, pltpu.roll(uv, s, 0), 0.0)
                uv = av * u_s + uv
                av = av * a_s
            h = av * hprev + uv
            h_ref[pl.ds(base, SUBLANES), :] = h
            hp_ref[pl.ds(base, SUBLANES), :] = jnp.where(row >= 1, pltpu.roll(h, 1, 0), hprev)
            return h[SUBLANES - 1:SUBLANES, :]

        carry[...] = lax.fori_loop(0, tt // SUBLANES, group, carry[...])
        gel, _ = _gelu(rg_ref[...].astype(F32))
        hg_ref[...] = (h_ref[...] * gel).astype(BF16)

    return _pallas(
        body, name=name, grid=(T // tt,),
        in_specs=[_rows(tt, C)] * 3,
        out_specs=[_rows(tt, C)] * 3,
        out_shape=[_sds((T, C), F32), _sds((T, C), F32), _sds((T, C), BF16)],
        scratch_shapes=[pltpu.VMEM((1, C), F32)],
        args=(a, u, rg))


def _mix_out_ln(cs, hg, gc, gr, hres, wcp, wrp, wout, g, b, name, comm=None):
    T, D = cs.shape
    C = hg.shape[1]
    tm = _tile(T, 512)

    def body(cs_ref, hg_ref, gc_ref, gr_ref, h_ref, wcp_ref, wrp_ref, wo_ref, g_ref, b_ref,
             yc_ref, yr_ref, m_ref, r_ref, y_ref, yb_ref):
        yc = _dot(cs_ref[...], wcp_ref[...])
        yr = _dot(hg_ref[...], wrp_ref[...])
        m = (_sigmoid(gc_ref[...].astype(F32)) * yc + _sigmoid(gr_ref[...].astype(F32)) * yr).astype(BF16)
        r = ALPHA * h_ref[...] + _dot(m, wo_ref[...])
        y = _ln_fwd(r, g_ref[...], b_ref[...])
        yc_ref[...] = yc.astype(BF16)
        yr_ref[...] = yr.astype(BF16)
        m_ref[...] = m
        r_ref[...] = r
        y_ref[...] = y
        yb_ref[...] = y.astype(BF16)

    return _pallas(
        body, name=name, grid=(T // tm,),
        in_specs=[_rows(tm, D), _rows(tm, C), _rows(tm, D), _rows(tm, D), _rows(tm, D), _res(wcp.shape),
                  _res(wrp.shape), _res(wout.shape), _res((1, D)), _res((1, D))],
        out_specs=[_rows(tm, D)] * 6,
        out_shape=[_sds((T, D), BF16), _sds((T, D), BF16), _sds((T, D), BF16), _sds((T, D), F32),
                   _sds((T, D), F32), _sds((T, D), BF16)],
        args=(cs, hg, gc, gr, hres, wcp, wrp, wout, g, b), comm=comm)


def _loss_ln_bwd(y, target, r, g, name):
    T, D = y.shape
    tm = _tile(T, 512)

    def body(y_ref, t_ref, r_ref, g_ref, loss_ref, dr_ref, drb_ref, dg_ref, db_ref):
        @pl.when(pl.program_id(0) == 0)
        def _():
            loss_ref[...] = jnp.zeros_like(loss_ref)
            dg_ref[...] = jnp.zeros_like(dg_ref)
            db_ref[...] = jnp.zeros_like(db_ref)

        e = y_ref[...] - t_ref[...]
        loss_ref[...] += (0.5 / D) * jnp.sum(e * e)
        dr, dg, db = _ln_bwd(e * (1.0 / D), r_ref[...], g_ref[...])
        dr_ref[...] = dr
        drb_ref[...] = dr.astype(BF16)
        dg_ref[...] += dg
        db_ref[...] += db

    return _pallas(
        body, name=name, grid=(T // tm,),
        in_specs=[_rows(tm, D), _rows(tm, D), _rows(tm, D), _res((1, D))],
        out_specs=[_acc((SUBLANES, LANES)), _rows(tm, D), _rows(tm, D), _acc((1, D)), _acc((1, D))],
        out_shape=[_sds((SUBLANES, LANES), F32), _sds((T, D), F32), _sds((T, D), BF16), _sds((1, D), F32),
                   _sds((1, D), F32)],
        args=(y, target, r, g))


def _ffn_bwd_a(drb, wd, gu, name, comm=None):
    T, D = drb.shape
    F = wd.shape[0]
    tm = _tile(T, 512)

    def body(d_ref, wd_ref, gu_ref, o_ref):
        d = d_ref[...]
        for j, cw in _chunks(F, 256):
            da = 0.5 * _dot_nt(d, wd_ref[j:j + cw, :])
            gt = gu_ref[:, j:j + cw].astype(F32)
            up = gu_ref[:, F + j:F + j + cw].astype(F32)
            sg = _sigmoid(gt)
            o_ref[:, j:j + cw] = (da * up * (sg * (1.0 + gt * (1.0 - sg)))).astype(BF16)
            o_ref[:, F + j:F + j + cw] = (da * (gt * sg)).astype(BF16)

    (dgu,) = _pallas(
        body, name=name, grid=(T // tm,),
        in_specs=[_rows(tm, D), _res(wd.shape), _rows(tm, 2 * F)],
        out_specs=[_rows(tm, 2 * F)],
        out_shape=[_sds((T, 2 * F), BF16)],
        args=(drb, wd, gu), comm=comm)
    return dgu


def _nt_res(dus, w, dres, name, ln=None, colsum=False, comm=None):
    T = dus[0].shape[0]
    widths = [d.shape[1] for d in dus]
    offs = [sum(widths[:p]) for p in range(len(dus))]
    K, D, P = sum(widths), w.shape[1], len(dus)
    tm = _tile(T, 256 if K > 6000 else 512)
    n_in = P + 2 + (2 if ln else 0)

    def body(*refs):
        du_refs, w_ref, dres_ref = refs[:P], refs[P], refs[P + 1]
        outs = refs[n_in:]
        dy = ALPHA * dres_ref[...]
        for du_ref, off, width in zip(du_refs, offs, widths):
            dy = dy + _dot(du_ref[...], w_ref[off:off + width, :])
        if ln:
            r_ref, g_ref = refs[P + 2:P + 4]

            @pl.when(pl.program_id(0) == 0)
            def _():
                outs[2][...] = jnp.zeros_like(outs[2])
                outs[3][...] = jnp.zeros_like(outs[3])

            dr, dg, db = _ln_bwd(dy, r_ref[...], g_ref[...])
            outs[0][...] = dr
            outs[1][...] = dr.astype(BF16)
            outs[2][...] += dg
            outs[3][...] += db
        else:
            outs[0][...] = dy
        if colsum:
            cs_ref = outs[-1]

            @pl.when(pl.program_id(0) == 0)
            def _():
                cs_ref[...] = jnp.zeros_like(cs_ref)

            for du_ref, off, width in zip(du_refs, offs, widths):
                cs_ref[:, off:off + width] += jnp.sum(du_ref[...].astype(F32), axis=0, keepdims=True)

    in_specs = [_rows(tm, width) for width in widths] + [_res(w.shape), _rows(tm, D)]
    args = list(dus) + [w, dres]
    if ln:
        in_specs += [_rows(tm, D), _res((1, D))]
        args += list(ln)
        out_specs = [_rows(tm, D), _rows(tm, D), _acc((1, D)), _acc((1, D))]
        out_shape = [_sds((T, D), F32), _sds((T, D), BF16), _sds((1, D), F32), _sds((1, D), F32)]
    else:
        out_specs = [_rows(tm, D)]
        out_shape = [_sds((T, D), F32)]
    if colsum:
        out_specs.append(_acc((1, K)))
        out_shape.append(_sds((1, K), F32))
    return _pallas(body, name=name, grid=(T // tm,), in_specs=in_specs, out_specs=out_specs, out_shape=out_shape,
                   args=args, comm=comm)


def _mm_tn(x, dy, name, scale=1.0, comm=None):
    T, K = x.shape
    N = dy.shape[1]
    tt = _tile(T, 1024)
    tn = next(N // d for d in range(1, N // LANES + 1)
              if N % d == 0 and (N // d) % LANES == 0 and K * (N // d) * 4 <= MM_TN_OUT_BYTES)
    nt = T // tt

    def body(x_ref, dy_ref, o_ref):
        t = pl.program_id(1)

        @pl.when(t == 0)
        def _():
            o_ref[...] = jnp.zeros_like(o_ref)

        o_ref[...] += _dot_tn(x_ref[...].astype(BF16), dy_ref[...])
        if scale != 1.0:
            @pl.when(t == nt - 1)
            def _():
                o_ref[...] = o_ref[...] * scale

    (out,) = _pallas(
        body, name=name, grid=(N // tn, nt),
        in_specs=[pl.BlockSpec((tt, K), lambda j, t: (t, 0)), pl.BlockSpec((tt, tn), lambda j, t: (t, j))],
        out_specs=[pl.BlockSpec((K, tn), lambda j, t: (0, j))],
        out_shape=[_sds((K, N), F32)],
        args=(x, dy), comm=comm)
    return out


def _mix_bwd1(drb, wout, wcp, wrp, gc, gr, yc, yr, rg, h, name, comm=None):
    T, D = drb.shape
    C = rg.shape[1]
    tm = _tile(T, 512)

    def body(d_ref, wo_ref, wcp_ref, wrp_ref, gc_ref, gr_ref, yc_ref, yr_ref, rg_ref, h_ref,
             dyc_ref, dyr_ref, tail_ref, dcs_ref, dh_ref):
        dm = _dot_nt(d_ref[...], wo_ref[...])
        sc = _sigmoid(gc_ref[...].astype(F32))
        sr = _sigmoid(gr_ref[...].astype(F32))
        dyc = (dm * sc).astype(BF16)
        dyr = (dm * sr).astype(BF16)
        dyc_ref[...] = dyc
        dyr_ref[...] = dyr
        tail_ref[:, C:C + D] = (dm * yc_ref[...].astype(F32) * sc * (1.0 - sc)).astype(BF16)
        tail_ref[:, C + D:] = (dm * yr_ref[...].astype(F32) * sr * (1.0 - sr)).astype(BF16)
        dcs_ref[...] = _dot_nt(dyc, wcp_ref[...])
        dhg = _dot_nt(dyr, wrp_ref[...])
        rgv = rg_ref[...].astype(F32)
        gel, t = _gelu(rgv)
        dh_ref[...] = dhg * gel
        tail_ref[:, 0:C] = (dhg * h_ref[...] * _gelu_grad(rgv, t)).astype(BF16)

    return _pallas(
        body, name=name, grid=(T // tm,),
        in_specs=[_rows(tm, D), _res(wout.shape), _res(wcp.shape), _res(wrp.shape), _rows(tm, D), _rows(tm, D),
                  _rows(tm, D), _rows(tm, D), _rows(tm, C), _rows(tm, C)],
        out_specs=[_rows(tm, D), _rows(tm, D), _rows(tm, C + 2 * D), _rows(tm, D), _rows(tm, C)],
        out_shape=[_sds((T, D), BF16), _sds((T, D), BF16), _sds((T, C + 2 * D), BF16), _sds((T, D), F32),
                   _sds((T, C), F32)],
        args=(drb, wout, wcp, wrp, gc, gr, yc, yr, rg, h), comm=comm)


def _scan_bwd(dh, a, name):
    T, C = dh.shape
    tt = _tile(T, 512)
    nt = T // tt
    ng = tt // SUBLANES

    def body(d_ref, a_ref, g_ref, carry):
        @pl.when(pl.program_id(0) == 0)
        def _():
            carry[...] = jnp.zeros_like(carry)

        row = lax.broadcasted_iota(jnp.int32, (SUBLANES, C), 0)

        def group(j, enext):
            base = pl.multiple_of((ng - 1 - j) * SUBLANES, SUBLANES)
            av = a_ref[pl.ds(base, SUBLANES), :]
            dv = d_ref[pl.ds(base, SUBLANES), :]
            bv = av * dv
            for s in (1, 2, 4):
                keep = row < SUBLANES - s
                a_s = jnp.where(keep, pltpu.roll(av, SUBLANES - s, 0), 1.0)
                b_s = jnp.where(keep, pltpu.roll(bv, SUBLANES - s, 0), 0.0)
                bv = av * b_s + bv
                av = av * a_s
            e = av * enext + bv
            e_up = jnp.where(row < SUBLANES - 1, pltpu.roll(e, SUBLANES - 1, 0), enext)
            g_ref[pl.ds(base, SUBLANES), :] = dv + e_up
            return e[0:1, :]

        carry[...] = lax.fori_loop(0, ng, group, carry[...])

    rev = pl.BlockSpec((tt, C), lambda i: (nt - 1 - i, 0))
    (g,) = _pallas(
        body, name=name, grid=(nt,), in_specs=[rev, rev], out_specs=[rev],
        out_shape=[_sds((T, C), F32)],
        scratch_shapes=[pltpu.VMEM((1, C), F32)],
        args=(dh, a))
    return g


def _gates_bwd(g, hp, ra, ri, r, lam, bda, bdx, name, comm=None):
    T, C = g.shape
    tm = _tile(T, 512)
    nt = T // tm
    chunks = _band_chunks(C, C // RNN_BLOCKS)

    def body(g_ref, hp_ref, ra_ref, ri_ref, r_ref, lam_ref, wa_ref, wx_ref,
             dr_ref, dpa_ref, dpx_ref, dlam_ref, dba_ref, dbx_ref):
        @pl.when(pl.program_id(0) == 0)
        def _():
            dlam_ref[...] = jnp.zeros_like(dlam_ref)
            dba_ref[...] = jnp.zeros_like(dba_ref)
            dbx_ref[...] = jnp.zeros_like(dbx_ref)

        for c0, cw, _, _ in chunks:
            cols = slice(c0, c0 + cw)
            gv, rav, riv, rv = g_ref[:, cols], ra_ref[:, cols], ri_ref[:, cols], r_ref[:, cols]
            sp = _softplus_neg(lam_ref[:, cols])
            log_a = (-RG_LRU_C) * rav * sp
            av = jnp.exp(log_a)
            mult = jnp.sqrt(-_expm1(2.0 * log_a))
            d_mult = gv * riv * rv
            d_i = gv * mult * rv
            d_loga = gv * hp_ref[:, cols] * av - d_mult * (av * av) / mult
            d_ra = d_loga * ((-RG_LRU_C) * sp)
            dpa = d_ra * rav * (1.0 - rav)
            dpx = d_i * riv * (1.0 - riv)
            dpa_ref[:, cols] = dpa.astype(BF16)
            dpx_ref[:, cols] = dpx.astype(BF16)
            dr_ref[:, cols] = gv * mult * riv
            dlam_ref[:, cols] += jnp.sum(d_loga * ((-RG_LRU_C) * rav), axis=0, keepdims=True)
            dba_ref[:, cols] += jnp.sum(dpa, axis=0, keepdims=True)
            dbx_ref[:, cols] += jnp.sum(dpx, axis=0, keepdims=True)
        for c0, cw, k0, k1 in chunks:
            cols = slice(c0, c0 + cw)
            dr_ref[:, k0:k1] += (_dot_nt(dpa_ref[:, cols], wa_ref[k0:k1, cols])
                                 + _dot_nt(dpx_ref[:, cols], wx_ref[k0:k1, cols]))

        @pl.when(pl.program_id(0) == nt - 1)
        def _():
            dlam_ref[...] = dlam_ref[...] * (-_sigmoid(-lam_ref[...]))

    return _pallas(
        body, name=name, grid=(nt,),
        in_specs=[_rows(tm, C)] * 5 + [_res((1, C)), _res(bda.shape), _res(bdx.shape)],
        out_specs=[_rows(tm, C)] * 3 + [_acc((1, C))] * 3,
        out_shape=[_sds((T, C), F32), _sds((T, C), BF16), _sds((T, C), BF16)] + [_sds((1, C), F32)] * 3,
        args=(g, hp, ra, ri, r, lam, bda, bdx), comm=comm)


def _band_dw(r, dpa, dpx, name):
    T, C = r.shape
    tt = _tile(T, 512)
    chunks = _band_chunks(C, C // RNN_BLOCKS)

    def body(r_ref, dpa_ref, dpx_ref, oa_ref, ox_ref):
        @pl.when(pl.program_id(0) == 0)
        def _():
            oa_ref[...] = jnp.zeros_like(oa_ref)
            ox_ref[...] = jnp.zeros_like(ox_ref)

        for c0, cw, k0, k1 in chunks:
            cols = slice(c0, c0 + cw)
            rb = r_ref[:, k0:k1].astype(BF16)
            oa_ref[k0:k1, cols] += _dot_tn(rb, dpa_ref[:, cols])
            ox_ref[k0:k1, cols] += _dot_tn(rb, dpx_ref[:, cols])

    return _pallas(
        body, name=name, grid=(T // tt,),
        in_specs=[_rows(tt, C)] * 3,
        out_specs=[_acc((C, C)), _acc((C, C))],
        out_shape=[_sds((C, C), F32), _sds((C, C), F32)],
        args=(r, dpa, dpx))


def _conv4_bwd(dr, rx, w, name):
    T, C = dr.shape
    K = w.shape[0]
    R = CONV_ROWS
    P = RNN_PAD

    def body(d_ref, x_ref, w_ref, dx_ref, dw_ref, db_ref, dpad, xpad):
        dpad[0:T, :] = d_ref[...]
        dpad[T:T + P, :] = jnp.zeros((P, LANES), F32)
        xpad[0:P, :] = jnp.zeros((P, LANES), F32)
        xpad[P:P + T, :] = x_ref[...]
        dw_ref[...] = jnp.zeros_like(dw_ref)
        db_ref[...] = jnp.zeros_like(db_ref)
        wv = w_ref[...]

        def step(i, carry):
            base = pl.multiple_of(i * R, R)
            dwin = dpad[pl.ds(base, R + P), :]
            xwin = xpad[pl.ds(base, R + P), :]
            dcur = dwin[0:R, :]
            acc = jnp.zeros((R, LANES), F32)
            for k in range(K):
                acc = acc + wv[k:k + 1, :] * dwin[K - 1 - k:K - 1 - k + R, :]
                s = P - (K - 1) + k
                dw_ref[k:k + 1, :] += jnp.sum(dcur * xwin[s:s + R, :], axis=0, keepdims=True)
            dx_ref[pl.ds(base, R), :] = acc.astype(BF16)
            db_ref[...] += jnp.sum(dcur, axis=0, keepdims=True)
            return carry

        lax.fori_loop(0, T // R, step, 0)

    return _pallas(
        body, name=name, grid=(C // LANES,),
        in_specs=[_cols(T), _cols(T), _cols(T, K)],
        out_specs=[_cols(T), _cols(T, SUBLANES), _cols(T, 1)],
        out_shape=[_sds((T, C), BF16), _sds((SUBLANES, C), F32), _sds((1, C), F32)],
        scratch_shapes=[pltpu.VMEM((T + P, LANES), F32), pltpu.VMEM((T + P, LANES), F32)],
        args=(dr, rx, w))


def _conv31_bwd(dcs, cc, c, cv, cg, w, gg, gb, name, comm=None):
    T, D = dcs.shape
    K = w.shape[0]
    R, B, P = _tile(T, NORM_ROWS), CONV_BLOCK, CONV_PAD
    d_groups = _tap_groups([K - 1 - k for k in range(K)])
    x_groups = _tap_groups([P - (K - 1) + k for k in range(K)])

    def body(dcs_ref, cc_ref, c_ref, cv_ref, cg_ref, w_ref, gg_ref, gb_ref,
             dcv_ref, dcg_ref, dw_ref, db_ref, dgg_ref, dgb_ref, dpad, xpad, dwacc):
        dpad[T:T + P, :] = jnp.zeros((P, LANES), F32)
        xpad[0:P, :] = jnp.zeros((P, LANES), F32)
        xpad[P:P + T, :] = c_ref[...]
        dwacc[...] = jnp.zeros_like(dwacc)
        db_ref[...] = jnp.zeros_like(db_ref)
        dgg_ref[...] = jnp.zeros_like(dgg_ref)
        dgb_ref[...] = jnp.zeros_like(dgb_ref)
        wv, ggv, gbv = w_ref[...], gg_ref[...], gb_ref[...]

        def norm_step(i, carry):
            base = pl.multiple_of(i * R, R)
            xhat, rstd = _gn_stats(cc_ref[pl.ds(base, R), :])
            gn = xhat * ggv + gbv
            sg = _sigmoid(gn)
            dgn = dcs_ref[pl.ds(base, R), :] * (sg * (1.0 + gn * (1.0 - sg)))
            dgg_ref[...] += jnp.sum(dgn * xhat, axis=0, keepdims=True)
            dgb_ref[...] += jnp.sum(dgn, axis=0, keepdims=True)
            dxh = dgn * ggv
            m1 = jnp.mean(dxh, axis=-1, keepdims=True)
            m2 = jnp.mean(dxh * xhat, axis=-1, keepdims=True)
            dcc = rstd * (dxh - m1 - xhat * m2)
            dpad[pl.ds(base, R), :] = dcc
            db_ref[...] += jnp.sum(dcc, axis=0, keepdims=True)
            return carry

        lax.fori_loop(0, T // R, norm_step, 0)

        def conv_step(i, carry):
            base = pl.multiple_of(i * B, B)
            dwin = dpad[pl.ds(base, B + P), :]
            xwin = xpad[pl.ds(base, B + P), :]
            dcur = dwin[0:B, :]
            acc = _tap_sum(dwin, wv, d_groups, B)
            for phase, taps in x_groups:
                sh = _shifted(xwin, phase)
                for k, off in taps:
                    prod = dcur * sh[off:off + B, :]
                    part = prod[0:SUBLANES, :]
                    for q in range(1, B // SUBLANES):
                        part = part + prod[q * SUBLANES:(q + 1) * SUBLANES, :]
                    dwacc[k * SUBLANES:(k + 1) * SUBLANES, :] += part
            cgv = cg_ref[pl.ds(base, B), :].astype(F32)
            cvv = cv_ref[pl.ds(base, B), :].astype(F32)
            sg = _sigmoid(cgv)
            dcv_ref[pl.ds(base, B), :] = (acc * sg).astype(BF16)
            dcg_ref[pl.ds(base, B), :] = (acc * cvv * sg * (1.0 - sg)).astype(BF16)
            return carry

        lax.fori_loop(0, T // B, conv_step, 0)
        dw_ref[...] = jnp.zeros_like(dw_ref)
        for k in range(K):
            dw_ref[k:k + 1, :] = jnp.sum(dwacc[k * SUBLANES:(k + 1) * SUBLANES, :], axis=0, keepdims=True)

    return _pallas(
        body, name=name, grid=(D // LANES,),
        in_specs=[_cols(T)] * 5 + [_cols(T, K), _cols(T, 1), _cols(T, 1)],
        out_specs=[_cols(T), _cols(T), _cols(T, P), _cols(T, 1), _cols(T, 1), _cols(T, 1)],
        out_shape=[_sds((T, D), BF16), _sds((T, D), BF16), _sds((P, D), F32), _sds((1, D), F32),
                   _sds((1, D), F32), _sds((1, D), F32)],
        scratch_shapes=[pltpu.VMEM((T + P, LANES), F32), pltpu.VMEM((T + P, LANES), F32),
                        pltpu.VMEM((P * SUBLANES, LANES), F32)],
        args=(dcs, cc, c, cv, cg, w, gg, gb), comm=comm)


def _reduce_adamw(recvs, w, m, v, name):
    L, R, C = w.shape
    assert len(recvs) == L
    tr = next((c for c in (256, 176, 128, 64, 8) if R % c == 0), R)
    nr = R // tr
    c1 = 1.0 - ADAM_B1 ** ADAM_STEP
    c2 = 1.0 - ADAM_B2 ** ADAM_STEP

    def body(*refs):
        recv_refs = refs[:L]
        w_ref, m_ref, v_ref, g_ref, d_ref, mo_ref, vo_ref = refs[L:]

        def update(recv_ref):
            g = recv_ref[0].astype(F32)
            for k in range(1, N_DEV):
                g = g + recv_ref[k].astype(F32)
            mn = ADAM_B1 * m_ref[0] + (1.0 - ADAM_B1) * g
            vn = ADAM_B2 * v_ref[0] + (1.0 - ADAM_B2) * (g * g)
            g_ref[0] = g
            mo_ref[0] = mn
            vo_ref[0] = vn
            d_ref[0] = (-ADAM_LR) * ((mn / c1) / (jnp.sqrt(vn / c2) + ADAM_EPS) + ADAM_WD * w_ref[0])

        for l in range(L):
            pl.when(pl.program_id(0) == l)(lambda l=l: update(recv_refs[l]))

    def recv_spec(l):
        return pl.BlockSpec((N_DEV, tr, C), lambda j, i: (0, jnp.where(j == l, i, jnp.where(j < l, 0, nr - 1)), 0))

    blk = pl.BlockSpec((1, tr, C), lambda j, i: (j, i, 0))
    return _pallas(
        body, name=name, grid=(L, nr),
        in_specs=[recv_spec(l) for l in range(L)] + [blk, blk, blk],
        out_specs=[blk] * 4,
        out_shape=[_sds((L, R, C), F32)] * 4,
        args=(*recvs, w, m, v))


def _unshard(name, gathered):
    n, r, c = gathered.shape
    if name in COL_SHARDED and name not in TRANSPOSED:
        return gathered.transpose(1, 0, 2).reshape(r, n * c)
    return gathered.reshape(n * r, c)


def _to_shards(name, full):
    R, C = full.shape
    wire = BF16 if name in BF16_ON_WIRE else F32
    if name in TRANSPOSED:
        return full.reshape(R, N_DEV, C // N_DEV).transpose(1, 2, 0).astype(wire)
    if name in COL_SHARDED:
        return full.reshape(R, N_DEV, C // N_DEV).transpose(1, 0, 2).astype(wire)
    return full.reshape(N_DEV, R // N_DEV, C).astype(wire)


def _block_diag(w):
    H, b, _ = w.shape
    eye = jnp.eye(H, dtype=w.dtype)
    return (w[:, :, None, :] * eye[:, None, :, None]).reshape(H * b, H * b)


def _diag_blocks(dense, H):
    b = dense.shape[0] // H
    eye = jnp.eye(H, dtype=dense.dtype)
    return jnp.sum(dense.reshape(H, b, H, b) * eye[:, None, :, None], axis=2)


def _pack(arrs, rows):
    flat = jnp.concatenate([a.reshape(-1) for a in arrs])
    return jnp.pad(flat, (0, rows * 1024 - flat.shape[0])).reshape(1, rows, 1024)


def _unpack(packed, shapes):
    flat = packed.reshape(-1)
    out, off = [], 0
    for s in shapes:
        n = math.prod(s)
        out.append(flat[off:off + n].reshape(s))
        off += n
    return out


class _Queue:
    def __init__(self, kind, us_per_mb):
        self.kind, self.us_per_mb, self.items, self.done = kind, us_per_mb, [], {}

    def push(self, key, arr):
        self.items.append((key, arr))

    def mb(self, arr):
        return _mbytes(arr) / (N_DEV if self.kind == "scatter" else 1)

    def take(self, micros):
        taken, budget = [], micros / self.us_per_mb
        while self.items and (not taken or self.mb(self.items[0][1]) <= budget):
            budget -= self.mb(self.items[0][1])
            taken.append(self.items.pop(0))
        return {"kind": self.kind, "keys": [k for k, _ in taken], "arrs": [a for _, a in taken]} if taken else None

    def landed(self, comm):
        if comm:
            self.done.update(zip(comm["keys"], comm["recv"]))

    def flush(self, name, upto=None):
        n = len(self.items)
        if upto is not None:
            keys = [k for k, _ in self.items]
            n = keys.index(upto) + 1 if upto in keys else 0
        if n:
            taken, self.items = self.items[:n], self.items[n:]
            self.done.update(zip([k for k, _ in taken], _exchange(self.kind, [a for _, a in taken], name)))


def kernel(x, ffn1_w_gu, ffn1_w_down, ln1_g, ln1_b, mix_w_in, mix_b_in, conv_dw_w, conv_dw_b, conv_gn_g, conv_gn_b, conv_w_proj, rnn_conv_w, rnn_conv_b, rnn_w_a, rnn_b_a, rnn_w_x, rnn_b_x, rnn_lambda, rnn_w_proj, mix_w_out, ln2_g, ln2_b, ffn2_w_gu, ffn2_w_down, ln3_g, ln3_b, loss_target, m_ffn1_w_gu, m_ffn1_w_down, m_ln1_g, m_ln1_b, m_mix_w_in, m_mix_b_in, m_conv_dw_w, m_conv_dw_b, m_conv_gn_g, m_conv_gn_b, m_conv_w_proj, m_rnn_conv_w, m_rnn_conv_b, m_rnn_w_a, m_rnn_b_a, m_rnn_w_x, m_rnn_b_x, m_rnn_lambda, m_rnn_w_proj, m_mix_w_out, m_ln2_g, m_ln2_b, m_ffn2_w_gu, m_ffn2_w_down, m_ln3_g, m_ln3_b, v_ffn1_w_gu, v_ffn1_w_down, v_ln1_g, v_ln1_b, v_mix_w_in, v_mix_b_in, v_conv_dw_w, v_conv_dw_b, v_conv_gn_g, v_conv_gn_b, v_conv_w_proj, v_rnn_conv_w, v_rnn_conv_b, v_rnn_w_a, v_rnn_b_a, v_rnn_w_x, v_rnn_b_x, v_rnn_lambda, v_rnn_w_proj, v_mix_w_out, v_ln2_g, v_ln2_b, v_ffn2_w_gu, v_ffn2_w_down, v_ln3_g, v_ln3_b):
    given = dict(locals())
    W = {n: given[n] for n in WEIGHTS}
    M = {n: given["m_" + n] for n in WEIGHTS}
    V = {n: given["v_" + n] for n in WEIGHTS}
    T, D = x.shape[1], x.shape[2]
    L = DEPTH
    x2 = x.reshape(T, D)
    target = loss_target.reshape(T, D)
    d_rnn = rnn_conv_b.shape[1]

    gather = _Queue("gather", GATHER_US_PER_MB)
    for l in range(L):
        for n in USE_ORDER:
            shard = W[n][l].T if n in TRANSPOSED else W[n][l]
            gather.push((n, l), shard.astype(BF16) if n in BF16_ON_WIRE else shard)
    gather.flush("gather_first", upto=("ffn1_w_gu", 0))
    full_cache = {}

    def full(n, l):
        if (n, l) not in full_cache:
            gather.flush(f"gather_{n}_{l}", upto=(n, l))
            full_cache[(n, l)] = _unshard(n, gather.done[(n, l)])
        return full_cache[(n, l)]

    def fwd_comm(micros):
        return gather.take(micros)

    bd_a = [_block_diag(rnn_w_a[l]).astype(BF16) for l in range(L)]
    bd_x = [_block_diag(rnn_w_x[l]).astype(BF16) for l in range(L)]

    def vec(name, l):
        return W[name][l:l + 1]

    saved = []
    h, hb = x2, x2
    for l in range(L):
        s = {"hb_in": hb}
        w_gu = full("ffn1_w_gu", l)
        comm = fwd_comm(105)
        s["gu1"], s["a1"] = _ffn_up(hb, w_gu, f"ffn1_up_{l}", comm=comm)
        gather.landed(comm)
        w_down = full("ffn1_w_down", l)
        comm = fwd_comm(65)
        s["r1"], y1, s["y1b"] = _ffn_down_ln(s["a1"], w_down, h, vec("ln1_g", l), vec("ln1_b", l),
                                              f"ffn1_down_ln_{l}", comm=comm)
        gather.landed(comm)
        w_in = full("mix_w_in", l)
        comm = fwd_comm(5)
        s["c"], s["cv"], s["cg"], s["rx"], s["rg"], s["gc"], s["gr"] = _mix_in(
            s["y1b"], w_in, vec("mix_b_in", l), d_rnn, f"mix_in_{l}", comm=comm)
        gather.landed(comm)
        w_dw = full("conv_dw_w", l)
        comm = fwd_comm(195)
        s["cc"], s["cs"] = _conv31_gn(s["c"], w_dw, vec("conv_dw_b", l), vec("conv_gn_g", l),
                                      vec("conv_gn_b", l), f"conv31_gn_{l}", comm=comm)
        gather.landed(comm)
        s["r"] = _conv4(s["rx"], full("rnn_conv_w", l), vec("rnn_conv_b", l), f"conv4_{l}")
        comm = fwd_comm(115)
        s["ra"], s["ri"], s["a"], uu = _gates(s["r"], bd_a[l], bd_x[l], vec("rnn_b_a", l), vec("rnn_b_x", l),
                                              vec("rnn_lambda", l), f"gates_{l}", comm=comm)
        gather.landed(comm)
        s["h"], s["hp"], s["hg"] = _scan_fwd(s["a"], uu, s["rg"], f"scan_{l}")
        w_cp, w_rp, w_out = full("conv_w_proj", l), full("rnn_w_proj", l), full("mix_w_out", l)
        comm = fwd_comm(85)
        s["yc"], s["yr"], s["m"], s["r2"], y2, s["y2b"] = _mix_out_ln(
            s["cs"], s["hg"], s["gc"], s["gr"], y1, w_cp, w_rp, w_out, vec("ln2_g", l), vec("ln2_b", l),
            f"mix_out_ln_{l}", comm=comm)
        gather.landed(comm)
        w_gu2 = full("ffn2_w_gu", l)
        comm = fwd_comm(105)
        s["gu2"], s["a2"] = _ffn_up(s["y2b"], w_gu2, f"ffn2_up_{l}", comm=comm)
        gather.landed(comm)
        w_down2 = full("ffn2_w_down", l)
        comm = fwd_comm(65)
        s["r3"], h, hb = _ffn_down_ln(s["a2"], w_down2, y2, vec("ln3_g", l), vec("ln3_b", l),
                                      f"ffn2_down_ln_{l}", comm=comm)
        gather.landed(comm)
        saved.append(s)

    scatter = _Queue("scatter", SCATTER_US_PER_MB)
    G = {n: [None] * L for n in WEIGHTS}

    def ready(n, l, grad):
        G[n][l] = grad
        scatter.push((n, l), _to_shards(n, grad))

    def bwd_comm(micros):
        return scatter.take(micros)

    loss_acc, dr3, drb3, G["ln3_g"][L - 1], G["ln3_b"][L - 1] = _loss_ln_bwd(
        h, target, saved[L - 1]["r3"], vec("ln3_g", L - 1), "loss_ln3_bwd")
    grad_x = small = None
    small_shapes = [W[n].shape for n in REPLICATED]
    small_rows = -(-sum(math.prod(s) for s in small_shapes) // (1024 * SUBLANES)) * SUBLANES
    for l in reversed(range(L)):
        s = saved[l]
        comm = bwd_comm(95)
        dgu2 = _ffn_bwd_a(drb3, full("ffn2_w_down", l), s["gu2"], f"ffn2_bwd_a_{l}", comm=comm)
        scatter.landed(comm)
        ready("ffn2_w_down", l, _mm_tn(s["a2"], drb3, f"ffn2_dw_down_{l}", scale=0.5))
        ready("ffn2_w_gu", l, _mm_tn(s["y2b"], dgu2, f"ffn2_dw_gu_{l}"))
        comm = bwd_comm(140)
        dr2, drb2, G["ln2_g"][l], G["ln2_b"][l] = _nt_res(
            [dgu2], full("ffn2_w_gu", l), dr3, f"ffn2_bwd_x_{l}", ln=(s["r2"], vec("ln2_g", l)), comm=comm)
        scatter.landed(comm)
        comm = bwd_comm(120)
        dyc, dyr, du_tail, dcs, dh = _mix_bwd1(
            drb2, full("mix_w_out", l), full("conv_w_proj", l), full("rnn_w_proj", l), s["gc"], s["gr"], s["yc"],
            s["yr"], s["rg"], s["h"], f"mix_bwd_out_{l}", comm=comm)
        scatter.landed(comm)
        ready("mix_w_out", l, _mm_tn(s["m"], drb2, f"mix_dw_out_{l}"))
        ready("conv_w_proj", l, _mm_tn(s["cs"], dyc, f"conv_dw_proj_{l}"))
        ready("rnn_w_proj", l, _mm_tn(s["hg"], dyr, f"rnn_dw_proj_{l}"))
        gsc = _scan_bwd(dh, s["a"], f"scan_bwd_{l}")
        comm = bwd_comm(160)
        dr_, dpa, dpx, G["rnn_lambda"][l], G["rnn_b_a"][l], G["rnn_b_x"][l] = _gates_bwd(
            gsc, s["hp"], s["ra"], s["ri"], s["r"], vec("rnn_lambda", l), bd_a[l], bd_x[l], f"gates_bwd_{l}",
            comm=comm)
        scatter.landed(comm)
        dwa, dwx = _band_dw(s["r"], dpa, dpx, f"rnn_dw_ax_{l}")
        G["rnn_w_a"][l] = _diag_blocks(dwa, RNN_BLOCKS)
        G["rnn_w_x"][l] = _diag_blocks(dwx, RNN_BLOCKS)
        drx, dw4, G["rnn_conv_b"][l] = _conv4_bwd(dr_, s["rx"], full("rnn_conv_w", l), f"conv4_bwd_{l}")
        ready("rnn_conv_w", l, dw4[:RNN_CONV_WIDTH])
        comm = bwd_comm(250)
        dcv, dcg, dw31, G["conv_dw_b"][l], G["conv_gn_g"][l], G["conv_gn_b"][l] = _conv31_bwd(
            dcs, s["cc"], s["c"], s["cv"], s["cg"], full("conv_dw_w", l), vec("conv_gn_g", l),
            vec("conv_gn_b", l), f"conv31_bwd_{l}", comm=comm)
        scatter.landed(comm)
        ready("conv_dw_w", l, dw31[:CONV_WIDTH])
        du = [dcv, dcg, drx, du_tail]
        ready("mix_w_in", l, jnp.concatenate(
            [_mm_tn(s["y1b"], piece, f"mix_dw_in_{l}_{p}") for p, piece in enumerate(du)], axis=1))
        comm = bwd_comm(175)
        dr1, drb1, G["ln1_g"][l], G["ln1_b"][l], G["mix_b_in"][l] = _nt_res(
            du, full("mix_w_in", l), dr2, f"mix_bwd_in_{l}", ln=(s["r1"], vec("ln1_g", l)), colsum=True, comm=comm)
        scatter.landed(comm)
        ready("ffn1_w_down", l, _mm_tn(s["a1"], drb1, f"ffn1_dw_down_{l}", scale=0.5))
        comm = bwd_comm(95)
        dgu1 = _ffn_bwd_a(drb1, full("ffn1_w_down", l), s["gu1"], f"ffn1_bwd_a_{l}", comm=comm)
        scatter.landed(comm)
        if l == 0:
            local_small = [jnp.stack([g.reshape(W[n].shape[1:]) for g in G[n]]) for n in REPLICATED]
            comm = {"kind": "gather", "arrs": [_pack(local_small, small_rows)[0]]}
            ready("ffn1_w_gu", l, _mm_tn(s["hb_in"], dgu1, f"ffn1_dw_gu_{l}", comm=comm))
            (small,) = comm["recv"]
        else:
            ready("ffn1_w_gu", l, _mm_tn(s["hb_in"], dgu1, f"ffn1_dw_gu_{l}"))
        if l > 0:
            comm = bwd_comm(130)
            dr3, drb3, G["ln3_g"][l - 1], G["ln3_b"][l - 1] = _nt_res(
                [dgu1], full("ffn1_w_gu", l), dr1, f"ffn1_bwd_x_{l}", ln=(saved[l - 1]["r3"], vec("ln3_g", l - 1)),
                comm=comm)
        else:
            comm = bwd_comm(1e9)
            (grad_x,) = _nt_res([dgu1], full("ffn1_w_gu", l), dr1, f"ffn1_bwd_x_{l}", comm=comm)
        scatter.landed(comm)
    scatter.flush("scatter_rest")

    loss = lax.psum(loss_acc[0, 0], ("x", "y", "c"))

    out = {}
    for n in SHARDED:
        recvs = [scatter.done[(n, l)] for l in range(L)]
        if n in TRANSPOSED:
            res = _reduce_adamw(recvs, *[jnp.swapaxes(a, 1, 2) for a in (W[n], M[n], V[n])], f"adamw_{n}")
            out[n] = [jnp.swapaxes(a, 1, 2) for a in res]
        else:
            out[n] = _reduce_adamw(recvs, W[n], M[n], V[n], f"adamw_{n}")
    packed = _reduce_adamw([small], _pack([W[n] for n in REPLICATED], small_rows),
                           _pack([M[n] for n in REPLICATED], small_rows),
                           _pack([V[n] for n in REPLICATED], small_rows), "adamw_small")
    unpacked = [_unpack(p, small_shapes) for p in packed]
    for i, n in enumerate(REPLICATED):
        out[n] = tuple(u[i] for u in unpacked)

    return (loss, grad_x.reshape(x.shape), *[out[n][0] for n in WEIGHTS], *[out[n][1] for n in WEIGHTS],
            *[out[n][2] for n in WEIGHTS], *[out[n][3] for n in WEIGHTS])
```

```python
import math

import jax
import jax.numpy as jnp
from jax import lax
from jax.experimental import pallas as pl
from jax.experimental.pallas import tpu as pltpu

F32 = jnp.float32
BF16 = jnp.bfloat16
MESH = pl.DeviceIdType.MESH

DEPTH = 2
ALPHA = (2 * DEPTH) ** 0.25
LN_EPS = 1e-5
RG_LRU_C = 8.0
CONV_WIDTH = 31
RNN_CONV_WIDTH = 4
RNN_BLOCKS = 16
N_DEV = 8
ADAM_LR, ADAM_B1, ADAM_B2, ADAM_EPS, ADAM_WD, ADAM_STEP = 0.001, 0.9, 0.999, 1e-08, 0.01, 10

LANES = 128
SUBLANES = 8
VMEM_LIMIT = 56 * 1024 * 1024
CONV_PAD = 32
RNN_PAD = 8
CONV_ROWS = 128
ADAMW_ROWS = 352
MM_TN_OUT_BYTES = 12 * 1024 * 1024
NORM_ROWS = 256
CONV_BLOCK = 32
GATHER_US_PER_MB = 43.0
SCATTER_US_PER_MB = 86.0

WEIGHTS = ['ffn1_w_gu', 'ffn1_w_down', 'ln1_g', 'ln1_b', 'mix_w_in', 'mix_b_in', 'conv_dw_w', 'conv_dw_b',
           'conv_gn_g', 'conv_gn_b', 'conv_w_proj', 'rnn_conv_w', 'rnn_conv_b', 'rnn_w_a', 'rnn_b_a', 'rnn_w_x',
           'rnn_b_x', 'rnn_lambda', 'rnn_w_proj', 'mix_w_out', 'ln2_g', 'ln2_b', 'ffn2_w_gu', 'ffn2_w_down',
           'ln3_g', 'ln3_b']
COL_SHARDED = ['ffn1_w_gu', 'mix_w_in', 'ffn2_w_gu', 'conv_dw_w', 'rnn_conv_w']
TRANSPOSED = ['ffn1_w_gu', 'mix_w_in', 'ffn2_w_gu']
ROW_SHARDED = ['ffn1_w_down', 'conv_w_proj', 'rnn_w_proj', 'mix_w_out', 'ffn2_w_down']
SHARDED = COL_SHARDED + ROW_SHARDED
BF16_ON_WIRE = ['ffn1_w_gu', 'mix_w_in', 'ffn2_w_gu', 'ffn1_w_down', 'conv_w_proj', 'rnn_w_proj', 'mix_w_out',
                'ffn2_w_down']
REPLICATED = [n for n in WEIGHTS if n not in SHARDED]
USE_ORDER = ['ffn1_w_gu', 'ffn1_w_down', 'mix_w_in', 'conv_dw_w', 'rnn_conv_w', 'conv_w_proj', 'rnn_w_proj',
             'mix_w_out', 'ffn2_w_gu', 'ffn2_w_down']


def _cp(n_axes=1):
    return pltpu.CompilerParams(dimension_semantics=("arbitrary",) * n_axes, vmem_limit_bytes=VMEM_LIMIT)


def _rows(tm, c):
    return pl.BlockSpec((tm, c), lambda i: (i, 0))


def _res(shape):
    nd = len(shape)
    return pl.BlockSpec(tuple(shape), lambda *_: (0,) * nd, pipeline_mode=pl.Buffered(1))


def _acc(shape):
    nd = len(shape)
    return pl.BlockSpec(tuple(shape), lambda *_: (0,) * nd)


def _tile(t, want):
    return want if t % want == 0 else t


def _sds(shape, dtype):
    return jax.ShapeDtypeStruct(tuple(shape), dtype)


def _mbytes(a):
    return a.size * a.dtype.itemsize / 1e6


def _ln_fwd(r, g, b):
    mu = jnp.mean(r, axis=-1, keepdims=True)
    xc = r - mu
    var = jnp.mean(xc * xc, axis=-1, keepdims=True)
    return xc * lax.rsqrt(var + LN_EPS) * g + b


def _ln_bwd(dy, r, g):
    mu = jnp.mean(r, axis=-1, keepdims=True)
    xc = r - mu
    var = jnp.mean(xc * xc, axis=-1, keepdims=True)
    rstd = lax.rsqrt(var + LN_EPS)
    xhat = xc * rstd
    dxh = dy * g
    m1 = jnp.mean(dxh, axis=-1, keepdims=True)
    m2 = jnp.mean(dxh * xhat, axis=-1, keepdims=True)
    dr = rstd * (dxh - m1 - xhat * m2)
    return dr, jnp.sum(dy * xhat, axis=0, keepdims=True), jnp.sum(dy, axis=0, keepdims=True)


def _sigmoid(x):
    return jax.nn.sigmoid(x)


_GELU_K = math.sqrt(2.0 / math.pi)


def _gelu(x):
    t = jnp.tanh(_GELU_K * (x + 0.044715 * x * x * x))
    return 0.5 * x * (1.0 + t), t


def _gelu_grad(x, t):
    return 0.5 * (1.0 + t) + 0.5 * x * (1.0 - t * t) * (_GELU_K * (1.0 + 3.0 * 0.044715 * x * x))


def _expm1(x):
    taylor = x * (1.0 + x * (0.5 + x * (1.0 / 6.0 + x * (1.0 / 24.0 + x * (1.0 / 120.0)))))
    return jnp.where(jnp.abs(x) < 0.03, taylor, jnp.exp(x) - 1.0)


def _softplus_neg(lam):
    return jnp.maximum(-lam, 0.0) + jnp.log1p(jnp.exp(-jnp.abs(lam)))


def _dot(a, b):
    return jnp.dot(a, b, preferred_element_type=F32)


def _dot_nt(a, b):
    return lax.dot_general(a, b, (((1,), (1,)), ((), ())), preferred_element_type=F32)


def _dot_tn(a, b):
    return lax.dot_general(a, b, (((0,), (0,)), ((), ())), preferred_element_type=F32)


def _chunks(width, cn):
    return [(j, min(cn, width - j)) for j in range(0, width, cn)]


def _band_chunks(width, block):
    out = []
    for c0, cw in _chunks(width, 256):
        lo = (c0 // block) * block
        hi = ((c0 + cw - 1) // block + 1) * block
        out.append((c0, cw, lo // LANES * LANES, min(width, -(-hi // LANES) * LANES)))
    return out


def _position():
    return lax.axis_index("x"), lax.axis_index("y"), lax.axis_index("c")


def _index(p):
    return 4 * p[0] + 2 * p[1] + p[2]


def _comm_out_shapes(kind, arrs):
    return [_sds((N_DEV,) + a.shape if kind == "gather" else a.shape, a.dtype) for a in arrs]


def _comm_scratch(n):
    return [pltpu.SemaphoreType.DMA((n, 7)), pltpu.SemaphoreType.DMA((n, 7)), pltpu.SemaphoreType.DMA((n,))]


def _comm_phases(kind, srcs, dsts, send_sems, recv_sems, local_sems):
    n = len(srcs)
    x, y, c = _position()
    me, sibling = (x, y, c), (x, y, 1 - c)
    chips = [(1 - x, y), (x, 1 - y), (1 - x, 1 - y)]

    if kind == "gather":
        def copy(a, k, block, to, src=None):
            dst = dsts[a].at[_index(block)]
            return pltpu.make_async_remote_copy(
                src_ref=dst if src is None else src, dst_ref=dst, send_sem=send_sems.at[a, k],
                recv_sem=recv_sems.at[a, k], device_id=to, device_id_type=MESH)

        def mine(a):
            return pltpu.make_async_copy(srcs[a], dsts[a].at[_index(me)], local_sems.at[a])

        def first(a):
            return [copy(a, 0, me, sibling, src=srcs[a])] + [
                copy(a, 1 + j, me, (*chip, c), src=srcs[a]) for j, chip in enumerate(chips)]

        def start():
            for a in range(n):
                mine(a).start()
            for a in range(n):
                for cp in first(a):
                    cp.start()

        def mid():
            for j, chip in enumerate(chips):
                for a in range(n):
                    copy(a, 1 + j, (*chip, c), me).wait_recv()
                    copy(a, 4 + j, (*chip, c), sibling).start()

        def end():
            for a in range(n):
                copy(a, 0, sibling, me).wait_recv()
            for j, chip in enumerate(chips):
                for a in range(n):
                    copy(a, 4 + j, (*chip, 1 - c), me).wait_recv()
            for a in range(n):
                for cp in first(a):
                    cp.wait_send()
                for j, chip in enumerate(chips):
                    copy(a, 4 + j, (*chip, c), sibling).wait_send()
                mine(a).wait()

        return start, mid, end

    def peer_of(k):
        return (1 - x if k & 4 else x, 1 - y if k & 2 else y, 1 - c if k & 1 else c)

    def own(a):
        return pltpu.make_async_copy(srcs[a].at[_index(me)], dsts[a].at[0], local_sems.at[a])

    def remote(a, k):
        peer = peer_of(k)
        return pltpu.make_async_remote_copy(
            src_ref=srcs[a].at[_index(peer)], dst_ref=dsts[a].at[k], send_sem=send_sems.at[a, k - 1],
            recv_sem=recv_sems.at[a, k - 1], device_id=peer, device_id_type=MESH)

    def start():
        for a in range(n):
            own(a).start()
        for k in range(1, N_DEV):
            for a in range(n):
                remote(a, k).start()

    def end():
        for k in range(1, N_DEV):
            for a in range(n):
                remote(a, k).wait()
        for a in range(n):
            own(a).wait()

    return start, (lambda: None), end


def _exchange(kind, arrs, name):
    n = len(arrs)
    hbm = pl.BlockSpec(memory_space=pl.ANY)

    def body(*refs):
        start, mid, end = _comm_phases(kind, refs[:n], refs[n:2 * n], *refs[2 * n:])
        start()
        mid()
        end()

    return pl.pallas_call(
        body, name=name, in_specs=[hbm] * n, out_specs=[hbm] * n, out_shape=_comm_out_shapes(kind, arrs),
        scratch_shapes=_comm_scratch(n))(*arrs)


def _pallas(body, *, name, grid, in_specs, out_specs, out_shape, args, scratch_shapes=(), comm=None):
    in_specs, out_specs, out_shape = list(in_specs), list(out_specs), list(out_shape)
    scratch_shapes = list(scratch_shapes)
    if not comm:
        return pl.pallas_call(body, name=name, grid=grid, in_specs=in_specs, out_specs=out_specs,
                              out_shape=out_shape, scratch_shapes=scratch_shapes,
                              compiler_params=_cp(len(grid)))(*args)
    arrs = comm["arrs"]
    ns, n_in, n_out, n_scr = len(arrs), len(in_specs), len(out_specs), len(scratch_shapes)
    hbm = pl.BlockSpec(memory_space=pl.ANY)
    total = math.prod(grid)

    def carrier(*refs):
        ins, srcs = refs[:n_in], refs[n_in:n_in + ns]
        outs, dsts = refs[n_in + ns:n_in + ns + n_out], refs[n_in + ns + n_out:n_in + 2 * ns + n_out]
        scr, sems = refs[n_in + 2 * ns + n_out:n_in + 2 * ns + n_out + n_scr], refs[n_in + 2 * ns + n_out + n_scr:]
        step = pl.program_id(0)
        for ax in range(1, len(grid)):
            step = step * grid[ax] + pl.program_id(ax)
        start, mid, end = _comm_phases(comm["kind"], srcs, dsts, *sems)
        pl.when(step == 0)(start)
        body(*ins, *outs, *scr)
        pl.when(step == total - 1)(mid)
        pl.when(step == total - 1)(end)

    res = pl.pallas_call(
        carrier, name=name, grid=grid, in_specs=in_specs + [hbm] * ns, out_specs=out_specs + [hbm] * ns,
        out_shape=out_shape + _comm_out_shapes(comm["kind"], arrs),
        scratch_shapes=scratch_shapes + _comm_scratch(ns), compiler_params=_cp(len(grid)))(*args, *arrs)
    comm["recv"] = res[n_out:]
    return res[:n_out]


def _ffn_up(xb, w, name, comm=None):
    T, D = xb.shape
    F = w.shape[0] // 2
    tm = _tile(T, 512)

    def body(x_ref, w_ref, gu_ref, a_ref):
        x = x_ref[...].astype(BF16)
        for j, cw in _chunks(F, 256):
            g = _dot_nt(x, w_ref[j:j + cw, :])
            u = _dot_nt(x, w_ref[F + j:F + j + cw, :])
            gu_ref[:, j:j + cw] = g.astype(BF16)
            gu_ref[:, F + j:F + j + cw] = u.astype(BF16)
            a_ref[:, j:j + cw] = (g * _sigmoid(g) * u).astype(BF16)

    return _pallas(
        body, name=name, grid=(T // tm,),
        in_specs=[_rows(tm, D), _res(w.shape)],
        out_specs=[_rows(tm, 2 * F), _rows(tm, F)],
        out_shape=[_sds((T, 2 * F), BF16), _sds((T, F), BF16)],
        args=(xb, w), comm=comm)


def _ffn_down_ln(a, wd, xres, g, b, name, comm=None):
    T, F = a.shape
    D = wd.shape[1]
    tm = _tile(T, 512)

    def body(a_ref, wd_ref, x_ref, g_ref, b_ref, r_ref, y_ref, yb_ref):
        r = ALPHA * x_ref[...] + 0.5 * _dot(a_ref[...], wd_ref[...])
        y = _ln_fwd(r, g_ref[...], b_ref[...])
        r_ref[...] = r
        y_ref[...] = y
        yb_ref[...] = y.astype(BF16)

    return _pallas(
        body, name=name, grid=(T // tm,),
        in_specs=[_rows(tm, F), _res(wd.shape), _rows(tm, D), _res((1, D)), _res((1, D))],
        out_specs=[_rows(tm, D), _rows(tm, D), _rows(tm, D)],
        out_shape=[_sds((T, D), F32), _sds((T, D), F32), _sds((T, D), BF16)],
        args=(a, wd, xres, g, b), comm=comm)


def _mix_in(hb, w, bias, d_rnn, name, comm=None):
    T, D = hb.shape
    R = d_rnn
    tm = _tile(T, 512)
    o_cv, o_cg, o_rx, o_rg, o_gc, o_gr = 0, D, 2 * D, 2 * D + R, 2 * D + 2 * R, 3 * D + 2 * R

    def body(x_ref, w_ref, b_ref, c_ref, cv_ref, cg_ref, rx_ref, rg_ref, gc_ref, gr_ref):
        x = x_ref[...]

        def seg(off, j, cw):
            return _dot_nt(x, w_ref[off + j:off + j + cw, :]) + b_ref[:, off + j:off + j + cw]

        for j, cw in _chunks(D, 256):
            cv = seg(o_cv, j, cw)
            cg = seg(o_cg, j, cw)
            cv_ref[:, j:j + cw] = cv.astype(BF16)
            cg_ref[:, j:j + cw] = cg.astype(BF16)
            c_ref[:, j:j + cw] = cv * _sigmoid(cg)
            gc_ref[:, j:j + cw] = seg(o_gc, j, cw).astype(BF16)
            gr_ref[:, j:j + cw] = seg(o_gr, j, cw).astype(BF16)
        for j, cw in _chunks(R, 256):
            rx_ref[:, j:j + cw] = seg(o_rx, j, cw)
            rg_ref[:, j:j + cw] = seg(o_rg, j, cw).astype(BF16)

    return _pallas(
        body, name=name, grid=(T // tm,),
        in_specs=[_rows(tm, D), _res(w.shape), _res(bias.shape)],
        out_specs=[_rows(tm, D), _rows(tm, D), _rows(tm, D), _rows(tm, R), _rows(tm, R), _rows(tm, D),
                   _rows(tm, D)],
        out_shape=[_sds((T, D), F32), _sds((T, D), BF16), _sds((T, D), BF16), _sds((T, R), F32),
                   _sds((T, R), BF16), _sds((T, D), BF16), _sds((T, D), BF16)],
        args=(hb, w, bias), comm=comm)


def _cols(t, rows=None):
    return pl.BlockSpec((t if rows is None else rows, LANES), lambda j: (0, j))


def _gn_stats(cc):
    mu = jnp.mean(cc, axis=-1, keepdims=True)
    xc = cc - mu
    var = jnp.mean(xc * xc, axis=-1, keepdims=True)
    rstd = lax.rsqrt(var + LN_EPS)
    return xc * rstd, rstd


def _tap_groups(offsets):
    groups = {}
    for k, s in enumerate(offsets):
        groups.setdefault(s % SUBLANES, []).append((k, s - s % SUBLANES))
    return sorted(groups.items())


def _shifted(win, phase):
    return win if phase == 0 else pltpu.roll(win, win.shape[0] - phase, 0)


def _tap_sum(win, wv, groups, rows):
    parts, t = [None] * 4, 0
    for phase, taps in groups:
        sh = _shifted(win, phase)
        for k, off in taps:
            term = wv[k:k + 1, :] * sh[off:off + rows, :]
            parts[t % 4] = term if parts[t % 4] is None else parts[t % 4] + term
            t += 1
    return (parts[0] + parts[1]) + (parts[2] + parts[3])


def _conv31_gn(c, w, bias, gg, gb, name, comm=None):
    T, D = c.shape
    K = w.shape[0]
    B, P, N = CONV_BLOCK, CONV_PAD, _tile(T, NORM_ROWS)
    assert D % LANES == 0 and T % B == 0 and K - 1 <= P
    groups = _tap_groups([P - (K - 1) + k for k in range(K)])

    def body(c_ref, w_ref, b_ref, gg_ref, gb_ref, cc_ref, cs_ref, xpad):
        xpad[0:P, :] = jnp.zeros((P, LANES), F32)
        xpad[P:P + T, :] = c_ref[...]
        wv = w_ref[...]
        bv, ggv, gbv = b_ref[...], gg_ref[...], gb_ref[...]

        def conv_step(i, carry):
            base = pl.multiple_of(i * B, B)
            win = xpad[pl.ds(base, B + P), :]
            cc_ref[pl.ds(base, B), :] = _tap_sum(win, wv, groups, B) + bv
            return carry

        lax.fori_loop(0, T // B, conv_step, 0)

        def norm_step(i, carry):
            base = pl.multiple_of(i * N, N)
            xhat, _ = _gn_stats(cc_ref[pl.ds(base, N), :])
            gn = xhat * ggv + gbv
            cs_ref[pl.ds(base, N), :] = (gn * _sigmoid(gn)).astype(BF16)
            return carry

        lax.fori_loop(0, T // N, norm_step, 0)

    return _pallas(
        body, name=name, grid=(D // LANES,),
        in_specs=[_cols(T), _cols(T, K), _cols(T, 1), _cols(T, 1), _cols(T, 1)],
        out_specs=[_cols(T), _cols(T)],
        out_shape=[_sds((T, D), F32), _sds((T, D), BF16)],
        scratch_shapes=[pltpu.VMEM((T + P, LANES), F32)],
        args=(c, w, bias, gg, gb), comm=comm)


def _conv4(rx, w, bias, name):
    T, C = rx.shape
    K = w.shape[0]
    R = CONV_ROWS
    assert C % LANES == 0 and T % R == 0 and K - 1 <= RNN_PAD

    def body(x_ref, w_ref, b_ref, r_ref, xpad):
        xpad[0:RNN_PAD, :] = jnp.zeros((RNN_PAD, LANES), F32)
        xpad[RNN_PAD:RNN_PAD + T, :] = x_ref[...]
        wv, bv = w_ref[...], b_ref[...]

        def step(i, carry):
            base = pl.multiple_of(i * R, R)
            win = xpad[pl.ds(base, R + RNN_PAD), :]
            acc = jnp.zeros((R, LANES), F32)
            for k in range(K):
                s = RNN_PAD - (K - 1) + k
                acc = acc + wv[k:k + 1, :] * win[s:s + R, :]
            r_ref[pl.ds(base, R), :] = acc + bv
            return carry

        lax.fori_loop(0, T // R, step, 0)

    (r,) = _pallas(
        body, name=name, grid=(C // LANES,),
        in_specs=[_cols(T), _cols(T, K), _cols(T, 1)],
        out_specs=[_cols(T)],
        out_shape=[_sds((T, C), F32)],
        scratch_shapes=[pltpu.VMEM((T + RNN_PAD, LANES), F32)],
        args=(rx, w, bias))
    return r


def _gates(r, bda, bdx, b_a, b_x, lam, name, comm=None):
    T, C = r.shape
    tm = _tile(T, 512)
    chunks = _band_chunks(C, C // RNN_BLOCKS)

    def body(r_ref, wa_ref, wx_ref, ba_ref, bx_ref, lam_ref, ra_ref, ri_ref, a_ref, u_ref):
        for c0, cw, k0, k1 in chunks:
            cols = slice(c0, c0 + cw)
            rb = r_ref[:, k0:k1].astype(BF16)
            ra = _sigmoid(_dot(rb, wa_ref[k0:k1, cols]) + ba_ref[:, cols])
            ri = _sigmoid(_dot(rb, wx_ref[k0:k1, cols]) + bx_ref[:, cols])
            log_a = (-RG_LRU_C) * ra * _softplus_neg(lam_ref[:, cols])
            ra_ref[:, cols] = ra
            ri_ref[:, cols] = ri
            a_ref[:, cols] = jnp.exp(log_a)
            u_ref[:, cols] = jnp.sqrt(-_expm1(2.0 * log_a)) * (ri * r_ref[:, cols])

    return _pallas(
        body, name=name, grid=(T // tm,),
        in_specs=[_rows(tm, C), _res(bda.shape), _res(bdx.shape), _res((1, C)), _res((1, C)), _res((1, C))],
        out_specs=[_rows(tm, C)] * 4,
        out_shape=[_sds((T, C), F32)] * 4,
        args=(r, bda, bdx, b_a, b_x, lam), comm=comm)


def _scan_fwd(a, u, rg, name, comm=None):
    T, C = a.shape
    tt = _tile(T, 512)

    def body(a_ref, u_ref, rg_ref, h_ref, hp_ref, hg_ref, carry):
        @pl.when(pl.program_id(0) == 0)
        def _():
            carry[...] = jnp.zeros_like(carry)

        row = lax.broadcasted_iota(jnp.int32, (SUBLANES, C), 0)

        def group(i, hprev):
            base = pl.multiple_of(i * SUBLANES, SUBLANES)
            av = a_ref[pl.ds(base, SUBLANES), :]
            uv = u_ref[pl.ds(base, SUBLANES), :]
            for s in (1, 2, 4):
                a_s = jnp.where(row >= s, pltpu.roll(av, s, 0), 1.0)
                u_s = jnp.where(row >= s, pltpu.roll(uv, s, 0), 0.0)
                uv = av * u_s + uv
                av = av * a_s
            h = av * hprev + uv
            h_ref[pl.ds(base, SUBLANES), :] = h
            hp_ref[pl.ds(base, SUBLANES), :] = jnp.where(row >= 1, pltpu.roll(h, 1, 0), hprev)
            return h[SUBLANES - 1:SUBLANES, :]

        carry[...] = lax.fori_loop(0, tt // SUBLANES, group, carry[...])
        gel, _ = _gelu(rg_ref[...].astype(F32))
        hg_ref[...] = (h_ref[...] * gel).astype(BF16)

    return _pallas(
        body, name=name, grid=(T // tt,),
        in_specs=[_rows(tt, C)] * 3,
        out_specs=[_rows(tt, C)] * 3,
        out_shape=[_sds((T, C), F32), _sds((T, C), F32), _sds((T, C), BF16)],
        scratch_shapes=[pltpu.VMEM((1, C), F32)],
        args=(a, u, rg), comm=comm)


def _mix_out_ln(cs, hg, gc, gr, hres, wcp, wrp, wout, g, b, name, comm=None):
    T, D = cs.shape
    C = hg.shape[1]
    tm = _tile(T, 512)

    def body(cs_ref, hg_ref, gc_ref, gr_ref, h_ref, wcp_ref, wrp_ref, wo_ref, g_ref, b_ref,
             yc_ref, yr_ref, m_ref, r_ref, y_ref, yb_ref):
        yc = _dot(cs_ref[...], wcp_ref[...])
        yr = _dot(hg_ref[...], wrp_ref[...])
        m = (_sigmoid(gc_ref[...].astype(F32)) * yc + _sigmoid(gr_ref[...].astype(F32)) * yr).astype(BF16)
        r = ALPHA * h_ref[...] + _dot(m, wo_ref[...])
        y = _ln_fwd(r, g_ref[...], b_ref[...])
        yc_ref[...] = yc.astype(BF16)
        yr_ref[...] = yr.astype(BF16)
        m_ref[...] = m
        r_ref[...] = r
        y_ref[...] = y
        yb_ref[...] = y.astype(BF16)

    return _pallas(
        body, name=name, grid=(T // tm,),
        in_specs=[_rows(tm, D), _rows(tm, C), _rows(tm, D), _rows(tm, D), _rows(tm, D), _res(wcp.shape),
                  _res(wrp.shape), _res(wout.shape), _res((1, D)), _res((1, D))],
        out_specs=[_rows(tm, D)] * 6,
        out_shape=[_sds((T, D), BF16), _sds((T, D), BF16), _sds((T, D), BF16), _sds((T, D), F32),
                   _sds((T, D), F32), _sds((T, D), BF16)],
        args=(cs, hg, gc, gr, hres, wcp, wrp, wout, g, b), comm=comm)


def _loss_ln_bwd(y, target, r, g, name):
    T, D = y.shape
    tm = _tile(T, 512)

    def body(y_ref, t_ref, r_ref, g_ref, loss_ref, dr_ref, drb_ref, dg_ref, db_ref):
        @pl.when(pl.program_id(0) == 0)
        def _():
            loss_ref[...] = jnp.zeros_like(loss_ref)
            dg_ref[...] = jnp.zeros_like(dg_ref)
            db_ref[...] = jnp.zeros_like(db_ref)

        e = y_ref[...] - t_ref[...]
        loss_ref[...] += (0.5 / D) * jnp.sum(e * e)
        dr, dg, db = _ln_bwd(e * (1.0 / D), r_ref[...], g_ref[...])
        dr_ref[...] = dr
        drb_ref[...] = dr.astype(BF16)
        dg_ref[...] += dg
        db_ref[...] += db

    return _pallas(
        body, name=name, grid=(T // tm,),
        in_specs=[_rows(tm, D), _rows(tm, D), _rows(tm, D), _res((1, D))],
        out_specs=[_acc((SUBLANES, LANES)), _rows(tm, D), _rows(tm, D), _acc((1, D)), _acc((1, D))],
        out_shape=[_sds((SUBLANES, LANES), F32), _sds((T, D), F32), _sds((T, D), BF16), _sds((1, D), F32),
                   _sds((1, D), F32)],
        args=(y, target, r, g))


def _ffn_bwd_a(drb, wd, gu, name, comm=None):
    T, D = drb.shape
    F = wd.shape[0]
    tm = _tile(T, 512)

    def body(d_ref, wd_ref, gu_ref, o_ref):
        d = d_ref[...]
        for j, cw in _chunks(F, 256):
            da = 0.5 * _dot_nt(d, wd_ref[j:j + cw, :])
            gt = gu_ref[:, j:j + cw].astype(F32)
            up = gu_ref[:, F + j:F + j + cw].astype(F32)
            sg = _sigmoid(gt)
            o_ref[:, j:j + cw] = (da * up * (sg * (1.0 + gt * (1.0 - sg)))).astype(BF16)
            o_ref[:, F + j:F + j + cw] = (da * (gt * sg)).astype(BF16)

    (dgu,) = _pallas(
        body, name=name, grid=(T // tm,),
        in_specs=[_rows(tm, D), _res(wd.shape), _rows(tm, 2 * F)],
        out_specs=[_rows(tm, 2 * F)],
        out_shape=[_sds((T, 2 * F), BF16)],
        args=(drb, wd, gu), comm=comm)
    return dgu


def _nt_res(dus, w, dres, name, ln=None, colsum=False, comm=None):
    T = dus[0].shape[0]
    widths = [d.shape[1] for d in dus]
    offs = [sum(widths[:p]) for p in range(len(dus))]
    K, D, P = sum(widths), w.shape[1], len(dus)
    tm = _tile(T, 512)
    n_in = P + 2 + (2 if ln else 0)

    def body(*refs):
        du_refs, w_ref, dres_ref = refs[:P], refs[P], refs[P + 1]
        outs = refs[n_in:]
        dy = ALPHA * dres_ref[...]
        for du_ref, off, width in zip(du_refs, offs, widths):
            dy = dy + _dot(du_ref[...], w_ref[off:off + width, :])
        if ln:
            r_ref, g_ref = refs[P + 2:P + 4]

            @pl.when(pl.program_id(0) == 0)
            def _():
                outs[2][...] = jnp.zeros_like(outs[2])
                outs[3][...] = jnp.zeros_like(outs[3])

            dr, dg, db = _ln_bwd(dy, r_ref[...], g_ref[...])
            outs[0][...] = dr
            outs[1][...] = dr.astype(BF16)
            outs[2][...] += dg
            outs[3][...] += db
        else:
            outs[0][...] = dy
        if colsum:
            cs_ref = outs[-1]

            @pl.when(pl.program_id(0) == 0)
            def _():
                cs_ref[...] = jnp.zeros_like(cs_ref)

            for du_ref, off, width in zip(du_refs, offs, widths):
                cs_ref[:, off:off + width] += jnp.sum(du_ref[...].astype(F32), axis=0, keepdims=True)

    in_specs = [_rows(tm, width) for width in widths] + [_res(w.shape), _rows(tm, D)]
    args = list(dus) + [w, dres]
    if ln:
        in_specs += [_rows(tm, D), _res((1, D))]
        args += list(ln)
        out_specs = [_rows(tm, D), _rows(tm, D), _acc((1, D)), _acc((1, D))]
        out_shape = [_sds((T, D), F32), _sds((T, D), BF16), _sds((1, D), F32), _sds((1, D), F32)]
    else:
        out_specs = [_rows(tm, D)]
        out_shape = [_sds((T, D), F32)]
    if colsum:
        out_specs.append(_acc((1, K)))
        out_shape.append(_sds((1, K), F32))
    return _pallas(body, name=name, grid=(T // tm,), in_specs=in_specs, out_specs=out_specs, out_shape=out_shape,
                   args=args, comm=comm)


def _mm_tn(x, dy, name, scale=1.0, comm=None):
    T, K = x.shape
    N = dy.shape[1]
    tt = _tile(T, 1024)
    tn = next(N // d for d in range(1, N // LANES + 1)
              if N % d == 0 and (N // d) % LANES == 0 and K * (N // d) * 4 <= MM_TN_OUT_BYTES)
    nt = T // tt

    def body(x_ref, dy_ref, o_ref):
        t = pl.program_id(1)

        @pl.when(t == 0)
        def _():
            o_ref[...] = jnp.zeros_like(o_ref)

        o_ref[...] += _dot_tn(x_ref[...].astype(BF16), dy_ref[...])
        if scale != 1.0:
            @pl.when(t == nt - 1)
            def _():
                o_ref[...] = o_ref[...] * scale

    (out,) = _pallas(
        body, name=name, grid=(N // tn, nt),
        in_specs=[pl.BlockSpec((tt, K), lambda j, t: (t, 0)), pl.BlockSpec((tt, tn), lambda j, t: (t, j))],
        out_specs=[pl.BlockSpec((K, tn), lambda j, t: (0, j))],
        out_shape=[_sds((K, N), F32)],
        args=(x, dy), comm=comm)
    return out


def _mix_bwd1(drb, wout, wcp, wrp, gc, gr, yc, yr, rg, h, name, comm=None):
    T, D = drb.shape
    C = rg.shape[1]
    tm = _tile(T, 512)

    def body(d_ref, wo_ref, wcp_ref, wrp_ref, gc_ref, gr_ref, yc_ref, yr_ref, rg_ref, h_ref,
             dyc_ref, dyr_ref, tail_ref, dcs_ref, dh_ref):
        dm = _dot_nt(d_ref[...], wo_ref[...])
        sc = _sigmoid(gc_ref[...].astype(F32))
        sr = _sigmoid(gr_ref[...].astype(F32))
        dyc = (dm * sc).astype(BF16)
        dyr = (dm * sr).astype(BF16)
        dyc_ref[...] = dyc
        dyr_ref[...] = dyr
        tail_ref[:, C:C + D] = (dm * yc_ref[...].astype(F32) * sc * (1.0 - sc)).astype(BF16)
        tail_ref[:, C + D:] = (dm * yr_ref[...].astype(F32) * sr * (1.0 - sr)).astype(BF16)
        dcs_ref[...] = _dot_nt(dyc, wcp_ref[...])
        dhg = _dot_nt(dyr, wrp_ref[...])
        rgv = rg_ref[...].astype(F32)
        gel, t = _gelu(rgv)
        dh_ref[...] = dhg * gel
        tail_ref[:, 0:C] = (dhg * h_ref[...] * _gelu_grad(rgv, t)).astype(BF16)

    return _pallas(
        body, name=name, grid=(T // tm,),
        in_specs=[_rows(tm, D), _res(wout.shape), _res(wcp.shape), _res(wrp.shape), _rows(tm, D), _rows(tm, D),
                  _rows(tm, D), _rows(tm, D), _rows(tm, C), _rows(tm, C)],
        out_specs=[_rows(tm, D), _rows(tm, D), _rows(tm, C + 2 * D), _rows(tm, D), _rows(tm, C)],
        out_shape=[_sds((T, D), BF16), _sds((T, D), BF16), _sds((T, C + 2 * D), BF16), _sds((T, D), F32),
                   _sds((T, C), F32)],
        args=(drb, wout, wcp, wrp, gc, gr, yc, yr, rg, h), comm=comm)


def _scan_bwd(dh, a, name):
    T, C = dh.shape
    tt = _tile(T, 512)
    nt = T // tt
    ng = tt // SUBLANES

    def body(d_ref, a_ref, g_ref, carry):
        @pl.when(pl.program_id(0) == 0)
        def _():
            carry[...] = jnp.zeros_like(carry)

        row = lax.broadcasted_iota(jnp.int32, (SUBLANES, C), 0)

        def group(j, enext):
            base = pl.multiple_of((ng - 1 - j) * SUBLANES, SUBLANES)
            av = a_ref[pl.ds(base, SUBLANES), :]
            dv = d_ref[pl.ds(base, SUBLANES), :]
            bv = av * dv
            for s in (1, 2, 4):
                keep = row < SUBLANES - s
                a_s = jnp.where(keep, pltpu.roll(av, SUBLANES - s, 0), 1.0)
                b_s = jnp.where(keep, pltpu.roll(bv, SUBLANES - s, 0), 0.0)
                bv = av * b_s + bv
                av = av * a_s
            e = av * enext + bv
            e_up = jnp.where(row < SUBLANES - 1, pltpu.roll(e, SUBLANES - 1, 0), enext)
            g_ref[pl.ds(base, SUBLANES), :] = dv + e_up
            return e[0:1, :]

        carry[...] = lax.fori_loop(0, ng, group, carry[...])

    rev = pl.BlockSpec((tt, C), lambda i: (nt - 1 - i, 0))
    (g,) = _pallas(
        body, name=name, grid=(nt,), in_specs=[rev, rev], out_specs=[rev],
        out_shape=[_sds((T, C), F32)],
        scratch_shapes=[pltpu.VMEM((1, C), F32)],
        args=(dh, a))
    return g


def _gates_bwd(g, hp, ra, ri, r, lam, bda, bdx, name, comm=None):
    T, C = g.shape
    tm = _tile(T, 512)
    nt = T // tm
    chunks = _band_chunks(C, C // RNN_BLOCKS)

    def body(g_ref, hp_ref, ra_ref, ri_ref, r_ref, lam_ref, wa_ref, wx_ref,
             dr_ref, dpa_ref, dpx_ref, dlam_ref, dba_ref, dbx_ref):
        @pl.when(pl.program_id(0) == 0)
        def _():
            dlam_ref[...] = jnp.zeros_like(dlam_ref)
            dba_ref[...] = jnp.zeros_like(dba_ref)
            dbx_ref[...] = jnp.zeros_like(dbx_ref)

        for c0, cw, _, _ in chunks:
            cols = slice(c0, c0 + cw)
            gv, rav, riv, rv = g_ref[:, cols], ra_ref[:, cols], ri_ref[:, cols], r_ref[:, cols]
            sp = _softplus_neg(lam_ref[:, cols])
            log_a = (-RG_LRU_C) * rav * sp
            av = jnp.exp(log_a)
            mult = jnp.sqrt(-_expm1(2.0 * log_a))
            d_mult = gv * riv * rv
            d_i = gv * mult * rv
            d_loga = gv * hp_ref[:, cols] * av - d_mult * (av * av) / mult
            d_ra = d_loga * ((-RG_LRU_C) * sp)
            dpa = d_ra * rav * (1.0 - rav)
            dpx = d_i * riv * (1.0 - riv)
            dpa_ref[:, cols] = dpa.astype(BF16)
            dpx_ref[:, cols] = dpx.astype(BF16)
            dr_ref[:, cols] = gv * mult * riv
            dlam_ref[:, cols] += jnp.sum(d_loga * ((-RG_LRU_C) * rav), axis=0, keepdims=True)
            dba_ref[:, cols] += jnp.sum(dpa, axis=0, keepdims=True)
            dbx_ref[:, cols] += jnp.sum(dpx, axis=0, keepdims=True)
        for c0, cw, k0, k1 in chunks:
            cols = slice(c0, c0 + cw)
            dr_ref[:, k0:k1] += (_dot_nt(dpa_ref[:, cols], wa_ref[k0:k1, cols])
                                 + _dot_nt(dpx_ref[:, cols], wx_ref[k0:k1, cols]))

        @pl.when(pl.program_id(0) == nt - 1)
        def _():
            dlam_ref[...] = dlam_ref[...] * (-_sigmoid(-lam_ref[...]))

    return _pallas(
        body, name=name, grid=(nt,),
        in_specs=[_rows(tm, C)] * 5 + [_res((1, C)), _res(bda.shape), _res(bdx.shape)],
        out_specs=[_rows(tm, C)] * 3 + [_acc((1, C))] * 3,
        out_shape=[_sds((T, C), F32), _sds((T, C), BF16), _sds((T, C), BF16)] + [_sds((1, C), F32)] * 3,
        args=(g, hp, ra, ri, r, lam, bda, bdx), comm=comm)


def _band_dw(r, dpa, dpx, name):
    T, C = r.shape
    tt = _tile(T, 512)
    chunks = _band_chunks(C, C // RNN_BLOCKS)

    def body(r_ref, dpa_ref, dpx_ref, oa_ref, ox_ref):
        @pl.when(pl.program_id(0) == 0)
        def _():
            oa_ref[...] = jnp.zeros_like(oa_ref)
            ox_ref[...] = jnp.zeros_like(ox_ref)

        for c0, cw, k0, k1 in chunks:
            cols = slice(c0, c0 + cw)
            rb = r_ref[:, k0:k1].astype(BF16)
            oa_ref[k0:k1, cols] += _dot_tn(rb, dpa_ref[:, cols])
            ox_ref[k0:k1, cols] += _dot_tn(rb, dpx_ref[:, cols])

    return _pallas(
        body, name=name, grid=(T // tt,),
        in_specs=[_rows(tt, C)] * 3,
        out_specs=[_acc((C, C)), _acc((C, C))],
        out_shape=[_sds((C, C), F32), _sds((C, C), F32)],
        args=(r, dpa, dpx))


def _conv4_bwd(dr, rx, w, name):
    T, C = dr.shape
    K = w.shape[0]
    R = CONV_ROWS
    P = RNN_PAD

    def body(d_ref, x_ref, w_ref, dx_ref, dw_ref, db_ref, dpad, xpad):
        dpad[0:T, :] = d_ref[...]
        dpad[T:T + P, :] = jnp.zeros((P, LANES), F32)
        xpad[0:P, :] = jnp.zeros((P, LANES), F32)
        xpad[P:P + T, :] = x_ref[...]
        dw_ref[...] = jnp.zeros_like(dw_ref)
        db_ref[...] = jnp.zeros_like(db_ref)
        wv = w_ref[...]

        def step(i, carry):
            base = pl.multiple_of(i * R, R)
            dwin = dpad[pl.ds(base, R + P), :]
            xwin = xpad[pl.ds(base, R + P), :]
            dcur = dwin[0:R, :]
            acc = jnp.zeros((R, LANES), F32)
            for k in range(K):
                acc = acc + wv[k:k + 1, :] * dwin[K - 1 - k:K - 1 - k + R, :]
                s = P - (K - 1) + k
                dw_ref[k:k + 1, :] += jnp.sum(dcur * xwin[s:s + R, :], axis=0, keepdims=True)
            dx_ref[pl.ds(base, R), :] = acc.astype(BF16)
            db_ref[...] += jnp.sum(dcur, axis=0, keepdims=True)
            return carry

        lax.fori_loop(0, T // R, step, 0)

    return _pallas(
        body, name=name, grid=(C // LANES,),
        in_specs=[_cols(T), _cols(T), _cols(T, K)],
        out_specs=[_cols(T), _cols(T, SUBLANES), _cols(T, 1)],
        out_shape=[_sds((T, C), BF16), _sds((SUBLANES, C), F32), _sds((1, C), F32)],
        scratch_shapes=[pltpu.VMEM((T + P, LANES), F32), pltpu.VMEM((T + P, LANES), F32)],
        args=(dr, rx, w))


def _conv31_bwd(dcs, cc, c, cv, cg, w, gg, gb, name, comm=None):
    T, D = dcs.shape
    K = w.shape[0]
    R, B, P = _tile(T, NORM_ROWS), CONV_BLOCK, CONV_PAD
    d_groups = _tap_groups([K - 1 - k for k in range(K)])
    x_groups = _tap_groups([P - (K - 1) + k for k in range(K)])

    def body(dcs_ref, cc_ref, c_ref, cv_ref, cg_ref, w_ref, gg_ref, gb_ref,
             dcv_ref, dcg_ref, dw_ref, db_ref, dgg_ref, dgb_ref, dpad, xpad, dwacc):
        dpad[T:T + P, :] = jnp.zeros((P, LANES), F32)
        xpad[0:P, :] = jnp.zeros((P, LANES), F32)
        xpad[P:P + T, :] = c_ref[...]
        dwacc[...] = jnp.zeros_like(dwacc)
        db_ref[...] = jnp.zeros_like(db_ref)
        dgg_ref[...] = jnp.zeros_like(dgg_ref)
        dgb_ref[...] = jnp.zeros_like(dgb_ref)
        wv, ggv, gbv = w_ref[...], gg_ref[...], gb_ref[...]

        def norm_step(i, carry):
            base = pl.multiple_of(i * R, R)
            xhat, rstd = _gn_stats(cc_ref[pl.ds(base, R), :])
            gn = xhat * ggv + gbv
            sg = _sigmoid(gn)
            dgn = dcs_ref[pl.ds(base, R), :] * (sg * (1.0 + gn * (1.0 - sg)))
            dgg_ref[...] += jnp.sum(dgn * xhat, axis=0, keepdims=True)
            dgb_ref[...] += jnp.sum(dgn, axis=0, keepdims=True)
            dxh = dgn * ggv
            m1 = jnp.mean(dxh, axis=-1, keepdims=True)
            m2 = jnp.mean(dxh * xhat, axis=-1, keepdims=True)
            dcc = rstd * (dxh - m1 - xhat * m2)
            dpad[pl.ds(base, R), :] = dcc
            db_ref[...] += jnp.sum(dcc, axis=0, keepdims=True)
            return carry

        lax.fori_loop(0, T // R, norm_step, 0)

        def conv_step(i, carry):
            base = pl.multiple_of(i * B, B)
            dwin = dpad[pl.ds(base, B + P), :]
            xwin = xpad[pl.ds(base, B + P), :]
            dcur = dwin[0:B, :]
            acc = _tap_sum(dwin, wv, d_groups, B)
            for phase, taps in x_groups:
                sh = _shifted(xwin, phase)
                for k, off in taps:
                    prod = dcur * sh[off:off + B, :]
                    part = prod[0:SUBLANES, :]
                    for q in range(1, B // SUBLANES):
                        part = part + prod[q * SUBLANES:(q + 1) * SUBLANES, :]
                    dwacc[k * SUBLANES:(k + 1) * SUBLANES, :] += part
            cgv = cg_ref[pl.ds(base, B), :].astype(F32)
            cvv = cv_ref[pl.ds(base, B), :].astype(F32)
            sg = _sigmoid(cgv)
            dcv_ref[pl.ds(base, B), :] = (acc * sg).astype(BF16)
            dcg_ref[pl.ds(base, B), :] = (acc * cvv * sg * (1.0 - sg)).astype(BF16)
            return carry

        lax.fori_loop(0, T // B, conv_step, 0)
        dw_ref[...] = jnp.zeros_like(dw_ref)
        for k in range(K):
            dw_ref[k:k + 1, :] = jnp.sum(dwacc[k * SUBLANES:(k + 1) * SUBLANES, :], axis=0, keepdims=True)

    return _pallas(
        body, name=name, grid=(D // LANES,),
        in_specs=[_cols(T)] * 5 + [_cols(T, K), _cols(T, 1), _cols(T, 1)],
        out_specs=[_cols(T), _cols(T), _cols(T, P), _cols(T, 1), _cols(T, 1), _cols(T, 1)],
        out_shape=[_sds((T, D), BF16), _sds((T, D), BF16), _sds((P, D), F32), _sds((1, D), F32),
                   _sds((1, D), F32), _sds((1, D), F32)],
        scratch_shapes=[pltpu.VMEM((T + P, LANES), F32), pltpu.VMEM((T + P, LANES), F32),
                        pltpu.VMEM((P * SUBLANES, LANES), F32)],
        args=(dcs, cc, c, cv, cg, w, gg, gb), comm=comm)


def _reduce_adamw(recvs, w, m, v, name):
    L, R, C = w.shape
    assert len(recvs) == L
    tr = next((R // d for d in range(1, R // SUBLANES + 1)
               if R % d == 0 and (R // d) % SUBLANES == 0 and R // d <= ADAMW_ROWS), R)
    nr = R // tr
    c1 = 1.0 - ADAM_B1 ** ADAM_STEP
    c2 = 1.0 - ADAM_B2 ** ADAM_STEP

    def body(*refs):
        recv_refs = refs[:L]
        w_ref, m_ref, v_ref, g_ref, d_ref, mo_ref, vo_ref = refs[L:]

        def update(recv_ref):
            g = recv_ref[0].astype(F32)
            for k in range(1, N_DEV):
                g = g + recv_ref[k].astype(F32)
            mn = ADAM_B1 * m_ref[0] + (1.0 - ADAM_B1) * g
            vn = ADAM_B2 * v_ref[0] + (1.0 - ADAM_B2) * (g * g)
            g_ref[0] = g
            mo_ref[0] = mn
            vo_ref[0] = vn
            d_ref[0] = (-ADAM_LR) * ((mn / c1) / (jnp.sqrt(vn / c2) + ADAM_EPS) + ADAM_WD * w_ref[0])

        for l in range(L):
            pl.when(pl.program_id(0) == l)(lambda l=l: update(recv_refs[l]))

    def recv_spec(l):
        return pl.BlockSpec((N_DEV, tr, C), lambda j, i: (0, jnp.where(j == l, i, jnp.where(j < l, 0, nr - 1)), 0))

    blk = pl.BlockSpec((1, tr, C), lambda j, i: (j, i, 0))
    return _pallas(
        body, name=name, grid=(L, nr),
        in_specs=[recv_spec(l) for l in range(L)] + [blk, blk, blk],
        out_specs=[blk] * 4,
        out_shape=[_sds((L, R, C), F32)] * 4,
        args=(*recvs, w, m, v))


def _unshard(name, gathered):
    n, r, c = gathered.shape
    if name in COL_SHARDED and name not in TRANSPOSED:
        return gathered.transpose(1, 0, 2).reshape(r, n * c)
    return gathered.reshape(n * r, c)


def _to_shards(name, full):
    R, C = full.shape
    wire = BF16 if name in BF16_ON_WIRE else F32
    if name in TRANSPOSED:
        return full.reshape(R, N_DEV, C // N_DEV).transpose(1, 2, 0).astype(wire)
    if name in COL_SHARDED:
        return full.reshape(R, N_DEV, C // N_DEV).transpose(1, 0, 2).astype(wire)
    return full.reshape(N_DEV, R // N_DEV, C).astype(wire)


def _block_diag(w):
    H, b, _ = w.shape
    eye = jnp.eye(H, dtype=w.dtype)
    return (w[:, :, None, :] * eye[:, None, :, None]).reshape(H * b, H * b)


def _diag_blocks(dense, H):
    b = dense.shape[0] // H
    eye = jnp.eye(H, dtype=dense.dtype)
    return jnp.sum(dense.reshape(H, b, H, b) * eye[:, None, :, None], axis=2)


def _pack(arrs, rows):
    flat = jnp.concatenate([a.reshape(-1) for a in arrs])
    return jnp.pad(flat, (0, rows * 1024 - flat.shape[0])).reshape(1, rows, 1024)


def _unpack(packed, shapes):
    flat = packed.reshape(-1)
    out, off = [], 0
    for s in shapes:
        n = math.prod(s)
        out.append(flat[off:off + n].reshape(s))
        off += n
    return out


class _Queue:
    def __init__(self, kind, us_per_mb):
        self.kind, self.us_per_mb, self.items, self.done = kind, us_per_mb, [], {}

    def push(self, key, arr):
        self.items.append((key, arr))

    def mb(self, arr):
        return _mbytes(arr) / (N_DEV if self.kind == "scatter" else 1)

    def take(self, micros):
        taken, budget = [], micros / self.us_per_mb
        while self.items and (not taken or self.mb(self.items[0][1]) <= budget):
            budget -= self.mb(self.items[0][1])
            taken.append(self.items.pop(0))
        return {"kind": self.kind, "keys": [k for k, _ in taken], "arrs": [a for _, a in taken]} if taken else None

    def landed(self, comm):
        if comm:
            self.done.update(zip(comm["keys"], comm["recv"]))

    def flush(self, name, upto=None):
        n = len(self.items)
        if upto is not None:
            keys = [k for k, _ in self.items]
            n = keys.index(upto) + 1 if upto in keys else 0
        if n:
            taken, self.items = self.items[:n], self.items[n:]
            self.done.update(zip([k for k, _ in taken], _exchange(self.kind, [a for _, a in taken], name)))


def kernel(x, ffn1_w_gu, ffn1_w_down, ln1_g, ln1_b, mix_w_in, mix_b_in, conv_dw_w, conv_dw_b, conv_gn_g, conv_gn_b, conv_w_proj, rnn_conv_w, rnn_conv_b, rnn_w_a, rnn_b_a, rnn_w_x, rnn_b_x, rnn_lambda, rnn_w_proj, mix_w_out, ln2_g, ln2_b, ffn2_w_gu, ffn2_w_down, ln3_g, ln3_b, loss_target, m_ffn1_w_gu, m_ffn1_w_down, m_ln1_g, m_ln1_b, m_mix_w_in, m_mix_b_in, m_conv_dw_w, m_conv_dw_b, m_conv_gn_g, m_conv_gn_b, m_conv_w_proj, m_rnn_conv_w, m_rnn_conv_b, m_rnn_w_a, m_rnn_b_a, m_rnn_w_x, m_rnn_b_x, m_rnn_lambda, m_rnn_w_proj, m_mix_w_out, m_ln2_g, m_ln2_b, m_ffn2_w_gu, m_ffn2_w_down, m_ln3_g, m_ln3_b, v_ffn1_w_gu, v_ffn1_w_down, v_ln1_g, v_ln1_b, v_mix_w_in, v_mix_b_in, v_conv_dw_w, v_conv_dw_b, v_conv_gn_g, v_conv_gn_b, v_conv_w_proj, v_rnn_conv_w, v_rnn_conv_b, v_rnn_w_a, v_rnn_b_a, v_rnn_w_x, v_rnn_b_x, v_rnn_lambda, v_rnn_w_proj, v_mix_w_out, v_ln2_g, v_ln2_b, v_ffn2_w_gu, v_ffn2_w_down, v_ln3_g, v_ln3_b):
    given = dict(locals())
    W = {n: given[n] for n in WEIGHTS}
    M = {n: given["m_" + n] for n in WEIGHTS}
    V = {n: given["v_" + n] for n in WEIGHTS}
    T, D = x.shape[1], x.shape[2]
    L = DEPTH
    x2 = x.reshape(T, D)
    target = loss_target.reshape(T, D)
    d_rnn = rnn_conv_b.shape[1]

    gather = _Queue("gather", GATHER_US_PER_MB)
    for l in range(L):
        for n in USE_ORDER:
            shard = W[n][l].T if n in TRANSPOSED else W[n][l]
            gather.push((n, l), shard.astype(BF16) if n in BF16_ON_WIRE else shard)
    gather.flush("gather_first", upto=("ffn1_w_gu", 0))
    full_cache = {}

    def full(n, l):
        if (n, l) not in full_cache:
            gather.flush(f"gather_{n}_{l}", upto=(n, l))
            full_cache[(n, l)] = _unshard(n, gather.done[(n, l)])
        return full_cache[(n, l)]

    def fwd_comm(micros):
        return gather.take(micros)

    bd_a = [_block_diag(rnn_w_a[l]).astype(BF16) for l in range(L)]
    bd_x = [_block_diag(rnn_w_x[l]).astype(BF16) for l in range(L)]

    def vec(name, l):
        return W[name][l:l + 1]

    saved = []
    h, hb = x2, x2
    for l in range(L):
        s = {"hb_in": hb}
        w_gu = full("ffn1_w_gu", l)
        comm = fwd_comm(105)
        s["gu1"], s["a1"] = _ffn_up(hb, w_gu, f"ffn1_up_{l}", comm=comm)
        gather.landed(comm)
        w_down = full("ffn1_w_down", l)
        comm = fwd_comm(65)
        s["r1"], y1, s["y1b"] = _ffn_down_ln(s["a1"], w_down, h, vec("ln1_g", l), vec("ln1_b", l),
                                              f"ffn1_down_ln_{l}", comm=comm)
        gather.landed(comm)
        w_in = full("mix_w_in", l)
        comm = fwd_comm(5)
        s["c"], s["cv"], s["cg"], s["rx"], s["rg"], s["gc"], s["gr"] = _mix_in(
            s["y1b"], w_in, vec("mix_b_in", l), d_rnn, f"mix_in_{l}", comm=comm)
        gather.landed(comm)
        w_dw = full("conv_dw_w", l)
        comm = fwd_comm(195)
        s["cc"], s["cs"] = _conv31_gn(s["c"], w_dw, vec("conv_dw_b", l), vec("conv_gn_g", l),
                                      vec("conv_gn_b", l), f"conv31_gn_{l}", comm=comm)
        gather.landed(comm)
        s["r"] = _conv4(s["rx"], full("rnn_conv_w", l), vec("rnn_conv_b", l), f"conv4_{l}")
        comm = fwd_comm(35)
        s["ra"], s["ri"], s["a"], uu = _gates(s["r"], bd_a[l], bd_x[l], vec("rnn_b_a", l), vec("rnn_b_x", l),
                                              vec("rnn_lambda", l), f"gates_{l}", comm=comm)
        gather.landed(comm)
        comm = fwd_comm(80)
        s["h"], s["hp"], s["hg"] = _scan_fwd(s["a"], uu, s["rg"], f"scan_{l}", comm=comm)
        gather.landed(comm)
        w_cp, w_rp, w_out = full("conv_w_proj", l), full("rnn_w_proj", l), full("mix_w_out", l)
        comm = fwd_comm(85)
        s["yc"], s["yr"], s["m"], s["r2"], y2, s["y2b"] = _mix_out_ln(
            s["cs"], s["hg"], s["gc"], s["gr"], y1, w_cp, w_rp, w_out, vec("ln2_g", l), vec("ln2_b", l),
            f"mix_out_ln_{l}", comm=comm)
        gather.landed(comm)
        w_gu2 = full("ffn2_w_gu", l)
        comm = fwd_comm(105)
        s["gu2"], s["a2"] = _ffn_up(s["y2b"], w_gu2, f"ffn2_up_{l}", comm=comm)
        gather.landed(comm)
        w_down2 = full("ffn2_w_down", l)
        comm = fwd_comm(65)
        s["r3"], h, hb = _ffn_down_ln(s["a2"], w_down2, y2, vec("ln3_g", l), vec("ln3_b", l),
                                      f"ffn2_down_ln_{l}", comm=comm)
        gather.landed(comm)
        saved.append(s)

    scatter = _Queue("scatter", SCATTER_US_PER_MB)
    G = {n: [None] * L for n in WEIGHTS}

    def ready(n, l, grad):
        G[n][l] = grad
        scatter.push((n, l), _to_shards(n, grad))

    def bwd_comm(micros):
        return scatter.take(micros)

    loss_acc, dr3, drb3, G["ln3_g"][L - 1], G["ln3_b"][L - 1] = _loss_ln_bwd(
        h, target, saved[L - 1]["r3"], vec("ln3_g", L - 1), "loss_ln3_bwd")
    grad_x = small = None
    small_shapes = [W[n].shape for n in REPLICATED]
    small_rows = -(-sum(math.prod(s) for s in small_shapes) // (1024 * SUBLANES)) * SUBLANES
    for l in reversed(range(L)):
        s = saved[l]
        comm = bwd_comm(95)
        dgu2 = _ffn_bwd_a(drb3, full("ffn2_w_down", l), s["gu2"], f"ffn2_bwd_a_{l}", comm=comm)
        scatter.landed(comm)
        ready("ffn2_w_down", l, _mm_tn(s["a2"], drb3, f"ffn2_dw_down_{l}", scale=0.5))
        ready("ffn2_w_gu", l, _mm_tn(s["y2b"], dgu2, f"ffn2_dw_gu_{l}"))
        comm = bwd_comm(140)
        dr2, drb2, G["ln2_g"][l], G["ln2_b"][l] = _nt_res(
            [dgu2], full("ffn2_w_gu", l), dr3, f"ffn2_bwd_x_{l}", ln=(s["r2"], vec("ln2_g", l)), comm=comm)
        scatter.landed(comm)
        comm = bwd_comm(120)
        dyc, dyr, du_tail, dcs, dh = _mix_bwd1(
            drb2, full("mix_w_out", l), full("conv_w_proj", l), full("rnn_w_proj", l), s["gc"], s["gr"], s["yc"],
            s["yr"], s["rg"], s["h"], f"mix_bwd_out_{l}", comm=comm)
        scatter.landed(comm)
        ready("mix_w_out", l, _mm_tn(s["m"], drb2, f"mix_dw_out_{l}"))
        ready("conv_w_proj", l, _mm_tn(s["cs"], dyc, f"conv_dw_proj_{l}"))
        ready("rnn_w_proj", l, _mm_tn(s["hg"], dyr, f"rnn_dw_proj_{l}"))
        gsc = _scan_bwd(dh, s["a"], f"scan_bwd_{l}")
        comm = bwd_comm(160)
        dr_, dpa, dpx, G["rnn_lambda"][l], G["rnn_b_a"][l], G["rnn_b_x"][l] = _gates_bwd(
            gsc, s["hp"], s["ra"], s["ri"], s["r"], vec("rnn_lambda", l), bd_a[l], bd_x[l], f"gates_bwd_{l}",
            comm=comm)
        scatter.landed(comm)
        dwa, dwx = _band_dw(s["r"], dpa, dpx, f"rnn_dw_ax_{l}")
        G["rnn_w_a"][l] = _diag_blocks(dwa, RNN_BLOCKS)
        G["rnn_w_x"][l] = _diag_blocks(dwx, RNN_BLOCKS)
        drx, dw4, G["rnn_conv_b"][l] = _conv4_bwd(dr_, s["rx"], full("rnn_conv_w", l), f"conv4_bwd_{l}")
        ready("rnn_conv_w", l, dw4[:RNN_CONV_WIDTH])
        comm = bwd_comm(250)
        dcv, dcg, dw31, G["conv_dw_b"][l], G["conv_gn_g"][l], G["conv_gn_b"][l] = _conv31_bwd(
            dcs, s["cc"], s["c"], s["cv"], s["cg"], full("conv_dw_w", l), vec("conv_gn_g", l),
            vec("conv_gn_b", l), f"conv31_bwd_{l}", comm=comm)
        scatter.landed(comm)
        ready("conv_dw_w", l, dw31[:CONV_WIDTH])
        du = [dcv, dcg, drx, du_tail]
        ready("mix_w_in", l, jnp.concatenate(
            [_mm_tn(s["y1b"], piece, f"mix_dw_in_{l}_{p}") for p, piece in enumerate(du)], axis=1))
        comm = bwd_comm(175)
        dr1, drb1, G["ln1_g"][l], G["ln1_b"][l], G["mix_b_in"][l] = _nt_res(
            du, full("mix_w_in", l), dr2, f"mix_bwd_in_{l}", ln=(s["r1"], vec("ln1_g", l)), colsum=True, comm=comm)
        scatter.landed(comm)
        ready("ffn1_w_down", l, _mm_tn(s["a1"], drb1, f"ffn1_dw_down_{l}", scale=0.5))
        comm = bwd_comm(95)
        dgu1 = _ffn_bwd_a(drb1, full("ffn1_w_down", l), s["gu1"], f"ffn1_bwd_a_{l}", comm=comm)
        scatter.landed(comm)
        if l == 0:
            local_small = [jnp.stack([g.reshape(W[n].shape[1:]) for g in G[n]]) for n in REPLICATED]
            comm = {"kind": "gather", "arrs": [_pack(local_small, small_rows)[0]]}
            ready("ffn1_w_gu", l, _mm_tn(s["hb_in"], dgu1, f"ffn1_dw_gu_{l}", comm=comm))
            (small,) = comm["recv"]
        else:
            ready("ffn1_w_gu", l, _mm_tn(s["hb_in"], dgu1, f"ffn1_dw_gu_{l}"))
        if l > 0:
            comm = bwd_comm(130)
            dr3, drb3, G["ln3_g"][l - 1], G["ln3_b"][l - 1] = _nt_res(
                [dgu1], full("ffn1_w_gu", l), dr1, f"ffn1_bwd_x_{l}", ln=(saved[l - 1]["r3"], vec("ln3_g", l - 1)),
                comm=comm)
        else:
            comm = bwd_comm(1e9)
            (grad_x,) = _nt_res([dgu1], full("ffn1_w_gu", l), dr1, f"ffn1_bwd_x_{l}", comm=comm)
        scatter.landed(comm)
    scatter.flush("scatter_rest")

    loss = lax.psum(loss_acc[0, 0], ("x", "y", "c"))

    out = {}
    for n in SHARDED:
        recvs = [scatter.done[(n, l)] for l in range(L)]
        if n in TRANSPOSED:
            res = _reduce_adamw(recvs, *[jnp.swapaxes(a, 1, 2) for a in (W[n], M[n], V[n])], f"adamw_{n}")
            out[n] = [jnp.swapaxes(a, 1, 2) for a in res]
        else:
            out[n] = _reduce_adamw(recvs, W[n], M[n], V[n], f"adamw_{n}")
    packed = _reduce_adamw([small], _pack([W[n] for n in REPLICATED], small_rows),
                           _pack([M[n] for n in REPLICATED], small_rows),
                           _pack([V[n] for n in REPLICATED], small_rows), "adamw_small")
    unpacked = [_unpack(p, small_shapes) for p in packed]
    for i, n in enumerate(REPLICATED):
        out[n] = tuple(u[i] for u in unpacked)

    return (loss, grad_x.reshape(x.shape), *[out[n][0] for n in WEIGHTS], *[out[n][1] for n in WEIGHTS],
            *[out[n][2] for n in WEIGHTS], *[out[n][3] for n in WEIGHTS])
```

```python
import math

import jax
import jax.numpy as jnp
from jax import lax
from jax.experimental import pallas as pl
from jax.experimental.pallas import tpu as pltpu

F32 = jnp.float32
BF16 = jnp.bfloat16
MESH = pl.DeviceIdType.MESH

DEPTH = 2
ALPHA = (2 * DEPTH) ** 0.25
LN_EPS = 1e-5
RG_LRU_C = 8.0
CONV_WIDTH = 31
RNN_CONV_WIDTH = 4
RNN_BLOCKS = 16
N_DEV = 8
ADAM_LR, ADAM_B1, ADAM_B2, ADAM_EPS, ADAM_WD, ADAM_STEP = 0.001, 0.9, 0.999, 1e-08, 0.01, 10

LANES = 128
SUBLANES = 8
VMEM_LIMIT = 56 * 1024 * 1024
CONV_PAD = 32
RNN_PAD = 8
ADAMW_ROWS = 352
MM_TN_OUT_BYTES = 12 * 1024 * 1024
NORM_ROWS = 256
CONV_BLOCK = 32
GATHER_US_PER_MB = 43.0
SCATTER_US_PER_MB = 86.0

WEIGHTS = ['ffn1_w_gu', 'ffn1_w_down', 'ln1_g', 'ln1_b', 'mix_w_in', 'mix_b_in', 'conv_dw_w', 'conv_dw_b',
           'conv_gn_g', 'conv_gn_b', 'conv_w_proj', 'rnn_conv_w', 'rnn_conv_b', 'rnn_w_a', 'rnn_b_a', 'rnn_w_x',
           'rnn_b_x', 'rnn_lambda', 'rnn_w_proj', 'mix_w_out', 'ln2_g', 'ln2_b', 'ffn2_w_gu', 'ffn2_w_down',
           'ln3_g', 'ln3_b']
COL_SHARDED = ['ffn1_w_gu', 'mix_w_in', 'ffn2_w_gu', 'conv_dw_w', 'rnn_conv_w']
TRANSPOSED = ['ffn1_w_gu', 'mix_w_in', 'ffn2_w_gu']
ROW_SHARDED = ['ffn1_w_down', 'conv_w_proj', 'rnn_w_proj', 'mix_w_out', 'ffn2_w_down']
SHARDED = COL_SHARDED + ROW_SHARDED
BF16_ON_WIRE = ['ffn1_w_gu', 'mix_w_in', 'ffn2_w_gu', 'ffn1_w_down', 'conv_w_proj', 'rnn_w_proj', 'mix_w_out',
                'ffn2_w_down']
REPLICATED = [n for n in WEIGHTS if n not in SHARDED]
USE_ORDER = ['ffn1_w_gu', 'ffn1_w_down', 'mix_w_in', 'conv_dw_w', 'rnn_conv_w', 'conv_w_proj', 'rnn_w_proj',
             'mix_w_out', 'ffn2_w_gu', 'ffn2_w_down']


def _cp(n_axes=1):
    return pltpu.CompilerParams(dimension_semantics=("arbitrary",) * n_axes, vmem_limit_bytes=VMEM_LIMIT)


def _rows(tm, c):
    return pl.BlockSpec((tm, c), lambda i: (i, 0))


def _res(shape):
    nd = len(shape)
    return pl.BlockSpec(tuple(shape), lambda *_: (0,) * nd, pipeline_mode=pl.Buffered(1))


def _acc(shape):
    nd = len(shape)
    return pl.BlockSpec(tuple(shape), lambda *_: (0,) * nd)


def _tile(t, want):
    return want if t % want == 0 else t


def _sds(shape, dtype):
    return jax.ShapeDtypeStruct(tuple(shape), dtype)


def _mbytes(a):
    return a.size * a.dtype.itemsize / 1e6


def _ln_fwd(r, g, b):
    mu = jnp.mean(r, axis=-1, keepdims=True)
    xc = r - mu
    var = jnp.mean(xc * xc, axis=-1, keepdims=True)
    return xc * lax.rsqrt(var + LN_EPS) * g + b


def _ln_bwd(dy, r, g):
    mu = jnp.mean(r, axis=-1, keepdims=True)
    xc = r - mu
    var = jnp.mean(xc * xc, axis=-1, keepdims=True)
    rstd = lax.rsqrt(var + LN_EPS)
    xhat = xc * rstd
    dxh = dy * g
    m1 = jnp.mean(dxh, axis=-1, keepdims=True)
    m2 = jnp.mean(dxh * xhat, axis=-1, keepdims=True)
    dr = rstd * (dxh - m1 - xhat * m2)
    return dr, jnp.sum(dy * xhat, axis=0, keepdims=True), jnp.sum(dy, axis=0, keepdims=True)


def _sigmoid(x):
    return jax.nn.sigmoid(x)


_GELU_K = math.sqrt(2.0 / math.pi)


def _gelu(x):
    t = jnp.tanh(_GELU_K * (x + 0.044715 * x * x * x))
    return 0.5 * x * (1.0 + t), t


def _gelu_grad(x, t):
    return 0.5 * (1.0 + t) + 0.5 * x * (1.0 - t * t) * (_GELU_K * (1.0 + 3.0 * 0.044715 * x * x))


def _expm1(x):
    taylor = x * (1.0 + x * (0.5 + x * (1.0 / 6.0 + x * (1.0 / 24.0 + x * (1.0 / 120.0)))))
    return jnp.where(jnp.abs(x) < 0.03, taylor, jnp.exp(x) - 1.0)


def _softplus_neg(lam):
    return jnp.maximum(-lam, 0.0) + jnp.log1p(jnp.exp(-jnp.abs(lam)))


def _dot(a, b):
    return jnp.dot(a, b, preferred_element_type=F32)


def _dot_nt(a, b):
    return lax.dot_general(a, b, (((1,), (1,)), ((), ())), preferred_element_type=F32)


def _dot_tn(a, b):
    return lax.dot_general(a, b, (((0,), (0,)), ((), ())), preferred_element_type=F32)


def _chunks(width, cn):
    return [(j, min(cn, width - j)) for j in range(0, width, cn)]


def _band_chunks(width, block):
    out = []
    for c0, cw in _chunks(width, 256):
        lo = (c0 // block) * block
        hi = ((c0 + cw - 1) // block + 1) * block
        out.append((c0, cw, lo // LANES * LANES, min(width, -(-hi // LANES) * LANES)))
    return out


def _position():
    return lax.axis_index("x"), lax.axis_index("y"), lax.axis_index("c")


def _index(p):
    return 4 * p[0] + 2 * p[1] + p[2]


def _comm_out_shapes(kind, arrs):
    return [_sds((N_DEV,) + a.shape if kind == "gather" else a.shape, a.dtype) for a in arrs]


def _comm_scratch(n):
    return [pltpu.SemaphoreType.DMA((n, 7)), pltpu.SemaphoreType.DMA((n, 7)), pltpu.SemaphoreType.DMA((n,))]


def _comm_phases(kind, srcs, dsts, send_sems, recv_sems, local_sems):
    n = len(srcs)
    x, y, c = _position()
    me, sibling = (x, y, c), (x, y, 1 - c)
    chips = [(1 - x, y), (x, 1 - y), (1 - x, 1 - y)]

    if kind == "gather":
        def copy(a, k, block, to, src=None):
            dst = dsts[a].at[_index(block)]
            return pltpu.make_async_remote_copy(
                src_ref=dst if src is None else src, dst_ref=dst, send_sem=send_sems.at[a, k],
                recv_sem=recv_sems.at[a, k], device_id=to, device_id_type=MESH)

        def mine(a):
            return pltpu.make_async_copy(srcs[a], dsts[a].at[_index(me)], local_sems.at[a])

        def first(a):
            return [copy(a, 0, me, sibling, src=srcs[a])] + [
                copy(a, 1 + j, me, (*chip, c), src=srcs[a]) for j, chip in enumerate(chips)]

        def start():
            for a in range(n):
                mine(a).start()
            for a in range(n):
                for cp in first(a):
                    cp.start()

        def mid():
            for j, chip in enumerate(chips):
                for a in range(n):
                    copy(a, 1 + j, (*chip, c), me).wait_recv()
                    copy(a, 4 + j, (*chip, c), sibling).start()

        def end():
            for a in range(n):
                copy(a, 0, sibling, me).wait_recv()
            for j, chip in enumerate(chips):
                for a in range(n):
                    copy(a, 4 + j, (*chip, 1 - c), me).wait_recv()
            for a in range(n):
                for cp in first(a):
                    cp.wait_send()
                for j, chip in enumerate(chips):
                    copy(a, 4 + j, (*chip, c), sibling).wait_send()
                mine(a).wait()

        return start, mid, end

    def peer_of(k):
        return (1 - x if k & 4 else x, 1 - y if k & 2 else y, 1 - c if k & 1 else c)

    def own(a):
        return pltpu.make_async_copy(srcs[a].at[_index(me)], dsts[a].at[0], local_sems.at[a])

    def remote(a, k):
        peer = peer_of(k)
        return pltpu.make_async_remote_copy(
            src_ref=srcs[a].at[_index(peer)], dst_ref=dsts[a].at[k], send_sem=send_sems.at[a, k - 1],
            recv_sem=recv_sems.at[a, k - 1], device_id=peer, device_id_type=MESH)

    def start():
        for a in range(n):
            own(a).start()
        for k in range(1, N_DEV):
            for a in range(n):
                remote(a, k).start()

    def end():
        for k in range(1, N_DEV):
            for a in range(n):
                remote(a, k).wait()
        for a in range(n):
            own(a).wait()

    return start, (lambda: None), end


def _exchange(kind, arrs, name):
    n = len(arrs)
    hbm = pl.BlockSpec(memory_space=pl.ANY)

    def body(*refs):
        start, mid, end = _comm_phases(kind, refs[:n], refs[n:2 * n], *refs[2 * n:])
        start()
        mid()
        end()

    return pl.pallas_call(
        body, name=name, in_specs=[hbm] * n, out_specs=[hbm] * n, out_shape=_comm_out_shapes(kind, arrs),
        scratch_shapes=_comm_scratch(n))(*arrs)


def _pallas(body, *, name, grid, in_specs, out_specs, out_shape, args, scratch_shapes=(), comm=None):
    in_specs, out_specs, out_shape = list(in_specs), list(out_specs), list(out_shape)
    scratch_shapes = list(scratch_shapes)
    if not comm:
        return pl.pallas_call(body, name=name, grid=grid, in_specs=in_specs, out_specs=out_specs,
                              out_shape=out_shape, scratch_shapes=scratch_shapes,
                              compiler_params=_cp(len(grid)))(*args)
    arrs = comm["arrs"]
    ns, n_in, n_out, n_scr = len(arrs), len(in_specs), len(out_specs), len(scratch_shapes)
    hbm = pl.BlockSpec(memory_space=pl.ANY)
    total = math.prod(grid)

    def carrier(*refs):
        ins, srcs = refs[:n_in], refs[n_in:n_in + ns]
        outs, dsts = refs[n_in + ns:n_in + ns + n_out], refs[n_in + ns + n_out:n_in + 2 * ns + n_out]
        scr, sems = refs[n_in + 2 * ns + n_out:n_in + 2 * ns + n_out + n_scr], refs[n_in + 2 * ns + n_out + n_scr:]
        step = pl.program_id(0)
        for ax in range(1, len(grid)):
            step = step * grid[ax] + pl.program_id(ax)
        start, mid, end = _comm_phases(comm["kind"], srcs, dsts, *sems)
        pl.when(step == 0)(start)
        body(*ins, *outs, *scr)
        pl.when(step == total - 1)(mid)
        pl.when(step == total - 1)(end)

    res = pl.pallas_call(
        carrier, name=name, grid=grid, in_specs=in_specs + [hbm] * ns, out_specs=out_specs + [hbm] * ns,
        out_shape=out_shape + _comm_out_shapes(comm["kind"], arrs),
        scratch_shapes=scratch_shapes + _comm_scratch(ns), compiler_params=_cp(len(grid)))(*args, *arrs)
    comm["recv"] = res[n_out:]
    return res[:n_out]


def _ffn_up(xb, w, name, comm=None):
    T, D = xb.shape
    F = w.shape[0] // 2
    tm = _tile(T, 512)

    def body(x_ref, w_ref, gu_ref, a_ref):
        x = x_ref[...].astype(BF16)
        for j, cw in _chunks(F, 256):
            g = _dot_nt(x, w_ref[j:j + cw, :])
            u = _dot_nt(x, w_ref[F + j:F + j + cw, :])
            gu_ref[:, j:j + cw] = g.astype(BF16)
            gu_ref[:, F + j:F + j + cw] = u.astype(BF16)
            a_ref[:, j:j + cw] = (g * _sigmoid(g) * u).astype(BF16)

    return _pallas(
        body, name=name, grid=(T // tm,),
        in_specs=[_rows(tm, D), _res(w.shape)],
        out_specs=[_rows(tm, 2 * F), _rows(tm, F)],
        out_shape=[_sds((T, 2 * F), BF16), _sds((T, F), BF16)],
        args=(xb, w), comm=comm)


def _ffn_down_ln(a, wd, xres, g, b, name, comm=None):
    T, F = a.shape
    D = wd.shape[1]
    tm = _tile(T, 512)

    def body(a_ref, wd_ref, x_ref, g_ref, b_ref, r_ref, y_ref, yb_ref):
        r = ALPHA * x_ref[...] + 0.5 * _dot(a_ref[...], wd_ref[...])
        y = _ln_fwd(r, g_ref[...], b_ref[...])
        r_ref[...] = r
        y_ref[...] = y
        yb_ref[...] = y.astype(BF16)

    return _pallas(
        body, name=name, grid=(T // tm,),
        in_specs=[_rows(tm, F), _res(wd.shape), _rows(tm, D), _res((1, D)), _res((1, D))],
        out_specs=[_rows(tm, D), _rows(tm, D), _rows(tm, D)],
        out_shape=[_sds((T, D), F32), _sds((T, D), F32), _sds((T, D), BF16)],
        args=(a, wd, xres, g, b), comm=comm)


def _mix_in(hb, w, bias, d_rnn, name, comm=None):
    T, D = hb.shape
    R = d_rnn
    tm = _tile(T, 512)
    o_cv, o_cg, o_rx, o_rg, o_gc, o_gr = 0, D, 2 * D, 2 * D + R, 2 * D + 2 * R, 3 * D + 2 * R

    def body(x_ref, w_ref, b_ref, c_ref, cv_ref, cg_ref, rx_ref, rg_ref, gc_ref, gr_ref):
        x = x_ref[...]

        def seg(off, j, cw):
            return _dot_nt(x, w_ref[off + j:off + j + cw, :]) + b_ref[:, off + j:off + j + cw]

        for j, cw in _chunks(D, 256):
            cv = seg(o_cv, j, cw)
            cg = seg(o_cg, j, cw)
            cv_ref[:, j:j + cw] = cv.astype(BF16)
            cg_ref[:, j:j + cw] = cg.astype(BF16)
            c_ref[:, j:j + cw] = cv * _sigmoid(cg)
            gc_ref[:, j:j + cw] = seg(o_gc, j, cw).astype(BF16)
            gr_ref[:, j:j + cw] = seg(o_gr, j, cw).astype(BF16)
        for j, cw in _chunks(R, 256):
            rx_ref[:, j:j + cw] = seg(o_rx, j, cw)
            rg_ref[:, j:j + cw] = seg(o_rg, j, cw).astype(BF16)

    return _pallas(
        body, name=name, grid=(T // tm,),
        in_specs=[_rows(tm, D), _res(w.shape), _res(bias.shape)],
        out_specs=[_rows(tm, D), _rows(tm, D), _rows(tm, D), _rows(tm, R), _rows(tm, R), _rows(tm, D),
                   _rows(tm, D)],
        out_shape=[_sds((T, D), F32), _sds((T, D), BF16), _sds((T, D), BF16), _sds((T, R), F32),
                   _sds((T, R), BF16), _sds((T, D), BF16), _sds((T, D), BF16)],
        args=(hb, w, bias), comm=comm)


def _cols(t, rows=None):
    return pl.BlockSpec((t if rows is None else rows, LANES), lambda j: (0, j))


def _gn_stats(cc):
    mu = jnp.mean(cc, axis=-1, keepdims=True)
    xc = cc - mu
    var = jnp.mean(xc * xc, axis=-1, keepdims=True)
    rstd = lax.rsqrt(var + LN_EPS)
    return xc * rstd, rstd


def _tap_groups(offsets):
    groups = {}
    for k, s in enumerate(offsets):
        groups.setdefault(s % SUBLANES, []).append((k, s - s % SUBLANES))
    return sorted(groups.items())


def _shifted(win, phase):
    return win if phase == 0 else pltpu.roll(win, win.shape[0] - phase, 0)


def _tap_sum(win, wv, groups, rows):
    parts, t = [None] * 4, 0
    for phase, taps in groups:
        sh = _shifted(win, phase)
        for k, off in taps:
            term = wv[k:k + 1, :] * sh[off:off + rows, :]
            parts[t % 4] = term if parts[t % 4] is None else parts[t % 4] + term
            t += 1
    return (parts[0] + parts[1]) + (parts[2] + parts[3])


def _conv31_gn(c, w, bias, gg, gb, name, comm=None):
    T, D = c.shape
    K = w.shape[0]
    B, P, N = CONV_BLOCK, CONV_PAD, _tile(T, NORM_ROWS)
    assert D % LANES == 0 and T % B == 0 and K - 1 <= P
    groups = _tap_groups([P - (K - 1) + k for k in range(K)])

    def body(c_ref, w_ref, b_ref, gg_ref, gb_ref, cc_ref, cs_ref, xpad):
        xpad[0:P, :] = jnp.zeros((P, LANES), F32)
        xpad[P:P + T, :] = c_ref[...]
        wv = w_ref[...]
        bv, ggv, gbv = b_ref[...], gg_ref[...], gb_ref[...]

        def conv_step(i, carry):
            base = pl.multiple_of(i * B, B)
            win = xpad[pl.ds(base, B + P), :]
            cc_ref[pl.ds(base, B), :] = _tap_sum(win, wv, groups, B) + bv
            return carry

        lax.fori_loop(0, T // B, conv_step, 0)

        def norm_step(i, carry):
            base = pl.multiple_of(i * N, N)
            xhat, _ = _gn_stats(cc_ref[pl.ds(base, N), :])
            gn = xhat * ggv + gbv
            cs_ref[pl.ds(base, N), :] = (gn * _sigmoid(gn)).astype(BF16)
            return carry

        lax.fori_loop(0, T // N, norm_step, 0)

    return _pallas(
        body, name=name, grid=(D // LANES,),
        in_specs=[_cols(T), _cols(T, K), _cols(T, 1), _cols(T, 1), _cols(T, 1)],
        out_specs=[_cols(T), _cols(T)],
        out_shape=[_sds((T, D), F32), _sds((T, D), BF16)],
        scratch_shapes=[pltpu.VMEM((T + P, LANES), F32)],
        args=(c, w, bias, gg, gb), comm=comm)


def _conv4(rx, w, bias, name):
    T, C = rx.shape
    K = w.shape[0]
    B, P = CONV_BLOCK, RNN_PAD
    assert C % LANES == 0 and T % B == 0 and K - 1 <= P
    groups = _tap_groups([P - (K - 1) + k for k in range(K)])

    def body(x_ref, w_ref, b_ref, r_ref, xpad):
        xpad[0:P, :] = jnp.zeros((P, LANES), F32)
        xpad[P:P + T, :] = x_ref[...]
        wv, bv = w_ref[...], b_ref[...]

        def step(i, carry):
            base = pl.multiple_of(i * B, B)
            win = xpad[pl.ds(base, B + P), :]
            r_ref[pl.ds(base, B), :] = _tap_sum(win, wv, groups, B) + bv
            return carry

        lax.fori_loop(0, T // B, step, 0)

    (r,) = _pallas(
        body, name=name, grid=(C // LANES,),
        in_specs=[_cols(T), _cols(T, K), _cols(T, 1)],
        out_specs=[_cols(T)],
        out_shape=[_sds((T, C), F32)],
        scratch_shapes=[pltpu.VMEM((T + RNN_PAD, LANES), F32)],
        args=(rx, w, bias))
    return r


def _gates(r, bda, bdx, b_a, b_x, lam, name, comm=None):
    T, C = r.shape
    tm = _tile(T, 512)
    chunks = _band_chunks(C, C // RNN_BLOCKS)

    def body(r_ref, wa_ref, wx_ref, ba_ref, bx_ref, lam_ref, ra_ref, ri_ref, a_ref, u_ref):
        for c0, cw, k0, k1 in chunks:
            cols = slice(c0, c0 + cw)
            rb = r_ref[:, k0:k1].astype(BF16)
            ra = _sigmoid(_dot(rb, wa_ref[k0:k1, cols]) + ba_ref[:, cols])
            ri = _sigmoid(_dot(rb, wx_ref[k0:k1, cols]) + bx_ref[:, cols])
            log_a = (-RG_LRU_C) * ra * _softplus_neg(lam_ref[:, cols])
            ra_ref[:, cols] = ra
            ri_ref[:, cols] = ri
            a_ref[:, cols] = jnp.exp(log_a)
            u_ref[:, cols] = jnp.sqrt(-_expm1(2.0 * log_a)) * (ri * r_ref[:, cols])

    return _pallas(
        body, name=name, grid=(T // tm,),
        in_specs=[_rows(tm, C), _res(bda.shape), _res(bdx.shape), _res((1, C)), _res((1, C)), _res((1, C))],
        out_specs=[_rows(tm, C)] * 4,
        out_shape=[_sds((T, C), F32)] * 4,
        args=(r, bda, bdx, b_a, b_x, lam), comm=comm)


def _scan_fwd(a, u, rg, name, comm=None):
    T, C = a.shape
    tt = _tile(T, 512)

    def body(a_ref, u_ref, rg_ref, h_ref, hp_ref, hg_ref, carry):
        @pl.when(pl.program_id(0) == 0)
        def _():
            carry[...] = jnp.zeros_like(carry)

        row = lax.broadcasted_iota(jnp.int32, (SUBLANES, C), 0)

        def group(i, hprev):
            base = pl.multiple_of(i * SUBLANES, SUBLANES)
            av = a_ref[pl.ds(base, SUBLANES), :]
            uv = u_ref[pl.ds(base, SUBLANES), :]
            for s in (1, 2, 4):
                a_s = jnp.where(row >= s, pltpu.roll(av, s, 0), 1.0)
                u_s = jnp.where(row >= s, pltpu.roll(uv, s, 0), 0.0)
                uv = av * u_s + uv
                av = av * a_s
            h = av * hprev + uv
            h_ref[pl.ds(base, SUBLANES), :] = h
            hp_ref[pl.ds(base, SUBLANES), :] = jnp.where(row >= 1, pltpu.roll(h, 1, 0), hprev)
            return h[SUBLANES - 1:SUBLANES, :]

        carry[...] = lax.fori_loop(0, tt // SUBLANES, group, carry[...])
        gel, _ = _gelu(rg_ref[...].astype(F32))
        hg_ref[...] = (h_ref[...] * gel).astype(BF16)

    return _pallas(
        body, name=name, grid=(T // tt,),
        in_specs=[_rows(tt, C)] * 3,
        out_specs=[_rows(tt, C)] * 3,
        out_shape=[_sds((T, C), F32), _sds((T, C), F32), _sds((T, C), BF16)],
        scratch_shapes=[pltpu.VMEM((1, C), F32)],
        args=(a, u, rg), comm=comm)


def _mix_out_ln(cs, hg, gc, gr, hres, wcp, wrp, wout, g, b, name, comm=None):
    T, D = cs.shape
    C = hg.shape[1]
    tm = _tile(T, 512)

    def body(cs_ref, hg_ref, gc_ref, gr_ref, h_ref, wcp_ref, wrp_ref, wo_ref, g_ref, b_ref,
             yc_ref, yr_ref, m_ref, r_ref, y_ref, yb_ref):
        yc = _dot(cs_ref[...], wcp_ref[...])
        yr = _dot(hg_ref[...], wrp_ref[...])
        m = (_sigmoid(gc_ref[...].astype(F32)) * yc + _sigmoid(gr_ref[...].astype(F32)) * yr).astype(BF16)
        r = ALPHA * h_ref[...] + _dot(m, wo_ref[...])
        y = _ln_fwd(r, g_ref[...], b_ref[...])
        yc_ref[...] = yc.astype(BF16)
        yr_ref[...] = yr.astype(BF16)
        m_ref[...] = m
        r_ref[...] = r
        y_ref[...] = y
        yb_ref[...] = y.astype(BF16)

    return _pallas(
        body, name=name, grid=(T // tm,),
        in_specs=[_rows(tm, D), _rows(tm, C), _rows(tm, D), _rows(tm, D), _rows(tm, D), _res(wcp.shape),
                  _res(wrp.shape), _res(wout.shape), _res((1, D)), _res((1, D))],
        out_specs=[_rows(tm, D)] * 6,
        out_shape=[_sds((T, D), BF16), _sds((T, D), BF16), _sds((T, D), BF16), _sds((T, D), F32),
                   _sds((T, D), F32), _sds((T, D), BF16)],
        args=(cs, hg, gc, gr, hres, wcp, wrp, wout, g, b), comm=comm)


def _loss_ln_bwd(y, target, r, g, name):
    T, D = y.shape
    tm = _tile(T, 512)

    def body(y_ref, t_ref, r_ref, g_ref, loss_ref, dr_ref, drb_ref, dg_ref, db_ref):
        @pl.when(pl.program_id(0) == 0)
        def _():
            loss_ref[...] = jnp.zeros_like(loss_ref)
            dg_ref[...] = jnp.zeros_like(dg_ref)
            db_ref[...] = jnp.zeros_like(db_ref)

        e = y_ref[...] - t_ref[...]
        loss_ref[...] += (0.5 / D) * jnp.sum(e * e)
        dr, dg, db = _ln_bwd(e * (1.0 / D), r_ref[...], g_ref[...])
        dr_ref[...] = dr
        drb_ref[...] = dr.astype(BF16)
        dg_ref[...] += dg
        db_ref[...] += db

    return _pallas(
        body, name=name, grid=(T // tm,),
        in_specs=[_rows(tm, D), _rows(tm, D), _rows(tm, D), _res((1, D))],
        out_specs=[_acc((SUBLANES, LANES)), _rows(tm, D), _rows(tm, D), _acc((1, D)), _acc((1, D))],
        out_shape=[_sds((SUBLANES, LANES), F32), _sds((T, D), F32), _sds((T, D), BF16), _sds((1, D), F32),
                   _sds((1, D), F32)],
        args=(y, target, r, g))


def _ffn_bwd_a(drb, wd, gu, name, comm=None):
    T, D = drb.shape
    F = wd.shape[0]
    tm = _tile(T, 512)

    def body(d_ref, wd_ref, gu_ref, o_ref):
        d = d_ref[...]
        for j, cw in _chunks(F, 256):
            da = 0.5 * _dot_nt(d, wd_ref[j:j + cw, :])
            gt = gu_ref[:, j:j + cw].astype(F32)
            up = gu_ref[:, F + j:F + j + cw].astype(F32)
            sg = _sigmoid(gt)
            o_ref[:, j:j + cw] = (da * up * (sg * (1.0 + gt * (1.0 - sg)))).astype(BF16)
            o_ref[:, F + j:F + j + cw] = (da * (gt * sg)).astype(BF16)

    (dgu,) = _pallas(
        body, name=name, grid=(T // tm,),
        in_specs=[_rows(tm, D), _res(wd.shape), _rows(tm, 2 * F)],
        out_specs=[_rows(tm, 2 * F)],
        out_shape=[_sds((T, 2 * F), BF16)],
        args=(drb, wd, gu), comm=comm)
    return dgu


def _nt_res(dus, w, dres, name, ln=None, colsum=False, comm=None):
    T = dus[0].shape[0]
    widths = [d.shape[1] for d in dus]
    offs = [sum(widths[:p]) for p in range(len(dus))]
    K, D, P = sum(widths), w.shape[1], len(dus)
    tm = _tile(T, 512)
    n_in = P + 2 + (2 if ln else 0)

    def body(*refs):
        du_refs, w_ref, dres_ref = refs[:P], refs[P], refs[P + 1]
        outs = refs[n_in:]
        dy = ALPHA * dres_ref[...]
        for du_ref, off, width in zip(du_refs, offs, widths):
            dy = dy + _dot(du_ref[...], w_ref[off:off + width, :])
        if ln:
            r_ref, g_ref = refs[P + 2:P + 4]

            @pl.when(pl.program_id(0) == 0)
            def _():
                outs[2][...] = jnp.zeros_like(outs[2])
                outs[3][...] = jnp.zeros_like(outs[3])

            dr, dg, db = _ln_bwd(dy, r_ref[...], g_ref[...])
            outs[0][...] = dr
            outs[1][...] = dr.astype(BF16)
            outs[2][...] += dg
            outs[3][...] += db
        else:
            outs[0][...] = dy
        if colsum:
            cs_ref = outs[-1]

            @pl.when(pl.program_id(0) == 0)
            def _():
                cs_ref[...] = jnp.zeros_like(cs_ref)

            for du_ref, off, width in zip(du_refs, offs, widths):
                cs_ref[:, off:off + width] += jnp.sum(du_ref[...].astype(F32), axis=0, keepdims=True)

    in_specs = [_rows(tm, width) for width in widths] + [_res(w.shape), _rows(tm, D)]
    args = list(dus) + [w, dres]
    if ln:
        in_specs += [_rows(tm, D), _res((1, D))]
        args += list(ln)
        out_specs = [_rows(tm, D), _rows(tm, D), _acc((1, D)), _acc((1, D))]
        out_shape = [_sds((T, D), F32), _sds((T, D), BF16), _sds((1, D), F32), _sds((1, D), F32)]
    else:
        out_specs = [_rows(tm, D)]
        out_shape = [_sds((T, D), F32)]
    if colsum:
        out_specs.append(_acc((1, K)))
        out_shape.append(_sds((1, K), F32))
    return _pallas(body, name=name, grid=(T // tm,), in_specs=in_specs, out_specs=out_specs, out_shape=out_shape,
                   args=args, comm=comm)


def _mm_tn(x, dy, name, scale=1.0, comm=None):
    T, K = x.shape
    N = dy.shape[1]
    tt = _tile(T, 1024)
    tn = next(N // d for d in range(1, N // LANES + 1)
              if N % d == 0 and (N // d) % LANES == 0 and K * (N // d) * 4 <= MM_TN_OUT_BYTES)
    nt = T // tt

    def body(x_ref, dy_ref, o_ref):
        t = pl.program_id(1)

        @pl.when(t == 0)
        def _():
            o_ref[...] = jnp.zeros_like(o_ref)

        o_ref[...] += _dot_tn(x_ref[...].astype(BF16), dy_ref[...])
        if scale != 1.0:
            @pl.when(t == nt - 1)
            def _():
                o_ref[...] = o_ref[...] * scale

    (out,) = _pallas(
        body, name=name, grid=(N // tn, nt),
        in_specs=[pl.BlockSpec((tt, K), lambda j, t: (t, 0)), pl.BlockSpec((tt, tn), lambda j, t: (t, j))],
        out_specs=[pl.BlockSpec((K, tn), lambda j, t: (0, j))],
        out_shape=[_sds((K, N), F32)],
        args=(x, dy), comm=comm)
    return out


def _mix_bwd1(drb, wout, wcp, wrp, gc, gr, yc, yr, rg, h, name, comm=None):
    T, D = drb.shape
    C = rg.shape[1]
    tm = _tile(T, 512)

    def body(d_ref, wo_ref, wcp_ref, wrp_ref, gc_ref, gr_ref, yc_ref, yr_ref, rg_ref, h_ref,
             dyc_ref, dyr_ref, tail_ref, dcs_ref, dh_ref):
        dm = _dot_nt(d_ref[...], wo_ref[...])
        sc = _sigmoid(gc_ref[...].astype(F32))
        sr = _sigmoid(gr_ref[...].astype(F32))
        dyc = (dm * sc).astype(BF16)
        dyr = (dm * sr).astype(BF16)
        dyc_ref[...] = dyc
        dyr_ref[...] = dyr
        tail_ref[:, C:C + D] = (dm * yc_ref[...].astype(F32) * sc * (1.0 - sc)).astype(BF16)
        tail_ref[:, C + D:] = (dm * yr_ref[...].astype(F32) * sr * (1.0 - sr)).astype(BF16)
        dcs_ref[...] = _dot_nt(dyc, wcp_ref[...])
        dhg = _dot_nt(dyr, wrp_ref[...])
        rgv = rg_ref[...].astype(F32)
        gel, t = _gelu(rgv)
        dh_ref[...] = dhg * gel
        tail_ref[:, 0:C] = (dhg * h_ref[...] * _gelu_grad(rgv, t)).astype(BF16)

    return _pallas(
        body, name=name, grid=(T // tm,),
        in_specs=[_rows(tm, D), _res(wout.shape), _res(wcp.shape), _res(wrp.shape), _rows(tm, D), _rows(tm, D),
                  _rows(tm, D), _rows(tm, D), _rows(tm, C), _rows(tm, C)],
        out_specs=[_rows(tm, D), _rows(tm, D), _rows(tm, C + 2 * D), _rows(tm, D), _rows(tm, C)],
        out_shape=[_sds((T, D), BF16), _sds((T, D), BF16), _sds((T, C + 2 * D), BF16), _sds((T, D), F32),
                   _sds((T, C), F32)],
        args=(drb, wout, wcp, wrp, gc, gr, yc, yr, rg, h), comm=comm)


def _scan_bwd(dh, a, name):
    T, C = dh.shape
    tt = _tile(T, 512)
    nt = T // tt
    ng = tt // SUBLANES

    def body(d_ref, a_ref, g_ref, carry):
        @pl.when(pl.program_id(0) == 0)
        def _():
            carry[...] = jnp.zeros_like(carry)

        row = lax.broadcasted_iota(jnp.int32, (SUBLANES, C), 0)

        def group(j, enext):
            base = pl.multiple_of((ng - 1 - j) * SUBLANES, SUBLANES)
            av = a_ref[pl.ds(base, SUBLANES), :]
            dv = d_ref[pl.ds(base, SUBLANES), :]
            bv = av * dv
            for s in (1, 2, 4):
                keep = row < SUBLANES - s
                a_s = jnp.where(keep, pltpu.roll(av, SUBLANES - s, 0), 1.0)
                b_s = jnp.where(keep, pltpu.roll(bv, SUBLANES - s, 0), 0.0)
                bv = av * b_s + bv
                av = av * a_s
            e = av * enext + bv
            e_up = jnp.where(row < SUBLANES - 1, pltpu.roll(e, SUBLANES - 1, 0), enext)
            g_ref[pl.ds(base, SUBLANES), :] = dv + e_up
            return e[0:1, :]

        carry[...] = lax.fori_loop(0, ng, group, carry[...])

    rev = pl.BlockSpec((tt, C), lambda i: (nt - 1 - i, 0))
    (g,) = _pallas(
        body, name=name, grid=(nt,), in_specs=[rev, rev], out_specs=[rev],
        out_shape=[_sds((T, C), F32)],
        scratch_shapes=[pltpu.VMEM((1, C), F32)],
        args=(dh, a))
    return g


def _gates_bwd(g, hp, ra, ri, r, lam, bda, bdx, name, comm=None):
    T, C = g.shape
    tm = _tile(T, 512)
    nt = T // tm
    chunks = _band_chunks(C, C // RNN_BLOCKS)

    def body(g_ref, hp_ref, ra_ref, ri_ref, r_ref, lam_ref, wa_ref, wx_ref,
             dr_ref, dpa_ref, dpx_ref, dlam_ref, dba_ref, dbx_ref):
        @pl.when(pl.program_id(0) == 0)
        def _():
            dlam_ref[...] = jnp.zeros_like(dlam_ref)
            dba_ref[...] = jnp.zeros_like(dba_ref)
            dbx_ref[...] = jnp.zeros_like(dbx_ref)

        for c0, cw, _, _ in chunks:
            cols = slice(c0, c0 + cw)
            gv, rav, riv, rv = g_ref[:, cols], ra_ref[:, cols], ri_ref[:, cols], r_ref[:, cols]
            sp = _softplus_neg(lam_ref[:, cols])
            log_a = (-RG_LRU_C) * rav * sp
            av = jnp.exp(log_a)
            mult = jnp.sqrt(-_expm1(2.0 * log_a))
            d_mult = gv * riv * rv
            d_i = gv * mult * rv
            d_loga = gv * hp_ref[:, cols] * av - d_mult * (av * av) / mult
            d_ra = d_loga * ((-RG_LRU_C) * sp)
            dpa = d_ra * rav * (1.0 - rav)
            dpx = d_i * riv * (1.0 - riv)
            dpa_ref[:, cols] = dpa.astype(BF16)
            dpx_ref[:, cols] = dpx.astype(BF16)
            dr_ref[:, cols] = gv * mult * riv
            dlam_ref[:, cols] += jnp.sum(d_loga * ((-RG_LRU_C) * rav), axis=0, keepdims=True)
            dba_ref[:, cols] += jnp.sum(dpa, axis=0, keepdims=True)
            dbx_ref[:, cols] += jnp.sum(dpx, axis=0, keepdims=True)
        for c0, cw, k0, k1 in chunks:
            cols = slice(c0, c0 + cw)
            dr_ref[:, k0:k1] += (_dot_nt(dpa_ref[:, cols], wa_ref[k0:k1, cols])
                                 + _dot_nt(dpx_ref[:, cols], wx_ref[k0:k1, cols]))

        @pl.when(pl.program_id(0) == nt - 1)
        def _():
            dlam_ref[...] = dlam_ref[...] * (-_sigmoid(-lam_ref[...]))

    return _pallas(
        body, name=name, grid=(nt,),
        in_specs=[_rows(tm, C)] * 5 + [_res((1, C)), _res(bda.shape), _res(bdx.shape)],
        out_specs=[_rows(tm, C)] * 3 + [_acc((1, C))] * 3,
        out_shape=[_sds((T, C), F32), _sds((T, C), BF16), _sds((T, C), BF16)] + [_sds((1, C), F32)] * 3,
        args=(g, hp, ra, ri, r, lam, bda, bdx), comm=comm)


def _band_dw(r, dpa, dpx, name):
    T, C = r.shape
    tt = _tile(T, 512)
    chunks = _band_chunks(C, C // RNN_BLOCKS)

    def body(r_ref, dpa_ref, dpx_ref, oa_ref, ox_ref):
        @pl.when(pl.program_id(0) == 0)
        def _():
            oa_ref[...] = jnp.zeros_like(oa_ref)
            ox_ref[...] = jnp.zeros_like(ox_ref)

        for c0, cw, k0, k1 in chunks:
            cols = slice(c0, c0 + cw)
            rb = r_ref[:, k0:k1].astype(BF16)
            oa_ref[k0:k1, cols] += _dot_tn(rb, dpa_ref[:, cols])
            ox_ref[k0:k1, cols] += _dot_tn(rb, dpx_ref[:, cols])

    return _pallas(
        body, name=name, grid=(T // tt,),
        in_specs=[_rows(tt, C)] * 3,
        out_specs=[_acc((C, C)), _acc((C, C))],
        out_shape=[_sds((C, C), F32), _sds((C, C), F32)],
        args=(r, dpa, dpx))


def _conv4_bwd(dr, rx, w, name):
    T, C = dr.shape
    K = w.shape[0]
    B, P = CONV_BLOCK, RNN_PAD
    assert K <= SUBLANES
    d_groups = _tap_groups([K - 1 - k for k in range(K)])
    x_groups = _tap_groups([P - (K - 1) + k for k in range(K)])

    def fold(v):
        part = v[0:SUBLANES, :]
        for q in range(1, B // SUBLANES):
            part = part + v[q * SUBLANES:(q + 1) * SUBLANES, :]
        return part

    def body(d_ref, x_ref, w_ref, dx_ref, dw_ref, db_ref, dpad, xpad, dwacc, dbacc):
        dpad[0:T, :] = d_ref[...]
        dpad[T:T + P, :] = jnp.zeros((P, LANES), F32)
        xpad[0:P, :] = jnp.zeros((P, LANES), F32)
        xpad[P:P + T, :] = x_ref[...]
        dwacc[...] = jnp.zeros_like(dwacc)
        dbacc[...] = jnp.zeros_like(dbacc)
        wv = w_ref[...]

        def step(i, carry):
            base = pl.multiple_of(i * B, B)
            dwin = dpad[pl.ds(base, B + P), :]
            xwin = xpad[pl.ds(base, B + P), :]
            dcur = dwin[0:B, :]
            dx_ref[pl.ds(base, B), :] = _tap_sum(dwin, wv, d_groups, B).astype(BF16)
            for phase, taps in x_groups:
                sh = _shifted(xwin, phase)
                for k, off in taps:
                    dwacc[k * SUBLANES:(k + 1) * SUBLANES, :] += fold(dcur * sh[off:off + B, :])
            dbacc[...] += fold(dcur)
            return carry

        lax.fori_loop(0, T // B, step, 0)
        dw_ref[...] = jnp.zeros_like(dw_ref)
        for k in range(K):
            dw_ref[k:k + 1, :] = jnp.sum(dwacc[k * SUBLANES:(k + 1) * SUBLANES, :], axis=0, keepdims=True)
        db_ref[...] = jnp.sum(dbacc[...], axis=0, keepdims=True)

    return _pallas(
        body, name=name, grid=(C // LANES,),
        in_specs=[_cols(T), _cols(T), _cols(T, K)],
        out_specs=[_cols(T), _cols(T, SUBLANES), _cols(T, 1)],
        out_shape=[_sds((T, C), BF16), _sds((SUBLANES, C), F32), _sds((1, C), F32)],
        scratch_shapes=[pltpu.VMEM((T + P, LANES), F32), pltpu.VMEM((T + P, LANES), F32),
                        pltpu.VMEM((SUBLANES * SUBLANES, LANES), F32), pltpu.VMEM((SUBLANES, LANES), F32)],
        args=(dr, rx, w))


def _conv31_bwd(dcs, cc, c, cv, cg, w, gg, gb, name, comm=None):
    T, D = dcs.shape
    K = w.shape[0]
    R, B, P = _tile(T, NORM_ROWS), CONV_BLOCK, CONV_PAD
    d_groups = _tap_groups([K - 1 - k for k in range(K)])
    x_groups = _tap_groups([P - (K - 1) + k for k in range(K)])

    def body(dcs_ref, cc_ref, c_ref, cv_ref, cg_ref, w_ref, gg_ref, gb_ref,
             dcv_ref, dcg_ref, dw_ref, db_ref, dgg_ref, dgb_ref, dpad, xpad, dwacc):
        dpad[T:T + P, :] = jnp.zeros((P, LANES), F32)
        xpad[0:P, :] = jnp.zeros((P, LANES), F32)
        xpad[P:P + T, :] = c_ref[...]
        dwacc[...] = jnp.zeros_like(dwacc)
        db_ref[...] = jnp.zeros_like(db_ref)
        dgg_ref[...] = jnp.zeros_like(dgg_ref)
        dgb_ref[...] = jnp.zeros_like(dgb_ref)
        wv, ggv, gbv = w_ref[...], gg_ref[...], gb_ref[...]

        def norm_step(i, carry):
            base = pl.multiple_of(i * R, R)
            xhat, rstd = _gn_stats(cc_ref[pl.ds(base, R), :])
            gn = xhat * ggv + gbv
            sg = _sigmoid(gn)
            dgn = dcs_ref[pl.ds(base, R), :] * (sg * (1.0 + gn * (1.0 - sg)))
            dgg_ref[...] += jnp.sum(dgn * xhat, axis=0, keepdims=True)
            dgb_ref[...] += jnp.sum(dgn, axis=0, keepdims=True)
            dxh = dgn * ggv
            m1 = jnp.mean(dxh, axis=-1, keepdims=True)
            m2 = jnp.mean(dxh * xhat, axis=-1, keepdims=True)
            dcc = rstd * (dxh - m1 - xhat * m2)
            dpad[pl.ds(base, R), :] = dcc
            db_ref[...] += jnp.sum(dcc, axis=0, keepdims=True)
            return carry

        lax.fori_loop(0, T // R, norm_step, 0)

        def conv_step(i, carry):
            base = pl.multiple_of(i * B, B)
            dwin = dpad[pl.ds(base, B + P), :]
            xwin = xpad[pl.ds(base, B + P), :]
            dcur = dwin[0:B, :]
            acc = _tap_sum(dwin, wv, d_groups, B)
            for phase, taps in x_groups:
                sh = _shifted(xwin, phase)
                for k, off in taps:
                    prod = dcur * sh[off:off + B, :]
                    part = prod[0:SUBLANES, :]
                    for q in range(1, B // SUBLANES):
                        part = part + prod[q * SUBLANES:(q + 1) * SUBLANES, :]
                    dwacc[k * SUBLANES:(k + 1) * SUBLANES, :] += part
            cgv = cg_ref[pl.ds(base, B), :].astype(F32)
            cvv = cv_ref[pl.ds(base, B), :].astype(F32)
            sg = _sigmoid(cgv)
            dcv_ref[pl.ds(base, B), :] = (acc * sg).astype(BF16)
            dcg_ref[pl.ds(base, B), :] = (acc * cvv * sg * (1.0 - sg)).astype(BF16)
            return carry

        lax.fori_loop(0, T // B, conv_step, 0)
        dw_ref[...] = jnp.zeros_like(dw_ref)
        for k in range(K):
            dw_ref[k:k + 1, :] = jnp.sum(dwacc[k * SUBLANES:(k + 1) * SUBLANES, :], axis=0, keepdims=True)

    return _pallas(
        body, name=name, grid=(D // LANES,),
        in_specs=[_cols(T)] * 5 + [_cols(T, K), _cols(T, 1), _cols(T, 1)],
        out_specs=[_cols(T), _cols(T), _cols(T, P), _cols(T, 1), _cols(T, 1), _cols(T, 1)],
        out_shape=[_sds((T, D), BF16), _sds((T, D), BF16), _sds((P, D), F32), _sds((1, D), F32),
                   _sds((1, D), F32), _sds((1, D), F32)],
        scratch_shapes=[pltpu.VMEM((T + P, LANES), F32), pltpu.VMEM((T + P, LANES), F32),
                        pltpu.VMEM((P * SUBLANES, LANES), F32)],
        args=(dcs, cc, c, cv, cg, w, gg, gb), comm=comm)


def _reduce_adamw(recvs, w, m, v, name):
    L, R, C = w.shape
    assert len(recvs) == L
    tr = next((R // d for d in range(1, R // SUBLANES + 1)
               if R % d == 0 and (R // d) % SUBLANES == 0 and R // d <= ADAMW_ROWS), R)
    nr = R // tr
    c1 = 1.0 - ADAM_B1 ** ADAM_STEP
    c2 = 1.0 - ADAM_B2 ** ADAM_STEP

    def body(*refs):
        recv_refs = refs[:L]
        w_ref, m_ref, v_ref, g_ref, d_ref, mo_ref, vo_ref = refs[L:]

        def update(recv_ref):
            g = recv_ref[0].astype(F32)
            for k in range(1, N_DEV):
                g = g + recv_ref[k].astype(F32)
            mn = ADAM_B1 * m_ref[0] + (1.0 - ADAM_B1) * g
            vn = ADAM_B2 * v_ref[0] + (1.0 - ADAM_B2) * (g * g)
            g_ref[0] = g
            mo_ref[0] = mn
            vo_ref[0] = vn
            d_ref[0] = (-ADAM_LR) * ((mn / c1) / (jnp.sqrt(vn / c2) + ADAM_EPS) + ADAM_WD * w_ref[0])

        for l in range(L):
            pl.when(pl.program_id(0) == l)(lambda l=l: update(recv_refs[l]))

    def recv_spec(l):
        return pl.BlockSpec((N_DEV, tr, C), lambda j, i: (0, jnp.where(j == l, i, jnp.where(j < l, 0, nr - 1)), 0))

    blk = pl.BlockSpec((1, tr, C), lambda j, i: (j, i, 0))
    return _pallas(
        body, name=name, grid=(L, nr),
        in_specs=[recv_spec(l) for l in range(L)] + [blk, blk, blk],
        out_specs=[blk] * 4,
        out_shape=[_sds((L, R, C), F32)] * 4,
        args=(*recvs, w, m, v))


def _unshard(name, gathered):
    n, r, c = gathered.shape
    if name in COL_SHARDED and name not in TRANSPOSED:
        return gathered.transpose(1, 0, 2).reshape(r, n * c)
    return gathered.reshape(n * r, c)


def _to_shards(name, full):
    R, C = full.shape
    wire = BF16 if name in BF16_ON_WIRE else F32
    if name in TRANSPOSED:
        return full.reshape(R, N_DEV, C // N_DEV).transpose(1, 2, 0).astype(wire)
    if name in COL_SHARDED:
        return full.reshape(R, N_DEV, C // N_DEV).transpose(1, 0, 2).astype(wire)
    return full.reshape(N_DEV, R // N_DEV, C).astype(wire)


def _block_diag(w):
    H, b, _ = w.shape
    eye = jnp.eye(H, dtype=w.dtype)
    return (w[:, :, None, :] * eye[:, None, :, None]).reshape(H * b, H * b)


def _diag_blocks(dense, H):
    b = dense.shape[0] // H
    eye = jnp.eye(H, dtype=dense.dtype)
    return jnp.sum(dense.reshape(H, b, H, b) * eye[:, None, :, None], axis=2)


def _pack(arrs, rows):
    flat = jnp.concatenate([a.reshape(-1) for a in arrs])
    return jnp.pad(flat, (0, rows * 1024 - flat.shape[0])).reshape(1, rows, 1024)


def _unpack(packed, shapes):
    flat = packed.reshape(-1)
    out, off = [], 0
    for s in shapes:
        n = math.prod(s)
        out.append(flat[off:off + n].reshape(s))
        off += n
    return out


class _Queue:
    def __init__(self, kind, us_per_mb):
        self.kind, self.us_per_mb, self.items, self.done = kind, us_per_mb, [], {}

    def push(self, key, arr):
        self.items.append((key, arr))

    def mb(self, arr):
        return _mbytes(arr) / (N_DEV if self.kind == "scatter" else 1)

    def take(self, micros):
        taken, budget = [], micros / self.us_per_mb
        while self.items and (not taken or self.mb(self.items[0][1]) <= budget):
            budget -= self.mb(self.items[0][1])
            taken.append(self.items.pop(0))
        return {"kind": self.kind, "keys": [k for k, _ in taken], "arrs": [a for _, a in taken]} if taken else None

    def landed(self, comm):
        if comm:
            self.done.update(zip(comm["keys"], comm["recv"]))

    def flush(self, name, upto=None):
        n = len(self.items)
        if upto is not None:
            keys = [k for k, _ in self.items]
            n = keys.index(upto) + 1 if upto in keys else 0
        if n:
            taken, self.items = self.items[:n], self.items[n:]
            self.done.update(zip([k for k, _ in taken], _exchange(self.kind, [a for _, a in taken], name)))


def kernel(x, ffn1_w_gu, ffn1_w_down, ln1_g, ln1_b, mix_w_in, mix_b_in, conv_dw_w, conv_dw_b, conv_gn_g, conv_gn_b, conv_w_proj, rnn_conv_w, rnn_conv_b, rnn_w_a, rnn_b_a, rnn_w_x, rnn_b_x, rnn_lambda, rnn_w_proj, mix_w_out, ln2_g, ln2_b, ffn2_w_gu, ffn2_w_down, ln3_g, ln3_b, loss_target, m_ffn1_w_gu, m_ffn1_w_down, m_ln1_g, m_ln1_b, m_mix_w_in, m_mix_b_in, m_conv_dw_w, m_conv_dw_b, m_conv_gn_g, m_conv_gn_b, m_conv_w_proj, m_rnn_conv_w, m_rnn_conv_b, m_rnn_w_a, m_rnn_b_a, m_rnn_w_x, m_rnn_b_x, m_rnn_lambda, m_rnn_w_proj, m_mix_w_out, m_ln2_g, m_ln2_b, m_ffn2_w_gu, m_ffn2_w_down, m_ln3_g, m_ln3_b, v_ffn1_w_gu, v_ffn1_w_down, v_ln1_g, v_ln1_b, v_mix_w_in, v_mix_b_in, v_conv_dw_w, v_conv_dw_b, v_conv_gn_g, v_conv_gn_b, v_conv_w_proj, v_rnn_conv_w, v_rnn_conv_b, v_rnn_w_a, v_rnn_b_a, v_rnn_w_x, v_rnn_b_x, v_rnn_lambda, v_rnn_w_proj, v_mix_w_out, v_ln2_g, v_ln2_b, v_ffn2_w_gu, v_ffn2_w_down, v_ln3_g, v_ln3_b):
    given = dict(locals())
    W = {n: given[n] for n in WEIGHTS}
    M = {n: given["m_" + n] for n in WEIGHTS}
    V = {n: given["v_" + n] for n in WEIGHTS}
    T, D = x.shape[1], x.shape[2]
    L = DEPTH
    x2 = x.reshape(T, D)
    target = loss_target.reshape(T, D)
    d_rnn = rnn_conv_b.shape[1]

    gather = _Queue("gather", GATHER_US_PER_MB)
    for l in range(L):
        for n in USE_ORDER:
            shard = W[n][l].T if n in TRANSPOSED else W[n][l]
            gather.push((n, l), shard.astype(BF16) if n in BF16_ON_WIRE else shard)
    gather.flush("gather_first", upto=("ffn1_w_gu", 0))
    full_cache = {}

    def full(n, l):
        if (n, l) not in full_cache:
            gather.flush(f"gather_{n}_{l}", upto=(n, l))
            full_cache[(n, l)] = _unshard(n, gather.done[(n, l)])
        return full_cache[(n, l)]

    def fwd_comm(micros):
        return gather.take(micros)

    bd_a = [_block_diag(rnn_w_a[l]).astype(BF16) for l in range(L)]
    bd_x = [_block_diag(rnn_w_x[l]).astype(BF16) for l in range(L)]

    def vec(name, l):
        return W[name][l:l + 1]

    saved = []
    h, hb = x2, x2
    for l in range(L):
        s = {"hb_in": hb}
        w_gu = full("ffn1_w_gu", l)
        comm = fwd_comm(105)
        s["gu1"], s["a1"] = _ffn_up(hb, w_gu, f"ffn1_up_{l}", comm=comm)
        gather.landed(comm)
        w_down = full("ffn1_w_down", l)
        comm = fwd_comm(65)
        s["r1"], y1, s["y1b"] = _ffn_down_ln(s["a1"], w_down, h, vec("ln1_g", l), vec("ln1_b", l),
                                              f"ffn1_down_ln_{l}", comm=comm)
        gather.landed(comm)
        w_in = full("mix_w_in", l)
        comm = fwd_comm(5)
        s["c"], s["cv"], s["cg"], s["rx"], s["rg"], s["gc"], s["gr"] = _mix_in(
            s["y1b"], w_in, vec("mix_b_in", l), d_rnn, f"mix_in_{l}", comm=comm)
        gather.landed(comm)
        w_dw = full("conv_dw_w", l)
        comm = fwd_comm(195)
        s["cc"], s["cs"] = _conv31_gn(s["c"], w_dw, vec("conv_dw_b", l), vec("conv_gn_g", l),
                                      vec("conv_gn_b", l), f"conv31_gn_{l}", comm=comm)
        gather.landed(comm)
        s["r"] = _conv4(s["rx"], full("rnn_conv_w", l), vec("rnn_conv_b", l), f"conv4_{l}")
        comm = fwd_comm(35)
        s["ra"], s["ri"], s["a"], uu = _gates(s["r"], bd_a[l], bd_x[l], vec("rnn_b_a", l), vec("rnn_b_x", l),
                                              vec("rnn_lambda", l), f"gates_{l}", comm=comm)
        gather.landed(comm)
        comm = fwd_comm(80)
        s["h"], s["hp"], s["hg"] = _scan_fwd(s["a"], uu, s["rg"], f"scan_{l}", comm=comm)
        gather.landed(comm)
        w_cp, w_rp, w_out = full("conv_w_proj", l), full("rnn_w_proj", l), full("mix_w_out", l)
        comm = fwd_comm(85)
        s["yc"], s["yr"], s["m"], s["r2"], y2, s["y2b"] = _mix_out_ln(
            s["cs"], s["hg"], s["gc"], s["gr"], y1, w_cp, w_rp, w_out, vec("ln2_g", l), vec("ln2_b", l),
            f"mix_out_ln_{l}", comm=comm)
        gather.landed(comm)
        w_gu2 = full("ffn2_w_gu", l)
        comm = fwd_comm(105)
        s["gu2"], s["a2"] = _ffn_up(s["y2b"], w_gu2, f"ffn2_up_{l}", comm=comm)
        gather.landed(comm)
        w_down2 = full("ffn2_w_down", l)
        comm = fwd_comm(65)
        s["r3"], h, hb = _ffn_down_ln(s["a2"], w_down2, y2, vec("ln3_g", l), vec("ln3_b", l),
                                      f"ffn2_down_ln_{l}", comm=comm)
        gather.landed(comm)
        saved.append(s)

    scatter = _Queue("scatter", SCATTER_US_PER_MB)
    G = {n: [None] * L for n in WEIGHTS}

    def ready(n, l, grad):
        G[n][l] = grad
        scatter.push((n, l), _to_shards(n, grad))

    def bwd_comm(micros):
        return scatter.take(micros)

    loss_acc, dr3, drb3, G["ln3_g"][L - 1], G["ln3_b"][L - 1] = _loss_ln_bwd(
        h, target, saved[L - 1]["r3"], vec("ln3_g", L - 1), "loss_ln3_bwd")
    grad_x = small = None
    small_shapes = [W[n].shape for n in REPLICATED]
    small_rows = -(-sum(math.prod(s) for s in small_shapes) // (1024 * SUBLANES)) * SUBLANES
    for l in reversed(range(L)):
        s = saved[l]
        comm = bwd_comm(95)
        dgu2 = _ffn_bwd_a(drb3, full("ffn2_w_down", l), s["gu2"], f"ffn2_bwd_a_{l}", comm=comm)
        scatter.landed(comm)
        ready("ffn2_w_down", l, _mm_tn(s["a2"], drb3, f"ffn2_dw_down_{l}", scale=0.5))
        ready("ffn2_w_gu", l, _mm_tn(s["y2b"], dgu2, f"ffn2_dw_gu_{l}"))
        comm = bwd_comm(140)
        dr2, drb2, G["ln2_g"][l], G["ln2_b"][l] = _nt_res(
            [dgu2], full("ffn2_w_gu", l), dr3, f"ffn2_bwd_x_{l}", ln=(s["r2"], vec("ln2_g", l)), comm=comm)
        scatter.landed(comm)
        comm = bwd_comm(120)
        dyc, dyr, du_tail, dcs, dh = _mix_bwd1(
            drb2, full("mix_w_out", l), full("conv_w_proj", l), full("rnn_w_proj", l), s["gc"], s["gr"], s["yc"],
            s["yr"], s["rg"], s["h"], f"mix_bwd_out_{l}", comm=comm)
        scatter.landed(comm)
        ready("mix_w_out", l, _mm_tn(s["m"], drb2, f"mix_dw_out_{l}"))
        ready("conv_w_proj", l, _mm_tn(s["cs"], dyc, f"conv_dw_proj_{l}"))
        ready("rnn_w_proj", l, _mm_tn(s["hg"], dyr, f"rnn_dw_proj_{l}"))
        gsc = _scan_bwd(dh, s["a"], f"scan_bwd_{l}")
        comm = bwd_comm(160)
        dr_, dpa, dpx, G["rnn_lambda"][l], G["rnn_b_a"][l], G["rnn_b_x"][l] = _gates_bwd(
            gsc, s["hp"], s["ra"], s["ri"], s["r"], vec("rnn_lambda", l), bd_a[l], bd_x[l], f"gates_bwd_{l}",
            comm=comm)
        scatter.landed(comm)
        dwa, dwx = _band_dw(s["r"], dpa, dpx, f"rnn_dw_ax_{l}")
        G["rnn_w_a"][l] = _diag_blocks(dwa, RNN_BLOCKS)
        G["rnn_w_x"][l] = _diag_blocks(dwx, RNN_BLOCKS)
        drx, dw4, G["rnn_conv_b"][l] = _conv4_bwd(dr_, s["rx"], full("rnn_conv_w", l), f"conv4_bwd_{l}")
        ready("rnn_conv_w", l, dw4[:RNN_CONV_WIDTH])
        comm = bwd_comm(250)
        dcv, dcg, dw31, G["conv_dw_b"][l], G["conv_gn_g"][l], G["conv_gn_b"][l] = _conv31_bwd(
            dcs, s["cc"], s["c"], s["cv"], s["cg"], full("conv_dw_w", l), vec("conv_gn_g", l),
            vec("conv_gn_b", l), f"conv31_bwd_{l}", comm=comm)
        scatter.landed(comm)
        ready("conv_dw_w", l, dw31[:CONV_WIDTH])
        du = [dcv, dcg, drx, du_tail]
        ready("mix_w_in", l, jnp.concatenate(
            [_mm_tn(s["y1b"], piece, f"mix_dw_in_{l}_{p}") for p, piece in enumerate(du)], axis=1))
        comm = bwd_comm(175)
        dr1, drb1, G["ln1_g"][l], G["ln1_b"][l], G["mix_b_in"][l] = _nt_res(
            du, full("mix_w_in", l), dr2, f"mix_bwd_in_{l}", ln=(s["r1"], vec("ln1_g", l)), colsum=True, comm=comm)
        scatter.landed(comm)
        ready("ffn1_w_down", l, _mm_tn(s["a1"], drb1, f"ffn1_dw_down_{l}", scale=0.5))
        comm = bwd_comm(95)
        dgu1 = _ffn_bwd_a(drb1, full("ffn1_w_down", l), s["gu1"], f"ffn1_bwd_a_{l}", comm=comm)
        scatter.landed(comm)
        if l == 0:
            local_small = [jnp.stack([g.reshape(W[n].shape[1:]) for g in G[n]]) for n in REPLICATED]
            comm = {"kind": "gather", "arrs": [_pack(local_small, small_rows)[0]]}
            ready("ffn1_w_gu", l, _mm_tn(s["hb_in"], dgu1, f"ffn1_dw_gu_{l}", comm=comm))
            (small,) = comm["recv"]
        else:
            ready("ffn1_w_gu", l, _mm_tn(s["hb_in"], dgu1, f"ffn1_dw_gu_{l}"))
        if l > 0:
            comm = bwd_comm(130)
            dr3, drb3, G["ln3_g"][l - 1], G["ln3_b"][l - 1] = _nt_res(
                [dgu1], full("ffn1_w_gu", l), dr1, f"ffn1_bwd_x_{l}", ln=(saved[l - 1]["r3"], vec("ln3_g", l - 1)),
                comm=comm)
        else:
            comm = bwd_comm(1e9)
            (grad_x,) = _nt_res([dgu1], full("ffn1_w_gu", l), dr1, f"ffn1_bwd_x_{l}", comm=comm)
        scatter.landed(comm)
    scatter.flush("scatter_rest")

    loss = lax.psum(loss_acc[0, 0], ("x", "y", "c"))

    out = {}
    for n in SHARDED:
        recvs = [scatter.done[(n, l)] for l in range(L)]
        if n in TRANSPOSED:
            res = _reduce_adamw(recvs, *[jnp.swapaxes(a, 1, 2) for a in (W[n], M[n], V[n])], f"adamw_{n}")
            out[n] = [jnp.swapaxes(a, 1, 2) for a in res]
        else:
            out[n] = _reduce_adamw(recvs, W[n], M[n], V[n], f"adamw_{n}")
    packed = _reduce_adamw([small], _pack([W[n] for n in REPLICATED], small_rows),
                           _pack([M[n] for n in REPLICATED], small_rows),
                           _pack([V[n] for n in REPLICATED], small_rows), "adamw_small")
    unpacked = [_unpack(p, small_shapes) for p in packed]
    for i, n in enumerate(REPLICATED):
        out[n] = tuple(u[i] for u in unpacked)

    return (loss, grad_x.reshape(x.shape), *[out[n][0] for n in WEIGHTS], *[out[n][1] for n in WEIGHTS],
            *[out[n][2] for n in WEIGHTS], *[out[n][3] for n in WEIGHTS])
```

```python
import math

import jax
import jax.numpy as jnp
from jax import lax
from jax.experimental import pallas as pl
from jax.experimental.pallas import tpu as pltpu

F32 = jnp.float32
BF16 = jnp.bfloat16
MESH = pl.DeviceIdType.MESH

DEPTH = 2
ALPHA = (2 * DEPTH) ** 0.25
LN_EPS = 1e-5
RG_LRU_C = 8.0
CONV_WIDTH = 31
RNN_CONV_WIDTH = 4
RNN_BLOCKS = 16
N_DEV = 8
ADAM_LR, ADAM_B1, ADAM_B2, ADAM_EPS, ADAM_WD, ADAM_STEP = 0.001, 0.9, 0.999, 1e-08, 0.01, 10

LANES = 128
SUBLANES = 8
VMEM_LIMIT = 56 * 1024 * 1024
CONV_PAD = 32
RNN_PAD = 8
ADAMW_ROWS = 352
MM_TN_OUT_BYTES = 14 * 1024 * 1024
NORM_ROWS = 256
CONV_BLOCK = 32
GATHER_US_PER_MB = 43.0
SCATTER_US_PER_MB = 86.0

WEIGHTS = ['ffn1_w_gu', 'ffn1_w_down', 'ln1_g', 'ln1_b', 'mix_w_in', 'mix_b_in', 'conv_dw_w', 'conv_dw_b',
           'conv_gn_g', 'conv_gn_b', 'conv_w_proj', 'rnn_conv_w', 'rnn_conv_b', 'rnn_w_a', 'rnn_b_a', 'rnn_w_x',
           'rnn_b_x', 'rnn_lambda', 'rnn_w_proj', 'mix_w_out', 'ln2_g', 'ln2_b', 'ffn2_w_gu', 'ffn2_w_down',
           'ln3_g', 'ln3_b']
COL_SHARDED = ['ffn1_w_gu', 'mix_w_in', 'ffn2_w_gu', 'conv_dw_w', 'rnn_conv_w']
TRANSPOSED = ['ffn1_w_gu', 'mix_w_in', 'ffn2_w_gu']
ROW_SHARDED = ['ffn1_w_down', 'conv_w_proj', 'rnn_w_proj', 'mix_w_out', 'ffn2_w_down']
SHARDED = COL_SHARDED + ROW_SHARDED
BF16_ON_WIRE = ['ffn1_w_gu', 'mix_w_in', 'ffn2_w_gu', 'ffn1_w_down', 'conv_w_proj', 'rnn_w_proj', 'mix_w_out',
                'ffn2_w_down']
REPLICATED = [n for n in WEIGHTS if n not in SHARDED]
USE_ORDER = ['ffn1_w_gu', 'ffn1_w_down', 'mix_w_in', 'conv_dw_w', 'rnn_conv_w', 'conv_w_proj', 'rnn_w_proj',
             'mix_w_out', 'ffn2_w_gu', 'ffn2_w_down']


def _cp(n_axes=1):
    return pltpu.CompilerParams(dimension_semantics=("arbitrary",) * n_axes, vmem_limit_bytes=VMEM_LIMIT)


def _rows(tm, c):
    return pl.BlockSpec((tm, c), lambda i: (i, 0))


def _res(shape):
    nd = len(shape)
    return pl.BlockSpec(tuple(shape), lambda *_: (0,) * nd, pipeline_mode=pl.Buffered(1))


def _acc(shape):
    nd = len(shape)
    return pl.BlockSpec(tuple(shape), lambda *_: (0,) * nd)


def _tile(t, want):
    return want if t % want == 0 else t


def _sds(shape, dtype):
    return jax.ShapeDtypeStruct(tuple(shape), dtype)


def _mbytes(a):
    return a.size * a.dtype.itemsize / 1e6


def _ln_fwd(r, g, b):
    mu = jnp.mean(r, axis=-1, keepdims=True)
    xc = r - mu
    var = jnp.mean(xc * xc, axis=-1, keepdims=True)
    return xc * lax.rsqrt(var + LN_EPS) * g + b


def _ln_bwd(dy, r, g):
    mu = jnp.mean(r, axis=-1, keepdims=True)
    xc = r - mu
    var = jnp.mean(xc * xc, axis=-1, keepdims=True)
    rstd = lax.rsqrt(var + LN_EPS)
    xhat = xc * rstd
    dxh = dy * g
    m1 = jnp.mean(dxh, axis=-1, keepdims=True)
    m2 = jnp.mean(dxh * xhat, axis=-1, keepdims=True)
    dr = rstd * (dxh - m1 - xhat * m2)
    return dr, jnp.sum(dy * xhat, axis=0, keepdims=True), jnp.sum(dy, axis=0, keepdims=True)


def _sigmoid(x):
    return jax.nn.sigmoid(x)


_GELU_K = math.sqrt(2.0 / math.pi)


def _gelu(x):
    t = jnp.tanh(_GELU_K * (x + 0.044715 * x * x * x))
    return 0.5 * x * (1.0 + t), t


def _gelu_grad(x, t):
    return 0.5 * (1.0 + t) + 0.5 * x * (1.0 - t * t) * (_GELU_K * (1.0 + 3.0 * 0.044715 * x * x))


def _expm1(x):
    taylor = x * (1.0 + x * (0.5 + x * (1.0 / 6.0 + x * (1.0 / 24.0 + x * (1.0 / 120.0)))))
    return jnp.where(jnp.abs(x) < 0.03, taylor, jnp.exp(x) - 1.0)


def _softplus_neg(lam):
    return jnp.maximum(-lam, 0.0) + jnp.log1p(jnp.exp(-jnp.abs(lam)))


def _dot(a, b):
    return jnp.dot(a, b, preferred_element_type=F32)


def _dot_nt(a, b):
    return lax.dot_general(a, b, (((1,), (1,)), ((), ())), preferred_element_type=F32)


def _dot_tn(a, b):
    return lax.dot_general(a, b, (((0,), (0,)), ((), ())), preferred_element_type=F32)


def _chunks(width, cn):
    return [(j, min(cn, width - j)) for j in range(0, width, cn)]


def _band_chunks(width, block):
    out = []
    for c0, cw in _chunks(width, 256):
        lo = (c0 // block) * block
        hi = ((c0 + cw - 1) // block + 1) * block
        out.append((c0, cw, lo // LANES * LANES, min(width, -(-hi // LANES) * LANES)))
    return out


def _position():
    return lax.axis_index("x"), lax.axis_index("y"), lax.axis_index("c")


def _index(p):
    return 4 * p[0] + 2 * p[1] + p[2]


def _comm_out_shapes(kind, arrs):
    return [_sds((N_DEV,) + a.shape if kind == "gather" else a.shape, a.dtype) for a in arrs]


def _comm_scratch(n):
    return [pltpu.SemaphoreType.DMA((n, 7)), pltpu.SemaphoreType.DMA((n, 7)), pltpu.SemaphoreType.DMA((n,))]


def _comm_phases(kind, srcs, dsts, send_sems, recv_sems, local_sems):
    n = len(srcs)
    x, y, c = _position()
    me, sibling = (x, y, c), (x, y, 1 - c)
    chips = [(1 - x, y), (x, 1 - y), (1 - x, 1 - y)]

    if kind == "gather":
        def copy(a, k, block, to, src=None):
            dst = dsts[a].at[_index(block)]
            return pltpu.make_async_remote_copy(
                src_ref=dst if src is None else src, dst_ref=dst, send_sem=send_sems.at[a, k],
                recv_sem=recv_sems.at[a, k], device_id=to, device_id_type=MESH)

        def mine(a):
            return pltpu.make_async_copy(srcs[a], dsts[a].at[_index(me)], local_sems.at[a])

        def first(a):
            return [copy(a, 0, me, sibling, src=srcs[a])] + [
                copy(a, 1 + j, me, (*chip, c), src=srcs[a]) for j, chip in enumerate(chips)]

        def start():
            for a in range(n):
                mine(a).start()
            for a in range(n):
                for cp in first(a):
                    cp.start()

        def mid():
            for j, chip in enumerate(chips):
                for a in range(n):
                    copy(a, 1 + j, (*chip, c), me).wait_recv()
                    copy(a, 4 + j, (*chip, c), sibling).start()

        def end():
            for a in range(n):
                copy(a, 0, sibling, me).wait_recv()
            for j, chip in enumerate(chips):
                for a in range(n):
                    copy(a, 4 + j, (*chip, 1 - c), me).wait_recv()
            for a in range(n):
                for cp in first(a):
                    cp.wait_send()
                for j, chip in enumerate(chips):
                    copy(a, 4 + j, (*chip, c), sibling).wait_send()
                mine(a).wait()

        return start, mid, end

    def peer_of(k):
        return (1 - x if k & 4 else x, 1 - y if k & 2 else y, 1 - c if k & 1 else c)

    def own(a):
        return pltpu.make_async_copy(srcs[a].at[_index(me)], dsts[a].at[0], local_sems.at[a])

    def remote(a, k):
        peer = peer_of(k)
        return pltpu.make_async_remote_copy(
            src_ref=srcs[a].at[_index(peer)], dst_ref=dsts[a].at[k], send_sem=send_sems.at[a, k - 1],
            recv_sem=recv_sems.at[a, k - 1], device_id=peer, device_id_type=MESH)

    def start():
        for a in range(n):
            own(a).start()
        for k in range(1, N_DEV):
            for a in range(n):
                remote(a, k).start()

    def end():
        for k in range(1, N_DEV):
            for a in range(n):
                remote(a, k).wait()
        for a in range(n):
            own(a).wait()

    return start, (lambda: None), end


def _exchange(kind, arrs, name):
    n = len(arrs)
    hbm = pl.BlockSpec(memory_space=pl.ANY)

    def body(*refs):
        start, mid, end = _comm_phases(kind, refs[:n], refs[n:2 * n], *refs[2 * n:])
        start()
        mid()
        end()

    return pl.pallas_call(
        body, name=name, in_specs=[hbm] * n, out_specs=[hbm] * n, out_shape=_comm_out_shapes(kind, arrs),
        scratch_shapes=_comm_scratch(n))(*arrs)


def _pallas(body, *, name, grid, in_specs, out_specs, out_shape, args, scratch_shapes=(), comm=None):
    in_specs, out_specs, out_shape = list(in_specs), list(out_specs), list(out_shape)
    scratch_shapes = list(scratch_shapes)
    if not comm:
        return pl.pallas_call(body, name=name, grid=grid, in_specs=in_specs, out_specs=out_specs,
                              out_shape=out_shape, scratch_shapes=scratch_shapes,
                              compiler_params=_cp(len(grid)))(*args)
    arrs = comm["arrs"]
    ns, n_in, n_out, n_scr = len(arrs), len(in_specs), len(out_specs), len(scratch_shapes)
    hbm = pl.BlockSpec(memory_space=pl.ANY)
    total = math.prod(grid)

    def carrier(*refs):
        ins, srcs = refs[:n_in], refs[n_in:n_in + ns]
        outs, dsts = refs[n_in + ns:n_in + ns + n_out], refs[n_in + ns + n_out:n_in + 2 * ns + n_out]
        scr, sems = refs[n_in + 2 * ns + n_out:n_in + 2 * ns + n_out + n_scr], refs[n_in + 2 * ns + n_out + n_scr:]
        step = pl.program_id(0)
        for ax in range(1, len(grid)):
            step = step * grid[ax] + pl.program_id(ax)
        start, mid, end = _comm_phases(comm["kind"], srcs, dsts, *sems)
        pl.when(step == 0)(start)
        body(*ins, *outs, *scr)
        pl.when(step == total - 1)(mid)
        pl.when(step == total - 1)(end)

    res = pl.pallas_call(
        carrier, name=name, grid=grid, in_specs=in_specs + [hbm] * ns, out_specs=out_specs + [hbm] * ns,
        out_shape=out_shape + _comm_out_shapes(comm["kind"], arrs),
        scratch_shapes=scratch_shapes + _comm_scratch(ns), compiler_params=_cp(len(grid)))(*args, *arrs)
    comm["recv"] = res[n_out:]
    return res[:n_out]


def _ffn_up(xb, w, name, comm=None):
    T, D = xb.shape
    F = w.shape[0] // 2
    tm = _tile(T, 512)

    def body(x_ref, w_ref, gu_ref, a_ref):
        x = x_ref[...].astype(BF16)
        for j, cw in _chunks(F, 256):
            g = _dot_nt(x, w_ref[j:j + cw, :])
            u = _dot_nt(x, w_ref[F + j:F + j + cw, :])
            gu_ref[:, j:j + cw] = g.astype(BF16)
            gu_ref[:, F + j:F + j + cw] = u.astype(BF16)
            a_ref[:, j:j + cw] = (g * _sigmoid(g) * u).astype(BF16)

    return _pallas(
        body, name=name, grid=(T // tm,),
        in_specs=[_rows(tm, D), _res(w.shape)],
        out_specs=[_rows(tm, 2 * F), _rows(tm, F)],
        out_shape=[_sds((T, 2 * F), BF16), _sds((T, F), BF16)],
        args=(xb, w), comm=comm)


def _ffn_down_ln(a, wd, xres, g, b, name, comm=None):
    T, F = a.shape
    D = wd.shape[1]
    tm = _tile(T, 512)

    def body(a_ref, wd_ref, x_ref, g_ref, b_ref, r_ref, y_ref, yb_ref):
        r = ALPHA * x_ref[...] + 0.5 * _dot(a_ref[...], wd_ref[...])
        y = _ln_fwd(r, g_ref[...], b_ref[...])
        r_ref[...] = r
        y_ref[...] = y
        yb_ref[...] = y.astype(BF16)

    return _pallas(
        body, name=name, grid=(T // tm,),
        in_specs=[_rows(tm, F), _res(wd.shape), _rows(tm, D), _res((1, D)), _res((1, D))],
        out_specs=[_rows(tm, D), _rows(tm, D), _rows(tm, D)],
        out_shape=[_sds((T, D), F32), _sds((T, D), F32), _sds((T, D), BF16)],
        args=(a, wd, xres, g, b), comm=comm)


def _mix_in(hb, w, bias, d_rnn, name, comm=None):
    T, D = hb.shape
    R = d_rnn
    tm = _tile(T, 512)
    o_cv, o_cg, o_rx, o_rg, o_gc, o_gr = 0, D, 2 * D, 2 * D + R, 2 * D + 2 * R, 3 * D + 2 * R

    def body(x_ref, w_ref, b_ref, c_ref, cv_ref, cg_ref, rx_ref, rg_ref, gc_ref, gr_ref):
        x = x_ref[...]

        def seg(off, j, cw):
            return _dot_nt(x, w_ref[off + j:off + j + cw, :]) + b_ref[:, off + j:off + j + cw]

        for j, cw in _chunks(D, 256):
            cv = seg(o_cv, j, cw)
            cg = seg(o_cg, j, cw)
            cv_ref[:, j:j + cw] = cv.astype(BF16)
            cg_ref[:, j:j + cw] = cg.astype(BF16)
            c_ref[:, j:j + cw] = cv * _sigmoid(cg)
            gc_ref[:, j:j + cw] = seg(o_gc, j, cw).astype(BF16)
            gr_ref[:, j:j + cw] = seg(o_gr, j, cw).astype(BF16)
        for j, cw in _chunks(R, 256):
            rx_ref[:, j:j + cw] = seg(o_rx, j, cw)
            rg_ref[:, j:j + cw] = seg(o_rg, j, cw).astype(BF16)

    return _pallas(
        body, name=name, grid=(T // tm,),
        in_specs=[_rows(tm, D), _res(w.shape), _res(bias.shape)],
        out_specs=[_rows(tm, D), _rows(tm, D), _rows(tm, D), _rows(tm, R), _rows(tm, R), _rows(tm, D),
                   _rows(tm, D)],
        out_shape=[_sds((T, D), F32), _sds((T, D), BF16), _sds((T, D), BF16), _sds((T, R), F32),
                   _sds((T, R), BF16), _sds((T, D), BF16), _sds((T, D), BF16)],
        args=(hb, w, bias), comm=comm)


def _cols(t, rows=None):
    return pl.BlockSpec((t if rows is None else rows, LANES), lambda j: (0, j))


def _gn_stats(cc):
    mu = jnp.mean(cc, axis=-1, keepdims=True)
    xc = cc - mu
    var = jnp.mean(xc * xc, axis=-1, keepdims=True)
    rstd = lax.rsqrt(var + LN_EPS)
    return xc * rstd, rstd


def _tap_groups(offsets):
    groups = {}
    for k, s in enumerate(offsets):
        groups.setdefault(s % SUBLANES, []).append((k, s - s % SUBLANES))
    return sorted(groups.items())


def _shifted(win, phase):
    return win if phase == 0 else pltpu.roll(win, win.shape[0] - phase, 0)


def _tap_sum(win, wv, groups, rows):
    parts, t = [None] * 4, 0
    for phase, taps in groups:
        sh = _shifted(win, phase)
        for k, off in taps:
            term = wv[k:k + 1, :] * sh[off:off + rows, :]
            parts[t % 4] = term if parts[t % 4] is None else parts[t % 4] + term
            t += 1
    return (parts[0] + parts[1]) + (parts[2] + parts[3])


def _conv31_gn(c, w, bias, gg, gb, name, comm=None):
    T, D = c.shape
    K = w.shape[0]
    B, P, N = CONV_BLOCK, CONV_PAD, _tile(T, NORM_ROWS)
    assert D % LANES == 0 and T % B == 0 and K - 1 <= P
    groups = _tap_groups([P - (K - 1) + k for k in range(K)])

    def body(c_ref, w_ref, b_ref, gg_ref, gb_ref, cc_ref, cs_ref, xpad):
        xpad[0:P, :] = jnp.zeros((P, LANES), F32)
        xpad[P:P + T, :] = c_ref[...]
        wv = w_ref[...]
        bv, ggv, gbv = b_ref[...], gg_ref[...], gb_ref[...]

        def conv_step(i, carry):
            base = pl.multiple_of(i * B, B)
            win = xpad[pl.ds(base, B + P), :]
            cc_ref[pl.ds(base, B), :] = _tap_sum(win, wv, groups, B) + bv
            return carry

        lax.fori_loop(0, T // B, conv_step, 0)

        def norm_step(i, carry):
            base = pl.multiple_of(i * N, N)
            xhat, _ = _gn_stats(cc_ref[pl.ds(base, N), :])
            gn = xhat * ggv + gbv
            cs_ref[pl.ds(base, N), :] = (gn * _sigmoid(gn)).astype(BF16)
            return carry

        lax.fori_loop(0, T // N, norm_step, 0)

    return _pallas(
        body, name=name, grid=(D // LANES,),
        in_specs=[_cols(T), _cols(T, K), _cols(T, 1), _cols(T, 1), _cols(T, 1)],
        out_specs=[_cols(T), _cols(T)],
        out_shape=[_sds((T, D), F32), _sds((T, D), BF16)],
        scratch_shapes=[pltpu.VMEM((T + P, LANES), F32)],
        args=(c, w, bias, gg, gb), comm=comm)


def _conv4(rx, w, bias, name):
    T, C = rx.shape
    K = w.shape[0]
    B, P = 4 * CONV_BLOCK, RNN_PAD
    assert C % LANES == 0 and T % B == 0 and K - 1 <= P

    def body(x_ref, w_ref, b_ref, r_ref, xpad):
        xpad[0:P, :] = jnp.zeros((P, LANES), F32)
        xpad[P:P + T, :] = x_ref[...]
        wv, bv = w_ref[...], b_ref[...]

        def step(i, carry):
            base = pl.multiple_of(i * B, B)
            win = xpad[pl.ds(base, B + P), :]
            acc = jnp.zeros((B, LANES), F32)
            for k in range(K):
                s = P - (K - 1) + k
                acc = acc + wv[k:k + 1, :] * win[s:s + B, :]
            r_ref[pl.ds(base, B), :] = acc + bv
            return carry

        lax.fori_loop(0, T // B, step, 0)

    (r,) = _pallas(
        body, name=name, grid=(C // LANES,),
        in_specs=[_cols(T), _cols(T, K), _cols(T, 1)],
        out_specs=[_cols(T)],
        out_shape=[_sds((T, C), F32)],
        scratch_shapes=[pltpu.VMEM((T + RNN_PAD, LANES), F32)],
        args=(rx, w, bias))
    return r


def _gates(r, bda, bdx, b_a, b_x, lam, name, comm=None):
    T, C = r.shape
    tm = _tile(T, 512)
    chunks = _band_chunks(C, C // RNN_BLOCKS)

    def body(r_ref, wa_ref, wx_ref, ba_ref, bx_ref, lam_ref, ra_ref, ri_ref, a_ref, u_ref):
        for c0, cw, k0, k1 in chunks:
            cols = slice(c0, c0 + cw)
            rb = r_ref[:, k0:k1].astype(BF16)
            ra = _sigmoid(_dot(rb, wa_ref[k0:k1, cols]) + ba_ref[:, cols])
            ri = _sigmoid(_dot(rb, wx_ref[k0:k1, cols]) + bx_ref[:, cols])
            log_a = (-RG_LRU_C) * ra * _softplus_neg(lam_ref[:, cols])
            ra_ref[:, cols] = ra
            ri_ref[:, cols] = ri
            a_ref[:, cols] = jnp.exp(log_a)
            u_ref[:, cols] = jnp.sqrt(-_expm1(2.0 * log_a)) * (ri * r_ref[:, cols])

    return _pallas(
        body, name=name, grid=(T // tm,),
        in_specs=[_rows(tm, C), _res(bda.shape), _res(bdx.shape), _res((1, C)), _res((1, C)), _res((1, C))],
        out_specs=[_rows(tm, C)] * 4,
        out_shape=[_sds((T, C), F32)] * 4,
        args=(r, bda, bdx, b_a, b_x, lam), comm=comm)


def _scan_fwd(a, u, rg, name, comm=None):
    T, C = a.shape
    tt = _tile(T, 512)

    def body(a_ref, u_ref, rg_ref, h_ref, hp_ref, hg_ref, carry):
        @pl.when(pl.program_id(0) == 0)
        def _():
            carry[...] = jnp.zeros_like(carry)

        row = lax.broadcasted_iota(jnp.int32, (SUBLANES, C), 0)

        def group(i, hprev):
            base = pl.multiple_of(i * SUBLANES, SUBLANES)
            av = a_ref[pl.ds(base, SUBLANES), :]
            uv = u_ref[pl.ds(base, SUBLANES), :]
            for s in (1, 2, 4):
                a_s = jnp.where(row >= s, pltpu.roll(av, s, 0), 1.0)
                u_s = jnp.where(row >= s, pltpu.roll(uv, s, 0), 0.0)
                uv = av * u_s + uv
                av = av * a_s
            h = av * hprev + uv
            h_ref[pl.ds(base, SUBLANES), :] = h
            hp_ref[pl.ds(base, SUBLANES), :] = jnp.where(row >= 1, pltpu.roll(h, 1, 0), hprev)
            return h[SUBLANES - 1:SUBLANES, :]

        carry[...] = lax.fori_loop(0, tt // SUBLANES, group, carry[...])
        gel, _ = _gelu(rg_ref[...].astype(F32))
        hg_ref[...] = (h_ref[...] * gel).astype(BF16)

    return _pallas(
        body, name=name, grid=(T // tt,),
        in_specs=[_rows(tt, C)] * 3,
        out_specs=[_rows(tt, C)] * 3,
        out_shape=[_sds((T, C), F32), _sds((T, C), F32), _sds((T, C), BF16)],
        scratch_shapes=[pltpu.VMEM((1, C), F32)],
        args=(a, u, rg), comm=comm)


def _mix_out_ln(cs, hg, gc, gr, hres, wcp, wrp, wout, g, b, name, comm=None):
    T, D = cs.shape
    C = hg.shape[1]
    tm = _tile(T, 512)

    def body(cs_ref, hg_ref, gc_ref, gr_ref, h_ref, wcp_ref, wrp_ref, wo_ref, g_ref, b_ref,
             yc_ref, yr_ref, m_ref, r_ref, y_ref, yb_ref):
        yc = _dot(cs_ref[...], wcp_ref[...])
        yr = _dot(hg_ref[...], wrp_ref[...])
        m = (_sigmoid(gc_ref[...].astype(F32)) * yc + _sigmoid(gr_ref[...].astype(F32)) * yr).astype(BF16)
        r = ALPHA * h_ref[...] + _dot(m, wo_ref[...])
        y = _ln_fwd(r, g_ref[...], b_ref[...])
        yc_ref[...] = yc.astype(BF16)
        yr_ref[...] = yr.astype(BF16)
        m_ref[...] = m
        r_ref[...] = r
        y_ref[...] = y
        yb_ref[...] = y.astype(BF16)

    return _pallas(
        body, name=name, grid=(T // tm,),
        in_specs=[_rows(tm, D), _rows(tm, C), _rows(tm, D), _rows(tm, D), _rows(tm, D), _res(wcp.shape),
                  _res(wrp.shape), _res(wout.shape), _res((1, D)), _res((1, D))],
        out_specs=[_rows(tm, D)] * 6,
        out_shape=[_sds((T, D), BF16), _sds((T, D), BF16), _sds((T, D), BF16), _sds((T, D), F32),
                   _sds((T, D), F32), _sds((T, D), BF16)],
        args=(cs, hg, gc, gr, hres, wcp, wrp, wout, g, b), comm=comm)


def _loss_ln_bwd(y, target, r, g, name):
    T, D = y.shape
    tm = _tile(T, 512)

    def body(y_ref, t_ref, r_ref, g_ref, loss_ref, dr_ref, drb_ref, dg_ref, db_ref):
        @pl.when(pl.program_id(0) == 0)
        def _():
            loss_ref[...] = jnp.zeros_like(loss_ref)
            dg_ref[...] = jnp.zeros_like(dg_ref)
            db_ref[...] = jnp.zeros_like(db_ref)

        e = y_ref[...] - t_ref[...]
        loss_ref[...] += (0.5 / D) * jnp.sum(e * e)
        dr, dg, db = _ln_bwd(e * (1.0 / D), r_ref[...], g_ref[...])
        dr_ref[...] = dr
        drb_ref[...] = dr.astype(BF16)
        dg_ref[...] += dg
        db_ref[...] += db

    return _pallas(
        body, name=name, grid=(T // tm,),
        in_specs=[_rows(tm, D), _rows(tm, D), _rows(tm, D), _res((1, D))],
        out_specs=[_acc((SUBLANES, LANES)), _rows(tm, D), _rows(tm, D), _acc((1, D)), _acc((1, D))],
        out_shape=[_sds((SUBLANES, LANES), F32), _sds((T, D), F32), _sds((T, D), BF16), _sds((1, D), F32),
                   _sds((1, D), F32)],
        args=(y, target, r, g))


def _ffn_bwd_a(drb, wd, gu, name, comm=None):
    T, D = drb.shape
    F = wd.shape[0]
    tm = _tile(T, 512)

    def body(d_ref, wd_ref, gu_ref, o_ref):
        d = d_ref[...]
        for j, cw in _chunks(F, 256):
            da = 0.5 * _dot_nt(d, wd_ref[j:j + cw, :])
            gt = gu_ref[:, j:j + cw].astype(F32)
            up = gu_ref[:, F + j:F + j + cw].astype(F32)
            sg = _sigmoid(gt)
            o_ref[:, j:j + cw] = (da * up * (sg * (1.0 + gt * (1.0 - sg)))).astype(BF16)
            o_ref[:, F + j:F + j + cw] = (da * (gt * sg)).astype(BF16)

    (dgu,) = _pallas(
        body, name=name, grid=(T // tm,),
        in_specs=[_rows(tm, D), _res(wd.shape), _rows(tm, 2 * F)],
        out_specs=[_rows(tm, 2 * F)],
        out_shape=[_sds((T, 2 * F), BF16)],
        args=(drb, wd, gu), comm=comm)
    return dgu


def _nt_res(dus, w, dres, name, ln=None, colsum=False, comm=None):
    T = dus[0].shape[0]
    widths = [d.shape[1] for d in dus]
    offs = [sum(widths[:p]) for p in range(len(dus))]
    K, D, P = sum(widths), w.shape[1], len(dus)
    tm = _tile(T, 512)
    n_in = P + 2 + (2 if ln else 0)

    def body(*refs):
        du_refs, w_ref, dres_ref = refs[:P], refs[P], refs[P + 1]
        outs = refs[n_in:]
        dy = ALPHA * dres_ref[...]
        for du_ref, off, width in zip(du_refs, offs, widths):
            dy = dy + _dot(du_ref[...], w_ref[off:off + width, :])
        if ln:
            r_ref, g_ref = refs[P + 2:P + 4]

            @pl.when(pl.program_id(0) == 0)
            def _():
                outs[2][...] = jnp.zeros_like(outs[2])
                outs[3][...] = jnp.zeros_like(outs[3])

            dr, dg, db = _ln_bwd(dy, r_ref[...], g_ref[...])
            outs[0][...] = dr
            outs[1][...] = dr.astype(BF16)
            outs[2][...] += dg
            outs[3][...] += db
        else:
            outs[0][...] = dy
        if colsum:
            cs_ref = outs[-1]

            @pl.when(pl.program_id(0) == 0)
            def _():
                cs_ref[...] = jnp.zeros_like(cs_ref)

            for du_ref, off, width in zip(du_refs, offs, widths):
                cs_ref[:, off:off + width] += jnp.sum(du_ref[...].astype(F32), axis=0, keepdims=True)

    in_specs = [_rows(tm, width) for width in widths] + [_res(w.shape), _rows(tm, D)]
    args = list(dus) + [w, dres]
    if ln:
        in_specs += [_rows(tm, D), _res((1, D))]
        args += list(ln)
        out_specs = [_rows(tm, D), _rows(tm, D), _acc((1, D)), _acc((1, D))]
        out_shape = [_sds((T, D), F32), _sds((T, D), BF16), _sds((1, D), F32), _sds((1, D), F32)]
    else:
        out_specs = [_rows(tm, D)]
        out_shape = [_sds((T, D), F32)]
    if colsum:
        out_specs.append(_acc((1, K)))
        out_shape.append(_sds((1, K), F32))
    return _pallas(body, name=name, grid=(T // tm,), in_specs=in_specs, out_specs=out_specs, out_shape=out_shape,
                   args=args, comm=comm)


def _mm_tn(x, dy, name, scale=1.0, comm=None):
    T, K = x.shape
    N = dy.shape[1]
    tt = _tile(T, 1024)
    tn = next(N // d for d in range(1, N // LANES + 1)
              if N % d == 0 and (N // d) % LANES == 0 and K * (N // d) * 4 <= MM_TN_OUT_BYTES)
    nt = T // tt

    def body(x_ref, dy_ref, o_ref):
        t = pl.program_id(1)

        @pl.when(t == 0)
        def _():
            o_ref[...] = jnp.zeros_like(o_ref)

        o_ref[...] += _dot_tn(x_ref[...].astype(BF16), dy_ref[...])
        if scale != 1.0:
            @pl.when(t == nt - 1)
            def _():
                o_ref[...] = o_ref[...] * scale

    (out,) = _pallas(
        body, name=name, grid=(N // tn, nt),
        in_specs=[pl.BlockSpec((tt, K), lambda j, t: (t, 0)), pl.BlockSpec((tt, tn), lambda j, t: (t, j))],
        out_specs=[pl.BlockSpec((K, tn), lambda j, t: (0, j))],
        out_shape=[_sds((K, N), F32)],
        args=(x, dy), comm=comm)
    return out


def _mix_bwd1(drb, wout, wcp, wrp, gc, gr, yc, yr, rg, h, name, comm=None):
    T, D = drb.shape
    C = rg.shape[1]
    tm = _tile(T, 512)

    def body(d_ref, wo_ref, wcp_ref, wrp_ref, gc_ref, gr_ref, yc_ref, yr_ref, rg_ref, h_ref,
             dyc_ref, dyr_ref, tail_ref, dcs_ref, dh_ref):
        dm = _dot_nt(d_ref[...], wo_ref[...])
        sc = _sigmoid(gc_ref[...].astype(F32))
        sr = _sigmoid(gr_ref[...].astype(F32))
        dyc = (dm * sc).astype(BF16)
        dyr = (dm * sr).astype(BF16)
        dyc_ref[...] = dyc
        dyr_ref[...] = dyr
        tail_ref[:, C:C + D] = (dm * yc_ref[...].astype(F32) * sc * (1.0 - sc)).astype(BF16)
        tail_ref[:, C + D:] = (dm * yr_ref[...].astype(F32) * sr * (1.0 - sr)).astype(BF16)
        dcs_ref[...] = _dot_nt(dyc, wcp_ref[...])
        dhg = _dot_nt(dyr, wrp_ref[...])
        rgv = rg_ref[...].astype(F32)
        gel, t = _gelu(rgv)
        dh_ref[...] = dhg * gel
        tail_ref[:, 0:C] = (dhg * h_ref[...] * _gelu_grad(rgv, t)).astype(BF16)

    return _pallas(
        body, name=name, grid=(T // tm,),
        in_specs=[_rows(tm, D), _res(wout.shape), _res(wcp.shape), _res(wrp.shape), _rows(tm, D), _rows(tm, D),
                  _rows(tm, D), _rows(tm, D), _rows(tm, C), _rows(tm, C)],
        out_specs=[_rows(tm, D), _rows(tm, D), _rows(tm, C + 2 * D), _rows(tm, D), _rows(tm, C)],
        out_shape=[_sds((T, D), BF16), _sds((T, D), BF16), _sds((T, C + 2 * D), BF16), _sds((T, D), F32),
                   _sds((T, C), F32)],
        args=(drb, wout, wcp, wrp, gc, gr, yc, yr, rg, h), comm=comm)


def _scan_bwd(dh, a, name):
    T, C = dh.shape
    tt = _tile(T, 512)
    nt = T // tt
    ng = tt // SUBLANES

    def body(d_ref, a_ref, g_ref, carry):
        @pl.when(pl.program_id(0) == 0)
        def _():
            carry[...] = jnp.zeros_like(carry)

        row = lax.broadcasted_iota(jnp.int32, (SUBLANES, C), 0)

        def group(j, enext):
            base = pl.multiple_of((ng - 1 - j) * SUBLANES, SUBLANES)
            av = a_ref[pl.ds(base, SUBLANES), :]
            dv = d_ref[pl.ds(base, SUBLANES), :]
            bv = av * dv
            for s in (1, 2, 4):
                keep = row < SUBLANES - s
                a_s = jnp.where(keep, pltpu.roll(av, SUBLANES - s, 0), 1.0)
                b_s = jnp.where(keep, pltpu.roll(bv, SUBLANES - s, 0), 0.0)
                bv = av * b_s + bv
                av = av * a_s
            e = av * enext + bv
            e_up = jnp.where(row < SUBLANES - 1, pltpu.roll(e, SUBLANES - 1, 0), enext)
            g_ref[pl.ds(base, SUBLANES), :] = dv + e_up
            return e[0:1, :]

        carry[...] = lax.fori_loop(0, ng, group, carry[...])

    rev = pl.BlockSpec((tt, C), lambda i: (nt - 1 - i, 0))
    (g,) = _pallas(
        body, name=name, grid=(nt,), in_specs=[rev, rev], out_specs=[rev],
        out_shape=[_sds((T, C), F32)],
        scratch_shapes=[pltpu.VMEM((1, C), F32)],
        args=(dh, a))
    return g


def _gates_bwd(g, hp, ra, ri, r, lam, bda, bdx, name, comm=None):
    T, C = g.shape
    tm = _tile(T, 512)
    nt = T // tm
    chunks = _band_chunks(C, C // RNN_BLOCKS)

    def body(g_ref, hp_ref, ra_ref, ri_ref, r_ref, lam_ref, wa_ref, wx_ref,
             dr_ref, dpa_ref, dpx_ref, dlam_ref, dba_ref, dbx_ref):
        @pl.when(pl.program_id(0) == 0)
        def _():
            dlam_ref[...] = jnp.zeros_like(dlam_ref)
            dba_ref[...] = jnp.zeros_like(dba_ref)
            dbx_ref[...] = jnp.zeros_like(dbx_ref)

        for c0, cw, _, _ in chunks:
            cols = slice(c0, c0 + cw)
            gv, rav, riv, rv = g_ref[:, cols], ra_ref[:, cols], ri_ref[:, cols], r_ref[:, cols]
            sp = _softplus_neg(lam_ref[:, cols])
            log_a = (-RG_LRU_C) * rav * sp
            av = jnp.exp(log_a)
            mult = jnp.sqrt(-_expm1(2.0 * log_a))
            d_mult = gv * riv * rv
            d_i = gv * mult * rv
            d_loga = gv * hp_ref[:, cols] * av - d_mult * (av * av) / mult
            d_ra = d_loga * ((-RG_LRU_C) * sp)
            dpa = d_ra * rav * (1.0 - rav)
            dpx = d_i * riv * (1.0 - riv)
            dpa_ref[:, cols] = dpa.astype(BF16)
            dpx_ref[:, cols] = dpx.astype(BF16)
            dr_ref[:, cols] = gv * mult * riv
            dlam_ref[:, cols] += jnp.sum(d_loga * ((-RG_LRU_C) * rav), axis=0, keepdims=True)
            dba_ref[:, cols] += jnp.sum(dpa, axis=0, keepdims=True)
            dbx_ref[:, cols] += jnp.sum(dpx, axis=0, keepdims=True)
        for c0, cw, k0, k1 in chunks:
            cols = slice(c0, c0 + cw)
            dr_ref[:, k0:k1] += (_dot_nt(dpa_ref[:, cols], wa_ref[k0:k1, cols])
                                 + _dot_nt(dpx_ref[:, cols], wx_ref[k0:k1, cols]))

        @pl.when(pl.program_id(0) == nt - 1)
        def _():
            dlam_ref[...] = dlam_ref[...] * (-_sigmoid(-lam_ref[...]))

    return _pallas(
        body, name=name, grid=(nt,),
        in_specs=[_rows(tm, C)] * 5 + [_res((1, C)), _res(bda.shape), _res(bdx.shape)],
        out_specs=[_rows(tm, C)] * 3 + [_acc((1, C))] * 3,
        out_shape=[_sds((T, C), F32), _sds((T, C), BF16), _sds((T, C), BF16)] + [_sds((1, C), F32)] * 3,
        args=(g, hp, ra, ri, r, lam, bda, bdx), comm=comm)


def _band_dw(r, dpa, dpx, name):
    T, C = r.shape
    tt = _tile(T, 512)
    chunks = _band_chunks(C, C // RNN_BLOCKS)

    def body(r_ref, dpa_ref, dpx_ref, oa_ref, ox_ref):
        @pl.when(pl.program_id(0) == 0)
        def _():
            oa_ref[...] = jnp.zeros_like(oa_ref)
            ox_ref[...] = jnp.zeros_like(ox_ref)

        for c0, cw, k0, k1 in chunks:
            cols = slice(c0, c0 + cw)
            rb = r_ref[:, k0:k1].astype(BF16)
            oa_ref[k0:k1, cols] += _dot_tn(rb, dpa_ref[:, cols])
            ox_ref[k0:k1, cols] += _dot_tn(rb, dpx_ref[:, cols])

    return _pallas(
        body, name=name, grid=(T // tt,),
        in_specs=[_rows(tt, C)] * 3,
        out_specs=[_acc((C, C)), _acc((C, C))],
        out_shape=[_sds((C, C), F32), _sds((C, C), F32)],
        args=(r, dpa, dpx))


def _conv4_bwd(dr, rx, w, name):
    T, C = dr.shape
    K = w.shape[0]
    B, P = CONV_BLOCK, RNN_PAD
    assert K <= SUBLANES
    d_groups = _tap_groups([K - 1 - k for k in range(K)])
    x_groups = _tap_groups([P - (K - 1) + k for k in range(K)])

    def fold(v):
        part = v[0:SUBLANES, :]
        for q in range(1, B // SUBLANES):
            part = part + v[q * SUBLANES:(q + 1) * SUBLANES, :]
        return part

    def body(d_ref, x_ref, w_ref, dx_ref, dw_ref, db_ref, dpad, xpad, dwacc, dbacc):
        dpad[0:T, :] = d_ref[...]
        dpad[T:T + P, :] = jnp.zeros((P, LANES), F32)
        xpad[0:P, :] = jnp.zeros((P, LANES), F32)
        xpad[P:P + T, :] = x_ref[...]
        dwacc[...] = jnp.zeros_like(dwacc)
        dbacc[...] = jnp.zeros_like(dbacc)
        wv = w_ref[...]

        def step(i, carry):
            base = pl.multiple_of(i * B, B)
            dwin = dpad[pl.ds(base, B + P), :]
            xwin = xpad[pl.ds(base, B + P), :]
            dcur = dwin[0:B, :]
            dx_ref[pl.ds(base, B), :] = _tap_sum(dwin, wv, d_groups, B).astype(BF16)
            for phase, taps in x_groups:
                sh = _shifted(xwin, phase)
                for k, off in taps:
                    dwacc[k * SUBLANES:(k + 1) * SUBLANES, :] += fold(dcur * sh[off:off + B, :])
            dbacc[...] += fold(dcur)
            return carry

        lax.fori_loop(0, T // B, step, 0)
        dw_ref[...] = jnp.zeros_like(dw_ref)
        for k in range(K):
            dw_ref[k:k + 1, :] = jnp.sum(dwacc[k * SUBLANES:(k + 1) * SUBLANES, :], axis=0, keepdims=True)
        db_ref[...] = jnp.sum(dbacc[...], axis=0, keepdims=True)

    return _pallas(
        body, name=name, grid=(C // LANES,),
        in_specs=[_cols(T), _cols(T), _cols(T, K)],
        out_specs=[_cols(T), _cols(T, SUBLANES), _cols(T, 1)],
        out_shape=[_sds((T, C), BF16), _sds((SUBLANES, C), F32), _sds((1, C), F32)],
        scratch_shapes=[pltpu.VMEM((T + P, LANES), F32), pltpu.VMEM((T + P, LANES), F32),
                        pltpu.VMEM((SUBLANES * SUBLANES, LANES), F32), pltpu.VMEM((SUBLANES, LANES), F32)],
        args=(dr, rx, w))


def _conv31_bwd(dcs, cc, c, cv, cg, w, gg, gb, name, comm=None):
    T, D = dcs.shape
    K = w.shape[0]
    R, B, P = _tile(T, NORM_ROWS), CONV_BLOCK, CONV_PAD
    d_groups = _tap_groups([K - 1 - k for k in range(K)])
    x_groups = _tap_groups([P - (K - 1) + k for k in range(K)])

    def body(dcs_ref, cc_ref, c_ref, cv_ref, cg_ref, w_ref, gg_ref, gb_ref,
             dcv_ref, dcg_ref, dw_ref, db_ref, dgg_ref, dgb_ref, dpad, xpad, dwacc):
        dpad[T:T + P, :] = jnp.zeros((P, LANES), F32)
        xpad[0:P, :] = jnp.zeros((P, LANES), F32)
        xpad[P:P + T, :] = c_ref[...]
        dwacc[...] = jnp.zeros_like(dwacc)
        db_ref[...] = jnp.zeros_like(db_ref)
        dgg_ref[...] = jnp.zeros_like(dgg_ref)
        dgb_ref[...] = jnp.zeros_like(dgb_ref)
        wv, ggv, gbv = w_ref[...], gg_ref[...], gb_ref[...]

        def norm_step(i, carry):
            base = pl.multiple_of(i * R, R)
            xhat, rstd = _gn_stats(cc_ref[pl.ds(base, R), :])
            gn = xhat * ggv + gbv
            sg = _sigmoid(gn)
            dgn = dcs_ref[pl.ds(base, R), :] * (sg * (1.0 + gn * (1.0 - sg)))
            dgg_ref[...] += jnp.sum(dgn * xhat, axis=0, keepdims=True)
            dgb_ref[...] += jnp.sum(dgn, axis=0, keepdims=True)
            dxh = dgn * ggv
            m1 = jnp.mean(dxh, axis=-1, keepdims=True)
            m2 = jnp.mean(dxh * xhat, axis=-1, keepdims=True)
            dcc = rstd * (dxh - m1 - xhat * m2)
            dpad[pl.ds(base, R), :] = dcc
            db_ref[...] += jnp.sum(dcc, axis=0, keepdims=True)
            return carry

        lax.fori_loop(0, T // R, norm_step, 0)

        def conv_step(i, carry):
            base = pl.multiple_of(i * B, B)
            dwin = dpad[pl.ds(base, B + P), :]
            xwin = xpad[pl.ds(base, B + P), :]
            dcur = dwin[0:B, :]
            acc = _tap_sum(dwin, wv, d_groups, B)
            for phase, taps in x_groups:
                sh = _shifted(xwin, phase)
                for k, off in taps:
                    prod = dcur * sh[off:off + B, :]
                    part = prod[0:SUBLANES, :]
                    for q in range(1, B // SUBLANES):
                        part = part + prod[q * SUBLANES:(q + 1) * SUBLANES, :]
                    dwacc[k * SUBLANES:(k + 1) * SUBLANES, :] += part
            cgv = cg_ref[pl.ds(base, B), :].astype(F32)
            cvv = cv_ref[pl.ds(base, B), :].astype(F32)
            sg = _sigmoid(cgv)
            dcv_ref[pl.ds(base, B), :] = (acc * sg).astype(BF16)
            dcg_ref[pl.ds(base, B), :] = (acc * cvv * sg * (1.0 - sg)).astype(BF16)
            return carry

        lax.fori_loop(0, T // B, conv_step, 0)
        dw_ref[...] = jnp.zeros_like(dw_ref)
        for k in range(K):
            dw_ref[k:k + 1, :] = jnp.sum(dwacc[k * SUBLANES:(k + 1) * SUBLANES, :], axis=0, keepdims=True)

    return _pallas(
        body, name=name, grid=(D // LANES,),
        in_specs=[_cols(T)] * 5 + [_cols(T, K), _cols(T, 1), _cols(T, 1)],
        out_specs=[_cols(T), _cols(T), _cols(T, P), _cols(T, 1), _cols(T, 1), _cols(T, 1)],
        out_shape=[_sds((T, D), BF16), _sds((T, D), BF16), _sds((P, D), F32), _sds((1, D), F32),
                   _sds((1, D), F32), _sds((1, D), F32)],
        scratch_shapes=[pltpu.VMEM((T + P, LANES), F32), pltpu.VMEM((T + P, LANES), F32),
                        pltpu.VMEM((P * SUBLANES, LANES), F32)],
        args=(dcs, cc, c, cv, cg, w, gg, gb), comm=comm)


def _reduce_adamw(recvs, w, m, v, name):
    L, R, C = w.shape
    assert len(recvs) == L
    tr = next((R // d for d in range(1, R // SUBLANES + 1)
               if R % d == 0 and (R // d) % SUBLANES == 0 and R // d <= ADAMW_ROWS), R)
    nr = R // tr
    c1 = 1.0 - ADAM_B1 ** ADAM_STEP
    c2 = 1.0 - ADAM_B2 ** ADAM_STEP

    def body(*refs):
        recv_refs = refs[:L]
        w_ref, m_ref, v_ref, g_ref, d_ref, mo_ref, vo_ref = refs[L:]

        def update(recv_ref):
            g = recv_ref[0].astype(F32)
            for k in range(1, N_DEV):
                g = g + recv_ref[k].astype(F32)
            mn = ADAM_B1 * m_ref[0] + (1.0 - ADAM_B1) * g
            vn = ADAM_B2 * v_ref[0] + (1.0 - ADAM_B2) * (g * g)
            g_ref[0] = g
            mo_ref[0] = mn
            vo_ref[0] = vn
            d_ref[0] = (-ADAM_LR) * ((mn / c1) / (jnp.sqrt(vn / c2) + ADAM_EPS) + ADAM_WD * w_ref[0])

        for l in range(L):
            pl.when(pl.program_id(0) == l)(lambda l=l: update(recv_refs[l]))

    def recv_spec(l):
        return pl.BlockSpec((N_DEV, tr, C), lambda j, i: (0, jnp.where(j == l, i, jnp.where(j < l, 0, nr - 1)), 0))

    blk = pl.BlockSpec((1, tr, C), lambda j, i: (j, i, 0))
    return _pallas(
        body, name=name, grid=(L, nr),
        in_specs=[recv_spec(l) for l in range(L)] + [blk, blk, blk],
        out_specs=[blk] * 4,
        out_shape=[_sds((L, R, C), F32)] * 4,
        args=(*recvs, w, m, v))


def _unshard(name, gathered):
    n, r, c = gathered.shape
    if name in COL_SHARDED and name not in TRANSPOSED:
        return gathered.transpose(1, 0, 2).reshape(r, n * c)
    return gathered.reshape(n * r, c)


def _to_shards(name, full):
    R, C = full.shape
    wire = BF16 if name in BF16_ON_WIRE else F32
    if name in TRANSPOSED:
        return full.reshape(R, N_DEV, C // N_DEV).transpose(1, 2, 0).astype(wire)
    if name in COL_SHARDED:
        return full.reshape(R, N_DEV, C // N_DEV).transpose(1, 0, 2).astype(wire)
    return full.reshape(N_DEV, R // N_DEV, C).astype(wire)


def _block_diag(w):
    H, b, _ = w.shape
    eye = jnp.eye(H, dtype=w.dtype)
    return (w[:, :, None, :] * eye[:, None, :, None]).reshape(H * b, H * b)


def _diag_blocks(dense, H):
    b = dense.shape[0] // H
    eye = jnp.eye(H, dtype=dense.dtype)
    return jnp.sum(dense.reshape(H, b, H, b) * eye[:, None, :, None], axis=2)


def _pack(arrs, rows):
    flat = jnp.concatenate([a.reshape(-1) for a in arrs])
    return jnp.pad(flat, (0, rows * 1024 - flat.shape[0])).reshape(1, rows, 1024)


def _unpack(packed, shapes):
    flat = packed.reshape(-1)
    out, off = [], 0
    for s in shapes:
        n = math.prod(s)
        out.append(flat[off:off + n].reshape(s))
        off += n
    return out


class _Queue:
    def __init__(self, kind, us_per_mb):
        self.kind, self.us_per_mb, self.items, self.done = kind, us_per_mb, [], {}

    def push(self, key, arr):
        self.items.append((key, arr))

    def mb(self, arr):
        return _mbytes(arr) / (N_DEV if self.kind == "scatter" else 1)

    def take(self, micros):
        taken, budget = [], micros / self.us_per_mb
        while self.items and (not taken or self.mb(self.items[0][1]) <= budget):
            budget -= self.mb(self.items[0][1])
            taken.append(self.items.pop(0))
        return {"kind": self.kind, "keys": [k for k, _ in taken], "arrs": [a for _, a in taken]} if taken else None

    def landed(self, comm):
        if comm:
            self.done.update(zip(comm["keys"], comm["recv"]))

    def flush(self, name, upto=None):
        n = len(self.items)
        if upto is not None:
            keys = [k for k, _ in self.items]
            n = keys.index(upto) + 1 if upto in keys else 0
        if n:
            taken, self.items = self.items[:n], self.items[n:]
            self.done.update(zip([k for k, _ in taken], _exchange(self.kind, [a for _, a in taken], name)))


def kernel(x, ffn1_w_gu, ffn1_w_down, ln1_g, ln1_b, mix_w_in, mix_b_in, conv_dw_w, conv_dw_b, conv_gn_g, conv_gn_b, conv_w_proj, rnn_conv_w, rnn_conv_b, rnn_w_a, rnn_b_a, rnn_w_x, rnn_b_x, rnn_lambda, rnn_w_proj, mix_w_out, ln2_g, ln2_b, ffn2_w_gu, ffn2_w_down, ln3_g, ln3_b, loss_target, m_ffn1_w_gu, m_ffn1_w_down, m_ln1_g, m_ln1_b, m_mix_w_in, m_mix_b_in, m_conv_dw_w, m_conv_dw_b, m_conv_gn_g, m_conv_gn_b, m_conv_w_proj, m_rnn_conv_w, m_rnn_conv_b, m_rnn_w_a, m_rnn_b_a, m_rnn_w_x, m_rnn_b_x, m_rnn_lambda, m_rnn_w_proj, m_mix_w_out, m_ln2_g, m_ln2_b, m_ffn2_w_gu, m_ffn2_w_down, m_ln3_g, m_ln3_b, v_ffn1_w_gu, v_ffn1_w_down, v_ln1_g, v_ln1_b, v_mix_w_in, v_mix_b_in, v_conv_dw_w, v_conv_dw_b, v_conv_gn_g, v_conv_gn_b, v_conv_w_proj, v_rnn_conv_w, v_rnn_conv_b, v_rnn_w_a, v_rnn_b_a, v_rnn_w_x, v_rnn_b_x, v_rnn_lambda, v_rnn_w_proj, v_mix_w_out, v_ln2_g, v_ln2_b, v_ffn2_w_gu, v_ffn2_w_down, v_ln3_g, v_ln3_b):
    given = dict(locals())
    W = {n: given[n] for n in WEIGHTS}
    M = {n: given["m_" + n] for n in WEIGHTS}
    V = {n: given["v_" + n] for n in WEIGHTS}
    T, D = x.shape[1], x.shape[2]
    L = DEPTH
    x2 = x.reshape(T, D)
    target = loss_target.reshape(T, D)
    d_rnn = rnn_conv_b.shape[1]

    gather = _Queue("gather", GATHER_US_PER_MB)
    for l in range(L):
        for n in USE_ORDER:
            shard = W[n][l].T if n in TRANSPOSED else W[n][l]
            gather.push((n, l), shard.astype(BF16) if n in BF16_ON_WIRE else shard)
    gather.flush("gather_first", upto=("ffn1_w_gu", 0))
    full_cache = {}

    def full(n, l):
        if (n, l) not in full_cache:
            gather.flush(f"gather_{n}_{l}", upto=(n, l))
            full_cache[(n, l)] = _unshard(n, gather.done[(n, l)])
        return full_cache[(n, l)]

    def fwd_comm(micros):
        return gather.take(micros)

    bd_a = [_block_diag(rnn_w_a[l]).astype(BF16) for l in range(L)]
    bd_x = [_block_diag(rnn_w_x[l]).astype(BF16) for l in range(L)]

    def vec(name, l):
        return W[name][l:l + 1]

    saved = []
    h, hb = x2, x2
    for l in range(L):
        s = {"hb_in": hb}
        w_gu = full("ffn1_w_gu", l)
        comm = fwd_comm(105)
        s["gu1"], s["a1"] = _ffn_up(hb, w_gu, f"ffn1_up_{l}", comm=comm)
        gather.landed(comm)
        w_down = full("ffn1_w_down", l)
        comm = fwd_comm(65)
        s["r1"], y1, s["y1b"] = _ffn_down_ln(s["a1"], w_down, h, vec("ln1_g", l), vec("ln1_b", l),
                                              f"ffn1_down_ln_{l}", comm=comm)
        gather.landed(comm)
        w_in = full("mix_w_in", l)
        comm = fwd_comm(5)
        s["c"], s["cv"], s["cg"], s["rx"], s["rg"], s["gc"], s["gr"] = _mix_in(
            s["y1b"], w_in, vec("mix_b_in", l), d_rnn, f"mix_in_{l}", comm=comm)
        gather.landed(comm)
        w_dw = full("conv_dw_w", l)
        comm = fwd_comm(195)
        s["cc"], s["cs"] = _conv31_gn(s["c"], w_dw, vec("conv_dw_b", l), vec("conv_gn_g", l),
                                      vec("conv_gn_b", l), f"conv31_gn_{l}", comm=comm)
        gather.landed(comm)
        s["r"] = _conv4(s["rx"], full("rnn_conv_w", l), vec("rnn_conv_b", l), f"conv4_{l}")
        comm = fwd_comm(35)
        s["ra"], s["ri"], s["a"], uu = _gates(s["r"], bd_a[l], bd_x[l], vec("rnn_b_a", l), vec("rnn_b_x", l),
                                              vec("rnn_lambda", l), f"gates_{l}", comm=comm)
        gather.landed(comm)
        comm = fwd_comm(80)
        s["h"], s["hp"], s["hg"] = _scan_fwd(s["a"], uu, s["rg"], f"scan_{l}", comm=comm)
        gather.landed(comm)
        w_cp, w_rp, w_out = full("conv_w_proj", l), full("rnn_w_proj", l), full("mix_w_out", l)
        comm = fwd_comm(85)
        s["yc"], s["yr"], s["m"], s["r2"], y2, s["y2b"] = _mix_out_ln(
            s["cs"], s["hg"], s["gc"], s["gr"], y1, w_cp, w_rp, w_out, vec("ln2_g", l), vec("ln2_b", l),
            f"mix_out_ln_{l}", comm=comm)
        gather.landed(comm)
        w_gu2 = full("ffn2_w_gu", l)
        comm = fwd_comm(105)
        s["gu2"], s["a2"] = _ffn_up(s["y2b"], w_gu2, f"ffn2_up_{l}", comm=comm)
        gather.landed(comm)
        w_down2 = full("ffn2_w_down", l)
        comm = fwd_comm(65)
        s["r3"], h, hb = _ffn_down_ln(s["a2"], w_down2, y2, vec("ln3_g", l), vec("ln3_b", l),
                                      f"ffn2_down_ln_{l}", comm=comm)
        gather.landed(comm)
        saved.append(s)

    scatter = _Queue("scatter", SCATTER_US_PER_MB)
    G = {n: [None] * L for n in WEIGHTS}

    def ready(n, l, grad):
        G[n][l] = grad
        scatter.push((n, l), _to_shards(n, grad))

    def bwd_comm(micros):
        return scatter.take(micros)

    loss_acc, dr3, drb3, G["ln3_g"][L - 1], G["ln3_b"][L - 1] = _loss_ln_bwd(
        h, target, saved[L - 1]["r3"], vec("ln3_g", L - 1), "loss_ln3_bwd")
    grad_x = small = None
    small_shapes = [W[n].shape for n in REPLICATED]
    small_rows = -(-sum(math.prod(s) for s in small_shapes) // (1024 * SUBLANES)) * SUBLANES
    for l in reversed(range(L)):
        s = saved[l]
        comm = bwd_comm(95)
        dgu2 = _ffn_bwd_a(drb3, full("ffn2_w_down", l), s["gu2"], f"ffn2_bwd_a_{l}", comm=comm)
        scatter.landed(comm)
        ready("ffn2_w_down", l, _mm_tn(s["a2"], drb3, f"ffn2_dw_down_{l}", scale=0.5))
        ready("ffn2_w_gu", l, _mm_tn(s["y2b"], dgu2, f"ffn2_dw_gu_{l}"))
        comm = bwd_comm(140)
        dr2, drb2, G["ln2_g"][l], G["ln2_b"][l] = _nt_res(
            [dgu2], full("ffn2_w_gu", l), dr3, f"ffn2_bwd_x_{l}", ln=(s["r2"], vec("ln2_g", l)), comm=comm)
        scatter.landed(comm)
        comm = bwd_comm(120)
        dyc, dyr, du_tail, dcs, dh = _mix_bwd1(
            drb2, full("mix_w_out", l), full("conv_w_proj", l), full("rnn_w_proj", l), s["gc"], s["gr"], s["yc"],
            s["yr"], s["rg"], s["h"], f"mix_bwd_out_{l}", comm=comm)
        scatter.landed(comm)
        ready("mix_w_out", l, _mm_tn(s["m"], drb2, f"mix_dw_out_{l}"))
        ready("conv_w_proj", l, _mm_tn(s["cs"], dyc, f"conv_dw_proj_{l}"))
        ready("rnn_w_proj", l, _mm_tn(s["hg"], dyr, f"rnn_dw_proj_{l}"))
        gsc = _scan_bwd(dh, s["a"], f"scan_bwd_{l}")
        comm = bwd_comm(160)
        dr_, dpa, dpx, G["rnn_lambda"][l], G["rnn_b_a"][l], G["rnn_b_x"][l] = _gates_bwd(
            gsc, s["hp"], s["ra"], s["ri"], s["r"], vec("rnn_lambda", l), bd_a[l], bd_x[l], f"gates_bwd_{l}",
            comm=comm)
        scatter.landed(comm)
        dwa, dwx = _band_dw(s["r"], dpa, dpx, f"rnn_dw_ax_{l}")
        G["rnn_w_a"][l] = _diag_blocks(dwa, RNN_BLOCKS)
        G["rnn_w_x"][l] = _diag_blocks(dwx, RNN_BLOCKS)
        drx, dw4, G["rnn_conv_b"][l] = _conv4_bwd(dr_, s["rx"], full("rnn_conv_w", l), f"conv4_bwd_{l}")
        ready("rnn_conv_w", l, dw4[:RNN_CONV_WIDTH])
        comm = bwd_comm(250)
        dcv, dcg, dw31, G["conv_dw_b"][l], G["conv_gn_g"][l], G["conv_gn_b"][l] = _conv31_bwd(
            dcs, s["cc"], s["c"], s["cv"], s["cg"], full("conv_dw_w", l), vec("conv_gn_g", l),
            vec("conv_gn_b", l), f"conv31_bwd_{l}", comm=comm)
        scatter.landed(comm)
        ready("conv_dw_w", l, dw31[:CONV_WIDTH])
        du = [dcv, dcg, drx, du_tail]
        ready("mix_w_in", l, jnp.concatenate(
            [_mm_tn(s["y1b"], piece, f"mix_dw_in_{l}_{p}") for p, piece in enumerate(du)], axis=1))
        comm = bwd_comm(175)
        dr1, drb1, G["ln1_g"][l], G["ln1_b"][l], G["mix_b_in"][l] = _nt_res(
            du, full("mix_w_in", l), dr2, f"mix_bwd_in_{l}", ln=(s["r1"], vec("ln1_g", l)), colsum=True, comm=comm)
        scatter.landed(comm)
        ready("ffn1_w_down", l, _mm_tn(s["a1"], drb1, f"ffn1_dw_down_{l}", scale=0.5))
        comm = bwd_comm(95)
        dgu1 = _ffn_bwd_a(drb1, full("ffn1_w_down", l), s["gu1"], f"ffn1_bwd_a_{l}", comm=comm)
        scatter.landed(comm)
        if l == 0:
            local_small = [jnp.stack([g.reshape(W[n].shape[1:]) for g in G[n]]) for n in REPLICATED]
            comm = {"kind": "gather", "arrs": [_pack(local_small, small_rows)[0]]}
            ready("ffn1_w_gu", l, _mm_tn(s["hb_in"], dgu1, f"ffn1_dw_gu_{l}", comm=comm))
            (small,) = comm["recv"]
        else:
            ready("ffn1_w_gu", l, _mm_tn(s["hb_in"], dgu1, f"ffn1_dw_gu_{l}"))
        if l > 0:
            comm = bwd_comm(130)
            dr3, drb3, G["ln3_g"][l - 1], G["ln3_b"][l - 1] = _nt_res(
                [dgu1], full("ffn1_w_gu", l), dr1, f"ffn1_bwd_x_{l}", ln=(saved[l - 1]["r3"], vec("ln3_g", l - 1)),
                comm=comm)
        else:
            comm = bwd_comm(1e9)
            (grad_x,) = _nt_res([dgu1], full("ffn1_w_gu", l), dr1, f"ffn1_bwd_x_{l}", comm=comm)
        scatter.landed(comm)
    scatter.flush("scatter_rest")

    loss = lax.psum(loss_acc[0, 0], ("x", "y", "c"))

    out = {}
    for n in SHARDED:
        recvs = [scatter.done[(n, l)] for l in range(L)]
        if n in TRANSPOSED:
            res = _reduce_adamw(recvs, *[jnp.swapaxes(a, 1, 2) for a in (W[n], M[n], V[n])], f"adamw_{n}")
            out[n] = [jnp.swapaxes(a, 1, 2) for a in res]
        else:
            out[n] = _reduce_adamw(recvs, W[n], M[n], V[n], f"adamw_{n}")
    packed = _reduce_adamw([small], _pack([W[n] for n in REPLICATED], small_rows),
                           _pack([M[n] for n in REPLICATED], small_rows),
                           _pack([V[n] for n in REPLICATED], small_rows), "adamw_small")
    unpacked = [_unpack(p, small_shapes) for p in packed]
    for i, n in enumerate(REPLICATED):
        out[n] = tuple(u[i] for u in unpacked)

    return (loss, grad_x.reshape(x.shape), *[out[n][0] for n in WEIGHTS], *[out[n][1] for n in WEIGHTS],
            *[out[n][2] for n in WEIGHTS], *[out[n][3] for n in WEIGHTS])
```

```python
import math

import jax
import jax.numpy as jnp
from jax import lax
from jax.experimental import pallas as pl
from jax.experimental.pallas import tpu as pltpu

F32 = jnp.float32
BF16 = jnp.bfloat16
MESH = pl.DeviceIdType.MESH

DEPTH = 2
ALPHA = (2 * DEPTH) ** 0.25
LN_EPS = 1e-5
RG_LRU_C = 8.0
CONV_WIDTH = 31
RNN_CONV_WIDTH = 4
RNN_BLOCKS = 16
N_DEV = 8
ADAM_LR, ADAM_B1, ADAM_B2, ADAM_EPS, ADAM_WD, ADAM_STEP = 0.001, 0.9, 0.999, 1e-08, 0.01, 10

LANES = 128
SUBLANES = 8
VMEM_LIMIT = 56 * 1024 * 1024
CONV_PAD = 32
RNN_PAD = 8
ADAMW_ROWS = 352
MM_TN_OUT_BYTES = 14 * 1024 * 1024
NORM_ROWS = 512
CONV_BLOCK = 32
GATHER_US_PER_MB = 43.0
SCATTER_US_PER_MB = 86.0

WEIGHTS = ['ffn1_w_gu', 'ffn1_w_down', 'ln1_g', 'ln1_b', 'mix_w_in', 'mix_b_in', 'conv_dw_w', 'conv_dw_b',
           'conv_gn_g', 'conv_gn_b', 'conv_w_proj', 'rnn_conv_w', 'rnn_conv_b', 'rnn_w_a', 'rnn_b_a', 'rnn_w_x',
           'rnn_b_x', 'rnn_lambda', 'rnn_w_proj', 'mix_w_out', 'ln2_g', 'ln2_b', 'ffn2_w_gu', 'ffn2_w_down',
           'ln3_g', 'ln3_b']
COL_SHARDED = ['ffn1_w_gu', 'mix_w_in', 'ffn2_w_gu', 'conv_dw_w', 'rnn_conv_w']
TRANSPOSED = ['ffn1_w_gu', 'mix_w_in', 'ffn2_w_gu']
ROW_SHARDED = ['ffn1_w_down', 'conv_w_proj', 'rnn_w_proj', 'mix_w_out', 'ffn2_w_down']
SHARDED = COL_SHARDED + ROW_SHARDED
BF16_ON_WIRE = ['ffn1_w_gu', 'mix_w_in', 'ffn2_w_gu', 'ffn1_w_down', 'conv_w_proj', 'rnn_w_proj', 'mix_w_out',
                'ffn2_w_down']
REPLICATED = [n for n in WEIGHTS if n not in SHARDED]
USE_ORDER = ['ffn1_w_gu', 'ffn1_w_down', 'mix_w_in', 'conv_dw_w', 'rnn_conv_w', 'conv_w_proj', 'rnn_w_proj',
             'mix_w_out', 'ffn2_w_gu', 'ffn2_w_down']


def _cp(n_axes=1):
    return pltpu.CompilerParams(dimension_semantics=("arbitrary",) * n_axes, vmem_limit_bytes=VMEM_LIMIT)


def _rows(tm, c):
    return pl.BlockSpec((tm, c), lambda i: (i, 0))


def _res(shape):
    nd = len(shape)
    return pl.BlockSpec(tuple(shape), lambda *_: (0,) * nd, pipeline_mode=pl.Buffered(1))


def _acc(shape):
    nd = len(shape)
    return pl.BlockSpec(tuple(shape), lambda *_: (0,) * nd)


def _tile(t, want):
    return want if t % want == 0 else t


def _sds(shape, dtype):
    return jax.ShapeDtypeStruct(tuple(shape), dtype)


def _mbytes(a):
    return a.size * a.dtype.itemsize / 1e6


def _ln_fwd(r, g, b):
    mu = jnp.mean(r, axis=-1, keepdims=True)
    xc = r - mu
    var = jnp.mean(xc * xc, axis=-1, keepdims=True)
    return xc * lax.rsqrt(var + LN_EPS) * g + b


def _ln_bwd(dy, r, g):
    mu = jnp.mean(r, axis=-1, keepdims=True)
    xc = r - mu
    var = jnp.mean(xc * xc, axis=-1, keepdims=True)
    rstd = lax.rsqrt(var + LN_EPS)
    xhat = xc * rstd
    dxh = dy * g
    m1 = jnp.mean(dxh, axis=-1, keepdims=True)
    m2 = jnp.mean(dxh * xhat, axis=-1, keepdims=True)
    dr = rstd * (dxh - m1 - xhat * m2)
    return dr, jnp.sum(dy * xhat, axis=0, keepdims=True), jnp.sum(dy, axis=0, keepdims=True)


def _sigmoid(x):
    return jax.nn.sigmoid(x)


_GELU_K = math.sqrt(2.0 / math.pi)


def _gelu(x):
    t = jnp.tanh(_GELU_K * (x + 0.044715 * x * x * x))
    return 0.5 * x * (1.0 + t), t


def _gelu_grad(x, t):
    return 0.5 * (1.0 + t) + 0.5 * x * (1.0 - t * t) * (_GELU_K * (1.0 + 3.0 * 0.044715 * x * x))


def _expm1(x):
    taylor = x * (1.0 + x * (0.5 + x * (1.0 / 6.0 + x * (1.0 / 24.0 + x * (1.0 / 120.0)))))
    return jnp.where(jnp.abs(x) < 0.03, taylor, jnp.exp(x) - 1.0)


def _softplus_neg(lam):
    return jnp.maximum(-lam, 0.0) + jnp.log1p(jnp.exp(-jnp.abs(lam)))


def _dot(a, b):
    return jnp.dot(a, b, preferred_element_type=F32)


def _dot_nt(a, b):
    return lax.dot_general(a, b, (((1,), (1,)), ((), ())), preferred_element_type=F32)


def _dot_tn(a, b):
    return lax.dot_general(a, b, (((0,), (0,)), ((), ())), preferred_element_type=F32)


def _chunks(width, cn):
    return [(j, min(cn, width - j)) for j in range(0, width, cn)]


def _band_chunks(width, block):
    out = []
    for c0, cw in _chunks(width, 256):
        lo = (c0 // block) * block
        hi = ((c0 + cw - 1) // block + 1) * block
        out.append((c0, cw, lo // LANES * LANES, min(width, -(-hi // LANES) * LANES)))
    return out


def _position():
    return lax.axis_index("x"), lax.axis_index("y"), lax.axis_index("c")


def _index(p):
    return 4 * p[0] + 2 * p[1] + p[2]


def _comm_out_shapes(kind, arrs):
    return [_sds((N_DEV,) + a.shape if kind == "gather" else a.shape, a.dtype) for a in arrs]


def _comm_scratch(n):
    return [pltpu.SemaphoreType.DMA((n, 7)), pltpu.SemaphoreType.DMA((n, 7)), pltpu.SemaphoreType.DMA((n,))]


def _comm_phases(kind, srcs, dsts, send_sems, recv_sems, local_sems):
    n = len(srcs)
    x, y, c = _position()
    me, sibling = (x, y, c), (x, y, 1 - c)
    chips = [(1 - x, y), (x, 1 - y), (1 - x, 1 - y)]

    if kind == "gather":
        def copy(a, k, block, to, src=None):
            dst = dsts[a].at[_index(block)]
            return pltpu.make_async_remote_copy(
                src_ref=dst if src is None else src, dst_ref=dst, send_sem=send_sems.at[a, k],
                recv_sem=recv_sems.at[a, k], device_id=to, device_id_type=MESH)

        def mine(a):
            return pltpu.make_async_copy(srcs[a], dsts[a].at[_index(me)], local_sems.at[a])

        def first(a):
            return [copy(a, 0, me, sibling, src=srcs[a])] + [
                copy(a, 1 + j, me, (*chip, c), src=srcs[a]) for j, chip in enumerate(chips)]

        def start():
            for a in range(n):
                mine(a).start()
            for a in range(n):
                for cp in first(a):
                    cp.start()

        def mid():
            for j, chip in enumerate(chips):
                for a in range(n):
                    copy(a, 1 + j, (*chip, c), me).wait_recv()
                    copy(a, 4 + j, (*chip, c), sibling).start()

        def end():
            for a in range(n):
                copy(a, 0, sibling, me).wait_recv()
            for j, chip in enumerate(chips):
                for a in range(n):
                    copy(a, 4 + j, (*chip, 1 - c), me).wait_recv()
            for a in range(n):
                for cp in first(a):
                    cp.wait_send()
                for j, chip in enumerate(chips):
                    copy(a, 4 + j, (*chip, c), sibling).wait_send()
                mine(a).wait()

        return start, mid, end

    def peer_of(k):
        return (1 - x if k & 4 else x, 1 - y if k & 2 else y, 1 - c if k & 1 else c)

    def own(a):
        return pltpu.make_async_copy(srcs[a].at[_index(me)], dsts[a].at[0], local_sems.at[a])

    def remote(a, k):
        peer = peer_of(k)
        return pltpu.make_async_remote_copy(
            src_ref=srcs[a].at[_index(peer)], dst_ref=dsts[a].at[k], send_sem=send_sems.at[a, k - 1],
            recv_sem=recv_sems.at[a, k - 1], device_id=peer, device_id_type=MESH)

    def start():
        for a in range(n):
            own(a).start()
        for k in range(1, N_DEV):
            for a in range(n):
                remote(a, k).start()

    def end():
        for k in range(1, N_DEV):
            for a in range(n):
                remote(a, k).wait()
        for a in range(n):
            own(a).wait()

    return start, (lambda: None), end


def _exchange(kind, arrs, name):
    n = len(arrs)
    hbm = pl.BlockSpec(memory_space=pl.ANY)

    def body(*refs):
        start, mid, end = _comm_phases(kind, refs[:n], refs[n:2 * n], *refs[2 * n:])
        start()
        mid()
        end()

    return pl.pallas_call(
        body, name=name, in_specs=[hbm] * n, out_specs=[hbm] * n, out_shape=_comm_out_shapes(kind, arrs),
        scratch_shapes=_comm_scratch(n))(*arrs)


def _pallas(body, *, name, grid, in_specs, out_specs, out_shape, args, scratch_shapes=(), comm=None):
    in_specs, out_specs, out_shape = list(in_specs), list(out_specs), list(out_shape)
    scratch_shapes = list(scratch_shapes)
    if not comm:
        return pl.pallas_call(body, name=name, grid=grid, in_specs=in_specs, out_specs=out_specs,
                              out_shape=out_shape, scratch_shapes=scratch_shapes,
                              compiler_params=_cp(len(grid)))(*args)
    arrs = comm["arrs"]
    ns, n_in, n_out, n_scr = len(arrs), len(in_specs), len(out_specs), len(scratch_shapes)
    hbm = pl.BlockSpec(memory_space=pl.ANY)
    total = math.prod(grid)

    def carrier(*refs):
        ins, srcs = refs[:n_in], refs[n_in:n_in + ns]
        outs, dsts = refs[n_in + ns:n_in + ns + n_out], refs[n_in + ns + n_out:n_in + 2 * ns + n_out]
        scr, sems = refs[n_in + 2 * ns + n_out:n_in + 2 * ns + n_out + n_scr], refs[n_in + 2 * ns + n_out + n_scr:]
        step = pl.program_id(0)
        for ax in range(1, len(grid)):
            step = step * grid[ax] + pl.program_id(ax)
        start, mid, end = _comm_phases(comm["kind"], srcs, dsts, *sems)
        pl.when(step == 0)(start)
        body(*ins, *outs, *scr)
        pl.when(step == total - 1)(mid)
        pl.when(step == total - 1)(end)

    res = pl.pallas_call(
        carrier, name=name, grid=grid, in_specs=in_specs + [hbm] * ns, out_specs=out_specs + [hbm] * ns,
        out_shape=out_shape + _comm_out_shapes(comm["kind"], arrs),
        scratch_shapes=scratch_shapes + _comm_scratch(ns), compiler_params=_cp(len(grid)))(*args, *arrs)
    comm["recv"] = res[n_out:]
    return res[:n_out]


def _ffn_up(xb, w, name, comm=None):
    T, D = xb.shape
    F = w.shape[0] // 2
    tm = _tile(T, 512)

    def body(x_ref, w_ref, gu_ref, a_ref):
        x = x_ref[...].astype(BF16)
        for j, cw in _chunks(F, 256):
            g = _dot_nt(x, w_ref[j:j + cw, :])
            u = _dot_nt(x, w_ref[F + j:F + j + cw, :])
            gu_ref[:, j:j + cw] = g.astype(BF16)
            gu_ref[:, F + j:F + j + cw] = u.astype(BF16)
            a_ref[:, j:j + cw] = (g * _sigmoid(g) * u).astype(BF16)

    return _pallas(
        body, name=name, grid=(T // tm,),
        in_specs=[_rows(tm, D), _res(w.shape)],
        out_specs=[_rows(tm, 2 * F), _rows(tm, F)],
        out_shape=[_sds((T, 2 * F), BF16), _sds((T, F), BF16)],
        args=(xb, w), comm=comm)


def _ffn_down_ln(a, wd, xres, g, b, name, comm=None):
    T, F = a.shape
    D = wd.shape[1]
    tm = _tile(T, 512)

    def body(a_ref, wd_ref, x_ref, g_ref, b_ref, r_ref, y_ref, yb_ref):
        r = ALPHA * x_ref[...] + 0.5 * _dot(a_ref[...], wd_ref[...])
        y = _ln_fwd(r, g_ref[...], b_ref[...])
        r_ref[...] = r
        y_ref[...] = y
        yb_ref[...] = y.astype(BF16)

    return _pallas(
        body, name=name, grid=(T // tm,),
        in_specs=[_rows(tm, F), _res(wd.shape), _rows(tm, D), _res((1, D)), _res((1, D))],
        out_specs=[_rows(tm, D), _rows(tm, D), _rows(tm, D)],
        out_shape=[_sds((T, D), F32), _sds((T, D), F32), _sds((T, D), BF16)],
        args=(a, wd, xres, g, b), comm=comm)


def _mix_in(hb, w, bias, d_rnn, name, comm=None):
    T, D = hb.shape
    R = d_rnn
    tm = _tile(T, 512)
    o_cv, o_cg, o_rx, o_rg, o_gc, o_gr = 0, D, 2 * D, 2 * D + R, 2 * D + 2 * R, 3 * D + 2 * R

    def body(x_ref, w_ref, b_ref, c_ref, cv_ref, cg_ref, rx_ref, rg_ref, gc_ref, gr_ref):
        x = x_ref[...]

        def seg(off, j, cw):
            return _dot_nt(x, w_ref[off + j:off + j + cw, :]) + b_ref[:, off + j:off + j + cw]

        for j, cw in _chunks(D, 256):
            cv = seg(o_cv, j, cw)
            cg = seg(o_cg, j, cw)
            cv_ref[:, j:j + cw] = cv.astype(BF16)
            cg_ref[:, j:j + cw] = cg.astype(BF16)
            c_ref[:, j:j + cw] = cv * _sigmoid(cg)
            gc_ref[:, j:j + cw] = seg(o_gc, j, cw).astype(BF16)
            gr_ref[:, j:j + cw] = seg(o_gr, j, cw).astype(BF16)
        for j, cw in _chunks(R, 256):
            rx_ref[:, j:j + cw] = seg(o_rx, j, cw)
            rg_ref[:, j:j + cw] = seg(o_rg, j, cw).astype(BF16)

    return _pallas(
        body, name=name, grid=(T // tm,),
        in_specs=[_rows(tm, D), _res(w.shape), _res(bias.shape)],
        out_specs=[_rows(tm, D), _rows(tm, D), _rows(tm, D), _rows(tm, R), _rows(tm, R), _rows(tm, D),
                   _rows(tm, D)],
        out_shape=[_sds((T, D), F32), _sds((T, D), BF16), _sds((T, D), BF16), _sds((T, R), F32),
                   _sds((T, R), BF16), _sds((T, D), BF16), _sds((T, D), BF16)],
        args=(hb, w, bias), comm=comm)


def _cols(t, rows=None):
    return pl.BlockSpec((t if rows is None else rows, LANES), lambda j: (0, j))


def _gn_stats(cc):
    mu = jnp.mean(cc, axis=-1, keepdims=True)
    xc = cc - mu
    var = jnp.mean(xc * xc, axis=-1, keepdims=True)
    rstd = lax.rsqrt(var + LN_EPS)
    return xc * rstd, rstd


def _tap_groups(offsets):
    groups = {}
    for k, s in enumerate(offsets):
        groups.setdefault(s % SUBLANES, []).append((k, s - s % SUBLANES))
    return sorted(groups.items())


def _shifted(win, phase):
    return win if phase == 0 else pltpu.roll(win, win.shape[0] - phase, 0)


def _tap_sum(win, wv, groups, rows):
    parts, t = [None] * 4, 0
    for phase, taps in groups:
        sh = _shifted(win, phase)
        for k, off in taps:
            term = wv[k:k + 1, :] * sh[off:off + rows, :]
            parts[t % 4] = term if parts[t % 4] is None else parts[t % 4] + term
            t += 1
    return (parts[0] + parts[1]) + (parts[2] + parts[3])


def _conv31_gn(c, w, bias, gg, gb, name, comm=None):
    T, D = c.shape
    K = w.shape[0]
    B, P, N = CONV_BLOCK, CONV_PAD, _tile(T, NORM_ROWS)
    assert D % LANES == 0 and T % B == 0 and K - 1 <= P
    groups = _tap_groups([P - (K - 1) + k for k in range(K)])

    def body(c_ref, w_ref, b_ref, gg_ref, gb_ref, cc_ref, cs_ref, xpad):
        xpad[0:P, :] = jnp.zeros((P, LANES), F32)
        xpad[P:P + T, :] = c_ref[...]
        wv = w_ref[...]
        bv, ggv, gbv = b_ref[...], gg_ref[...], gb_ref[...]

        def conv_step(i, carry):
            base = pl.multiple_of(i * B, B)
            win = xpad[pl.ds(base, B + P), :]
            cc_ref[pl.ds(base, B), :] = _tap_sum(win, wv, groups, B) + bv
            return carry

        lax.fori_loop(0, T // B, conv_step, 0)

        def norm_step(i, carry):
            base = pl.multiple_of(i * N, N)
            xhat, _ = _gn_stats(cc_ref[pl.ds(base, N), :])
            gn = xhat * ggv + gbv
            cs_ref[pl.ds(base, N), :] = (gn * _sigmoid(gn)).astype(BF16)
            return carry

        lax.fori_loop(0, T // N, norm_step, 0)

    return _pallas(
        body, name=name, grid=(D // LANES,),
        in_specs=[_cols(T), _cols(T, K), _cols(T, 1), _cols(T, 1), _cols(T, 1)],
        out_specs=[_cols(T), _cols(T)],
        out_shape=[_sds((T, D), F32), _sds((T, D), BF16)],
        scratch_shapes=[pltpu.VMEM((T + P, LANES), F32)],
        args=(c, w, bias, gg, gb), comm=comm)


def _conv4(rx, w, bias, name):
    T, C = rx.shape
    K = w.shape[0]
    B, P = 4 * CONV_BLOCK, RNN_PAD
    assert C % LANES == 0 and T % B == 0 and K - 1 <= P

    def body(x_ref, w_ref, b_ref, r_ref, xpad):
        xpad[0:P, :] = jnp.zeros((P, LANES), F32)
        xpad[P:P + T, :] = x_ref[...]
        wv, bv = w_ref[...], b_ref[...]

        def step(i, carry):
            base = pl.multiple_of(i * B, B)
            win = xpad[pl.ds(base, B + P), :]
            acc = jnp.zeros((B, LANES), F32)
            for k in range(K):
                s = P - (K - 1) + k
                acc = acc + wv[k:k + 1, :] * win[s:s + B, :]
            r_ref[pl.ds(base, B), :] = acc + bv
            return carry

        lax.fori_loop(0, T // B, step, 0)

    (r,) = _pallas(
        body, name=name, grid=(C // LANES,),
        in_specs=[_cols(T), _cols(T, K), _cols(T, 1)],
        out_specs=[_cols(T)],
        out_shape=[_sds((T, C), F32)],
        scratch_shapes=[pltpu.VMEM((T + RNN_PAD, LANES), F32)],
        args=(rx, w, bias))
    return r


def _gates(r, bda, bdx, b_a, b_x, lam, name, comm=None):
    T, C = r.shape
    tm = _tile(T, 512)
    chunks = _band_chunks(C, C // RNN_BLOCKS)

    def body(r_ref, wa_ref, wx_ref, ba_ref, bx_ref, lam_ref, ra_ref, ri_ref, a_ref, u_ref):
        for c0, cw, k0, k1 in chunks:
            cols = slice(c0, c0 + cw)
            rb = r_ref[:, k0:k1].astype(BF16)
            ra = _sigmoid(_dot(rb, wa_ref[k0:k1, cols]) + ba_ref[:, cols])
            ri = _sigmoid(_dot(rb, wx_ref[k0:k1, cols]) + bx_ref[:, cols])
            log_a = (-RG_LRU_C) * ra * _softplus_neg(lam_ref[:, cols])
            ra_ref[:, cols] = ra
            ri_ref[:, cols] = ri
            a_ref[:, cols] = jnp.exp(log_a)
            u_ref[:, cols] = jnp.sqrt(-_expm1(2.0 * log_a)) * (ri * r_ref[:, cols])

    return _pallas(
        body, name=name, grid=(T // tm,),
        in_specs=[_rows(tm, C), _res(bda.shape), _res(bdx.shape), _res((1, C)), _res((1, C)), _res((1, C))],
        out_specs=[_rows(tm, C)] * 4,
        out_shape=[_sds((T, C), F32)] * 4,
        args=(r, bda, bdx, b_a, b_x, lam), comm=comm)


def _scan_fwd(a, u, rg, name, comm=None):
    T, C = a.shape
    tt = _tile(T, 512)

    def body(a_ref, u_ref, rg_ref, h_ref, hp_ref, hg_ref, carry):
        @pl.when(pl.program_id(0) == 0)
        def _():
            carry[...] = jnp.zeros_like(carry)

        row = lax.broadcasted_iota(jnp.int32, (SUBLANES, C), 0)

        def group(i, hprev):
            base = pl.multiple_of(i * SUBLANES, SUBLANES)
            av = a_ref[pl.ds(base, SUBLANES), :]
            uv = u_ref[pl.ds(base, SUBLANES), :]
            for s in (1, 2, 4):
                a_s = jnp.where(row >= s, pltpu.roll(av, s, 0), 1.0)
                u_s = jnp.where(row >= s, pltpu.roll(uv, s, 0), 0.0)
                uv = av * u_s + uv
                av = av * a_s
            h = av * hprev + uv
            h_ref[pl.ds(base, SUBLANES), :] = h
            hp_ref[pl.ds(base, SUBLANES), :] = jnp.where(row >= 1, pltpu.roll(h, 1, 0), hprev)
            return h[SUBLANES - 1:SUBLANES, :]

        carry[...] = lax.fori_loop(0, tt // SUBLANES, group, carry[...])
        gel, _ = _gelu(rg_ref[...].astype(F32))
        hg_ref[...] = (h_ref[...] * gel).astype(BF16)

    return _pallas(
        body, name=name, grid=(T // tt,),
        in_specs=[_rows(tt, C)] * 3,
        out_specs=[_rows(tt, C)] * 3,
        out_shape=[_sds((T, C), F32), _sds((T, C), F32), _sds((T, C), BF16)],
        scratch_shapes=[pltpu.VMEM((1, C), F32)],
        args=(a, u, rg), comm=comm)


def _mix_out_ln(cs, hg, gc, gr, hres, wcp, wrp, wout, g, b, name, comm=None):
    T, D = cs.shape
    C = hg.shape[1]
    tm = _tile(T, 512)

    def body(cs_ref, hg_ref, gc_ref, gr_ref, h_ref, wcp_ref, wrp_ref, wo_ref, g_ref, b_ref,
             yc_ref, yr_ref, m_ref, r_ref, y_ref, yb_ref):
        yc = _dot(cs_ref[...], wcp_ref[...])
        yr = _dot(hg_ref[...], wrp_ref[...])
        m = (_sigmoid(gc_ref[...].astype(F32)) * yc + _sigmoid(gr_ref[...].astype(F32)) * yr).astype(BF16)
        r = ALPHA * h_ref[...] + _dot(m, wo_ref[...])
        y = _ln_fwd(r, g_ref[...], b_ref[...])
        yc_ref[...] = yc.astype(BF16)
        yr_ref[...] = yr.astype(BF16)
        m_ref[...] = m
        r_ref[...] = r
        y_ref[...] = y
        yb_ref[...] = y.astype(BF16)

    return _pallas(
        body, name=name, grid=(T // tm,),
        in_specs=[_rows(tm, D), _rows(tm, C), _rows(tm, D), _rows(tm, D), _rows(tm, D), _res(wcp.shape),
                  _res(wrp.shape), _res(wout.shape), _res((1, D)), _res((1, D))],
        out_specs=[_rows(tm, D)] * 6,
        out_shape=[_sds((T, D), BF16), _sds((T, D), BF16), _sds((T, D), BF16), _sds((T, D), F32),
                   _sds((T, D), F32), _sds((T, D), BF16)],
        args=(cs, hg, gc, gr, hres, wcp, wrp, wout, g, b), comm=comm)


def _loss_ln_bwd(y, target, r, g, name):
    T, D = y.shape
    tm = _tile(T, 512)

    def body(y_ref, t_ref, r_ref, g_ref, loss_ref, dr_ref, drb_ref, dg_ref, db_ref):
        @pl.when(pl.program_id(0) == 0)
        def _():
            loss_ref[...] = jnp.zeros_like(loss_ref)
            dg_ref[...] = jnp.zeros_like(dg_ref)
            db_ref[...] = jnp.zeros_like(db_ref)

        e = y_ref[...] - t_ref[...]
        loss_ref[...] += (0.5 / D) * jnp.sum(e * e)
        dr, dg, db = _ln_bwd(e * (1.0 / D), r_ref[...], g_ref[...])
        dr_ref[...] = dr
        drb_ref[...] = dr.astype(BF16)
        dg_ref[...] += dg
        db_ref[...] += db

    return _pallas(
        body, name=name, grid=(T // tm,),
        in_specs=[_rows(tm, D), _rows(tm, D), _rows(tm, D), _res((1, D))],
        out_specs=[_acc((SUBLANES, LANES)), _rows(tm, D), _rows(tm, D), _acc((1, D)), _acc((1, D))],
        out_shape=[_sds((SUBLANES, LANES), F32), _sds((T, D), F32), _sds((T, D), BF16), _sds((1, D), F32),
                   _sds((1, D), F32)],
        args=(y, target, r, g))


def _ffn_bwd_a(drb, wd, gu, name, comm=None):
    T, D = drb.shape
    F = wd.shape[0]
    tm = _tile(T, 512)

    def body(d_ref, wd_ref, gu_ref, o_ref):
        d = d_ref[...]
        for j, cw in _chunks(F, 256):
            da = 0.5 * _dot_nt(d, wd_ref[j:j + cw, :])
            gt = gu_ref[:, j:j + cw].astype(F32)
            up = gu_ref[:, F + j:F + j + cw].astype(F32)
            sg = _sigmoid(gt)
            o_ref[:, j:j + cw] = (da * up * (sg * (1.0 + gt * (1.0 - sg)))).astype(BF16)
            o_ref[:, F + j:F + j + cw] = (da * (gt * sg)).astype(BF16)

    (dgu,) = _pallas(
        body, name=name, grid=(T // tm,),
        in_specs=[_rows(tm, D), _res(wd.shape), _rows(tm, 2 * F)],
        out_specs=[_rows(tm, 2 * F)],
        out_shape=[_sds((T, 2 * F), BF16)],
        args=(drb, wd, gu), comm=comm)
    return dgu


def _nt_res(dus, w, dres, name, ln=None, colsum=False, comm=None):
    T = dus[0].shape[0]
    widths = [d.shape[1] for d in dus]
    offs = [sum(widths[:p]) for p in range(len(dus))]
    K, D, P = sum(widths), w.shape[1], len(dus)
    tm = _tile(T, 512)
    n_in = P + 2 + (2 if ln else 0)

    def body(*refs):
        du_refs, w_ref, dres_ref = refs[:P], refs[P], refs[P + 1]
        outs = refs[n_in:]
        dy = ALPHA * dres_ref[...]
        for du_ref, off, width in zip(du_refs, offs, widths):
            dy = dy + _dot(du_ref[...], w_ref[off:off + width, :])
        if ln:
            r_ref, g_ref = refs[P + 2:P + 4]

            @pl.when(pl.program_id(0) == 0)
            def _():
                outs[2][...] = jnp.zeros_like(outs[2])
                outs[3][...] = jnp.zeros_like(outs[3])

            dr, dg, db = _ln_bwd(dy, r_ref[...], g_ref[...])
            outs[0][...] = dr
            outs[1][...] = dr.astype(BF16)
            outs[2][...] += dg
            outs[3][...] += db
        else:
            outs[0][...] = dy
        if colsum:
            cs_ref = outs[-1]

            @pl.when(pl.program_id(0) == 0)
            def _():
                cs_ref[...] = jnp.zeros_like(cs_ref)

            for du_ref, off, width in zip(du_refs, offs, widths):
                cs_ref[:, off:off + width] += jnp.sum(du_ref[...].astype(F32), axis=0, keepdims=True)

    in_specs = [_rows(tm, width) for width in widths] + [_res(w.shape), _rows(tm, D)]
    args = list(dus) + [w, dres]
    if ln:
        in_specs += [_rows(tm, D), _res((1, D))]
        args += list(ln)
        out_specs = [_rows(tm, D), _rows(tm, D), _acc((1, D)), _acc((1, D))]
        out_shape = [_sds((T, D), F32), _sds((T, D), BF16), _sds((1, D), F32), _sds((1, D), F32)]
    else:
        out_specs = [_rows(tm, D)]
        out_shape = [_sds((T, D), F32)]
    if colsum:
        out_specs.append(_acc((1, K)))
        out_shape.append(_sds((1, K), F32))
    return _pallas(body, name=name, grid=(T // tm,), in_specs=in_specs, out_specs=out_specs, out_shape=out_shape,
                   args=args, comm=comm)


def _mm_tn(x, dy, name, scale=1.0, comm=None):
    T, K = x.shape
    N = dy.shape[1]
    tt = _tile(T, 1024)
    tn = next(N // d for d in range(1, N // LANES + 1)
              if N % d == 0 and (N // d) % LANES == 0 and K * (N // d) * 4 <= MM_TN_OUT_BYTES)
    nt = T // tt

    def body(x_ref, dy_ref, o_ref):
        t = pl.program_id(1)

        @pl.when(t == 0)
        def _():
            o_ref[...] = jnp.zeros_like(o_ref)

        o_ref[...] += _dot_tn(x_ref[...].astype(BF16), dy_ref[...])
        if scale != 1.0:
            @pl.when(t == nt - 1)
            def _():
                o_ref[...] = o_ref[...] * scale

    (out,) = _pallas(
        body, name=name, grid=(N // tn, nt),
        in_specs=[pl.BlockSpec((tt, K), lambda j, t: (t, 0)), pl.BlockSpec((tt, tn), lambda j, t: (t, j))],
        out_specs=[pl.BlockSpec((K, tn), lambda j, t: (0, j))],
        out_shape=[_sds((K, N), F32)],
        args=(x, dy), comm=comm)
    return out


def _mix_bwd1(drb, wout, wcp, wrp, gc, gr, yc, yr, rg, h, name, comm=None):
    T, D = drb.shape
    C = rg.shape[1]
    tm = _tile(T, 512)

    def body(d_ref, wo_ref, wcp_ref, wrp_ref, gc_ref, gr_ref, yc_ref, yr_ref, rg_ref, h_ref,
             dyc_ref, dyr_ref, tail_ref, dcs_ref, dh_ref):
        dm = _dot_nt(d_ref[...], wo_ref[...])
        sc = _sigmoid(gc_ref[...].astype(F32))
        sr = _sigmoid(gr_ref[...].astype(F32))
        dyc = (dm * sc).astype(BF16)
        dyr = (dm * sr).astype(BF16)
        dyc_ref[...] = dyc
        dyr_ref[...] = dyr
        tail_ref[:, C:C + D] = (dm * yc_ref[...].astype(F32) * sc * (1.0 - sc)).astype(BF16)
        tail_ref[:, C + D:] = (dm * yr_ref[...].astype(F32) * sr * (1.0 - sr)).astype(BF16)
        dcs_ref[...] = _dot_nt(dyc, wcp_ref[...])
        dhg = _dot_nt(dyr, wrp_ref[...])
        rgv = rg_ref[...].astype(F32)
        gel, t = _gelu(rgv)
        dh_ref[...] = dhg * gel
        tail_ref[:, 0:C] = (dhg * h_ref[...] * _gelu_grad(rgv, t)).astype(BF16)

    return _pallas(
        body, name=name, grid=(T // tm,),
        in_specs=[_rows(tm, D), _res(wout.shape), _res(wcp.shape), _res(wrp.shape), _rows(tm, D), _rows(tm, D),
                  _rows(tm, D), _rows(tm, D), _rows(tm, C), _rows(tm, C)],
        out_specs=[_rows(tm, D), _rows(tm, D), _rows(tm, C + 2 * D), _rows(tm, D), _rows(tm, C)],
        out_shape=[_sds((T, D), BF16), _sds((T, D), BF16), _sds((T, C + 2 * D), BF16), _sds((T, D), F32),
                   _sds((T, C), F32)],
        args=(drb, wout, wcp, wrp, gc, gr, yc, yr, rg, h), comm=comm)


def _scan_bwd(dh, a, name):
    T, C = dh.shape
    tt = _tile(T, 512)
    nt = T // tt
    ng = tt // SUBLANES

    def body(d_ref, a_ref, g_ref, carry):
        @pl.when(pl.program_id(0) == 0)
        def _():
            carry[...] = jnp.zeros_like(carry)

        row = lax.broadcasted_iota(jnp.int32, (SUBLANES, C), 0)

        def group(j, enext):
            base = pl.multiple_of((ng - 1 - j) * SUBLANES, SUBLANES)
            av = a_ref[pl.ds(base, SUBLANES), :]
            dv = d_ref[pl.ds(base, SUBLANES), :]
            bv = av * dv
            for s in (1, 2, 4):
                keep = row < SUBLANES - s
                a_s = jnp.where(keep, pltpu.roll(av, SUBLANES - s, 0), 1.0)
                b_s = jnp.where(keep, pltpu.roll(bv, SUBLANES - s, 0), 0.0)
                bv = av * b_s + bv
                av = av * a_s
            e = av * enext + bv
            e_up = jnp.where(row < SUBLANES - 1, pltpu.roll(e, SUBLANES - 1, 0), enext)
            g_ref[pl.ds(base, SUBLANES), :] = dv + e_up
            return e[0:1, :]

        carry[...] = lax.fori_loop(0, ng, group, carry[...])

    rev = pl.BlockSpec((tt, C), lambda i: (nt - 1 - i, 0))
    (g,) = _pallas(
        body, name=name, grid=(nt,), in_specs=[rev, rev], out_specs=[rev],
        out_shape=[_sds((T, C), F32)],
        scratch_shapes=[pltpu.VMEM((1, C), F32)],
        args=(dh, a))
    return g


def _gates_bwd(g, hp, ra, ri, r, lam, bda, bdx, name, comm=None):
    T, C = g.shape
    tm = _tile(T, 512)
    nt = T // tm
    chunks = _band_chunks(C, C // RNN_BLOCKS)

    def body(g_ref, hp_ref, ra_ref, ri_ref, r_ref, lam_ref, wa_ref, wx_ref,
             dr_ref, dpa_ref, dpx_ref, dlam_ref, dba_ref, dbx_ref):
        @pl.when(pl.program_id(0) == 0)
        def _():
            dlam_ref[...] = jnp.zeros_like(dlam_ref)
            dba_ref[...] = jnp.zeros_like(dba_ref)
            dbx_ref[...] = jnp.zeros_like(dbx_ref)

        for c0, cw, _, _ in chunks:
            cols = slice(c0, c0 + cw)
            gv, rav, riv, rv = g_ref[:, cols], ra_ref[:, cols], ri_ref[:, cols], r_ref[:, cols]
            sp = _softplus_neg(lam_ref[:, cols])
            log_a = (-RG_LRU_C) * rav * sp
            av = jnp.exp(log_a)
            mult = jnp.sqrt(-_expm1(2.0 * log_a))
            d_mult = gv * riv * rv
            d_i = gv * mult * rv
            d_loga = gv * hp_ref[:, cols] * av - d_mult * (av * av) / mult
            d_ra = d_loga * ((-RG_LRU_C) * sp)
            dpa = d_ra * rav * (1.0 - rav)
            dpx = d_i * riv * (1.0 - riv)
            dpa_ref[:, cols] = dpa.astype(BF16)
            dpx_ref[:, cols] = dpx.astype(BF16)
            dr_ref[:, cols] = gv * mult * riv
            dlam_ref[:, cols] += jnp.sum(d_loga * ((-RG_LRU_C) * rav), axis=0, keepdims=True)
            dba_ref[:, cols] += jnp.sum(dpa, axis=0, keepdims=True)
            dbx_ref[:, cols] += jnp.sum(dpx, axis=0, keepdims=True)
        for c0, cw, k0, k1 in chunks:
            cols = slice(c0, c0 + cw)
            dr_ref[:, k0:k1] += (_dot_nt(dpa_ref[:, cols], wa_ref[k0:k1, cols])
                                 + _dot_nt(dpx_ref[:, cols], wx_ref[k0:k1, cols]))

        @pl.when(pl.program_id(0) == nt - 1)
        def _():
            dlam_ref[...] = dlam_ref[...] * (-_sigmoid(-lam_ref[...]))

    return _pallas(
        body, name=name, grid=(nt,),
        in_specs=[_rows(tm, C)] * 5 + [_res((1, C)), _res(bda.shape), _res(bdx.shape)],
        out_specs=[_rows(tm, C)] * 3 + [_acc((1, C))] * 3,
        out_shape=[_sds((T, C), F32), _sds((T, C), BF16), _sds((T, C), BF16)] + [_sds((1, C), F32)] * 3,
        args=(g, hp, ra, ri, r, lam, bda, bdx), comm=comm)


def _band_dw(r, dpa, dpx, name):
    T, C = r.shape
    tt = _tile(T, 512)
    chunks = _band_chunks(C, C // RNN_BLOCKS)

    def body(r_ref, dpa_ref, dpx_ref, oa_ref, ox_ref):
        @pl.when(pl.program_id(0) == 0)
        def _():
            oa_ref[...] = jnp.zeros_like(oa_ref)
            ox_ref[...] = jnp.zeros_like(ox_ref)

        for c0, cw, k0, k1 in chunks:
            cols = slice(c0, c0 + cw)
            rb = r_ref[:, k0:k1].astype(BF16)
            oa_ref[k0:k1, cols] += _dot_tn(rb, dpa_ref[:, cols])
            ox_ref[k0:k1, cols] += _dot_tn(rb, dpx_ref[:, cols])

    return _pallas(
        body, name=name, grid=(T // tt,),
        in_specs=[_rows(tt, C)] * 3,
        out_specs=[_acc((C, C)), _acc((C, C))],
        out_shape=[_sds((C, C), F32), _sds((C, C), F32)],
        args=(r, dpa, dpx))


def _conv4_bwd(dr, rx, w, name):
    T, C = dr.shape
    K = w.shape[0]
    B, P = CONV_BLOCK, RNN_PAD
    assert K <= SUBLANES
    d_groups = _tap_groups([K - 1 - k for k in range(K)])
    x_groups = _tap_groups([P - (K - 1) + k for k in range(K)])

    def fold(v):
        part = v[0:SUBLANES, :]
        for q in range(1, B // SUBLANES):
            part = part + v[q * SUBLANES:(q + 1) * SUBLANES, :]
        return part

    def body(d_ref, x_ref, w_ref, dx_ref, dw_ref, db_ref, dpad, xpad, dwacc, dbacc):
        dpad[0:T, :] = d_ref[...]
        dpad[T:T + P, :] = jnp.zeros((P, LANES), F32)
        xpad[0:P, :] = jnp.zeros((P, LANES), F32)
        xpad[P:P + T, :] = x_ref[...]
        dwacc[...] = jnp.zeros_like(dwacc)
        dbacc[...] = jnp.zeros_like(dbacc)
        wv = w_ref[...]

        def step(i, carry):
            base = pl.multiple_of(i * B, B)
            dwin = dpad[pl.ds(base, B + P), :]
            xwin = xpad[pl.ds(base, B + P), :]
            dcur = dwin[0:B, :]
            dx_ref[pl.ds(base, B), :] = _tap_sum(dwin, wv, d_groups, B).astype(BF16)
            for phase, taps in x_groups:
                sh = _shifted(xwin, phase)
                for k, off in taps:
                    dwacc[k * SUBLANES:(k + 1) * SUBLANES, :] += fold(dcur * sh[off:off + B, :])
            dbacc[...] += fold(dcur)
            return carry

        lax.fori_loop(0, T // B, step, 0)
        dw_ref[...] = jnp.zeros_like(dw_ref)
        for k in range(K):
            dw_ref[k:k + 1, :] = jnp.sum(dwacc[k * SUBLANES:(k + 1) * SUBLANES, :], axis=0, keepdims=True)
        db_ref[...] = jnp.sum(dbacc[...], axis=0, keepdims=True)

    return _pallas(
        body, name=name, grid=(C // LANES,),
        in_specs=[_cols(T), _cols(T), _cols(T, K)],
        out_specs=[_cols(T), _cols(T, SUBLANES), _cols(T, 1)],
        out_shape=[_sds((T, C), BF16), _sds((SUBLANES, C), F32), _sds((1, C), F32)],
        scratch_shapes=[pltpu.VMEM((T + P, LANES), F32), pltpu.VMEM((T + P, LANES), F32),
                        pltpu.VMEM((SUBLANES * SUBLANES, LANES), F32), pltpu.VMEM((SUBLANES, LANES), F32)],
        args=(dr, rx, w))


def _conv31_bwd(dcs, cc, c, cv, cg, w, gg, gb, name, comm=None):
    T, D = dcs.shape
    K = w.shape[0]
    R, B, P = _tile(T, NORM_ROWS), CONV_BLOCK, CONV_PAD
    d_groups = _tap_groups([K - 1 - k for k in range(K)])
    x_groups = _tap_groups([P - (K - 1) + k for k in range(K)])

    def body(dcs_ref, cc_ref, c_ref, cv_ref, cg_ref, w_ref, gg_ref, gb_ref,
             dcv_ref, dcg_ref, dw_ref, db_ref, dgg_ref, dgb_ref, dpad, xpad, dwacc):
        dpad[T:T + P, :] = jnp.zeros((P, LANES), F32)
        xpad[0:P, :] = jnp.zeros((P, LANES), F32)
        xpad[P:P + T, :] = c_ref[...]
        dwacc[...] = jnp.zeros_like(dwacc)
        db_ref[...] = jnp.zeros_like(db_ref)
        dgg_ref[...] = jnp.zeros_like(dgg_ref)
        dgb_ref[...] = jnp.zeros_like(dgb_ref)
        wv, ggv, gbv = w_ref[...], gg_ref[...], gb_ref[...]

        def norm_step(i, carry):
            base = pl.multiple_of(i * R, R)
            xhat, rstd = _gn_stats(cc_ref[pl.ds(base, R), :])
            gn = xhat * ggv + gbv
            sg = _sigmoid(gn)
            dgn = dcs_ref[pl.ds(base, R), :] * (sg * (1.0 + gn * (1.0 - sg)))
            dgg_ref[...] += jnp.sum(dgn * xhat, axis=0, keepdims=True)
            dgb_ref[...] += jnp.sum(dgn, axis=0, keepdims=True)
            dxh = dgn * ggv
            m1 = jnp.mean(dxh, axis=-1, keepdims=True)
            m2 = jnp.mean(dxh * xhat, axis=-1, keepdims=True)
            dcc = rstd * (dxh - m1 - xhat * m2)
            dpad[pl.ds(base, R), :] = dcc
            db_ref[...] += jnp.sum(dcc, axis=0, keepdims=True)
            return carry

        lax.fori_loop(0, T // R, norm_step, 0)

        def conv_step(i, carry):
            base = pl.multiple_of(i * B, B)
            dwin = dpad[pl.ds(base, B + P), :]
            xwin = xpad[pl.ds(base, B + P), :]
            dcur = dwin[0:B, :]
            acc = _tap_sum(dwin, wv, d_groups, B)
            for phase, taps in x_groups:
                sh = _shifted(xwin, phase)
                for k, off in taps:
                    prod = dcur * sh[off:off + B, :]
                    part = prod[0:SUBLANES, :]
                    for q in range(1, B // SUBLANES):
                        part = part + prod[q * SUBLANES:(q + 1) * SUBLANES, :]
                    dwacc[k * SUBLANES:(k + 1) * SUBLANES, :] += part
            cgv = cg_ref[pl.ds(base, B), :].astype(F32)
            cvv = cv_ref[pl.ds(base, B), :].astype(F32)
            sg = _sigmoid(cgv)
            dcv_ref[pl.ds(base, B), :] = (acc * sg).astype(BF16)
            dcg_ref[pl.ds(base, B), :] = (acc * cvv * sg * (1.0 - sg)).astype(BF16)
            return carry

        lax.fori_loop(0, T // B, conv_step, 0)
        dw_ref[...] = jnp.zeros_like(dw_ref)
        for k in range(K):
            dw_ref[k:k + 1, :] = jnp.sum(dwacc[k * SUBLANES:(k + 1) * SUBLANES, :], axis=0, keepdims=True)

    return _pallas(
        body, name=name, grid=(D // LANES,),
        in_specs=[_cols(T)] * 5 + [_cols(T, K), _cols(T, 1), _cols(T, 1)],
        out_specs=[_cols(T), _cols(T), _cols(T, P), _cols(T, 1), _cols(T, 1), _cols(T, 1)],
        out_shape=[_sds((T, D), BF16), _sds((T, D), BF16), _sds((P, D), F32), _sds((1, D), F32),
                   _sds((1, D), F32), _sds((1, D), F32)],
        scratch_shapes=[pltpu.VMEM((T + P, LANES), F32), pltpu.VMEM((T + P, LANES), F32),
                        pltpu.VMEM((P * SUBLANES, LANES), F32)],
        args=(dcs, cc, c, cv, cg, w, gg, gb), comm=comm)


def _reduce_adamw(recvs, w, m, v, name):
    L, R, C = w.shape
    assert len(recvs) == L
    tr = next((R // d for d in range(1, R // SUBLANES + 1)
               if R % d == 0 and (R // d) % SUBLANES == 0 and R // d <= ADAMW_ROWS), R)
    nr = R // tr
    c1 = 1.0 - ADAM_B1 ** ADAM_STEP
    c2 = 1.0 - ADAM_B2 ** ADAM_STEP

    def body(*refs):
        recv_refs = refs[:L]
        w_ref, m_ref, v_ref, g_ref, d_ref, mo_ref, vo_ref = refs[L:]

        def update(recv_ref):
            g = recv_ref[0].astype(F32)
            for k in range(1, N_DEV):
                g = g + recv_ref[k].astype(F32)
            mn = ADAM_B1 * m_ref[0] + (1.0 - ADAM_B1) * g
            vn = ADAM_B2 * v_ref[0] + (1.0 - ADAM_B2) * (g * g)
            g_ref[0] = g
            mo_ref[0] = mn
            vo_ref[0] = vn
            d_ref[0] = (-ADAM_LR) * ((mn / c1) / (jnp.sqrt(vn / c2) + ADAM_EPS) + ADAM_WD * w_ref[0])

        for l in range(L):
            pl.when(pl.program_id(0) == l)(lambda l=l: update(recv_refs[l]))

    def recv_spec(l):
        return pl.BlockSpec((N_DEV, tr, C), lambda j, i: (0, jnp.where(j == l, i, jnp.where(j < l, 0, nr - 1)), 0))

    blk = pl.BlockSpec((1, tr, C), lambda j, i: (j, i, 0))
    return _pallas(
        body, name=name, grid=(L, nr),
        in_specs=[recv_spec(l) for l in range(L)] + [blk, blk, blk],
        out_specs=[blk] * 4,
        out_shape=[_sds((L, R, C), F32)] * 4,
        args=(*recvs, w, m, v))


def _unshard(name, gathered):
    n, r, c = gathered.shape
    if name in COL_SHARDED and name not in TRANSPOSED:
        return gathered.transpose(1, 0, 2).reshape(r, n * c)
    return gathered.reshape(n * r, c)


def _to_shards(name, full):
    R, C = full.shape
    wire = BF16 if name in BF16_ON_WIRE else F32
    if name in TRANSPOSED:
        return full.reshape(R, N_DEV, C // N_DEV).transpose(1, 2, 0).astype(wire)
    if name in COL_SHARDED:
        return full.reshape(R, N_DEV, C // N_DEV).transpose(1, 0, 2).astype(wire)
    return full.reshape(N_DEV, R // N_DEV, C).astype(wire)


def _block_diag(w):
    H, b, _ = w.shape
    eye = jnp.eye(H, dtype=w.dtype)
    return (w[:, :, None, :] * eye[:, None, :, None]).reshape(H * b, H * b)


def _diag_blocks(dense, H):
    b = dense.shape[0] // H
    eye = jnp.eye(H, dtype=dense.dtype)
    return jnp.sum(dense.reshape(H, b, H, b) * eye[:, None, :, None], axis=2)


def _pack(arrs, rows):
    flat = jnp.concatenate([a.reshape(-1) for a in arrs])
    return jnp.pad(flat, (0, rows * 1024 - flat.shape[0])).reshape(1, rows, 1024)


def _unpack(packed, shapes):
    flat = packed.reshape(-1)
    out, off = [], 0
    for s in shapes:
        n = math.prod(s)
        out.append(flat[off:off + n].reshape(s))
        off += n
    return out


class _Queue:
    def __init__(self, kind, us_per_mb):
        self.kind, self.us_per_mb, self.items, self.done = kind, us_per_mb, [], {}

    def push(self, key, arr):
        self.items.append((key, arr))

    def mb(self, arr):
        return _mbytes(arr) / (N_DEV if self.kind == "scatter" else 1)

    def take(self, micros):
        taken, budget = [], micros / self.us_per_mb
        while self.items and (not taken or self.mb(self.items[0][1]) <= budget):
            budget -= self.mb(self.items[0][1])
            taken.append(self.items.pop(0))
        return {"kind": self.kind, "keys": [k for k, _ in taken], "arrs": [a for _, a in taken]} if taken else None

    def landed(self, comm):
        if comm:
            self.done.update(zip(comm["keys"], comm["recv"]))

    def flush(self, name, upto=None):
        n = len(self.items)
        if upto is not None:
            keys = [k for k, _ in self.items]
            n = keys.index(upto) + 1 if upto in keys else 0
        if n:
            taken, self.items = self.items[:n], self.items[n:]
            self.done.update(zip([k for k, _ in taken], _exchange(self.kind, [a for _, a in taken], name)))


def kernel(x, ffn1_w_gu, ffn1_w_down, ln1_g, ln1_b, mix_w_in, mix_b_in, conv_dw_w, conv_dw_b, conv_gn_g, conv_gn_b, conv_w_proj, rnn_conv_w, rnn_conv_b, rnn_w_a, rnn_b_a, rnn_w_x, rnn_b_x, rnn_lambda, rnn_w_proj, mix_w_out, ln2_g, ln2_b, ffn2_w_gu, ffn2_w_down, ln3_g, ln3_b, loss_target, m_ffn1_w_gu, m_ffn1_w_down, m_ln1_g, m_ln1_b, m_mix_w_in, m_mix_b_in, m_conv_dw_w, m_conv_dw_b, m_conv_gn_g, m_conv_gn_b, m_conv_w_proj, m_rnn_conv_w, m_rnn_conv_b, m_rnn_w_a, m_rnn_b_a, m_rnn_w_x, m_rnn_b_x, m_rnn_lambda, m_rnn_w_proj, m_mix_w_out, m_ln2_g, m_ln2_b, m_ffn2_w_gu, m_ffn2_w_down, m_ln3_g, m_ln3_b, v_ffn1_w_gu, v_ffn1_w_down, v_ln1_g, v_ln1_b, v_mix_w_in, v_mix_b_in, v_conv_dw_w, v_conv_dw_b, v_conv_gn_g, v_conv_gn_b, v_conv_w_proj, v_rnn_conv_w, v_rnn_conv_b, v_rnn_w_a, v_rnn_b_a, v_rnn_w_x, v_rnn_b_x, v_rnn_lambda, v_rnn_w_proj, v_mix_w_out, v_ln2_g, v_ln2_b, v_ffn2_w_gu, v_ffn2_w_down, v_ln3_g, v_ln3_b):
    given = dict(locals())
    W = {n: given[n] for n in WEIGHTS}
    M = {n: given["m_" + n] for n in WEIGHTS}
    V = {n: given["v_" + n] for n in WEIGHTS}
    T, D = x.shape[1], x.shape[2]
    L = DEPTH
    x2 = x.reshape(T, D)
    target = loss_target.reshape(T, D)
    d_rnn = rnn_conv_b.shape[1]

    gather = _Queue("gather", GATHER_US_PER_MB)
    for l in range(L):
        for n in USE_ORDER:
            shard = W[n][l].T if n in TRANSPOSED else W[n][l]
            gather.push((n, l), shard.astype(BF16) if n in BF16_ON_WIRE else shard)
    gather.flush("gather_first", upto=("ffn1_w_gu", 0))
    full_cache = {}

    def full(n, l):
        if (n, l) not in full_cache:
            gather.flush(f"gather_{n}_{l}", upto=(n, l))
            full_cache[(n, l)] = _unshard(n, gather.done[(n, l)])
        return full_cache[(n, l)]

    def fwd_comm(micros):
        return gather.take(micros)

    bd_a = [_block_diag(rnn_w_a[l]).astype(BF16) for l in range(L)]
    bd_x = [_block_diag(rnn_w_x[l]).astype(BF16) for l in range(L)]

    def vec(name, l):
        return W[name][l:l + 1]

    saved = []
    h, hb = x2, x2
    for l in range(L):
        s = {"hb_in": hb}
        w_gu = full("ffn1_w_gu", l)
        comm = fwd_comm(105)
        s["gu1"], s["a1"] = _ffn_up(hb, w_gu, f"ffn1_up_{l}", comm=comm)
        gather.landed(comm)
        w_down = full("ffn1_w_down", l)
        comm = fwd_comm(65)
        s["r1"], y1, s["y1b"] = _ffn_down_ln(s["a1"], w_down, h, vec("ln1_g", l), vec("ln1_b", l),
                                              f"ffn1_down_ln_{l}", comm=comm)
        gather.landed(comm)
        w_in = full("mix_w_in", l)
        comm = fwd_comm(5)
        s["c"], s["cv"], s["cg"], s["rx"], s["rg"], s["gc"], s["gr"] = _mix_in(
            s["y1b"], w_in, vec("mix_b_in", l), d_rnn, f"mix_in_{l}", comm=comm)
        gather.landed(comm)
        w_dw = full("conv_dw_w", l)
        comm = fwd_comm(195)
        s["cc"], s["cs"] = _conv31_gn(s["c"], w_dw, vec("conv_dw_b", l), vec("conv_gn_g", l),
                                      vec("conv_gn_b", l), f"conv31_gn_{l}", comm=comm)
        gather.landed(comm)
        s["r"] = _conv4(s["rx"], full("rnn_conv_w", l), vec("rnn_conv_b", l), f"conv4_{l}")
        comm = fwd_comm(35)
        s["ra"], s["ri"], s["a"], uu = _gates(s["r"], bd_a[l], bd_x[l], vec("rnn_b_a", l), vec("rnn_b_x", l),
                                              vec("rnn_lambda", l), f"gates_{l}", comm=comm)
        gather.landed(comm)
        comm = fwd_comm(80)
        s["h"], s["hp"], s["hg"] = _scan_fwd(s["a"], uu, s["rg"], f"scan_{l}", comm=comm)
        gather.landed(comm)
        w_cp, w_rp, w_out = full("conv_w_proj", l), full("rnn_w_proj", l), full("mix_w_out", l)
        comm = fwd_comm(85)
        s["yc"], s["yr"], s["m"], s["r2"], y2, s["y2b"] = _mix_out_ln(
            s["cs"], s["hg"], s["gc"], s["gr"], y1, w_cp, w_rp, w_out, vec("ln2_g", l), vec("ln2_b", l),
            f"mix_out_ln_{l}", comm=comm)
        gather.landed(comm)
        w_gu2 = full("ffn2_w_gu", l)
        comm = fwd_comm(105)
        s["gu2"], s["a2"] = _ffn_up(s["y2b"], w_gu2, f"ffn2_up_{l}", comm=comm)
        gather.landed(comm)
        w_down2 = full("ffn2_w_down", l)
        comm = fwd_comm(65)
        s["r3"], h, hb = _ffn_down_ln(s["a2"], w_down2, y2, vec("ln3_g", l), vec("ln3_b", l),
                                      f"ffn2_down_ln_{l}", comm=comm)
        gather.landed(comm)
        saved.append(s)

    scatter = _Queue("scatter", SCATTER_US_PER_MB)
    G = {n: [None] * L for n in WEIGHTS}

    def ready(n, l, grad):
        G[n][l] = grad
        scatter.push((n, l), _to_shards(n, grad))

    def bwd_comm(micros):
        return scatter.take(micros)

    loss_acc, dr3, drb3, G["ln3_g"][L - 1], G["ln3_b"][L - 1] = _loss_ln_bwd(
        h, target, saved[L - 1]["r3"], vec("ln3_g", L - 1), "loss_ln3_bwd")
    grad_x = small = None
    small_shapes = [W[n].shape for n in REPLICATED]
    small_rows = -(-sum(math.prod(s) for s in small_shapes) // (1024 * SUBLANES)) * SUBLANES
    for l in reversed(range(L)):
        s = saved[l]
        comm = bwd_comm(95)
        dgu2 = _ffn_bwd_a(drb3, full("ffn2_w_down", l), s["gu2"], f"ffn2_bwd_a_{l}", comm=comm)
        scatter.landed(comm)
        ready("ffn2_w_down", l, _mm_tn(s["a2"], drb3, f"ffn2_dw_down_{l}", scale=0.5))
        ready("ffn2_w_gu", l, _mm_tn(s["y2b"], dgu2, f"ffn2_dw_gu_{l}"))
        comm = bwd_comm(140)
        dr2, drb2, G["ln2_g"][l], G["ln2_b"][l] = _nt_res(
            [dgu2], full("ffn2_w_gu", l), dr3, f"ffn2_bwd_x_{l}", ln=(s["r2"], vec("ln2_g", l)), comm=comm)
        scatter.landed(comm)
        comm = bwd_comm(120)
        dyc, dyr, du_tail, dcs, dh = _mix_bwd1(
            drb2, full("mix_w_out", l), full("conv_w_proj", l), full("rnn_w_proj", l), s["gc"], s["gr"], s["yc"],
            s["yr"], s["rg"], s["h"], f"mix_bwd_out_{l}", comm=comm)
        scatter.landed(comm)
        ready("mix_w_out", l, _mm_tn(s["m"], drb2, f"mix_dw_out_{l}"))
        ready("conv_w_proj", l, _mm_tn(s["cs"], dyc, f"conv_dw_proj_{l}"))
        ready("rnn_w_proj", l, _mm_tn(s["hg"], dyr, f"rnn_dw_proj_{l}"))
        gsc = _scan_bwd(dh, s["a"], f"scan_bwd_{l}")
        comm = bwd_comm(160)
        dr_, dpa, dpx, G["rnn_lambda"][l], G["rnn_b_a"][l], G["rnn_b_x"][l] = _gates_bwd(
            gsc, s["hp"], s["ra"], s["ri"], s["r"], vec("rnn_lambda", l), bd_a[l], bd_x[l], f"gates_bwd_{l}",
            comm=comm)
        scatter.landed(comm)
        dwa, dwx = _band_dw(s["r"], dpa, dpx, f"rnn_dw_ax_{l}")
        G["rnn_w_a"][l] = _diag_blocks(dwa, RNN_BLOCKS)
        G["rnn_w_x"][l] = _diag_blocks(dwx, RNN_BLOCKS)
        drx, dw4, G["rnn_conv_b"][l] = _conv4_bwd(dr_, s["rx"], full("rnn_conv_w", l), f"conv4_bwd_{l}")
        ready("rnn_conv_w", l, dw4[:RNN_CONV_WIDTH])
        comm = bwd_comm(250)
        dcv, dcg, dw31, G["conv_dw_b"][l], G["conv_gn_g"][l], G["conv_gn_b"][l] = _conv31_bwd(
            dcs, s["cc"], s["c"], s["cv"], s["cg"], full("conv_dw_w", l), vec("conv_gn_g", l),
            vec("conv_gn_b", l), f"conv31_bwd_{l}", comm=comm)
        scatter.landed(comm)
        ready("conv_dw_w", l, dw31[:CONV_WIDTH])
        du = [dcv, dcg, drx, du_tail]
        ready("mix_w_in", l, jnp.concatenate(
            [_mm_tn(s["y1b"], piece, f"mix_dw_in_{l}_{p}") for p, piece in enumerate(du)], axis=1))
        comm = bwd_comm(175)
        dr1, drb1, G["ln1_g"][l], G["ln1_b"][l], G["mix_b_in"][l] = _nt_res(
            du, full("mix_w_in", l), dr2, f"mix_bwd_in_{l}", ln=(s["r1"], vec("ln1_g", l)), colsum=True, comm=comm)
        scatter.landed(comm)
        ready("ffn1_w_down", l, _mm_tn(s["a1"], drb1, f"ffn1_dw_down_{l}", scale=0.5))
        comm = bwd_comm(95)
        dgu1 = _ffn_bwd_a(drb1, full("ffn1_w_down", l), s["gu1"], f"ffn1_bwd_a_{l}", comm=comm)
        scatter.landed(comm)
        if l == 0:
            local_small = [jnp.stack([g.reshape(W[n].shape[1:]) for g in G[n]]) for n in REPLICATED]
            comm = {"kind": "gather", "arrs": [_pack(local_small, small_rows)[0]]}
            ready("ffn1_w_gu", l, _mm_tn(s["hb_in"], dgu1, f"ffn1_dw_gu_{l}", comm=comm))
            (small,) = comm["recv"]
        else:
            ready("ffn1_w_gu", l, _mm_tn(s["hb_in"], dgu1, f"ffn1_dw_gu_{l}"))
        if l > 0:
            comm = bwd_comm(130)
            dr3, drb3, G["ln3_g"][l - 1], G["ln3_b"][l - 1] = _nt_res(
                [dgu1], full("ffn1_w_gu", l), dr1, f"ffn1_bwd_x_{l}", ln=(saved[l - 1]["r3"], vec("ln3_g", l - 1)),
                comm=comm)
        else:
            comm = bwd_comm(1e9)
            (grad_x,) = _nt_res([dgu1], full("ffn1_w_gu", l), dr1, f"ffn1_bwd_x_{l}", comm=comm)
        scatter.landed(comm)
    scatter.flush("scatter_rest")

    loss = lax.psum(loss_acc[0, 0], ("x", "y", "c"))

    out = {}
    for n in SHARDED:
        recvs = [scatter.done[(n, l)] for l in range(L)]
        if n in TRANSPOSED:
            res = _reduce_adamw(recvs, *[jnp.swapaxes(a, 1, 2) for a in (W[n], M[n], V[n])], f"adamw_{n}")
            out[n] = [jnp.swapaxes(a, 1, 2) for a in res]
        else:
            out[n] = _reduce_adamw(recvs, W[n], M[n], V[n], f"adamw_{n}")
    packed = _reduce_adamw([small], _pack([W[n] for n in REPLICATED], small_rows),
                           _pack([M[n] for n in REPLICATED], small_rows),
                           _pack([V[n] for n in REPLICATED], small_rows), "adamw_small")
    unpacked = [_unpack(p, small_shapes) for p in packed]
    for i, n in enumerate(REPLICATED):
        out[n] = tuple(u[i] for u in unpacked)

    return (loss, grad_x.reshape(x.shape), *[out[n][0] for n in WEIGHTS], *[out[n][1] for n in WEIGHTS],
            *[out[n][2] for n in WEIGHTS], *[out[n][3] for n in WEIGHTS])
```

```python
import math

import jax
import jax.numpy as jnp
from jax import lax
from jax.experimental import pallas as pl
from jax.experimental.pallas import tpu as pltpu

F32 = jnp.float32
BF16 = jnp.bfloat16
MESH = pl.DeviceIdType.MESH

DEPTH = 2
ALPHA = (2 * DEPTH) ** 0.25
LN_EPS = 1e-5
RG_LRU_C = 8.0
CONV_WIDTH = 31
RNN_CONV_WIDTH = 4
RNN_BLOCKS = 16
N_DEV = 8
ADAM_LR, ADAM_B1, ADAM_B2, ADAM_EPS, ADAM_WD, ADAM_STEP = 0.001, 0.9, 0.999, 1e-08, 0.01, 10

LANES = 128
SUBLANES = 8
VMEM_LIMIT = 56 * 1024 * 1024
CONV_PAD = 32
RNN_PAD = 8
ADAMW_ROWS = 352
MM_TN_OUT_BYTES = 14 * 1024 * 1024
NORM_ROWS = 1024
CONV_BLOCK = 32
GATHER_US_PER_MB = 43.0
SCATTER_US_PER_MB = 86.0

WEIGHTS = ['ffn1_w_gu', 'ffn1_w_down', 'ln1_g', 'ln1_b', 'mix_w_in', 'mix_b_in', 'conv_dw_w', 'conv_dw_b',
           'conv_gn_g', 'conv_gn_b', 'conv_w_proj', 'rnn_conv_w', 'rnn_conv_b', 'rnn_w_a', 'rnn_b_a', 'rnn_w_x',
           'rnn_b_x', 'rnn_lambda', 'rnn_w_proj', 'mix_w_out', 'ln2_g', 'ln2_b', 'ffn2_w_gu', 'ffn2_w_down',
           'ln3_g', 'ln3_b']
COL_SHARDED = ['ffn1_w_gu', 'mix_w_in', 'ffn2_w_gu', 'conv_dw_w', 'rnn_conv_w']
TRANSPOSED = ['ffn1_w_gu', 'mix_w_in', 'ffn2_w_gu']
ROW_SHARDED = ['ffn1_w_down', 'conv_w_proj', 'rnn_w_proj', 'mix_w_out', 'ffn2_w_down']
SHARDED = COL_SHARDED + ROW_SHARDED
BF16_ON_WIRE = ['ffn1_w_gu', 'mix_w_in', 'ffn2_w_gu', 'ffn1_w_down', 'conv_w_proj', 'rnn_w_proj', 'mix_w_out',
                'ffn2_w_down']
REPLICATED = [n for n in WEIGHTS if n not in SHARDED]
USE_ORDER = ['ffn1_w_gu', 'ffn1_w_down', 'mix_w_in', 'conv_dw_w', 'rnn_conv_w', 'conv_w_proj', 'rnn_w_proj',
             'mix_w_out', 'ffn2_w_gu', 'ffn2_w_down']


def _cp(n_axes=1):
    return pltpu.CompilerParams(dimension_semantics=("arbitrary",) * n_axes, vmem_limit_bytes=VMEM_LIMIT)


def _rows(tm, c):
    return pl.BlockSpec((tm, c), lambda i: (i, 0))


def _res(shape):
    nd = len(shape)
    return pl.BlockSpec(tuple(shape), lambda *_: (0,) * nd, pipeline_mode=pl.Buffered(1))


def _acc(shape):
    nd = len(shape)
    return pl.BlockSpec(tuple(shape), lambda *_: (0,) * nd)


def _tile(t, want):
    return want if t % want == 0 else t


def _sds(shape, dtype):
    return jax.ShapeDtypeStruct(tuple(shape), dtype)


def _mbytes(a):
    return a.size * a.dtype.itemsize / 1e6


def _ln_fwd(r, g, b):
    mu = jnp.mean(r, axis=-1, keepdims=True)
    xc = r - mu
    var = jnp.mean(xc * xc, axis=-1, keepdims=True)
    return xc * lax.rsqrt(var + LN_EPS) * g + b


def _ln_bwd(dy, r, g):
    mu = jnp.mean(r, axis=-1, keepdims=True)
    xc = r - mu
    var = jnp.mean(xc * xc, axis=-1, keepdims=True)
    rstd = lax.rsqrt(var + LN_EPS)
    xhat = xc * rstd
    dxh = dy * g
    m1 = jnp.mean(dxh, axis=-1, keepdims=True)
    m2 = jnp.mean(dxh * xhat, axis=-1, keepdims=True)
    dr = rstd * (dxh - m1 - xhat * m2)
    return dr, jnp.sum(dy * xhat, axis=0, keepdims=True), jnp.sum(dy, axis=0, keepdims=True)


def _sigmoid(x):
    return jax.nn.sigmoid(x)


_GELU_K = math.sqrt(2.0 / math.pi)


def _gelu(x):
    t = jnp.tanh(_GELU_K * (x + 0.044715 * x * x * x))
    return 0.5 * x * (1.0 + t), t


def _gelu_grad(x, t):
    return 0.5 * (1.0 + t) + 0.5 * x * (1.0 - t * t) * (_GELU_K * (1.0 + 3.0 * 0.044715 * x * x))


def _expm1(x):
    taylor = x * (1.0 + x * (0.5 + x * (1.0 / 6.0 + x * (1.0 / 24.0 + x * (1.0 / 120.0)))))
    return jnp.where(jnp.abs(x) < 0.03, taylor, jnp.exp(x) - 1.0)


def _softplus_neg(lam):
    return jnp.maximum(-lam, 0.0) + jnp.log1p(jnp.exp(-jnp.abs(lam)))


def _dot(a, b):
    return jnp.dot(a, b, preferred_element_type=F32)


def _dot_nt(a, b):
    return lax.dot_general(a, b, (((1,), (1,)), ((), ())), preferred_element_type=F32)


def _dot_tn(a, b):
    return lax.dot_general(a, b, (((0,), (0,)), ((), ())), preferred_element_type=F32)


def _chunks(width, cn):
    return [(j, min(cn, width - j)) for j in range(0, width, cn)]


def _band_chunks(width, block):
    out = []
    for c0, cw in _chunks(width, 256):
        lo = (c0 // block) * block
        hi = ((c0 + cw - 1) // block + 1) * block
        out.append((c0, cw, lo // LANES * LANES, min(width, -(-hi // LANES) * LANES)))
    return out


def _position():
    return lax.axis_index("x"), lax.axis_index("y"), lax.axis_index("c")


def _index(p):
    return 4 * p[0] + 2 * p[1] + p[2]


def _comm_out_shapes(kind, arrs):
    return [_sds((N_DEV,) + a.shape if kind == "gather" else a.shape, a.dtype) for a in arrs]


def _comm_scratch(n):
    return [pltpu.SemaphoreType.DMA((n, 7)), pltpu.SemaphoreType.DMA((n, 7)), pltpu.SemaphoreType.DMA((n,))]


def _comm_phases(kind, srcs, dsts, send_sems, recv_sems, local_sems):
    n = len(srcs)
    x, y, c = _position()
    me, sibling = (x, y, c), (x, y, 1 - c)
    chips = [(1 - x, y), (x, 1 - y), (1 - x, 1 - y)]

    if kind == "gather":
        def copy(a, k, block, to, src=None):
            dst = dsts[a].at[_index(block)]
            return pltpu.make_async_remote_copy(
                src_ref=dst if src is None else src, dst_ref=dst, send_sem=send_sems.at[a, k],
                recv_sem=recv_sems.at[a, k], device_id=to, device_id_type=MESH)

        def mine(a):
            return pltpu.make_async_copy(srcs[a], dsts[a].at[_index(me)], local_sems.at[a])

        def first(a):
            return [copy(a, 0, me, sibling, src=srcs[a])] + [
                copy(a, 1 + j, me, (*chip, c), src=srcs[a]) for j, chip in enumerate(chips)]

        def start():
            for a in range(n):
                mine(a).start()
            for a in range(n):
                for cp in first(a):
                    cp.start()

        def mid():
            for j, chip in enumerate(chips):
                for a in range(n):
                    copy(a, 1 + j, (*chip, c), me).wait_recv()
                    copy(a, 4 + j, (*chip, c), sibling).start()

        def end():
            for a in range(n):
                copy(a, 0, sibling, me).wait_recv()
            for j, chip in enumerate(chips):
                for a in range(n):
                    copy(a, 4 + j, (*chip, 1 - c), me).wait_recv()
            for a in range(n):
                for cp in first(a):
                    cp.wait_send()
                for j, chip in enumerate(chips):
                    copy(a, 4 + j, (*chip, c), sibling).wait_send()
                mine(a).wait()

        return start, mid, end

    def peer_of(k):
        return (1 - x if k & 4 else x, 1 - y if k & 2 else y, 1 - c if k & 1 else c)

    def own(a):
        return pltpu.make_async_copy(srcs[a].at[_index(me)], dsts[a].at[0], local_sems.at[a])

    def remote(a, k):
        peer = peer_of(k)
        return pltpu.make_async_remote_copy(
            src_ref=srcs[a].at[_index(peer)], dst_ref=dsts[a].at[k], send_sem=send_sems.at[a, k - 1],
            recv_sem=recv_sems.at[a, k - 1], device_id=peer, device_id_type=MESH)

    def start():
        for a in range(n):
            own(a).start()
        for k in range(1, N_DEV):
            for a in range(n):
                remote(a, k).start()

    def end():
        for k in range(1, N_DEV):
            for a in range(n):
                remote(a, k).wait()
        for a in range(n):
            own(a).wait()

    return start, (lambda: None), end


def _exchange(kind, arrs, name):
    n = len(arrs)
    hbm = pl.BlockSpec(memory_space=pl.ANY)

    def body(*refs):
        start, mid, end = _comm_phases(kind, refs[:n], refs[n:2 * n], *refs[2 * n:])
        start()
        mid()
        end()

    return pl.pallas_call(
        body, name=name, in_specs=[hbm] * n, out_specs=[hbm] * n, out_shape=_comm_out_shapes(kind, arrs),
        scratch_shapes=_comm_scratch(n))(*arrs)


def _pallas(body, *, name, grid, in_specs, out_specs, out_shape, args, scratch_shapes=(), comm=None):
    in_specs, out_specs, out_shape = list(in_specs), list(out_specs), list(out_shape)
    scratch_shapes = list(scratch_shapes)
    if not comm:
        return pl.pallas_call(body, name=name, grid=grid, in_specs=in_specs, out_specs=out_specs,
                              out_shape=out_shape, scratch_shapes=scratch_shapes,
                              compiler_params=_cp(len(grid)))(*args)
    arrs = comm["arrs"]
    ns, n_in, n_out, n_scr = len(arrs), len(in_specs), len(out_specs), len(scratch_shapes)
    hbm = pl.BlockSpec(memory_space=pl.ANY)
    total = math.prod(grid)

    def carrier(*refs):
        ins, srcs = refs[:n_in], refs[n_in:n_in + ns]
        outs, dsts = refs[n_in + ns:n_in + ns + n_out], refs[n_in + ns + n_out:n_in + 2 * ns + n_out]
        scr, sems = refs[n_in + 2 * ns + n_out:n_in + 2 * ns + n_out + n_scr], refs[n_in + 2 * ns + n_out + n_scr:]
        step = pl.program_id(0)
        for ax in range(1, len(grid)):
            step = step * grid[ax] + pl.program_id(ax)
        start, mid, end = _comm_phases(comm["kind"], srcs, dsts, *sems)
        pl.when(step == 0)(start)
        body(*ins, *outs, *scr)
        pl.when(step == total - 1)(mid)
        pl.when(step == total - 1)(end)

    res = pl.pallas_call(
        carrier, name=name, grid=grid, in_specs=in_specs + [hbm] * ns, out_specs=out_specs + [hbm] * ns,
        out_shape=out_shape + _comm_out_shapes(comm["kind"], arrs),
        scratch_shapes=scratch_shapes + _comm_scratch(ns), compiler_params=_cp(len(grid)))(*args, *arrs)
    comm["recv"] = res[n_out:]
    return res[:n_out]


def _ffn_up(xb, w, name, comm=None):
    T, D = xb.shape
    F = w.shape[0] // 2
    tm = _tile(T, 512)

    def body(x_ref, w_ref, gu_ref, a_ref):
        x = x_ref[...].astype(BF16)
        for j, cw in _chunks(F, 256):
            g = _dot_nt(x, w_ref[j:j + cw, :])
            u = _dot_nt(x, w_ref[F + j:F + j + cw, :])
            gu_ref[:, j:j + cw] = g.astype(BF16)
            gu_ref[:, F + j:F + j + cw] = u.astype(BF16)
            a_ref[:, j:j + cw] = (g * _sigmoid(g) * u).astype(BF16)

    return _pallas(
        body, name=name, grid=(T // tm,),
        in_specs=[_rows(tm, D), _res(w.shape)],
        out_specs=[_rows(tm, 2 * F), _rows(tm, F)],
        out_shape=[_sds((T, 2 * F), BF16), _sds((T, F), BF16)],
        args=(xb, w), comm=comm)


def _ffn_down_ln(a, wd, xres, g, b, name, comm=None):
    T, F = a.shape
    D = wd.shape[1]
    tm = _tile(T, 512)

    def body(a_ref, wd_ref, x_ref, g_ref, b_ref, r_ref, y_ref, yb_ref):
        r = ALPHA * x_ref[...] + 0.5 * _dot(a_ref[...], wd_ref[...])
        y = _ln_fwd(r, g_ref[...], b_ref[...])
        r_ref[...] = r
        y_ref[...] = y
        yb_ref[...] = y.astype(BF16)

    return _pallas(
        body, name=name, grid=(T // tm,),
        in_specs=[_rows(tm, F), _res(wd.shape), _rows(tm, D), _res((1, D)), _res((1, D))],
        out_specs=[_rows(tm, D), _rows(tm, D), _rows(tm, D)],
        out_shape=[_sds((T, D), F32), _sds((T, D), F32), _sds((T, D), BF16)],
        args=(a, wd, xres, g, b), comm=comm)


def _mix_in(hb, w, bias, d_rnn, name, comm=None):
    T, D = hb.shape
    R = d_rnn
    tm = _tile(T, 512)
    o_cv, o_cg, o_rx, o_rg, o_gc, o_gr = 0, D, 2 * D, 2 * D + R, 2 * D + 2 * R, 3 * D + 2 * R

    def body(x_ref, w_ref, b_ref, c_ref, cv_ref, cg_ref, rx_ref, rg_ref, gc_ref, gr_ref):
        x = x_ref[...]

        def seg(off, j, cw):
            return _dot_nt(x, w_ref[off + j:off + j + cw, :]) + b_ref[:, off + j:off + j + cw]

        for j, cw in _chunks(D, 256):
            cv = seg(o_cv, j, cw)
            cg = seg(o_cg, j, cw)
            cv_ref[:, j:j + cw] = cv.astype(BF16)
            cg_ref[:, j:j + cw] = cg.astype(BF16)
            c_ref[:, j:j + cw] = cv * _sigmoid(cg)
            gc_ref[:, j:j + cw] = seg(o_gc, j, cw).astype(BF16)
            gr_ref[:, j:j + cw] = seg(o_gr, j, cw).astype(BF16)
        for j, cw in _chunks(R, 256):
            rx_ref[:, j:j + cw] = seg(o_rx, j, cw)
            rg_ref[:, j:j + cw] = seg(o_rg, j, cw).astype(BF16)

    return _pallas(
        body, name=name, grid=(T // tm,),
        in_specs=[_rows(tm, D), _res(w.shape), _res(bias.shape)],
        out_specs=[_rows(tm, D), _rows(tm, D), _rows(tm, D), _rows(tm, R), _rows(tm, R), _rows(tm, D),
                   _rows(tm, D)],
        out_shape=[_sds((T, D), F32), _sds((T, D), BF16), _sds((T, D), BF16), _sds((T, R), F32),
                   _sds((T, R), BF16), _sds((T, D), BF16), _sds((T, D), BF16)],
        args=(hb, w, bias), comm=comm)


def _cols(t, rows=None):
    return pl.BlockSpec((t if rows is None else rows, LANES), lambda j: (0, j))


def _gn_stats(cc):
    mu = jnp.mean(cc, axis=-1, keepdims=True)
    xc = cc - mu
    var = jnp.mean(xc * xc, axis=-1, keepdims=True)
    rstd = lax.rsqrt(var + LN_EPS)
    return xc * rstd, rstd


def _tap_groups(offsets):
    groups = {}
    for k, s in enumerate(offsets):
        groups.setdefault(s % SUBLANES, []).append((k, s - s % SUBLANES))
    return sorted(groups.items())


def _shifted(win, phase):
    return win if phase == 0 else pltpu.roll(win, win.shape[0] - phase, 0)


def _tap_sum(win, wv, groups, rows):
    parts, t = [None] * 4, 0
    for phase, taps in groups:
        sh = _shifted(win, phase)
        for k, off in taps:
            term = wv[k:k + 1, :] * sh[off:off + rows, :]
            parts[t % 4] = term if parts[t % 4] is None else parts[t % 4] + term
            t += 1
    return (parts[0] + parts[1]) + (parts[2] + parts[3])


def _conv31_gn(c, w, bias, gg, gb, name, comm=None):
    T, D = c.shape
    K = w.shape[0]
    B, P, N = CONV_BLOCK, CONV_PAD, _tile(T, NORM_ROWS)
    assert D % LANES == 0 and T % B == 0 and K - 1 <= P
    groups = _tap_groups([P - (K - 1) + k for k in range(K)])

    def body(c_ref, w_ref, b_ref, gg_ref, gb_ref, cc_ref, cs_ref, xpad):
        xpad[0:P, :] = jnp.zeros((P, LANES), F32)
        xpad[P:P + T, :] = c_ref[...]
        wv = w_ref[...]
        bv, ggv, gbv = b_ref[...], gg_ref[...], gb_ref[...]

        def conv_step(i, carry):
            base = pl.multiple_of(i * B, B)
            win = xpad[pl.ds(base, B + P), :]
            cc_ref[pl.ds(base, B), :] = _tap_sum(win, wv, groups, B) + bv
            return carry

        lax.fori_loop(0, T // B, conv_step, 0)

        def norm_step(i, carry):
            base = pl.multiple_of(i * N, N)
            xhat, _ = _gn_stats(cc_ref[pl.ds(base, N), :])
            gn = xhat * ggv + gbv
            cs_ref[pl.ds(base, N), :] = (gn * _sigmoid(gn)).astype(BF16)
            return carry

        lax.fori_loop(0, T // N, norm_step, 0)

    return _pallas(
        body, name=name, grid=(D // LANES,),
        in_specs=[_cols(T), _cols(T, K), _cols(T, 1), _cols(T, 1), _cols(T, 1)],
        out_specs=[_cols(T), _cols(T)],
        out_shape=[_sds((T, D), F32), _sds((T, D), BF16)],
        scratch_shapes=[pltpu.VMEM((T + P, LANES), F32)],
        args=(c, w, bias, gg, gb), comm=comm)


def _conv4(rx, w, bias, name):
    T, C = rx.shape
    K = w.shape[0]
    B, P = 4 * CONV_BLOCK, RNN_PAD
    assert C % LANES == 0 and T % B == 0 and K - 1 <= P

    def body(x_ref, w_ref, b_ref, r_ref, xpad):
        xpad[0:P, :] = jnp.zeros((P, LANES), F32)
        xpad[P:P + T, :] = x_ref[...]
        wv, bv = w_ref[...], b_ref[...]

        def step(i, carry):
            base = pl.multiple_of(i * B, B)
            win = xpad[pl.ds(base, B + P), :]
            acc = jnp.zeros((B, LANES), F32)
            for k in range(K):
                s = P - (K - 1) + k
                acc = acc + wv[k:k + 1, :] * win[s:s + B, :]
            r_ref[pl.ds(base, B), :] = acc + bv
            return carry

        lax.fori_loop(0, T // B, step, 0)

    (r,) = _pallas(
        body, name=name, grid=(C // LANES,),
        in_specs=[_cols(T), _cols(T, K), _cols(T, 1)],
        out_specs=[_cols(T)],
        out_shape=[_sds((T, C), F32)],
        scratch_shapes=[pltpu.VMEM((T + RNN_PAD, LANES), F32)],
        args=(rx, w, bias))
    return r


def _gates(r, bda, bdx, b_a, b_x, lam, name, comm=None):
    T, C = r.shape
    tm = _tile(T, 512)
    chunks = _band_chunks(C, C // RNN_BLOCKS)

    def body(r_ref, wa_ref, wx_ref, ba_ref, bx_ref, lam_ref, ra_ref, ri_ref, a_ref, u_ref):
        for c0, cw, k0, k1 in chunks:
            cols = slice(c0, c0 + cw)
            rb = r_ref[:, k0:k1].astype(BF16)
            ra = _sigmoid(_dot(rb, wa_ref[k0:k1, cols]) + ba_ref[:, cols])
            ri = _sigmoid(_dot(rb, wx_ref[k0:k1, cols]) + bx_ref[:, cols])
            log_a = (-RG_LRU_C) * ra * _softplus_neg(lam_ref[:, cols])
            ra_ref[:, cols] = ra
            ri_ref[:, cols] = ri
            a_ref[:, cols] = jnp.exp(log_a)
            u_ref[:, cols] = jnp.sqrt(-_expm1(2.0 * log_a)) * (ri * r_ref[:, cols])

    return _pallas(
        body, name=name, grid=(T // tm,),
        in_specs=[_rows(tm, C), _res(bda.shape), _res(bdx.shape), _res((1, C)), _res((1, C)), _res((1, C))],
        out_specs=[_rows(tm, C)] * 4,
        out_shape=[_sds((T, C), F32)] * 4,
        args=(r, bda, bdx, b_a, b_x, lam), comm=comm)


def _scan_fwd(a, u, rg, name, comm=None):
    T, C = a.shape
    tt = _tile(T, 512)

    def body(a_ref, u_ref, rg_ref, h_ref, hp_ref, hg_ref, carry):
        @pl.when(pl.program_id(0) == 0)
        def _():
            carry[...] = jnp.zeros_like(carry)

        row = lax.broadcasted_iota(jnp.int32, (SUBLANES, C), 0)

        def group(i, hprev):
            base = pl.multiple_of(i * SUBLANES, SUBLANES)
            av = a_ref[pl.ds(base, SUBLANES), :]
            uv = u_ref[pl.ds(base, SUBLANES), :]
            for s in (1, 2, 4):
                a_s = jnp.where(row >= s, pltpu.roll(av, s, 0), 1.0)
                u_s = jnp.where(row >= s, pltpu.roll(uv, s, 0), 0.0)
                uv = av * u_s + uv
                av = av * a_s
            h = av * hprev + uv
            h_ref[pl.ds(base, SUBLANES), :] = h
            hp_ref[pl.ds(base, SUBLANES), :] = jnp.where(row >= 1, pltpu.roll(h, 1, 0), hprev)
            return h[SUBLANES - 1:SUBLANES, :]

        carry[...] = lax.fori_loop(0, tt // SUBLANES, group, carry[...])
        gel, _ = _gelu(rg_ref[...].astype(F32))
        hg_ref[...] = (h_ref[...] * gel).astype(BF16)

    return _pallas(
        body, name=name, grid=(T // tt,),
        in_specs=[_rows(tt, C)] * 3,
        out_specs=[_rows(tt, C)] * 3,
        out_shape=[_sds((T, C), F32), _sds((T, C), F32), _sds((T, C), BF16)],
        scratch_shapes=[pltpu.VMEM((1, C), F32)],
        args=(a, u, rg), comm=comm)


def _mix_out_ln(cs, hg, gc, gr, hres, wcp, wrp, wout, g, b, name, comm=None):
    T, D = cs.shape
    C = hg.shape[1]
    tm = _tile(T, 512)

    def body(cs_ref, hg_ref, gc_ref, gr_ref, h_ref, wcp_ref, wrp_ref, wo_ref, g_ref, b_ref,
             yc_ref, yr_ref, m_ref, r_ref, y_ref, yb_ref):
        yc = _dot(cs_ref[...], wcp_ref[...])
        yr = _dot(hg_ref[...], wrp_ref[...])
        m = (_sigmoid(gc_ref[...].astype(F32)) * yc + _sigmoid(gr_ref[...].astype(F32)) * yr).astype(BF16)
        r = ALPHA * h_ref[...] + _dot(m, wo_ref[...])
        y = _ln_fwd(r, g_ref[...], b_ref[...])
        yc_ref[...] = yc.astype(BF16)
        yr_ref[...] = yr.astype(BF16)
        m_ref[...] = m
        r_ref[...] = r
        y_ref[...] = y
        yb_ref[...] = y.astype(BF16)

    return _pallas(
        body, name=name, grid=(T // tm,),
        in_specs=[_rows(tm, D), _rows(tm, C), _rows(tm, D), _rows(tm, D), _rows(tm, D), _res(wcp.shape),
                  _res(wrp.shape), _res(wout.shape), _res((1, D)), _res((1, D))],
        out_specs=[_rows(tm, D)] * 6,
        out_shape=[_sds((T, D), BF16), _sds((T, D), BF16), _sds((T, D), BF16), _sds((T, D), F32),
                   _sds((T, D), F32), _sds((T, D), BF16)],
        args=(cs, hg, gc, gr, hres, wcp, wrp, wout, g, b), comm=comm)


def _loss_ln_bwd(y, target, r, g, name):
    T, D = y.shape
    tm = _tile(T, 512)

    def body(y_ref, t_ref, r_ref, g_ref, loss_ref, dr_ref, drb_ref, dg_ref, db_ref):
        @pl.when(pl.program_id(0) == 0)
        def _():
            loss_ref[...] = jnp.zeros_like(loss_ref)
            dg_ref[...] = jnp.zeros_like(dg_ref)
            db_ref[...] = jnp.zeros_like(db_ref)

        e = y_ref[...] - t_ref[...]
        loss_ref[...] += (0.5 / D) * jnp.sum(e * e)
        dr, dg, db = _ln_bwd(e * (1.0 / D), r_ref[...], g_ref[...])
        dr_ref[...] = dr
        drb_ref[...] = dr.astype(BF16)
        dg_ref[...] += dg
        db_ref[...] += db

    return _pallas(
        body, name=name, grid=(T // tm,),
        in_specs=[_rows(tm, D), _rows(tm, D), _rows(tm, D), _res((1, D))],
        out_specs=[_acc((SUBLANES, LANES)), _rows(tm, D), _rows(tm, D), _acc((1, D)), _acc((1, D))],
        out_shape=[_sds((SUBLANES, LANES), F32), _sds((T, D), F32), _sds((T, D), BF16), _sds((1, D), F32),
                   _sds((1, D), F32)],
        args=(y, target, r, g))


def _ffn_bwd_a(drb, wd, gu, name, comm=None):
    T, D = drb.shape
    F = wd.shape[0]
    tm = _tile(T, 512)

    def body(d_ref, wd_ref, gu_ref, o_ref):
        d = d_ref[...]
        for j, cw in _chunks(F, 256):
            da = 0.5 * _dot_nt(d, wd_ref[j:j + cw, :])
            gt = gu_ref[:, j:j + cw].astype(F32)
            up = gu_ref[:, F + j:F + j + cw].astype(F32)
            sg = _sigmoid(gt)
            o_ref[:, j:j + cw] = (da * up * (sg * (1.0 + gt * (1.0 - sg)))).astype(BF16)
            o_ref[:, F + j:F + j + cw] = (da * (gt * sg)).astype(BF16)

    (dgu,) = _pallas(
        body, name=name, grid=(T // tm,),
        in_specs=[_rows(tm, D), _res(wd.shape), _rows(tm, 2 * F)],
        out_specs=[_rows(tm, 2 * F)],
        out_shape=[_sds((T, 2 * F), BF16)],
        args=(drb, wd, gu), comm=comm)
    return dgu


def _nt_res(dus, w, dres, name, ln=None, colsum=False, comm=None):
    T = dus[0].shape[0]
    widths = [d.shape[1] for d in dus]
    offs = [sum(widths[:p]) for p in range(len(dus))]
    K, D, P = sum(widths), w.shape[1], len(dus)
    tm = _tile(T, 512)
    n_in = P + 2 + (2 if ln else 0)

    def body(*refs):
        du_refs, w_ref, dres_ref = refs[:P], refs[P], refs[P + 1]
        outs = refs[n_in:]
        dy = ALPHA * dres_ref[...]
        for du_ref, off, width in zip(du_refs, offs, widths):
            dy = dy + _dot(du_ref[...], w_ref[off:off + width, :])
        if ln:
            r_ref, g_ref = refs[P + 2:P + 4]

            @pl.when(pl.program_id(0) == 0)
            def _():
                outs[2][...] = jnp.zeros_like(outs[2])
                outs[3][...] = jnp.zeros_like(outs[3])

            dr, dg, db = _ln_bwd(dy, r_ref[...], g_ref[...])
            outs[0][...] = dr
            outs[1][...] = dr.astype(BF16)
            outs[2][...] += dg
            outs[3][...] += db
        else:
            outs[0][...] = dy
        if colsum:
            cs_ref = outs[-1]

            @pl.when(pl.program_id(0) == 0)
            def _():
                cs_ref[...] = jnp.zeros_like(cs_ref)

            for du_ref, off, width in zip(du_refs, offs, widths):
                cs_ref[:, off:off + width] += jnp.sum(du_ref[...].astype(F32), axis=0, keepdims=True)

    in_specs = [_rows(tm, width) for width in widths] + [_res(w.shape), _rows(tm, D)]
    args = list(dus) + [w, dres]
    if ln:
        in_specs += [_rows(tm, D), _res((1, D))]
        args += list(ln)
        out_specs = [_rows(tm, D), _rows(tm, D), _acc((1, D)), _acc((1, D))]
        out_shape = [_sds((T, D), F32), _sds((T, D), BF16), _sds((1, D), F32), _sds((1, D), F32)]
    else:
        out_specs = [_rows(tm, D)]
        out_shape = [_sds((T, D), F32)]
    if colsum:
        out_specs.append(_acc((1, K)))
        out_shape.append(_sds((1, K), F32))
    return _pallas(body, name=name, grid=(T // tm,), in_specs=in_specs, out_specs=out_specs, out_shape=out_shape,
                   args=args, comm=comm)


def _mm_tn(x, dy, name, scale=1.0, comm=None):
    T, K = x.shape
    N = dy.shape[1]
    tt = _tile(T, 1024)
    tn = next(N // d for d in range(1, N // LANES + 1)
              if N % d == 0 and (N // d) % LANES == 0 and K * (N // d) * 4 <= MM_TN_OUT_BYTES)
    nt = T // tt

    def body(x_ref, dy_ref, o_ref):
        t = pl.program_id(1)

        @pl.when(t == 0)
        def _():
            o_ref[...] = jnp.zeros_like(o_ref)

        o_ref[...] += _dot_tn(x_ref[...].astype(BF16), dy_ref[...])
        if scale != 1.0:
            @pl.when(t == nt - 1)
            def _():
                o_ref[...] = o_ref[...] * scale

    (out,) = _pallas(
        body, name=name, grid=(N // tn, nt),
        in_specs=[pl.BlockSpec((tt, K), lambda j, t: (t, 0)), pl.BlockSpec((tt, tn), lambda j, t: (t, j))],
        out_specs=[pl.BlockSpec((K, tn), lambda j, t: (0, j))],
        out_shape=[_sds((K, N), F32)],
        args=(x, dy), comm=comm)
    return out


def _mix_bwd1(drb, wout, wcp, wrp, gc, gr, yc, yr, rg, h, name, comm=None):
    T, D = drb.shape
    C = rg.shape[1]
    tm = _tile(T, 512)

    def body(d_ref, wo_ref, wcp_ref, wrp_ref, gc_ref, gr_ref, yc_ref, yr_ref, rg_ref, h_ref,
             dyc_ref, dyr_ref, tail_ref, dcs_ref, dh_ref):
        dm = _dot_nt(d_ref[...], wo_ref[...])
        sc = _sigmoid(gc_ref[...].astype(F32))
        sr = _sigmoid(gr_ref[...].astype(F32))
        dyc = (dm * sc).astype(BF16)
        dyr = (dm * sr).astype(BF16)
        dyc_ref[...] = dyc
        dyr_ref[...] = dyr
        tail_ref[:, C:C + D] = (dm * yc_ref[...].astype(F32) * sc * (1.0 - sc)).astype(BF16)
        tail_ref[:, C + D:] = (dm * yr_ref[...].astype(F32) * sr * (1.0 - sr)).astype(BF16)
        dcs_ref[...] = _dot_nt(dyc, wcp_ref[...])
        dhg = _dot_nt(dyr, wrp_ref[...])
        rgv = rg_ref[...].astype(F32)
        gel, t = _gelu(rgv)
        dh_ref[...] = dhg * gel
        tail_ref[:, 0:C] = (dhg * h_ref[...] * _gelu_grad(rgv, t)).astype(BF16)

    return _pallas(
        body, name=name, grid=(T // tm,),
        in_specs=[_rows(tm, D), _res(wout.shape), _res(wcp.shape), _res(wrp.shape), _rows(tm, D), _rows(tm, D),
                  _rows(tm, D), _rows(tm, D), _rows(tm, C), _rows(tm, C)],
        out_specs=[_rows(tm, D), _rows(tm, D), _rows(tm, C + 2 * D), _rows(tm, D), _rows(tm, C)],
        out_shape=[_sds((T, D), BF16), _sds((T, D), BF16), _sds((T, C + 2 * D), BF16), _sds((T, D), F32),
                   _sds((T, C), F32)],
        args=(drb, wout, wcp, wrp, gc, gr, yc, yr, rg, h), comm=comm)


def _scan_bwd(dh, a, name):
    T, C = dh.shape
    tt = _tile(T, 512)
    nt = T // tt
    ng = tt // SUBLANES

    def body(d_ref, a_ref, g_ref, carry):
        @pl.when(pl.program_id(0) == 0)
        def _():
            carry[...] = jnp.zeros_like(carry)

        row = lax.broadcasted_iota(jnp.int32, (SUBLANES, C), 0)

        def group(j, enext):
            base = pl.multiple_of((ng - 1 - j) * SUBLANES, SUBLANES)
            av = a_ref[pl.ds(base, SUBLANES), :]
            dv = d_ref[pl.ds(base, SUBLANES), :]
            bv = av * dv
            for s in (1, 2, 4):
                keep = row < SUBLANES - s
                a_s = jnp.where(keep, pltpu.roll(av, SUBLANES - s, 0), 1.0)
                b_s = jnp.where(keep, pltpu.roll(bv, SUBLANES - s, 0), 0.0)
                bv = av * b_s + bv
                av = av * a_s
            e = av * enext + bv
            e_up = jnp.where(row < SUBLANES - 1, pltpu.roll(e, SUBLANES - 1, 0), enext)
            g_ref[pl.ds(base, SUBLANES), :] = dv + e_up
            return e[0:1, :]

        carry[...] = lax.fori_loop(0, ng, group, carry[...])

    rev = pl.BlockSpec((tt, C), lambda i: (nt - 1 - i, 0))
    (g,) = _pallas(
        body, name=name, grid=(nt,), in_specs=[rev, rev], out_specs=[rev],
        out_shape=[_sds((T, C), F32)],
        scratch_shapes=[pltpu.VMEM((1, C), F32)],
        args=(dh, a))
    return g


def _gates_bwd(g, hp, ra, ri, r, lam, bda, bdx, name, comm=None):
    T, C = g.shape
    tm = _tile(T, 512)
    nt = T // tm
    chunks = _band_chunks(C, C // RNN_BLOCKS)

    def body(g_ref, hp_ref, ra_ref, ri_ref, r_ref, lam_ref, wa_ref, wx_ref,
             dr_ref, dpa_ref, dpx_ref, dlam_ref, dba_ref, dbx_ref):
        @pl.when(pl.program_id(0) == 0)
        def _():
            dlam_ref[...] = jnp.zeros_like(dlam_ref)
            dba_ref[...] = jnp.zeros_like(dba_ref)
            dbx_ref[...] = jnp.zeros_like(dbx_ref)

        for c0, cw, _, _ in chunks:
            cols = slice(c0, c0 + cw)
            gv, rav, riv, rv = g_ref[:, cols], ra_ref[:, cols], ri_ref[:, cols], r_ref[:, cols]
            sp = _softplus_neg(lam_ref[:, cols])
            log_a = (-RG_LRU_C) * rav * sp
            av = jnp.exp(log_a)
            mult = jnp.sqrt(-_expm1(2.0 * log_a))
            d_mult = gv * riv * rv
            d_i = gv * mult * rv
            d_loga = gv * hp_ref[:, cols] * av - d_mult * (av * av) / mult
            d_ra = d_loga * ((-RG_LRU_C) * sp)
            dpa = d_ra * rav * (1.0 - rav)
            dpx = d_i * riv * (1.0 - riv)
            dpa_ref[:, cols] = dpa.astype(BF16)
            dpx_ref[:, cols] = dpx.astype(BF16)
            dr_ref[:, cols] = gv * mult * riv
            dlam_ref[:, cols] += jnp.sum(d_loga * ((-RG_LRU_C) * rav), axis=0, keepdims=True)
            dba_ref[:, cols] += jnp.sum(dpa, axis=0, keepdims=True)
            dbx_ref[:, cols] += jnp.sum(dpx, axis=0, keepdims=True)
        for c0, cw, k0, k1 in chunks:
            cols = slice(c0, c0 + cw)
            dr_ref[:, k0:k1] += (_dot_nt(dpa_ref[:, cols], wa_ref[k0:k1, cols])
                                 + _dot_nt(dpx_ref[:, cols], wx_ref[k0:k1, cols]))

        @pl.when(pl.program_id(0) == nt - 1)
        def _():
            dlam_ref[...] = dlam_ref[...] * (-_sigmoid(-lam_ref[...]))

    return _pallas(
        body, name=name, grid=(nt,),
        in_specs=[_rows(tm, C)] * 5 + [_res((1, C)), _res(bda.shape), _res(bdx.shape)],
        out_specs=[_rows(tm, C)] * 3 + [_acc((1, C))] * 3,
        out_shape=[_sds((T, C), F32), _sds((T, C), BF16), _sds((T, C), BF16)] + [_sds((1, C), F32)] * 3,
        args=(g, hp, ra, ri, r, lam, bda, bdx), comm=comm)


def _band_dw(r, dpa, dpx, name):
    T, C = r.shape
    tt = _tile(T, 512)
    chunks = _band_chunks(C, C // RNN_BLOCKS)

    def body(r_ref, dpa_ref, dpx_ref, oa_ref, ox_ref):
        @pl.when(pl.program_id(0) == 0)
        def _():
            oa_ref[...] = jnp.zeros_like(oa_ref)
            ox_ref[...] = jnp.zeros_like(ox_ref)

        for c0, cw, k0, k1 in chunks:
            cols = slice(c0, c0 + cw)
            rb = r_ref[:, k0:k1].astype(BF16)
            oa_ref[k0:k1, cols] += _dot_tn(rb, dpa_ref[:, cols])
            ox_ref[k0:k1, cols] += _dot_tn(rb, dpx_ref[:, cols])

    return _pallas(
        body, name=name, grid=(T // tt,),
        in_specs=[_rows(tt, C)] * 3,
        out_specs=[_acc((C, C)), _acc((C, C))],
        out_shape=[_sds((C, C), F32), _sds((C, C), F32)],
        args=(r, dpa, dpx))


def _conv4_bwd(dr, rx, w, name):
    T, C = dr.shape
    K = w.shape[0]
    B, P = CONV_BLOCK, RNN_PAD
    assert K <= SUBLANES
    d_groups = _tap_groups([K - 1 - k for k in range(K)])
    x_groups = _tap_groups([P - (K - 1) + k for k in range(K)])

    def fold(v):
        part = v[0:SUBLANES, :]
        for q in range(1, B // SUBLANES):
            part = part + v[q * SUBLANES:(q + 1) * SUBLANES, :]
        return part

    def body(d_ref, x_ref, w_ref, dx_ref, dw_ref, db_ref, dpad, xpad, dwacc, dbacc):
        dpad[0:T, :] = d_ref[...]
        dpad[T:T + P, :] = jnp.zeros((P, LANES), F32)
        xpad[0:P, :] = jnp.zeros((P, LANES), F32)
        xpad[P:P + T, :] = x_ref[...]
        dwacc[...] = jnp.zeros_like(dwacc)
        dbacc[...] = jnp.zeros_like(dbacc)
        wv = w_ref[...]

        def step(i, carry):
            base = pl.multiple_of(i * B, B)
            dwin = dpad[pl.ds(base, B + P), :]
            xwin = xpad[pl.ds(base, B + P), :]
            dcur = dwin[0:B, :]
            dx_ref[pl.ds(base, B), :] = _tap_sum(dwin, wv, d_groups, B).astype(BF16)
            for phase, taps in x_groups:
                sh = _shifted(xwin, phase)
                for k, off in taps:
                    dwacc[k * SUBLANES:(k + 1) * SUBLANES, :] += fold(dcur * sh[off:off + B, :])
            dbacc[...] += fold(dcur)
            return carry

        lax.fori_loop(0, T // B, step, 0)
        dw_ref[...] = jnp.zeros_like(dw_ref)
        for k in range(K):
            dw_ref[k:k + 1, :] = jnp.sum(dwacc[k * SUBLANES:(k + 1) * SUBLANES, :], axis=0, keepdims=True)
        db_ref[...] = jnp.sum(dbacc[...], axis=0, keepdims=True)

    return _pallas(
        body, name=name, grid=(C // LANES,),
        in_specs=[_cols(T), _cols(T), _cols(T, K)],
        out_specs=[_cols(T), _cols(T, SUBLANES), _cols(T, 1)],
        out_shape=[_sds((T, C), BF16), _sds((SUBLANES, C), F32), _sds((1, C), F32)],
        scratch_shapes=[pltpu.VMEM((T + P, LANES), F32), pltpu.VMEM((T + P, LANES), F32),
                        pltpu.VMEM((SUBLANES * SUBLANES, LANES), F32), pltpu.VMEM((SUBLANES, LANES), F32)],
        args=(dr, rx, w))


def _conv31_bwd(dcs, cc, c, cv, cg, w, gg, gb, name, comm=None):
    T, D = dcs.shape
    K = w.shape[0]
    R, B, P = _tile(T, NORM_ROWS), CONV_BLOCK, CONV_PAD
    d_groups = _tap_groups([K - 1 - k for k in range(K)])
    x_groups = _tap_groups([P - (K - 1) + k for k in range(K)])

    def body(dcs_ref, cc_ref, c_ref, cv_ref, cg_ref, w_ref, gg_ref, gb_ref,
             dcv_ref, dcg_ref, dw_ref, db_ref, dgg_ref, dgb_ref, dpad, xpad, dwacc):
        dpad[T:T + P, :] = jnp.zeros((P, LANES), F32)
        xpad[0:P, :] = jnp.zeros((P, LANES), F32)
        xpad[P:P + T, :] = c_ref[...]
        dwacc[...] = jnp.zeros_like(dwacc)
        db_ref[...] = jnp.zeros_like(db_ref)
        dgg_ref[...] = jnp.zeros_like(dgg_ref)
        dgb_ref[...] = jnp.zeros_like(dgb_ref)
        wv, ggv, gbv = w_ref[...], gg_ref[...], gb_ref[...]

        def norm_step(i, carry):
            base = pl.multiple_of(i * R, R)
            xhat, rstd = _gn_stats(cc_ref[pl.ds(base, R), :])
            gn = xhat * ggv + gbv
            sg = _sigmoid(gn)
            dgn = dcs_ref[pl.ds(base, R), :] * (sg * (1.0 + gn * (1.0 - sg)))
            dgg_ref[...] += jnp.sum(dgn * xhat, axis=0, keepdims=True)
            dgb_ref[...] += jnp.sum(dgn, axis=0, keepdims=True)
            dxh = dgn * ggv
            m1 = jnp.mean(dxh, axis=-1, keepdims=True)
            m2 = jnp.mean(dxh * xhat, axis=-1, keepdims=True)
            dcc = rstd * (dxh - m1 - xhat * m2)
            dpad[pl.ds(base, R), :] = dcc
            db_ref[...] += jnp.sum(dcc, axis=0, keepdims=True)
            return carry

        lax.fori_loop(0, T // R, norm_step, 0)

        def conv_step(i, carry):
            base = pl.multiple_of(i * B, B)
            dwin = dpad[pl.ds(base, B + P), :]
            xwin = xpad[pl.ds(base, B + P), :]
            dcur = dwin[0:B, :]
            acc = _tap_sum(dwin, wv, d_groups, B)
            for phase, taps in x_groups:
                sh = _shifted(xwin, phase)
                for k, off in taps:
                    prod = dcur * sh[off:off + B, :]
                    part = prod[0:SUBLANES, :]
                    for q in range(1, B // SUBLANES):
                        part = part + prod[q * SUBLANES:(q + 1) * SUBLANES, :]
                    dwacc[k * SUBLANES:(k + 1) * SUBLANES, :] += part
            cgv = cg_ref[pl.ds(base, B), :].astype(F32)
            cvv = cv_ref[pl.ds(base, B), :].astype(F32)
            sg = _sigmoid(cgv)
            dcv_ref[pl.ds(base, B), :] = (acc * sg).astype(BF16)
            dcg_ref[pl.ds(base, B), :] = (acc * cvv * sg * (1.0 - sg)).astype(BF16)
            return carry

        lax.fori_loop(0, T // B, conv_step, 0)
        dw_ref[...] = jnp.zeros_like(dw_ref)
        for k in range(K):
            dw_ref[k:k + 1, :] = jnp.sum(dwacc[k * SUBLANES:(k + 1) * SUBLANES, :], axis=0, keepdims=True)

    return _pallas(
        body, name=name, grid=(D // LANES,),
        in_specs=[_cols(T)] * 5 + [_cols(T, K), _cols(T, 1), _cols(T, 1)],
        out_specs=[_cols(T), _cols(T), _cols(T, P), _cols(T, 1), _cols(T, 1), _cols(T, 1)],
        out_shape=[_sds((T, D), BF16), _sds((T, D), BF16), _sds((P, D), F32), _sds((1, D), F32),
                   _sds((1, D), F32), _sds((1, D), F32)],
        scratch_shapes=[pltpu.VMEM((T + P, LANES), F32), pltpu.VMEM((T + P, LANES), F32),
                        pltpu.VMEM((P * SUBLANES, LANES), F32)],
        args=(dcs, cc, c, cv, cg, w, gg, gb), comm=comm)


def _reduce_adamw(recvs, w, m, v, name):
    L, R, C = w.shape
    assert len(recvs) == L
    tr = next((R // d for d in range(1, R // SUBLANES + 1)
               if R % d == 0 and (R // d) % SUBLANES == 0 and R // d <= ADAMW_ROWS), R)
    nr = R // tr
    c1 = 1.0 - ADAM_B1 ** ADAM_STEP
    c2 = 1.0 - ADAM_B2 ** ADAM_STEP

    def body(*refs):
        recv_refs = refs[:L]
        w_ref, m_ref, v_ref, g_ref, d_ref, mo_ref, vo_ref = refs[L:]

        def update(recv_ref):
            g = recv_ref[0].astype(F32)
            for k in range(1, N_DEV):
                g = g + recv_ref[k].astype(F32)
            mn = ADAM_B1 * m_ref[0] + (1.0 - ADAM_B1) * g
            vn = ADAM_B2 * v_ref[0] + (1.0 - ADAM_B2) * (g * g)
            g_ref[0] = g
            mo_ref[0] = mn
            vo_ref[0] = vn
            d_ref[0] = (-ADAM_LR) * ((mn / c1) / (jnp.sqrt(vn / c2) + ADAM_EPS) + ADAM_WD * w_ref[0])

        for l in range(L):
            pl.when(pl.program_id(0) == l)(lambda l=l: update(recv_refs[l]))

    def recv_spec(l):
        return pl.BlockSpec((N_DEV, tr, C), lambda j, i: (0, jnp.where(j == l, i, jnp.where(j < l, 0, nr - 1)), 0))

    blk = pl.BlockSpec((1, tr, C), lambda j, i: (j, i, 0))
    return _pallas(
        body, name=name, grid=(L, nr),
        in_specs=[recv_spec(l) for l in range(L)] + [blk, blk, blk],
        out_specs=[blk] * 4,
        out_shape=[_sds((L, R, C), F32)] * 4,
        args=(*recvs, w, m, v))


def _unshard(name, gathered):
    n, r, c = gathered.shape
    if name in COL_SHARDED and name not in TRANSPOSED:
        return gathered.transpose(1, 0, 2).reshape(r, n * c)
    return gathered.reshape(n * r, c)


def _to_shards(name, full):
    R, C = full.shape
    wire = BF16 if name in BF16_ON_WIRE else F32
    if name in TRANSPOSED:
        return full.reshape(R, N_DEV, C // N_DEV).transpose(1, 2, 0).astype(wire)
    if name in COL_SHARDED:
        return full.reshape(R, N_DEV, C // N_DEV).transpose(1, 0, 2).astype(wire)
    return full.reshape(N_DEV, R // N_DEV, C).astype(wire)


def _block_diag(w):
    H, b, _ = w.shape
    eye = jnp.eye(H, dtype=w.dtype)
    return (w[:, :, None, :] * eye[:, None, :, None]).reshape(H * b, H * b)


def _diag_blocks(dense, H):
    b = dense.shape[0] // H
    eye = jnp.eye(H, dtype=dense.dtype)
    return jnp.sum(dense.reshape(H, b, H, b) * eye[:, None, :, None], axis=2)


def _pack(arrs, rows):
    flat = jnp.concatenate([a.reshape(-1) for a in arrs])
    return jnp.pad(flat, (0, rows * 1024 - flat.shape[0])).reshape(1, rows, 1024)


def _unpack(packed, shapes):
    flat = packed.reshape(-1)
    out, off = [], 0
    for s in shapes:
        n = math.prod(s)
        out.append(flat[off:off + n].reshape(s))
        off += n
    return out


class _Queue:
    def __init__(self, kind, us_per_mb):
        self.kind, self.us_per_mb, self.items, self.done = kind, us_per_mb, [], {}

    def push(self, key, arr):
        self.items.append((key, arr))

    def mb(self, arr):
        return _mbytes(arr) / (N_DEV if self.kind == "scatter" else 1)

    def take(self, micros):
        taken, budget = [], micros / self.us_per_mb
        while self.items and (not taken or self.mb(self.items[0][1]) <= budget):
            budget -= self.mb(self.items[0][1])
            taken.append(self.items.pop(0))
        return {"kind": self.kind, "keys": [k for k, _ in taken], "arrs": [a for _, a in taken]} if taken else None

    def landed(self, comm):
        if comm:
            self.done.update(zip(comm["keys"], comm["recv"]))

    def flush(self, name, upto=None):
        n = len(self.items)
        if upto is not None:
            keys = [k for k, _ in self.items]
            n = keys.index(upto) + 1 if upto in keys else 0
        if n:
            taken, self.items = self.items[:n], self.items[n:]
            self.done.update(zip([k for k, _ in taken], _exchange(self.kind, [a for _, a in taken], name)))


def kernel(x, ffn1_w_gu, ffn1_w_down, ln1_g, ln1_b, mix_w_in, mix_b_in, conv_dw_w, conv_dw_b, conv_gn_g, conv_gn_b, conv_w_proj, rnn_conv_w, rnn_conv_b, rnn_w_a, rnn_b_a, rnn_w_x, rnn_b_x, rnn_lambda, rnn_w_proj, mix_w_out, ln2_g, ln2_b, ffn2_w_gu, ffn2_w_down, ln3_g, ln3_b, loss_target, m_ffn1_w_gu, m_ffn1_w_down, m_ln1_g, m_ln1_b, m_mix_w_in, m_mix_b_in, m_conv_dw_w, m_conv_dw_b, m_conv_gn_g, m_conv_gn_b, m_conv_w_proj, m_rnn_conv_w, m_rnn_conv_b, m_rnn_w_a, m_rnn_b_a, m_rnn_w_x, m_rnn_b_x, m_rnn_lambda, m_rnn_w_proj, m_mix_w_out, m_ln2_g, m_ln2_b, m_ffn2_w_gu, m_ffn2_w_down, m_ln3_g, m_ln3_b, v_ffn1_w_gu, v_ffn1_w_down, v_ln1_g, v_ln1_b, v_mix_w_in, v_mix_b_in, v_conv_dw_w, v_conv_dw_b, v_conv_gn_g, v_conv_gn_b, v_conv_w_proj, v_rnn_conv_w, v_rnn_conv_b, v_rnn_w_a, v_rnn_b_a, v_rnn_w_x, v_rnn_b_x, v_rnn_lambda, v_rnn_w_proj, v_mix_w_out, v_ln2_g, v_ln2_b, v_ffn2_w_gu, v_ffn2_w_down, v_ln3_g, v_ln3_b):
    given = dict(locals())
    W = {n: given[n] for n in WEIGHTS}
    M = {n: given["m_" + n] for n in WEIGHTS}
    V = {n: given["v_" + n] for n in WEIGHTS}
    T, D = x.shape[1], x.shape[2]
    L = DEPTH
    x2 = x.reshape(T, D)
    target = loss_target.reshape(T, D)
    d_rnn = rnn_conv_b.shape[1]

    gather = _Queue("gather", GATHER_US_PER_MB)
    for l in range(L):
        for n in USE_ORDER:
            shard = W[n][l].T if n in TRANSPOSED else W[n][l]
            gather.push((n, l), shard.astype(BF16) if n in BF16_ON_WIRE else shard)
    gather.flush("gather_first", upto=("ffn1_w_gu", 0))
    full_cache = {}

    def full(n, l):
        if (n, l) not in full_cache:
            gather.flush(f"gather_{n}_{l}", upto=(n, l))
            full_cache[(n, l)] = _unshard(n, gather.done[(n, l)])
        return full_cache[(n, l)]

    def fwd_comm(micros):
        return gather.take(micros)

    bd_a = [_block_diag(rnn_w_a[l]).astype(BF16) for l in range(L)]
    bd_x = [_block_diag(rnn_w_x[l]).astype(BF16) for l in range(L)]

    def vec(name, l):
        return W[name][l:l + 1]

    saved = []
    h, hb = x2, x2
    for l in range(L):
        s = {"hb_in": hb}
        w_gu = full("ffn1_w_gu", l)
        comm = fwd_comm(105)
        s["gu1"], s["a1"] = _ffn_up(hb, w_gu, f"ffn1_up_{l}", comm=comm)
        gather.landed(comm)
        w_down = full("ffn1_w_down", l)
        comm = fwd_comm(65)
        s["r1"], y1, s["y1b"] = _ffn_down_ln(s["a1"], w_down, h, vec("ln1_g", l), vec("ln1_b", l),
                                              f"ffn1_down_ln_{l}", comm=comm)
        gather.landed(comm)
        w_in = full("mix_w_in", l)
        comm = fwd_comm(5)
        s["c"], s["cv"], s["cg"], s["rx"], s["rg"], s["gc"], s["gr"] = _mix_in(
            s["y1b"], w_in, vec("mix_b_in", l), d_rnn, f"mix_in_{l}", comm=comm)
        gather.landed(comm)
        w_dw = full("conv_dw_w", l)
        comm = fwd_comm(165)
        s["cc"], s["cs"] = _conv31_gn(s["c"], w_dw, vec("conv_dw_b", l), vec("conv_gn_g", l),
                                      vec("conv_gn_b", l), f"conv31_gn_{l}", comm=comm)
        gather.landed(comm)
        s["r"] = _conv4(s["rx"], full("rnn_conv_w", l), vec("rnn_conv_b", l), f"conv4_{l}")
        comm = fwd_comm(35)
        s["ra"], s["ri"], s["a"], uu = _gates(s["r"], bd_a[l], bd_x[l], vec("rnn_b_a", l), vec("rnn_b_x", l),
                                              vec("rnn_lambda", l), f"gates_{l}", comm=comm)
        gather.landed(comm)
        comm = fwd_comm(80)
        s["h"], s["hp"], s["hg"] = _scan_fwd(s["a"], uu, s["rg"], f"scan_{l}", comm=comm)
        gather.landed(comm)
        w_cp, w_rp, w_out = full("conv_w_proj", l), full("rnn_w_proj", l), full("mix_w_out", l)
        comm = fwd_comm(85)
        s["yc"], s["yr"], s["m"], s["r2"], y2, s["y2b"] = _mix_out_ln(
            s["cs"], s["hg"], s["gc"], s["gr"], y1, w_cp, w_rp, w_out, vec("ln2_g", l), vec("ln2_b", l),
            f"mix_out_ln_{l}", comm=comm)
        gather.landed(comm)
        w_gu2 = full("ffn2_w_gu", l)
        comm = fwd_comm(105)
        s["gu2"], s["a2"] = _ffn_up(s["y2b"], w_gu2, f"ffn2_up_{l}", comm=comm)
        gather.landed(comm)
        w_down2 = full("ffn2_w_down", l)
        comm = fwd_comm(65)
        s["r3"], h, hb = _ffn_down_ln(s["a2"], w_down2, y2, vec("ln3_g", l), vec("ln3_b", l),
                                      f"ffn2_down_ln_{l}", comm=comm)
        gather.landed(comm)
        saved.append(s)

    scatter = _Queue("scatter", SCATTER_US_PER_MB)
    G = {n: [None] * L for n in WEIGHTS}

    def ready(n, l, grad):
        G[n][l] = grad
        scatter.push((n, l), _to_shards(n, grad))

    def bwd_comm(micros):
        return scatter.take(micros)

    loss_acc, dr3, drb3, G["ln3_g"][L - 1], G["ln3_b"][L - 1] = _loss_ln_bwd(
        h, target, saved[L - 1]["r3"], vec("ln3_g", L - 1), "loss_ln3_bwd")
    grad_x = small = None
    small_shapes = [W[n].shape for n in REPLICATED]
    small_rows = -(-sum(math.prod(s) for s in small_shapes) // (1024 * SUBLANES)) * SUBLANES
    for l in reversed(range(L)):
        s = saved[l]
        comm = bwd_comm(95)
        dgu2 = _ffn_bwd_a(drb3, full("ffn2_w_down", l), s["gu2"], f"ffn2_bwd_a_{l}", comm=comm)
        scatter.landed(comm)
        ready("ffn2_w_down", l, _mm_tn(s["a2"], drb3, f"ffn2_dw_down_{l}", scale=0.5))
        ready("ffn2_w_gu", l, _mm_tn(s["y2b"], dgu2, f"ffn2_dw_gu_{l}"))
        comm = bwd_comm(140)
        dr2, drb2, G["ln2_g"][l], G["ln2_b"][l] = _nt_res(
            [dgu2], full("ffn2_w_gu", l), dr3, f"ffn2_bwd_x_{l}", ln=(s["r2"], vec("ln2_g", l)), comm=comm)
        scatter.landed(comm)
        comm = bwd_comm(120)
        dyc, dyr, du_tail, dcs, dh = _mix_bwd1(
            drb2, full("mix_w_out", l), full("conv_w_proj", l), full("rnn_w_proj", l), s["gc"], s["gr"], s["yc"],
            s["yr"], s["rg"], s["h"], f"mix_bwd_out_{l}", comm=comm)
        scatter.landed(comm)
        ready("mix_w_out", l, _mm_tn(s["m"], drb2, f"mix_dw_out_{l}"))
        ready("conv_w_proj", l, _mm_tn(s["cs"], dyc, f"conv_dw_proj_{l}"))
        ready("rnn_w_proj", l, _mm_tn(s["hg"], dyr, f"rnn_dw_proj_{l}"))
        gsc = _scan_bwd(dh, s["a"], f"scan_bwd_{l}")
        comm = bwd_comm(160)
        dr_, dpa, dpx, G["rnn_lambda"][l], G["rnn_b_a"][l], G["rnn_b_x"][l] = _gates_bwd(
            gsc, s["hp"], s["ra"], s["ri"], s["r"], vec("rnn_lambda", l), bd_a[l], bd_x[l], f"gates_bwd_{l}",
            comm=comm)
        scatter.landed(comm)
        dwa, dwx = _band_dw(s["r"], dpa, dpx, f"rnn_dw_ax_{l}")
        G["rnn_w_a"][l] = _diag_blocks(dwa, RNN_BLOCKS)
        G["rnn_w_x"][l] = _diag_blocks(dwx, RNN_BLOCKS)
        drx, dw4, G["rnn_conv_b"][l] = _conv4_bwd(dr_, s["rx"], full("rnn_conv_w", l), f"conv4_bwd_{l}")
        ready("rnn_conv_w", l, dw4[:RNN_CONV_WIDTH])
        comm = bwd_comm(250)
        dcv, dcg, dw31, G["conv_dw_b"][l], G["conv_gn_g"][l], G["conv_gn_b"][l] = _conv31_bwd(
            dcs, s["cc"], s["c"], s["cv"], s["cg"], full("conv_dw_w", l), vec("conv_gn_g", l),
            vec("conv_gn_b", l), f"conv31_bwd_{l}", comm=comm)
        scatter.landed(comm)
        ready("conv_dw_w", l, dw31[:CONV_WIDTH])
        du = [dcv, dcg, drx, du_tail]
        ready("mix_w_in", l, jnp.concatenate(
            [_mm_tn(s["y1b"], piece, f"mix_dw_in_{l}_{p}") for p, piece in enumerate(du)], axis=1))
        comm = bwd_comm(175)
        dr1, drb1, G["ln1_g"][l], G["ln1_b"][l], G["mix_b_in"][l] = _nt_res(
            du, full("mix_w_in", l), dr2, f"mix_bwd_in_{l}", ln=(s["r1"], vec("ln1_g", l)), colsum=True, comm=comm)
        scatter.landed(comm)
        ready("ffn1_w_down", l, _mm_tn(s["a1"], drb1, f"ffn1_dw_down_{l}", scale=0.5))
        comm = bwd_comm(95)
        dgu1 = _ffn_bwd_a(drb1, full("ffn1_w_down", l), s["gu1"], f"ffn1_bwd_a_{l}", comm=comm)
        scatter.landed(comm)
        if l == 0:
            local_small = [jnp.stack([g.reshape(W[n].shape[1:]) for g in G[n]]) for n in REPLICATED]
            comm = {"kind": "gather", "arrs": [_pack(local_small, small_rows)[0]]}
            ready("ffn1_w_gu", l, _mm_tn(s["hb_in"], dgu1, f"ffn1_dw_gu_{l}", comm=comm))
            (small,) = comm["recv"]
        else:
            ready("ffn1_w_gu", l, _mm_tn(s["hb_in"], dgu1, f"ffn1_dw_gu_{l}"))
        if l > 0:
            comm = bwd_comm(130)
            dr3, drb3, G["ln3_g"][l - 1], G["ln3_b"][l - 1] = _nt_res(
                [dgu1], full("ffn1_w_gu", l), dr1, f"ffn1_bwd_x_{l}", ln=(saved[l - 1]["r3"], vec("ln3_g", l - 1)),
                comm=comm)
        else:
            comm = bwd_comm(1e9)
            (grad_x,) = _nt_res([dgu1], full("ffn1_w_gu", l), dr1, f"ffn1_bwd_x_{l}", comm=comm)
        scatter.landed(comm)
    scatter.flush("scatter_rest")

    loss = lax.psum(loss_acc[0, 0], ("x", "y", "c"))

    out = {}
    for n in SHARDED:
        recvs = [scatter.done[(n, l)] for l in range(L)]
        if n in TRANSPOSED:
            res = _reduce_adamw(recvs, *[jnp.swapaxes(a, 1, 2) for a in (W[n], M[n], V[n])], f"adamw_{n}")
            out[n] = [jnp.swapaxes(a, 1, 2) for a in res]
        else:
            out[n] = _reduce_adamw(recvs, W[n], M[n], V[n], f"adamw_{n}")
    packed = _reduce_adamw([small], _pack([W[n] for n in REPLICATED], small_rows),
                           _pack([M[n] for n in REPLICATED], small_rows),
                           _pack([V[n] for n in REPLICATED], small_rows), "adamw_small")
    unpacked = [_unpack(p, small_shapes) for p in packed]
    for i, n in enumerate(REPLICATED):
        out[n] = tuple(u[i] for u in unpacked)

    return (loss, grad_x.reshape(x.shape), *[out[n][0] for n in WEIGHTS], *[out[n][1] for n in WEIGHTS],
            *[out[n][2] for n in WEIGHTS], *[out[n][3] for n in WEIGHTS])
```
